```python
import math
import jax, jax.numpy as jnp
from jax import lax
import numpy as np

D_MODEL = 1024
BATCH = 4
SEQ = 4096
DEPTH = 1

MEM_LEN = 256
HEAD_DIM = 64
CONV_CH = D_MODEL // 2
CONV_WIDTH = 31
ATT_HEADS = (D_MODEL // 2) // HEAD_DIM
KV_HEADS = 2
GQA_GROUP = ATT_HEADS // KV_HEADS
WINDOW = 128
BLOCK = 128
REL_BUCKETS = 32
REL_MAX_DIST = 128
Q_COLS = ATT_HEADS * HEAD_DIM
KV_COLS = KV_HEADS * HEAD_DIM
MIX_WIDTH = CONV_CH + Q_COLS
IN_COLS = 2 * CONV_CH + Q_COLS + 2 * KV_COLS
X_HEADS = 4
X_HEAD_DIM = D_MODEL // X_HEADS
N_EXPERTS = 64
TOP_K = 8
N_GROUPS = 8
TOPK_GROUPS = 4
EXPERT_FF = D_MODEL // 4
SHARED_FF = EXPERT_FF
ROUTED_SCALE = 2.5
EXPERT_BLOCK = 256
ALPHA = (2 * DEPTH) ** 0.25
BETA = (8 * DEPTH) ** -0.25
LN_EPS = 1e-5
NEG_INF = -1e30

kernel_name = 'hymba_conformer_swa_sink_moe_deepnorm'


def layer_norm(x, g, b):
    xf = x.astype(jnp.float32)
    mu = jnp.mean(xf, axis=-1, keepdims=True)
    var = jnp.mean(jnp.square(xf - mu), axis=-1, keepdims=True)
    return ((xf - mu) * lax.rsqrt(var + LN_EPS) * g + b).astype(x.dtype)


def t5_bucket(dist):
    n = jnp.maximum(dist, 0)
    exact = REL_BUCKETS // 2
    large = exact + (jnp.log(jnp.maximum(n, 1).astype(jnp.float32) / exact)
                     / math.log(REL_MAX_DIST / exact) * (REL_BUCKETS - exact)).astype(jnp.int32)
    large = jnp.minimum(large, REL_BUCKETS - 1)
    return jnp.where(n < exact, n, large)


def band_bias_and_mask(rel_bias, seq):
    n_blocks = seq // BLOCK
    qi = jnp.arange(BLOCK)[:, None]
    kj = jnp.arange(2 * BLOCK)[None, :]
    dist = qi + BLOCK - kj
    bias = rel_bias[t5_bucket(dist)].astype(jnp.float32)
    bias = jnp.transpose(bias, (2, 0, 1)).reshape(KV_HEADS, GQA_GROUP, BLOCK, 2 * BLOCK)
    key_pos = jnp.arange(n_blocks)[:, None, None] * BLOCK - BLOCK + kj[None]
    mask = (dist >= 0) & (dist < WINDOW) & (key_pos >= 0)
    return bias, mask


def sliding_window_sink_attention(q, k, v, sinks, bias, mask):
    b, s = q.shape[:2]
    nb = s // BLOCK
    qb = q.reshape(b, nb, BLOCK, KV_HEADS, GQA_GROUP, HEAD_DIM)

    def band(t):
        prev = jnp.pad(t, ((0, 0), (BLOCK, 0), (0, 0), (0, 0)))[:, :s]
        prev = prev.reshape(b, nb, BLOCK, KV_HEADS, HEAD_DIM)
        cur = t.reshape(b, nb, BLOCK, KV_HEADS, HEAD_DIM)
        return jnp.concatenate([prev, cur], axis=2)

    kb, vb = band(k), band(v)
    logits = jnp.einsum('bnqhgd,bnkhd->bnhgqk', qb, kb).astype(jnp.float32) * (HEAD_DIM ** -0.5)
    logits = jnp.where(mask[None, :, None, None], logits + bias, NEG_INF)
    sink = jnp.broadcast_to(sinks.astype(jnp.float32).reshape(KV_HEADS, GQA_GROUP, 1, 1),
                            logits.shape[:-1] + (1,))
    probs = jax.nn.softmax(jnp.concatenate([logits, sink], axis=-1), axis=-1)[..., :-1]
    out = jnp.einsum('bnhgqk,bnkhd->bnqhgd', probs.astype(vb.dtype), vb)
    return out.reshape(b, s, Q_COLS)


def conformer_conv_group(a, g, conv_w, conv_b, ln_g, ln_b):
    u = a * jax.nn.sigmoid(g)
    u = jnp.pad(u, ((0, 0), (CONV_WIDTH - 1, 0), (0, 0)))
    y = lax.conv_general_dilated(u, conv_w[:, None, :], window_strides=(1,), padding='VALID',
                                 dimension_numbers=('NWC', 'WIO', 'NWC'),
                                 feature_group_count=CONV_CH) + conv_b
    y = layer_norm(y, ln_g, ln_b)
    return y * jax.nn.sigmoid(y)


def hybrid_mixer(x, w_in, b_in, conv_w, conv_b, conv_ln_g, conv_ln_b, sinks, bias, mask, w_out, b_out):
    b, s, _ = x.shape
    proj = x @ w_in + b_in
    cuts = [CONV_CH, 2 * CONV_CH, 2 * CONV_CH + Q_COLS, 2 * CONV_CH + Q_COLS + KV_COLS]
    a, g, q, k, v = jnp.split(proj, cuts, axis=-1)
    conv_out = conformer_conv_group(a, g, conv_w, conv_b, conv_ln_g, conv_ln_b)
    attn_out = sliding_window_sink_attention(
        q.reshape(b, s, KV_HEADS, GQA_GROUP, HEAD_DIM),
        k.reshape(b, s, KV_HEADS, HEAD_DIM),
        v.reshape(b, s, KV_HEADS, HEAD_DIM), sinks, bias, mask)
    return jnp.concatenate([conv_out, attn_out], axis=-1) @ w_out + b_out


def memory_cross_attention(x, mem, wq, wkv, wo):
    b, s, _ = x.shape
    q = (x @ wq).reshape(b, s, X_HEADS, X_HEAD_DIM)
    k, v = jnp.split(mem @ wkv, 2, axis=-1)
    k = k.reshape(b, -1, X_HEADS, X_HEAD_DIM)
    v = v.reshape(b, -1, X_HEADS, X_HEAD_DIM)
    logits = jnp.einsum('bshd,bmhd->bhsm', q, k).astype(jnp.float32) * (X_HEAD_DIM ** -0.5)
    probs = jax.nn.softmax(logits, axis=-1).astype(v.dtype)
    out = jnp.einsum('bhsm,bmhd->bshd', probs, v).reshape(b, s, D_MODEL)
    return out @ wo


def route(xt, router_w, router_b):
    t = xt.shape[0]
    scores = jax.nn.sigmoid((xt @ router_w).astype(jnp.float32))
    choice = scores + router_b.astype(jnp.float32)
    grp = choice.reshape(t, N_GROUPS, N_EXPERTS // N_GROUPS)
    grp_score = jnp.sum(lax.top_k(grp, 2)[0], axis=-1)
    _, grp_idx = lax.top_k(grp_score, TOPK_GROUPS)
    grp_keep = jnp.any(grp_idx[:, :, None] == jnp.arange(N_GROUPS)[None, None, :], axis=1)
    keep = jnp.repeat(grp_keep, N_EXPERTS // N_GROUPS, axis=1)
    _, idx = lax.top_k(jnp.where(keep, choice, -jnp.inf), TOP_K)
    w = jnp.take_along_axis(scores, idx, axis=1)
    w = w / jnp.sum(w, axis=-1, keepdims=True) * ROUTED_SCALE
    return idx, w


def moe_ffn(h, router_w, router_b, exp_gate, exp_up, exp_down, sh_gate, sh_up, sh_down):
    b, s, d = h.shape
    xt = h.reshape(-1, d)
    t = xt.shape[0]
    idx, gate = route(xt, router_w, router_b)
    n_assign = t * TOP_K
    flat_e = idx.reshape(-1)
    flat_tok = jnp.repeat(jnp.arange(t, dtype=jnp.int32), TOP_K)
    flat_w = gate.reshape(-1)
    order = jnp.argsort(flat_e)
    e_s, tok_s, w_s = flat_e[order], flat_tok[order], flat_w[order]
    counts = jnp.bincount(flat_e, length=N_EXPERTS)
    seg_start = jnp.cumsum(counts) - counts
    pad_counts = (counts + EXPERT_BLOCK - 1) // EXPERT_BLOCK * EXPERT_BLOCK
    pad_end = jnp.cumsum(pad_counts)
    pad_start = pad_end - pad_counts
    dest = pad_start[e_s] + (jnp.arange(n_assign) - seg_start[e_s])
    n_blocks = -(-n_assign // EXPERT_BLOCK) + N_EXPERTS
    n_rows = n_blocks * EXPERT_BLOCK
    row_tok = jnp.full((n_rows,), t, jnp.int32).at[dest].set(tok_s)
    row_w = jnp.zeros((n_rows,), jnp.float32).at[dest].set(w_s)
    block_e = jnp.minimum(jnp.searchsorted(pad_end, jnp.arange(n_blocks) * EXPERT_BLOCK, side='right'),
                          N_EXPERTS - 1)
    x_pad = jnp.concatenate([xt, jnp.zeros((1, d), xt.dtype)], axis=0)
    xb = x_pad[row_tok].reshape(n_blocks, EXPERT_BLOCK, d)

    def expert_block(args):
        xblk, e = args
        return (jax.nn.silu(xblk @ exp_gate[e]) * (xblk @ exp_up[e])) @ exp_down[e]

    yb = lax.map(expert_block, (xb, block_e)).reshape(n_rows, d)
    routed = jax.ops.segment_sum(yb * row_w[:, None].astype(yb.dtype), row_tok, num_segments=t + 1)[:t]
    shared = (jax.nn.silu(xt @ sh_gate) * (xt @ sh_up)) @ sh_down
    return (routed + shared).reshape(b, s, d)


def setup_inputs(seed: int = 0) -> dict:
    key = jax.random.key(seed)
    ks = jax.random.split(key, 32)
    L, D = DEPTH, D_MODEL
    f32 = jnp.float32

    def nrm(k, shape, scale):
        return jax.random.normal(k, shape, f32) * scale

    def gain(k, shape):
        return 1.0 + 0.05 * jax.random.normal(k, shape, f32)

    return {
        'x': nrm(ks[0], (BATCH, SEQ, D), 1.0),
        'mem': nrm(ks[1], (BATCH, MEM_LEN, D), 1.0),
        'w_in': nrm(ks[2], (L, D, IN_COLS), D ** -0.5),
        'b_in': nrm(ks[3], (L, IN_COLS), 0.01),
        'conv_w': nrm(ks[4], (L, CONV_WIDTH, CONV_CH), CONV_WIDTH ** -0.5),
        'conv_b': nrm(ks[5], (L, CONV_CH), 0.01),
        'conv_ln_g': gain(ks[6], (L, CONV_CH)),
        'conv_ln_b': nrm(ks[7], (L, CONV_CH), 0.01),
        'attn_sinks': nrm(ks[8], (L, ATT_HEADS), 0.5),
        'rel_bias': nrm(ks[9], (REL_BUCKETS, ATT_HEADS), 0.5),
        'w_out': nrm(ks[10], (L, MIX_WIDTH, D), BETA * MIX_WIDTH ** -0.5),
        'b_out': nrm(ks[11], (L, D), 0.01),
        'ln1_g': gain(ks[12], (L, D)),
        'ln1_b': nrm(ks[13], (L, D), 0.01),
        'xq_w': nrm(ks[14], (L, D, D), D ** -0.5),
        'xkv_w': nrm(ks[15], (L, D, 2 * D), D ** -0.5),
        'xo_w': nrm(ks[16], (L, D, D), BETA * D ** -0.5),
        'ln2_g': gain(ks[17], (L, D)),
        'ln2_b': nrm(ks[18], (L, D), 0.01),
        'router_w': nrm(ks[19], (L, D, N_EXPERTS), D ** -0.5),
        'router_b': nrm(ks[20], (L, N_EXPERTS), 0.01),
        'exp_gate': nrm(ks[21], (L, N_EXPERTS, D, EXPERT_FF), D ** -0.5),
        'exp_up': nrm(ks[22], (L, N_EXPERTS, D, EXPERT_FF), D ** -0.5),
        'exp_down': nrm(ks[23], (L, N_EXPERTS, EXPERT_FF, D), BETA * EXPERT_FF ** -0.5),
        'sh_gate': nrm(ks[24], (L, D, SHARED_FF), D ** -0.5),
        'sh_up': nrm(ks[25], (L, D, SHARED_FF), D ** -0.5),
        'sh_down': nrm(ks[26], (L, SHARED_FF, D), BETA * SHARED_FF ** -0.5),
        'ln3_g': gain(ks[27], (L, D)),
        'ln3_b': nrm(ks[28], (L, D), 0.01),
    }


def reference(x, mem, w_in, b_in, conv_w, conv_b, conv_ln_g, conv_ln_b, attn_sinks, rel_bias,
              w_out, b_out, ln1_g, ln1_b, xq_w, xkv_w, xo_w, ln2_g, ln2_b, router_w, router_b,
              exp_gate, exp_up, exp_down, sh_gate, sh_up, sh_down, ln3_g, ln3_b):
    bias, mask = band_bias_and_mask(rel_bias, x.shape[1])
    for l in range(DEPTH):
        mix = hybrid_mixer(x, w_in[l], b_in[l], conv_w[l], conv_b[l], conv_ln_g[l], conv_ln_b[l],
                           attn_sinks[l], bias, mask, w_out[l], b_out[l])
        x = layer_norm(ALPHA * x + mix, ln1_g[l], ln1_b[l])
        cross = memory_cross_attention(x, mem, xq_w[l], xkv_w[l], xo_w[l])
        x = layer_norm(ALPHA * x + cross, ln2_g[l], ln2_b[l])
        ffn = moe_ffn(x, router_w[l], router_b[l], exp_gate[l], exp_up[l], exp_down[l],
                      sh_gate[l], sh_up[l], sh_down[l])
        x = layer_norm(ALPHA * x + ffn, ln3_g[l], ln3_b[l])
    return x
```

```python
import functools
import math

import jax
import jax.numpy as jnp
from jax import lax
from jax.experimental import pallas as pl
from jax.experimental.pallas import tpu as pltpu

D_MODEL = 1024
MEM_LEN = 256
HEAD_DIM = 64
CONV_CH = D_MODEL // 2
CONV_WIDTH = 31
ATT_HEADS = 8
KV_HEADS = 2
WINDOW = 128
BLOCK = 128
REL_BUCKETS = 32
REL_MAX_DIST = 128
Q_COLS = ATT_HEADS * HEAD_DIM
KV_COLS = KV_HEADS * HEAD_DIM
IN_COLS = 2 * CONV_CH + Q_COLS + 2 * KV_COLS
X_HEADS = 4
X_HEAD_DIM = D_MODEL // X_HEADS
N_EXPERTS = 64
TOP_K = 8
N_GROUPS = 8
GROUP_SIZE = N_EXPERTS // N_GROUPS
TOPK_GROUPS = 4
EXPERT_FF = D_MODEL // 4
ROUTED_SCALE = 2.5
DEPTH = 1
ALPHA = (2 * DEPTH) ** 0.25
LN_EPS = 1e-5
NEG_INF = -1e30

F32 = jnp.float32
BF16 = jnp.bfloat16

VMEM_LIMIT_BYTES = 56 * 1024 * 1024

ROW_TILE = 512
CONV_ROWS = 32
CONV_HALO = 32
ROUTE_TILE = 512
EXPERT_ROWS = 256
COMBINE_TILE = 256


def _cparams(sem):
    return pltpu.CompilerParams(dimension_semantics=sem, vmem_limit_bytes=VMEM_LIMIT_BYTES)


def _layer_norm(h, g, b):
    mu = jnp.mean(h, axis=-1, keepdims=True)
    d = h - mu
    var = jnp.mean(d * d, axis=-1, keepdims=True)
    return d * lax.rsqrt(var + LN_EPS) * g + b


def _sigmoid(x):
    return 1.0 / (1.0 + jnp.exp(-x))


def _mix_kernel(x_ref, w_ref, b_ref, cw_ref, cb_ref, cg_ref, cbeta_ref,
                conv_ref, q_ref, k_ref, v_ref, u_ext):
    j = pl.program_id(1)
    xb = x_ref[0].astype(BF16)
    proj = jnp.dot(xb, w_ref[...], preferred_element_type=F32) + b_ref[...]
    a = proj[:, :CONV_CH]
    g = proj[:, CONV_CH:2 * CONV_CH]
    q_ref[0] = (proj[:, 2 * CONV_CH:2 * CONV_CH + Q_COLS] * (HEAD_DIM ** -0.5)).astype(BF16)
    k_ref[0] = proj[:, 2 * CONV_CH + Q_COLS:2 * CONV_CH + Q_COLS + KV_COLS].astype(BF16)
    v_ref[0] = proj[:, 2 * CONV_CH + Q_COLS + KV_COLS:].astype(BF16)

    @pl.when(j == 0)
    def _():
        u_ext[0:CONV_HALO, :] = jnp.zeros((CONV_HALO, CONV_CH), F32)

    u_ext[CONV_HALO:CONV_HALO + ROW_TILE, :] = a * _sigmoid(g)

    first_tap = CONV_HALO - (CONV_WIDTH - 1)
    for c in range(ROW_TILE // CONV_ROWS):
        base = c * CONV_ROWS + first_tap
        acc = jnp.zeros((CONV_ROWS, CONV_CH), F32) + cb_ref[...]
        for t in range(CONV_WIDTH):
            acc = acc + u_ext[base + t:base + t + CONV_ROWS, :] * cw_ref[t:t + 1, :]
        y = _layer_norm(acc, cg_ref[...], cbeta_ref[...])
        conv_ref[0, c * CONV_ROWS:(c + 1) * CONV_ROWS, :] = (y * _sigmoid(y)).astype(BF16)

    u_ext[0:CONV_HALO, :] = u_ext[ROW_TILE:ROW_TILE + CONV_HALO, :]


def _mix(x, w_in, b_in, conv_w, conv_b, conv_g, conv_beta):
    bsz, seq, d = x.shape
    nt = seq // ROW_TILE
    row = lambda b, j: (b, j, 0)
    const2 = lambda b, j: (0, 0)
    return pl.pallas_call(
        _mix_kernel,
        grid=(bsz, nt),
        in_specs=[
            pl.BlockSpec((1, ROW_TILE, d), row),
            pl.BlockSpec((d, IN_COLS), const2),
            pl.BlockSpec((1, IN_COLS), const2),
            pl.BlockSpec((CONV_WIDTH, CONV_CH), const2),
            pl.BlockSpec((1, CONV_CH), const2),
            pl.BlockSpec((1, CONV_CH), const2),
            pl.BlockSpec((1, CONV_CH), const2),
        ],
        out_specs=[
            pl.BlockSpec((1, ROW_TILE, CONV_CH), row),
            pl.BlockSpec((1, ROW_TILE, Q_COLS), row),
            pl.BlockSpec((1, ROW_TILE, KV_COLS), row),
            pl.BlockSpec((1, ROW_TILE, KV_COLS), row),
        ],
        out_shape=[
            jax.ShapeDtypeStruct((bsz, seq, CONV_CH), BF16),
            jax.ShapeDtypeStruct((bsz, seq, Q_COLS), BF16),
            jax.ShapeDtypeStruct((bsz, seq, KV_COLS), BF16),
            jax.ShapeDtypeStruct((bsz, seq, KV_COLS), BF16),
        ],
        scratch_shapes=[pltpu.VMEM((ROW_TILE + CONV_HALO, CONV_CH), F32)],
        compiler_params=_cparams(("arbitrary", "arbitrary")),
        name="mix",
    )(x, w_in, b_in, conv_w, conv_b, conv_g, conv_beta)


def _swa_kernel(q_ref, kp_ref, kc_ref, vp_ref, vc_ref, bias_ref, sink_ref, o_ref):
    band = 2 * BLOCK
    lane = lax.broadcasted_iota(jnp.int32, (band, 2 * HEAD_DIM), 1)
    low = lane < HEAD_DIM

    def placements(prev_ref, cur_ref):
        t = jnp.concatenate([prev_ref[0], cur_ref[0]], axis=0).astype(F32)
        tr = pltpu.roll(t, HEAD_DIM, 1)
        zero = jnp.zeros_like(t)
        kv0_low = jnp.where(low, t, zero).astype(BF16)
        kv1_high = jnp.where(low, zero, t).astype(BF16)
        kv1_low = jnp.where(low, tr, zero).astype(BF16)
        kv0_high = jnp.where(low, zero, tr).astype(BF16)
        return (kv0_low, kv0_high, kv1_low, kv1_high)

    ks = placements(kp_ref, kc_ref)
    vs = placements(vp_ref, vc_ref)
    q = q_ref[0]
    slab = 2 * HEAD_DIM
    q_kv0 = jnp.concatenate([q[:, 0:slab], q[:, slab:2 * slab]], axis=0)
    q_kv1 = jnp.concatenate([q[:, 2 * slab:3 * slab], q[:, 3 * slab:4 * slab]], axis=0)
    outs = []
    for s in range(4):
        qs = q_kv0 if s < 2 else q_kv1
        logits = lax.dot_general(qs, ks[s], (((1,), (1,)), ((), ())),
                                 preferred_element_type=F32) + bias_ref[0, s]
        sink = sink_ref[s]
        m = jnp.maximum(jnp.max(logits, axis=-1, keepdims=True), sink)
        p = jnp.exp(logits - m)
        den = jnp.sum(p, axis=-1, keepdims=True) + jnp.exp(sink - m)
        o = jnp.dot(p.astype(BF16), vs[s], preferred_element_type=F32)
        outs.append(o / den)
    o_kv0 = outs[0] + outs[1]
    o_kv1 = outs[2] + outs[3]
    o_ref[0, :, 0:slab] = o_kv0[0:BLOCK].astype(BF16)
    o_ref[0, :, slab:2 * slab] = o_kv0[BLOCK:2 * BLOCK].astype(BF16)
    o_ref[0, :, 2 * slab:3 * slab] = o_kv1[0:BLOCK].astype(BF16)
    o_ref[0, :, 3 * slab:4 * slab] = o_kv1[BLOCK:2 * BLOCK].astype(BF16)


def _swa(q, k, v, bias_tab, sink_tab):
    bsz, seq, _ = q.shape
    nb = seq // BLOCK
    cur = lambda b, n: (b, n, 0)
    prev = lambda b, n: (b, jnp.maximum(n - 1, 0), 0)
    return pl.pallas_call(
        _swa_kernel,
        grid=(bsz, nb),
        in_specs=[
            pl.BlockSpec((1, BLOCK, Q_COLS), cur),
            pl.BlockSpec((1, BLOCK, KV_COLS), prev),
            pl.BlockSpec((1, BLOCK, KV_COLS), cur),
            pl.BlockSpec((1, BLOCK, KV_COLS), prev),
            pl.BlockSpec((1, BLOCK, KV_COLS), cur),
            pl.BlockSpec((1, 4, 2 * BLOCK, 2 * BLOCK), lambda b, n: (jnp.minimum(n, 1), 0, 0, 0)),
            pl.BlockSpec((4, 2 * BLOCK, 1), lambda b, n: (0, 0, 0)),
        ],
        out_specs=pl.BlockSpec((1, BLOCK, Q_COLS), cur),
        out_shape=jax.ShapeDtypeStruct((bsz, seq, Q_COLS), BF16),
        compiler_params=_cparams(("arbitrary", "arbitrary")),
        name="swa",
    )(q, k, k, v, v, bias_tab, sink_tab)


def _t5_bucket(dist):
    n = jnp.maximum(dist, 0)
    exact = REL_BUCKETS // 2
    large = exact + (jnp.log(jnp.maximum(n, 1).astype(F32) / exact)
                     / math.log(REL_MAX_DIST / exact) * (REL_BUCKETS - exact)).astype(jnp.int32)
    large = jnp.minimum(large, REL_BUCKETS - 1)
    return jnp.where(n < exact, n, large)


def _band_tables(rel_bias, sinks):
    qi = jnp.arange(BLOCK)[:, None]
    kj = jnp.arange(2 * BLOCK)[None, :]
    dist = qi + BLOCK - kj
    bias = jnp.transpose(rel_bias[_t5_bucket(dist)].astype(F32), (2, 0, 1))
    in_window = (dist >= 0) & (dist < WINDOW)
    masks = jnp.stack([in_window & (kj >= BLOCK), in_window])
    masked = jnp.where(masks[:, None], bias[None], NEG_INF)
    pairs = ((0, 2), (1, 3), (4, 6), (5, 7))
    bias_tab = jnp.stack([jnp.concatenate([masked[:, a], masked[:, b]], axis=1) for a, b in pairs], axis=1)
    s = sinks.astype(F32)
    sink_tab = jnp.stack([jnp.concatenate([jnp.full((BLOCK, 1), s[a]), jnp.full((BLOCK, 1), s[b])], axis=0)
                          for a, b in pairs])
    return bias_tab, sink_tab


def _outproj_kernel(x_ref, conv_ref, att_ref, w_ref, b_ref, g_ref, beta_ref, o_ref):
    mix = jnp.dot(conv_ref[...], w_ref[0:CONV_CH, :], preferred_element_type=F32)
    mix = mix + jnp.dot(att_ref[...], w_ref[CONV_CH:, :], preferred_element_type=F32)
    h = ALPHA * x_ref[...] + mix + b_ref[...]
    o_ref[...] = _layer_norm(h, g_ref[...], beta_ref[...])


def _outproj(x2d, conv2d, att2d, w_out, b_out, g, beta):
    t, d = x2d.shape
    row = lambda i: (i, 0)
    const = lambda i: (0, 0)
    return pl.pallas_call(
        _outproj_kernel,
        grid=(t // ROW_TILE,),
        in_specs=[
            pl.BlockSpec((ROW_TILE, d), row),
            pl.BlockSpec((ROW_TILE, CONV_CH), row),
            pl.BlockSpec((ROW_TILE, Q_COLS), row),
            pl.BlockSpec((d, d), const),
            pl.BlockSpec((1, d), const),
            pl.BlockSpec((1, d), const),
            pl.BlockSpec((1, d), const),
        ],
        out_specs=pl.BlockSpec((ROW_TILE, d), row),
        out_shape=jax.ShapeDtypeStruct((t, d), F32),
        compiler_params=_cparams(("arbitrary",)),
        name="outproj",
    )(x2d, conv2d, att2d, w_out, b_out, g, beta)


def _memkv_kernel(mem_ref, w_ref, k_ref, v_ref):
    kv = jnp.dot(mem_ref[0].astype(BF16), w_ref[...], preferred_element_type=F32)
    k_ref[0] = kv[:, :D_MODEL].astype(BF16)
    v_ref[0] = kv[:, D_MODEL:].astype(BF16)


def _memkv(mem, wkv):
    bsz, m, d = mem.shape
    return pl.pallas_call(
        _memkv_kernel,
        grid=(bsz,),
        in_specs=[pl.BlockSpec((1, m, d), lambda b: (b, 0, 0)),
                  pl.BlockSpec((d, 2 * d), lambda b: (0, 0))],
        out_specs=[pl.BlockSpec((1, m, d), lambda b: (b, 0, 0)),
                   pl.BlockSpec((1, m, d), lambda b: (b, 0, 0))],
        out_shape=[jax.ShapeDtypeStruct((bsz, m, d), BF16)] * 2,
        compiler_params=_cparams(("arbitrary",)),
        name="memkv",
    )(mem, wkv)


def _cross_kernel(x_ref, k_ref, v_ref, wq_ref, wo_ref, g_ref, beta_ref, rw_ref, o_ref, lt_ref):
    x = x_ref[0]
    q = jnp.dot(x.astype(BF16), wq_ref[...], preferred_element_type=F32) * (X_HEAD_DIM ** -0.5)
    q = q.astype(BF16)
    heads = []
    for h in range(X_HEADS):
        cols = slice(h * X_HEAD_DIM, (h + 1) * X_HEAD_DIM)
        logits = lax.dot_general(q[:, cols], k_ref[0, :, cols], (((1,), (1,)), ((), ())),
                                 preferred_element_type=F32)
        m = jnp.max(logits, axis=-1, keepdims=True)
        p = jnp.exp(logits - m)
        den = jnp.sum(p, axis=-1, keepdims=True)
        o = jnp.dot(p.astype(BF16), v_ref[0, :, cols], preferred_element_type=F32)
        heads.append((o / den).astype(BF16))
    att = jnp.concatenate(heads, axis=-1)
    cross = jnp.dot(att, wo_ref[...], preferred_element_type=F32)
    y = _layer_norm(ALPHA * x + cross, g_ref[...], beta_ref[...])
    o_ref[0] = y
    lt_ref[...] = lax.dot_general(rw_ref[...], y.astype(BF16), (((1,), (1,)), ((), ())),
                                  preferred_element_type=F32)


def _cross(x1, kmem, vmem, wq, wo, g, beta, rw_t):
    bsz, seq, d = x1.shape
    nt = seq // ROW_TILE
    row = lambda b, j: (b, j, 0)
    mem = lambda b, j: (b, 0, 0)
    const = lambda b, j: (0, 0)
    return pl.pallas_call(
        _cross_kernel,
        grid=(bsz, nt),
        in_specs=[
            pl.BlockSpec((1, ROW_TILE, d), row),
            pl.BlockSpec((1, MEM_LEN, d), mem),
            pl.BlockSpec((1, MEM_LEN, d), mem),
            pl.BlockSpec((d, d), const),
            pl.BlockSpec((d, d), const),
            pl.BlockSpec((1, d), const),
            pl.BlockSpec((1, d), const),
            pl.BlockSpec((N_EXPERTS, d), const),
        ],
        out_specs=[
            pl.BlockSpec((1, ROW_TILE, d), row),
            pl.BlockSpec((N_EXPERTS, ROW_TILE), lambda b, j: (0, b * nt + j)),
        ],
        out_shape=[
            jax.ShapeDtypeStruct((bsz, seq, d), F32),
            jax.ShapeDtypeStruct((N_EXPERTS, bsz * seq), F32),
        ],
        compiler_params=_cparams(("arbitrary", "arbitrary")),
        name="cross",
    )(x1, kmem, vmem, wq, wo, g, beta, rw_t)


def _route_kernel(lt_ref, rb_ref, tri_ref, idx_ref, rank_ref, gate_ref, cnt_ref, carry):
    i = pl.program_id(0)

    @pl.when(i == 0)
    def _():
        carry[...] = jnp.zeros_like(carry)

    tn = ROUTE_TILE
    scores = _sigmoid(lt_ref[...])
    choice = scores + rb_ref[...]

    gscore = []
    member = lax.broadcasted_iota(jnp.int32, (GROUP_SIZE, tn), 0).astype(F32)
    for g in range(N_GROUPS):
        c = choice[g * GROUP_SIZE:(g + 1) * GROUP_SIZE, :]
        m1 = jnp.max(c, axis=0, keepdims=True)
        first = jnp.min(jnp.where(c == m1, member, float(GROUP_SIZE)), axis=0, keepdims=True)
        m2 = jnp.max(jnp.where(member == first, -jnp.inf, c), axis=0, keepdims=True)
        gscore.append(m1 + m2)

    keep_rows = []
    for g in range(N_GROUPS):
        beaten = jnp.zeros((1, tn), F32)
        for o in range(N_GROUPS):
            if o == g:
                continue
            ahead = (gscore[o] >= gscore[g]) if o < g else (gscore[o] > gscore[g])
            beaten = beaten + jnp.where(ahead, 1.0, 0.0)
        keep_rows.append(jnp.broadcast_to(beaten, (GROUP_SIZE, tn)))
    beaten_all = jnp.concatenate(keep_rows, axis=0)

    masked = jnp.where(beaten_all < TOPK_GROUPS, choice, -jnp.inf)
    eidx = lax.broadcasted_iota(jnp.int32, (N_EXPERTS, tn), 0).astype(F32)
    sel = jnp.zeros((N_EXPERTS, tn), F32)
    picks, weights = [], []
    for r in range(TOP_K):
        mx = jnp.max(masked, axis=0, keepdims=True)
        first = jnp.min(jnp.where(masked == mx, eidx, float(N_EXPERTS)), axis=0, keepdims=True)
        pick = eidx == first
        picks.append((pick, first))
        weights.append(jnp.sum(jnp.where(pick, scores, 0.0), axis=0, keepdims=True))
        masked = jnp.where(pick, -jnp.inf, masked)
        sel = jnp.where(pick, 1.0, sel)

    wsum = weights[0]
    for r in range(1, TOP_K):
        wsum = wsum + weights[r]

    incl = jnp.dot(sel.astype(BF16), tri_ref[...], preferred_element_type=F32)
    rank_mat = carry[...] + incl - sel
    carry[...] = carry[...] + jnp.sum(sel, axis=1, keepdims=True)
    cnt_ref[...] = jnp.broadcast_to(carry[...], cnt_ref.shape).astype(jnp.int32)

    for r in range(TOP_K):
        pick, first = picks[r]
        idx_ref[0, r:r + 1, :] = first.astype(jnp.int32)
        rank_ref[0, r:r + 1, :] = jnp.sum(jnp.where(pick, rank_mat, 0.0), axis=0,
                                          keepdims=True).astype(jnp.int32)
        gate_ref[0, r:r + 1, :] = weights[r] / wsum * ROUTED_SCALE


def _route(logits_t, router_b, tri):
    e, t = logits_t.shape
    nt = t // ROUTE_TILE
    blk = lambda i: (i, 0, 0)
    return pl.pallas_call(
        _route_kernel,
        grid=(nt,),
        in_specs=[
            pl.BlockSpec((e, ROUTE_TILE), lambda i: (0, i)),
            pl.BlockSpec((e, 1), lambda i: (0, 0)),
            pl.BlockSpec((ROUTE_TILE, ROUTE_TILE), lambda i: (0, 0)),
        ],
        out_specs=[
            pl.BlockSpec((1, TOP_K, ROUTE_TILE), blk),
            pl.BlockSpec((1, TOP_K, ROUTE_TILE), blk),
            pl.BlockSpec((1, TOP_K, ROUTE_TILE), blk),
            pl.BlockSpec((e, 128), lambda i: (0, 0)),
        ],
        out_shape=[
            jax.ShapeDtypeStruct((nt, TOP_K, ROUTE_TILE), jnp.int32),
            jax.ShapeDtypeStruct((nt, TOP_K, ROUTE_TILE), jnp.int32),
            jax.ShapeDtypeStruct((nt, TOP_K, ROUTE_TILE), F32),
            jax.ShapeDtypeStruct((e, 128), jnp.int32),
        ],
        scratch_shapes=[pltpu.VMEM((e, 1), F32)],
        compiler_params=_cparams(("arbitrary",)),
        name="route",
    )(logits_t, router_b, tri)


def _dispatch_kernel(seg_ref, idx_ref, rank_ref, x_ref, xs_ref, sem):
    tn = ROUTE_TILE

    def body(t, carry):
        for k in range(TOP_K):
            dst = seg_ref[idx_ref[0, k, t]] + rank_ref[0, k, t]
            pltpu.make_async_copy(x_ref.at[pl.ds(t, 1), :], xs_ref.at[pl.ds(dst, 1), :], sem).start()
        return carry

    lax.fori_loop(0, tn, body, 0)
    for k in range(TOP_K):
        pltpu.make_async_copy(x_ref, xs_ref.at[pl.ds(0, tn), :], sem).wait()


def _dispatch(seg_start, idx, rank, x2d):
    t, d = x2d.shape
    nt = t // ROUTE_TILE
    smem_blk = pl.BlockSpec((1, TOP_K, ROUTE_TILE), lambda i, seg: (i, 0, 0), memory_space=pltpu.SMEM)
    return pl.pallas_call(
        _dispatch_kernel,
        grid_spec=pltpu.PrefetchScalarGridSpec(
            num_scalar_prefetch=1,
            grid=(nt,),
            in_specs=[smem_blk, smem_blk, pl.BlockSpec((ROUTE_TILE, d), lambda i, seg: (i, 0))],
            out_specs=pl.BlockSpec(memory_space=pl.ANY),
            scratch_shapes=[pltpu.SemaphoreType.DMA(())],
        ),
        out_shape=jax.ShapeDtypeStruct((t * TOP_K, d), F32),
        compiler_params=_cparams(("arbitrary",)),
        name="dispatch",
    )(seg_start, idx, rank, x2d)


def _experts_kernel(ib_ref, ie_ref, flag_ref, seg_ref, xs_ref, wg_ref, wu_ref, wd_ref, y_ref,
                    wg_bf, wu_bf, wd_bf):
    w = pl.program_id(0)
    flags = flag_ref[w]
    valid = (flags & 1) != 0
    first_of_block = (flags & 2) != 0
    new_expert = (flags & 4) != 0

    @pl.when(new_expert)
    def _():
        wg_bf[...] = wg_ref[0].astype(BF16)
        wu_bf[...] = wu_ref[0].astype(BF16)
        wd_bf[...] = wd_ref[0].astype(BF16)

    @pl.when(valid)
    def _():
        e = ie_ref[w]
        x = xs_ref[...].astype(BF16)
        gte = jnp.dot(x, wg_bf[...], preferred_element_type=F32)
        up = jnp.dot(x, wu_bf[...], preferred_element_type=F32)
        h = (gte * _sigmoid(gte) * up).astype(BF16)
        y = jnp.dot(h, wd_bf[...], preferred_element_type=F32)
        rows = ib_ref[w] * EXPERT_ROWS + lax.broadcasted_iota(jnp.int32, (EXPERT_ROWS, 1), 0)
        mine = (rows >= seg_ref[e]) & (rows < seg_ref[e + 1])
        contrib = jnp.where(mine, y, 0.0)

        @pl.when(first_of_block)
        def _():
            y_ref[...] = contrib

        @pl.when(jnp.logical_not(first_of_block))
        def _():
            y_ref[...] = y_ref[...] + contrib


def _experts(item_block, item_expert, item_flags, seg_bounds, xs, wg, wu, wd):
    n, d = xs.shape
    n_items = item_block.shape[0]
    ff = wg.shape[-1]
    return pl.pallas_call(
        _experts_kernel,
        grid_spec=pltpu.PrefetchScalarGridSpec(
            num_scalar_prefetch=4,
            grid=(n_items,),
            in_specs=[
                pl.BlockSpec((EXPERT_ROWS, d), lambda w, ib, ie, fl, sg: (ib[w], 0)),
                pl.BlockSpec((1, d, ff), lambda w, ib, ie, fl, sg: (ie[w], 0, 0)),
                pl.BlockSpec((1, d, ff), lambda w, ib, ie, fl, sg: (ie[w], 0, 0)),
                pl.BlockSpec((1, ff, d), lambda w, ib, ie, fl, sg: (ie[w], 0, 0)),
            ],
            out_specs=pl.BlockSpec((EXPERT_ROWS, d), lambda w, ib, ie, fl, sg: (ib[w], 0)),
            scratch_shapes=[pltpu.VMEM((d, ff), BF16), pltpu.VMEM((d, ff), BF16), pltpu.VMEM((ff, d), BF16)],
        ),
        out_shape=jax.ShapeDtypeStruct((n, d), F32),
        compiler_params=_cparams(("arbitrary",)),
        name="experts",
    )(item_block, item_expert, item_flags, seg_bounds, xs, wg, wu, wd)


def _work_items(counts, n_rows):
    n_items = n_rows // EXPERT_ROWS + N_EXPERTS
    seg_end = jnp.cumsum(counts)
    seg_start = seg_end - counts
    first_blk = seg_start // EXPERT_ROWS
    n_blk = jnp.where(counts > 0, (seg_end - 1) // EXPERT_ROWS - first_blk + 1, 0)
    item_end = jnp.cumsum(n_blk)
    item_start = item_end - n_blk
    total = item_end[-1]
    w = jnp.arange(n_items, dtype=jnp.int32)
    wc = jnp.minimum(w, total - 1)
    e = jnp.minimum(jnp.searchsorted(item_end, wc, side='right'), N_EXPERTS - 1).astype(jnp.int32)
    b = (first_blk[e] + (wc - item_start[e])).astype(jnp.int32)
    valid = w < total
    prev_b = jnp.concatenate([jnp.full((1,), -1, jnp.int32), b[:-1]])
    prev_e = jnp.concatenate([jnp.full((1,), -1, jnp.int32), e[:-1]])
    flags = (valid.astype(jnp.int32) + 2 * (valid & (b != prev_b)).astype(jnp.int32)
             + 4 * (valid & (e != prev_e)).astype(jnp.int32))
    seg_bounds = jnp.concatenate([seg_start, seg_end[-1:]]).astype(jnp.int32)
    return b, e, flags, seg_bounds


def _combine_kernel(seg_ref, idx_ref, rank_ref, x_ref, gate_ref, sg_ref, su_ref, sd_ref,
                    g_ref, beta_ref, yb_ref, o_ref, buf, sem):
    tn = COMBINE_TILE
    off = (pl.program_id(0) % (ROUTE_TILE // COMBINE_TILE)) * COMBINE_TILE

    def body(t, carry):
        for k in range(TOP_K):
            src = seg_ref[idx_ref[0, k, off + t]] + rank_ref[0, k, off + t]
            pltpu.make_async_copy(yb_ref.at[pl.ds(src, 1), :], buf.at[k, pl.ds(t, 1), :], sem).start()
        return carry

    lax.fori_loop(0, tn, body, 0)

    x = x_ref[...]
    xb = x.astype(BF16)
    gte = jnp.dot(xb, sg_ref[...], preferred_element_type=F32)
    up = jnp.dot(xb, su_ref[...], preferred_element_type=F32)
    h = (gte * _sigmoid(gte) * up).astype(BF16)
    acc = jnp.dot(h, sd_ref[...], preferred_element_type=F32)

    for k in range(TOP_K):
        pltpu.make_async_copy(yb_ref.at[pl.ds(0, tn), :], buf.at[k], sem).wait()
    gates = gate_ref[...]
    for k in range(TOP_K):
        acc = acc + buf[k] * gates[:, k:k + 1]
    o_ref[...] = _layer_norm(ALPHA * x + acc, g_ref[...], beta_ref[...])


def _combine(seg_start, idx, rank, x2d, gate_t, sg, su, sd, g, beta, yb):
    t, d = x2d.shape
    nt = t // COMBINE_TILE
    per = ROUTE_TILE // COMBINE_TILE
    ff = sg.shape[-1]
    smem_blk = pl.BlockSpec((1, TOP_K, ROUTE_TILE), lambda i, seg: (i // per, 0, 0), memory_space=pltpu.SMEM)
    row = lambda i, seg: (i, 0)
    const = lambda i, seg: (0, 0)
    return pl.pallas_call(
        _combine_kernel,
        grid_spec=pltpu.PrefetchScalarGridSpec(
            num_scalar_prefetch=1,
            grid=(nt,),
            in_specs=[
                smem_blk, smem_blk,
                pl.BlockSpec((COMBINE_TILE, d), row),
                pl.BlockSpec((COMBINE_TILE, TOP_K), row),
                pl.BlockSpec((d, ff), const),
                pl.BlockSpec((d, ff), const),
                pl.BlockSpec((ff, d), const),
                pl.BlockSpec((1, d), const),
                pl.BlockSpec((1, d), const),
                pl.BlockSpec(memory_space=pl.ANY),
            ],
            out_specs=pl.BlockSpec((COMBINE_TILE, d), row),
            scratch_shapes=[pltpu.VMEM((TOP_K, COMBINE_TILE, d), F32), pltpu.SemaphoreType.DMA(())],
        ),
        out_shape=jax.ShapeDtypeStruct((t, d), F32),
        compiler_params=_cparams(("arbitrary",)),
        name="combine",
    )(seg_start, idx, rank, x2d, gate_t, sg, su, sd, g, beta, yb)


def kernel(x, mem, w_in, b_in, conv_w, conv_b, conv_ln_g, conv_ln_b, attn_sinks, rel_bias, w_out, b_out, ln1_g, ln1_b, xq_w, xkv_w, xo_w, ln2_g, ln2_b, router_w, router_b, exp_gate, exp_up, exp_down, sh_gate, sh_up, sh_down, ln3_g, ln3_b):
    bsz, seq, d = x.shape
    t = bsz * seq
    bias_tab, sink_tab = _band_tables(rel_bias, attn_sinks[0])
    tri = (jnp.arange(ROUTE_TILE)[:, None] <= jnp.arange(ROUTE_TILE)[None, :]).astype(BF16)
    row = lambda p: p.reshape(1, -1)
    for l in range(DEPTH):
        conv_out, q, k, v = _mix(x, w_in[l].astype(BF16), row(b_in[l]), conv_w[l], row(conv_b[l]),
                                 row(conv_ln_g[l]), row(conv_ln_b[l]))
        att = _swa(q, k, v, bias_tab, sink_tab)
        x1 = _outproj(x.reshape(t, d), conv_out.reshape(t, CONV_CH), att.reshape(t, Q_COLS),
                      w_out[l].astype(BF16), row(b_out[l]), row(ln1_g[l]), row(ln1_b[l]))
        kmem, vmem = _memkv(mem, xkv_w[l].astype(BF16))
        x2, logits_t = _cross(x1.reshape(bsz, seq, d), kmem, vmem, xq_w[l].astype(BF16),
                              xo_w[l].astype(BF16), row(ln2_g[l]), row(ln2_b[l]),
                              router_w[l].T.astype(BF16))
        x2 = x2.reshape(t, d)
        idx, rank, gate, counts = _route(logits_t, router_b[l].reshape(-1, 1), tri)
        item_block, item_expert, item_flags, seg_bounds = _work_items(counts[:, 0], t * TOP_K)
        xs = _dispatch(seg_bounds, idx, rank, x2)
        yb = _experts(item_block, item_expert, item_flags, seg_bounds, xs,
                      exp_gate[l], exp_up[l], exp_down[l])
        gate_t = jnp.transpose(gate, (0, 2, 1)).reshape(t, TOP_K)
        x = _combine(seg_bounds, idx, rank, x2, gate_t, sh_gate[l].astype(BF16), sh_up[l].astype(BF16),
                     sh_down[l].astype(BF16), row(ln3_g[l]), row(ln3_b[l]), yb).reshape(bsz, seq, d)
    return x
```

```python
import functools
import math

import jax
import jax.numpy as jnp
from jax import lax
from jax.experimental import pallas as pl
from jax.experimental.pallas import tpu as pltpu

D_MODEL = 1024
MEM_LEN = 256
HEAD_DIM = 64
CONV_CH = D_MODEL // 2
CONV_WIDTH = 31
ATT_HEADS = 8
KV_HEADS = 2
WINDOW = 128
BLOCK = 128
REL_BUCKETS = 32
REL_MAX_DIST = 128
Q_COLS = ATT_HEADS * HEAD_DIM
KV_COLS = KV_HEADS * HEAD_DIM
IN_COLS = 2 * CONV_CH + Q_COLS + 2 * KV_COLS
X_HEADS = 4
X_HEAD_DIM = D_MODEL // X_HEADS
N_EXPERTS = 64
TOP_K = 8
N_GROUPS = 8
GROUP_SIZE = N_EXPERTS // N_GROUPS
TOPK_GROUPS = 4
EXPERT_FF = D_MODEL // 4
ROUTED_SCALE = 2.5
DEPTH = 1
ALPHA = (2 * DEPTH) ** 0.25
LN_EPS = 1e-5
NEG_INF = -1e30

F32 = jnp.float32
BF16 = jnp.bfloat16

VMEM_LIMIT_BYTES = 56 * 1024 * 1024

ROW_TILE = 512
CONV_ROWS = 32
CONV_HALO = 32
ROUTE_TILE = 512
EXPERT_ROWS = 256
COMBINE_TILE = 256


def _cparams(sem):
    return pltpu.CompilerParams(dimension_semantics=sem, vmem_limit_bytes=VMEM_LIMIT_BYTES)


def _layer_norm(h, g, b):
    mu = jnp.mean(h, axis=-1, keepdims=True)
    d = h - mu
    var = jnp.mean(d * d, axis=-1, keepdims=True)
    return d * lax.rsqrt(var + LN_EPS) * g + b


def _sigmoid(x):
    return 1.0 / (1.0 + jnp.exp(-x))


def _mix_kernel(x_ref, w_ref, b_ref, cw_ref, cb_ref, cg_ref, cbeta_ref,
                conv_ref, q_ref, k_ref, v_ref, u_ext):
    j = pl.program_id(1)
    xb = x_ref[0].astype(BF16)
    proj = jnp.dot(xb, w_ref[...], preferred_element_type=F32) + b_ref[...]
    a = proj[:, :CONV_CH]
    g = proj[:, CONV_CH:2 * CONV_CH]
    q_ref[0] = (proj[:, 2 * CONV_CH:2 * CONV_CH + Q_COLS] * (HEAD_DIM ** -0.5)).astype(BF16)
    k_ref[0] = proj[:, 2 * CONV_CH + Q_COLS:2 * CONV_CH + Q_COLS + KV_COLS].astype(BF16)
    v_ref[0] = proj[:, 2 * CONV_CH + Q_COLS + KV_COLS:].astype(BF16)

    @pl.when(j == 0)
    def _():
        u_ext[0:CONV_HALO, :] = jnp.zeros((CONV_HALO, CONV_CH), F32)

    u_ext[CONV_HALO:CONV_HALO + ROW_TILE, :] = a * _sigmoid(g)

    first_tap = CONV_HALO - (CONV_WIDTH - 1)
    for c in range(ROW_TILE // CONV_ROWS):
        base = c * CONV_ROWS + first_tap
        acc = jnp.zeros((CONV_ROWS, CONV_CH), F32) + cb_ref[...]
        for t in range(CONV_WIDTH):
            acc = acc + u_ext[base + t:base + t + CONV_ROWS, :] * cw_ref[t:t + 1, :]
        y = _layer_norm(acc, cg_ref[...], cbeta_ref[...])
        conv_ref[0, c * CONV_ROWS:(c + 1) * CONV_ROWS, :] = (y * _sigmoid(y)).astype(BF16)

    u_ext[0:CONV_HALO, :] = u_ext[ROW_TILE:ROW_TILE + CONV_HALO, :]


def _mix(x, w_in, b_in, conv_w, conv_b, conv_g, conv_beta):
    bsz, seq, d = x.shape
    nt = seq // ROW_TILE
    row = lambda b, j: (b, j, 0)
    const2 = lambda b, j: (0, 0)
    return pl.pallas_call(
        _mix_kernel,
        grid=(bsz, nt),
        in_specs=[
            pl.BlockSpec((1, ROW_TILE, d), row),
            pl.BlockSpec((d, IN_COLS), const2),
            pl.BlockSpec((1, IN_COLS), const2),
            pl.BlockSpec((CONV_WIDTH, CONV_CH), const2),
            pl.BlockSpec((1, CONV_CH), const2),
            pl.BlockSpec((1, CONV_CH), const2),
            pl.BlockSpec((1, CONV_CH), const2),
        ],
        out_specs=[
            pl.BlockSpec((1, ROW_TILE, CONV_CH), row),
            pl.BlockSpec((1, ROW_TILE, Q_COLS), row),
            pl.BlockSpec((1, ROW_TILE, KV_COLS), row),
            pl.BlockSpec((1, ROW_TILE, KV_COLS), row),
        ],
        out_shape=[
            jax.ShapeDtypeStruct((bsz, seq, CONV_CH), BF16),
            jax.ShapeDtypeStruct((bsz, seq, Q_COLS), BF16),
            jax.ShapeDtypeStruct((bsz, seq, KV_COLS), BF16),
            jax.ShapeDtypeStruct((bsz, seq, KV_COLS), BF16),
        ],
        scratch_shapes=[pltpu.VMEM((ROW_TILE + CONV_HALO, CONV_CH), F32)],
        compiler_params=_cparams(("arbitrary", "arbitrary")),
        name="mix",
    )(x, w_in, b_in, conv_w, conv_b, conv_g, conv_beta)


def _swa_kernel(q_ref, kp_ref, kc_ref, vp_ref, vc_ref, bias_ref, sink_ref, o_ref):
    band = 2 * BLOCK
    lane = lax.broadcasted_iota(jnp.int32, (band, 2 * HEAD_DIM), 1)
    low = lane < HEAD_DIM

    def placements(prev_ref, cur_ref):
        t = jnp.concatenate([prev_ref[0], cur_ref[0]], axis=0).astype(F32)
        tr = pltpu.roll(t, HEAD_DIM, 1)
        zero = jnp.zeros_like(t)
        kv0_low = jnp.where(low, t, zero).astype(BF16)
        kv1_high = jnp.where(low, zero, t).astype(BF16)
        kv1_low = jnp.where(low, tr, zero).astype(BF16)
        kv0_high = jnp.where(low, zero, tr).astype(BF16)
        return (kv0_low, kv0_high, kv1_low, kv1_high)

    ks = placements(kp_ref, kc_ref)
    vs = placements(vp_ref, vc_ref)
    q = q_ref[0]
    slab = 2 * HEAD_DIM
    q_kv0 = jnp.concatenate([q[:, 0:slab], q[:, slab:2 * slab]], axis=0)
    q_kv1 = jnp.concatenate([q[:, 2 * slab:3 * slab], q[:, 3 * slab:4 * slab]], axis=0)
    outs = []
    for s in range(4):
        qs = q_kv0 if s < 2 else q_kv1
        logits = lax.dot_general(qs, ks[s], (((1,), (1,)), ((), ())),
                                 preferred_element_type=F32) + bias_ref[0, s]
        sink = sink_ref[s]
        m = jnp.maximum(jnp.max(logits, axis=-1, keepdims=True), sink)
        p = jnp.exp(logits - m)
        den = jnp.sum(p, axis=-1, keepdims=True) + jnp.exp(sink - m)
        o = jnp.dot(p.astype(BF16), vs[s], preferred_element_type=F32)
        outs.append(o / den)
    o_kv0 = outs[0] + outs[1]
    o_kv1 = outs[2] + outs[3]
    o_ref[0, :, 0:slab] = o_kv0[0:BLOCK].astype(BF16)
    o_ref[0, :, slab:2 * slab] = o_kv0[BLOCK:2 * BLOCK].astype(BF16)
    o_ref[0, :, 2 * slab:3 * slab] = o_kv1[0:BLOCK].astype(BF16)
    o_ref[0, :, 3 * slab:4 * slab] = o_kv1[BLOCK:2 * BLOCK].astype(BF16)


def _swa(q, k, v, bias_tab, sink_tab):
    bsz, seq, _ = q.shape
    nb = seq // BLOCK
    cur = lambda b, n: (b, n, 0)
    prev = lambda b, n: (b, jnp.maximum(n - 1, 0), 0)
    return pl.pallas_call(
        _swa_kernel,
        grid=(bsz, nb),
        in_specs=[
            pl.BlockSpec((1, BLOCK, Q_COLS), cur),
            pl.BlockSpec((1, BLOCK, KV_COLS), prev),
            pl.BlockSpec((1, BLOCK, KV_COLS), cur),
            pl.BlockSpec((1, BLOCK, KV_COLS), prev),
            pl.BlockSpec((1, BLOCK, KV_COLS), cur),
            pl.BlockSpec((1, 4, 2 * BLOCK, 2 * BLOCK), lambda b, n: (jnp.minimum(n, 1), 0, 0, 0)),
            pl.BlockSpec((4, 2 * BLOCK, 1), lambda b, n: (0, 0, 0)),
        ],
        out_specs=pl.BlockSpec((1, BLOCK, Q_COLS), cur),
        out_shape=jax.ShapeDtypeStruct((bsz, seq, Q_COLS), BF16),
        compiler_params=_cparams(("arbitrary", "arbitrary")),
        name="swa",
    )(q, k, k, v, v, bias_tab, sink_tab)


def _t5_bucket(dist):
    n = jnp.maximum(dist, 0)
    exact = REL_BUCKETS // 2
    large = exact + (jnp.log(jnp.maximum(n, 1).astype(F32) / exact)
                     / math.log(REL_MAX_DIST / exact) * (REL_BUCKETS - exact)).astype(jnp.int32)
    large = jnp.minimum(large, REL_BUCKETS - 1)
    return jnp.where(n < exact, n, large)


def _band_tables(rel_bias, sinks):
    qi = jnp.arange(BLOCK)[:, None]
    kj = jnp.arange(2 * BLOCK)[None, :]
    dist = qi + BLOCK - kj
    bucket = _t5_bucket(dist)
    bias = jnp.zeros((ATT_HEADS, BLOCK, 2 * BLOCK), F32)
    for bkt in range(REL_BUCKETS):
        bias = jnp.where(bucket[None] == bkt, rel_bias[bkt].astype(F32)[:, None, None], bias)
    in_window = (dist >= 0) & (dist < WINDOW)
    masks = jnp.stack([in_window & (kj >= BLOCK), in_window])
    masked = jnp.where(masks[:, None], bias[None], NEG_INF)
    pairs = ((0, 2), (1, 3), (4, 6), (5, 7))
    bias_tab = jnp.stack([jnp.concatenate([masked[:, a], masked[:, b]], axis=1) for a, b in pairs], axis=1)
    s = sinks.astype(F32)
    sink_tab = jnp.stack([jnp.concatenate([jnp.full((BLOCK, 1), s[a]), jnp.full((BLOCK, 1), s[b])], axis=0)
                          for a, b in pairs])
    return bias_tab, sink_tab


def _outproj_kernel(x_ref, conv_ref, att_ref, w_ref, b_ref, g_ref, beta_ref, o_ref):
    mix = jnp.dot(conv_ref[...], w_ref[0:CONV_CH, :], preferred_element_type=F32)
    mix = mix + jnp.dot(att_ref[...], w_ref[CONV_CH:, :], preferred_element_type=F32)
    h = ALPHA * x_ref[...] + mix + b_ref[...]
    o_ref[...] = _layer_norm(h, g_ref[...], beta_ref[...])


def _outproj(x2d, conv2d, att2d, w_out, b_out, g, beta):
    t, d = x2d.shape
    row = lambda i: (i, 0)
    const = lambda i: (0, 0)
    return pl.pallas_call(
        _outproj_kernel,
        grid=(t // ROW_TILE,),
        in_specs=[
            pl.BlockSpec((ROW_TILE, d), row),
            pl.BlockSpec((ROW_TILE, CONV_CH), row),
            pl.BlockSpec((ROW_TILE, Q_COLS), row),
            pl.BlockSpec((d, d), const),
            pl.BlockSpec((1, d), const),
            pl.BlockSpec((1, d), const),
            pl.BlockSpec((1, d), const),
        ],
        out_specs=pl.BlockSpec((ROW_TILE, d), row),
        out_shape=jax.ShapeDtypeStruct((t, d), F32),
        compiler_params=_cparams(("arbitrary",)),
        name="outproj",
    )(x2d, conv2d, att2d, w_out, b_out, g, beta)


def _memkv_kernel(mem_ref, w_ref, k_ref, v_ref):
    kv = jnp.dot(mem_ref[0].astype(BF16), w_ref[...], preferred_element_type=F32)
    k_ref[0] = kv[:, :D_MODEL].astype(BF16)
    v_ref[0] = kv[:, D_MODEL:].astype(BF16)


def _memkv(mem, wkv):
    bsz, m, d = mem.shape
    return pl.pallas_call(
        _memkv_kernel,
        grid=(bsz,),
        in_specs=[pl.BlockSpec((1, m, d), lambda b: (b, 0, 0)),
                  pl.BlockSpec((d, 2 * d), lambda b: (0, 0))],
        out_specs=[pl.BlockSpec((1, m, d), lambda b: (b, 0, 0)),
                   pl.BlockSpec((1, m, d), lambda b: (b, 0, 0))],
        out_shape=[jax.ShapeDtypeStruct((bsz, m, d), BF16)] * 2,
        compiler_params=_cparams(("arbitrary",)),
        name="memkv",
    )(mem, wkv)


def _cross_kernel(x_ref, k_ref, v_ref, wq_ref, wo_ref, g_ref, beta_ref, rw_ref, o_ref, lt_ref):
    x = x_ref[0]
    q = jnp.dot(x.astype(BF16), wq_ref[...], preferred_element_type=F32) * (X_HEAD_DIM ** -0.5)
    q = q.astype(BF16)
    heads = []
    for h in range(X_HEADS):
        cols = slice(h * X_HEAD_DIM, (h + 1) * X_HEAD_DIM)
        logits = lax.dot_general(q[:, cols], k_ref[0, :, cols], (((1,), (1,)), ((), ())),
                                 preferred_element_type=F32)
        m = jnp.max(logits, axis=-1, keepdims=True)
        p = jnp.exp(logits - m)
        den = jnp.sum(p, axis=-1, keepdims=True)
        o = jnp.dot(p.astype(BF16), v_ref[0, :, cols], preferred_element_type=F32)
        heads.append((o / den).astype(BF16))
    att = jnp.concatenate(heads, axis=-1)
    cross = jnp.dot(att, wo_ref[...], preferred_element_type=F32)
    y = _layer_norm(ALPHA * x + cross, g_ref[...], beta_ref[...])
    o_ref[0] = y
    lt_ref[...] = lax.dot_general(rw_ref[...], y.astype(BF16), (((1,), (1,)), ((), ())),
                                  preferred_element_type=F32)


def _cross(x1, kmem, vmem, wq, wo, g, beta, rw_t):
    bsz, seq, d = x1.shape
    nt = seq // ROW_TILE
    row = lambda b, j: (b, j, 0)
    mem = lambda b, j: (b, 0, 0)
    const = lambda b, j: (0, 0)
    return pl.pallas_call(
        _cross_kernel,
        grid=(bsz, nt),
        in_specs=[
            pl.BlockSpec((1, ROW_TILE, d), row),
            pl.BlockSpec((1, MEM_LEN, d), mem),
            pl.BlockSpec((1, MEM_LEN, d), mem),
            pl.BlockSpec((d, d), const),
            pl.BlockSpec((d, d), const),
            pl.BlockSpec((1, d), const),
            pl.BlockSpec((1, d), const),
            pl.BlockSpec((N_EXPERTS, d), const),
        ],
        out_specs=[
            pl.BlockSpec((1, ROW_TILE, d), row),
            pl.BlockSpec((N_EXPERTS, ROW_TILE), lambda b, j: (0, b * nt + j)),
        ],
        out_shape=[
            jax.ShapeDtypeStruct((bsz, seq, d), F32),
            jax.ShapeDtypeStruct((N_EXPERTS, bsz * seq), F32),
        ],
        compiler_params=_cparams(("arbitrary", "arbitrary")),
        name="cross",
    )(x1, kmem, vmem, wq, wo, g, beta, rw_t)


def _route_kernel(lt_ref, rb_ref, tri_ref, idx_ref, rank_ref, gate_ref, cnt_ref, carry):
    i = pl.program_id(0)

    @pl.when(i == 0)
    def _():
        carry[...] = jnp.zeros_like(carry)

    tn = ROUTE_TILE
    scores = _sigmoid(lt_ref[...])
    choice = scores + rb_ref[...]

    gscore = []
    member = lax.broadcasted_iota(jnp.int32, (GROUP_SIZE, tn), 0).astype(F32)
    for g in range(N_GROUPS):
        c = choice[g * GROUP_SIZE:(g + 1) * GROUP_SIZE, :]
        m1 = jnp.max(c, axis=0, keepdims=True)
        first = jnp.min(jnp.where(c == m1, member, float(GROUP_SIZE)), axis=0, keepdims=True)
        m2 = jnp.max(jnp.where(member == first, -jnp.inf, c), axis=0, keepdims=True)
        gscore.append(m1 + m2)

    keep_rows = []
    for g in range(N_GROUPS):
        beaten = jnp.zeros((1, tn), F32)
        for o in range(N_GROUPS):
            if o == g:
                continue
            ahead = (gscore[o] >= gscore[g]) if o < g else (gscore[o] > gscore[g])
            beaten = beaten + jnp.where(ahead, 1.0, 0.0)
        keep_rows.append(jnp.broadcast_to(beaten, (GROUP_SIZE, tn)))
    beaten_all = jnp.concatenate(keep_rows, axis=0)

    masked = jnp.where(beaten_all < TOPK_GROUPS, choice, -jnp.inf)
    eidx = lax.broadcasted_iota(jnp.int32, (N_EXPERTS, tn), 0).astype(F32)
    sel = jnp.zeros((N_EXPERTS, tn), F32)
    picks, weights = [], []
    for r in range(TOP_K):
        mx = jnp.max(masked, axis=0, keepdims=True)
        first = jnp.min(jnp.where(masked == mx, eidx, float(N_EXPERTS)), axis=0, keepdims=True)
        pick = eidx == first
        picks.append((pick, first))
        weights.append(jnp.sum(jnp.where(pick, scores, 0.0), axis=0, keepdims=True))
        masked = jnp.where(pick, -jnp.inf, masked)
        sel = jnp.where(pick, 1.0, sel)

    wsum = weights[0]
    for r in range(1, TOP_K):
        wsum = wsum + weights[r]

    incl = jnp.dot(sel.astype(BF16), tri_ref[...], preferred_element_type=F32)
    rank_mat = carry[...] + incl - sel
    carry[...] = carry[...] + jnp.sum(sel, axis=1, keepdims=True)
    cnt_ref[...] = jnp.broadcast_to(carry[...], cnt_ref.shape).astype(jnp.int32)

    for r in range(TOP_K):
        pick, first = picks[r]
        idx_ref[0, r:r + 1, :] = first.astype(jnp.int32)
        rank_ref[0, r:r + 1, :] = jnp.sum(jnp.where(pick, rank_mat, 0.0), axis=0,
                                          keepdims=True).astype(jnp.int32)
        gate_ref[0, r:r + 1, :] = weights[r] / wsum * ROUTED_SCALE


def _route(logits_t, router_b, tri):
    e, t = logits_t.shape
    nt = t // ROUTE_TILE
    blk = lambda i: (i, 0, 0)
    return pl.pallas_call(
        _route_kernel,
        grid=(nt,),
        in_specs=[
            pl.BlockSpec((e, ROUTE_TILE), lambda i: (0, i)),
            pl.BlockSpec((e, 1), lambda i: (0, 0)),
            pl.BlockSpec((ROUTE_TILE, ROUTE_TILE), lambda i: (0, 0)),
        ],
        out_specs=[
            pl.BlockSpec((1, TOP_K, ROUTE_TILE), blk),
            pl.BlockSpec((1, TOP_K, ROUTE_TILE), blk),
            pl.BlockSpec((1, TOP_K, ROUTE_TILE), blk),
            pl.BlockSpec((e, 128), lambda i: (0, 0)),
        ],
        out_shape=[
            jax.ShapeDtypeStruct((nt, TOP_K, ROUTE_TILE), jnp.int32),
            jax.ShapeDtypeStruct((nt, TOP_K, ROUTE_TILE), jnp.int32),
            jax.ShapeDtypeStruct((nt, TOP_K, ROUTE_TILE), F32),
            jax.ShapeDtypeStruct((e, 128), jnp.int32),
        ],
        scratch_shapes=[pltpu.VMEM((e, 1), F32)],
        compiler_params=_cparams(("arbitrary",)),
        name="route",
    )(logits_t, router_b, tri)


def _dispatch_kernel(seg_ref, idx_ref, rank_ref, x_ref, xs_ref, sem):
    tn = ROUTE_TILE

    def body(t, carry):
        for k in range(TOP_K):
            dst = seg_ref[idx_ref[0, k, t]] + rank_ref[0, k, t]
            pltpu.make_async_copy(x_ref.at[pl.ds(t, 1), :], xs_ref.at[pl.ds(dst, 1), :], sem).start()
        return carry

    lax.fori_loop(0, tn, body, 0)
    for k in range(TOP_K):
        pltpu.make_async_copy(x_ref, xs_ref.at[pl.ds(0, tn), :], sem).wait()


def _dispatch(seg_start, idx, rank, x2d):
    t, d = x2d.shape
    nt = t // ROUTE_TILE
    smem_blk = pl.BlockSpec((1, TOP_K, ROUTE_TILE), lambda i, seg: (i, 0, 0), memory_space=pltpu.SMEM)
    return pl.pallas_call(
        _dispatch_kernel,
        grid_spec=pltpu.PrefetchScalarGridSpec(
            num_scalar_prefetch=1,
            grid=(nt,),
            in_specs=[smem_blk, smem_blk, pl.BlockSpec((ROUTE_TILE, d), lambda i, seg: (i, 0))],
            out_specs=pl.BlockSpec(memory_space=pl.ANY),
            scratch_shapes=[pltpu.SemaphoreType.DMA(())],
        ),
        out_shape=jax.ShapeDtypeStruct((t * TOP_K, d), F32),
        compiler_params=_cparams(("arbitrary",)),
        name="dispatch",
    )(seg_start, idx, rank, x2d)


def _experts_kernel(ib_ref, ie_ref, flag_ref, seg_ref, xs_ref, wg_ref, wu_ref, wd_ref, y_ref,
                    wg_bf, wu_bf, wd_bf):
    w = pl.program_id(0)
    flags = flag_ref[w]
    valid = (flags & 1) != 0
    first_of_block = (flags & 2) != 0
    new_expert = (flags & 4) != 0

    @pl.when(new_expert)
    def _():
        wg_bf[...] = wg_ref[0].astype(BF16)
        wu_bf[...] = wu_ref[0].astype(BF16)
        wd_bf[...] = wd_ref[0].astype(BF16)

    @pl.when(valid)
    def _():
        e = ie_ref[w]
        x = xs_ref[...].astype(BF16)
        gte = jnp.dot(x, wg_bf[...], preferred_element_type=F32)
        up = jnp.dot(x, wu_bf[...], preferred_element_type=F32)
        h = (gte * _sigmoid(gte) * up).astype(BF16)
        y = jnp.dot(h, wd_bf[...], preferred_element_type=F32)
        rows = ib_ref[w] * EXPERT_ROWS + lax.broadcasted_iota(jnp.int32, (EXPERT_ROWS, 1), 0)
        mine = (rows >= seg_ref[e]) & (rows < seg_ref[e + 1])
        contrib = jnp.where(mine, y, 0.0)

        @pl.when(first_of_block)
        def _():
            y_ref[...] = contrib

        @pl.when(jnp.logical_not(first_of_block))
        def _():
            y_ref[...] = y_ref[...] + contrib


def _experts(item_block, item_expert, item_flags, seg_bounds, xs, wg, wu, wd):
    n, d = xs.shape
    n_items = item_block.shape[0]
    ff = wg.shape[-1]
    return pl.pallas_call(
        _experts_kernel,
        grid_spec=pltpu.PrefetchScalarGridSpec(
            num_scalar_prefetch=4,
            grid=(n_items,),
            in_specs=[
                pl.BlockSpec((EXPERT_ROWS, d), lambda w, ib, ie, fl, sg: (ib[w], 0)),
                pl.BlockSpec((1, d, ff), lambda w, ib, ie, fl, sg: (ie[w], 0, 0)),
                pl.BlockSpec((1, d, ff), lambda w, ib, ie, fl, sg: (ie[w], 0, 0)),
                pl.BlockSpec((1, ff, d), lambda w, ib, ie, fl, sg: (ie[w], 0, 0)),
            ],
            out_specs=pl.BlockSpec((EXPERT_ROWS, d), lambda w, ib, ie, fl, sg: (ib[w], 0)),
            scratch_shapes=[pltpu.VMEM((d, ff), BF16), pltpu.VMEM((d, ff), BF16), pltpu.VMEM((ff, d), BF16)],
        ),
        out_shape=jax.ShapeDtypeStruct((n, d), F32),
        compiler_params=_cparams(("arbitrary",)),
        name="experts",
    )(item_block, item_expert, item_flags, seg_bounds, xs, wg, wu, wd)


def _work_items(counts, n_rows):
    n_items = n_rows // EXPERT_ROWS + N_EXPERTS
    seg_end = jnp.cumsum(counts)
    seg_start = seg_end - counts
    first_blk = seg_start // EXPERT_ROWS
    n_blk = jnp.where(counts > 0, (seg_end - 1) // EXPERT_ROWS - first_blk + 1, 0)
    item_end = jnp.cumsum(n_blk)
    item_start = item_end - n_blk
    total = item_end[-1]
    w = jnp.arange(n_items, dtype=jnp.int32)
    wc = jnp.minimum(w, total - 1)
    e = jnp.minimum(jnp.sum((item_end[None, :] <= wc[:, None]).astype(jnp.int32), axis=1), N_EXPERTS - 1)
    onehot = (e[:, None] == jnp.arange(N_EXPERTS, dtype=jnp.int32)[None, :]).astype(jnp.int32)
    b = jnp.sum(onehot * (first_blk - item_start)[None, :], axis=1) + wc
    valid = w < total
    prev_b = jnp.concatenate([jnp.full((1,), -1, jnp.int32), b[:-1]])
    prev_e = jnp.concatenate([jnp.full((1,), -1, jnp.int32), e[:-1]])
    flags = (valid.astype(jnp.int32) + 2 * (valid & (b != prev_b)).astype(jnp.int32)
             + 4 * (valid & (e != prev_e)).astype(jnp.int32))
    seg_bounds = jnp.concatenate([seg_start, seg_end[-1:]]).astype(jnp.int32)
    return b, e, flags, seg_bounds


def _combine_kernel(seg_ref, idx_ref, rank_ref, x_ref, gate_ref, sg_ref, su_ref, sd_ref,
                    g_ref, beta_ref, yb_ref, o_ref, buf, sem):
    tn = COMBINE_TILE
    off = (pl.program_id(0) % (ROUTE_TILE // COMBINE_TILE)) * COMBINE_TILE

    def body(t, carry):
        for k in range(TOP_K):
            src = seg_ref[idx_ref[0, k, off + t]] + rank_ref[0, k, off + t]
            pltpu.make_async_copy(yb_ref.at[pl.ds(src, 1), :], buf.at[k, pl.ds(t, 1), :], sem).start()
        return carry

    lax.fori_loop(0, tn, body, 0)

    x = x_ref[...]
    xb = x.astype(BF16)
    gte = jnp.dot(xb, sg_ref[...], preferred_element_type=F32)
    up = jnp.dot(xb, su_ref[...], preferred_element_type=F32)
    h = (gte * _sigmoid(gte) * up).astype(BF16)
    acc = jnp.dot(h, sd_ref[...], preferred_element_type=F32)

    for k in range(TOP_K):
        pltpu.make_async_copy(yb_ref.at[pl.ds(0, tn), :], buf.at[k], sem).wait()
    gates = gate_ref[...]
    for k in range(TOP_K):
        acc = acc + buf[k] * gates[:, k:k + 1]
    o_ref[...] = _layer_norm(ALPHA * x + acc, g_ref[...], beta_ref[...])


def _combine(seg_start, idx, rank, x2d, gate_t, sg, su, sd, g, beta, yb):
    t, d = x2d.shape
    nt = t // COMBINE_TILE
    per = ROUTE_TILE // COMBINE_TILE
    ff = sg.shape[-1]
    smem_blk = pl.BlockSpec((1, TOP_K, ROUTE_TILE), lambda i, seg: (i // per, 0, 0), memory_space=pltpu.SMEM)
    row = lambda i, seg: (i, 0)
    const = lambda i, seg: (0, 0)
    return pl.pallas_call(
        _combine_kernel,
        grid_spec=pltpu.PrefetchScalarGridSpec(
            num_scalar_prefetch=1,
            grid=(nt,),
            in_specs=[
                smem_blk, smem_blk,
                pl.BlockSpec((COMBINE_TILE, d), row),
                pl.BlockSpec((COMBINE_TILE, TOP_K), row),
                pl.BlockSpec((d, ff), const),
                pl.BlockSpec((d, ff), const),
                pl.BlockSpec((ff, d), const),
                pl.BlockSpec((1, d), const),
                pl.BlockSpec((1, d), const),
                pl.BlockSpec(memory_space=pl.ANY),
            ],
            out_specs=pl.BlockSpec((COMBINE_TILE, d), row),
            scratch_shapes=[pltpu.VMEM((TOP_K, COMBINE_TILE, d), F32), pltpu.SemaphoreType.DMA(())],
        ),
        out_shape=jax.ShapeDtypeStruct((t, d), F32),
        compiler_params=_cparams(("arbitrary",)),
        name="combine",
    )(seg_start, idx, rank, x2d, gate_t, sg, su, sd, g, beta, yb)


def kernel(x, mem, w_in, b_in, conv_w, conv_b, conv_ln_g, conv_ln_b, attn_sinks, rel_bias, w_out, b_out, ln1_g, ln1_b, xq_w, xkv_w, xo_w, ln2_g, ln2_b, router_w, router_b, exp_gate, exp_up, exp_down, sh_gate, sh_up, sh_down, ln3_g, ln3_b):
    bsz, seq, d = x.shape
    t = bsz * seq
    bias_tab, sink_tab = _band_tables(rel_bias, attn_sinks[0])
    tri = (jnp.arange(ROUTE_TILE)[:, None] <= jnp.arange(ROUTE_TILE)[None, :]).astype(BF16)
    row = lambda p: p.reshape(1, -1)
    for l in range(DEPTH):
        conv_out, q, k, v = _mix(x, w_in[l].astype(BF16), row(b_in[l]), conv_w[l], row(conv_b[l]),
                                 row(conv_ln_g[l]), row(conv_ln_b[l]))
        att = _swa(q, k, v, bias_tab, sink_tab)
        x1 = _outproj(x.reshape(t, d), conv_out.reshape(t, CONV_CH), att.reshape(t, Q_COLS),
                      w_out[l].astype(BF16), row(b_out[l]), row(ln1_g[l]), row(ln1_b[l]))
        kmem, vmem = _memkv(mem, xkv_w[l].astype(BF16))
        x2, logits_t = _cross(x1.reshape(bsz, seq, d), kmem, vmem, xq_w[l].astype(BF16),
                              xo_w[l].astype(BF16), row(ln2_g[l]), row(ln2_b[l]),
                              router_w[l].T.astype(BF16))
        x2 = x2.reshape(t, d)
        idx, rank, gate, counts = _route(logits_t, router_b[l].reshape(-1, 1), tri)
        item_block, item_expert, item_flags, seg_bounds = _work_items(counts[:, 0], t * TOP_K)
        xs = _dispatch(seg_bounds, idx, rank, x2)
        yb = _experts(item_block, item_expert, item_flags, seg_bounds, xs,
                      exp_gate[l], exp_up[l], exp_down[l])
        gate_t = jnp.transpose(gate, (0, 2, 1)).reshape(t, TOP_K)
        x = _combine(seg_bounds, idx, rank, x2, gate_t, sh_gate[l].astype(BF16), sh_up[l].astype(BF16),
                     sh_down[l].astype(BF16), row(ln3_g[l]), row(ln3_b[l]), yb).reshape(bsz, seq, d)
    return x
```

```python
import functools
import math

import jax
import jax.numpy as jnp
from jax import lax
from jax.experimental import pallas as pl
from jax.experimental.pallas import tpu as pltpu

D_MODEL = 1024
MEM_LEN = 256
HEAD_DIM = 64
CONV_CH = D_MODEL // 2
CONV_WIDTH = 31
ATT_HEADS = 8
KV_HEADS = 2
WINDOW = 128
BLOCK = 128
REL_BUCKETS = 32
REL_MAX_DIST = 128
Q_COLS = ATT_HEADS * HEAD_DIM
KV_COLS = KV_HEADS * HEAD_DIM
IN_COLS = 2 * CONV_CH + Q_COLS + 2 * KV_COLS
X_HEADS = 4
X_HEAD_DIM = D_MODEL // X_HEADS
N_EXPERTS = 64
TOP_K = 8
N_GROUPS = 8
GROUP_SIZE = N_EXPERTS // N_GROUPS
TOPK_GROUPS = 4
EXPERT_FF = D_MODEL // 4
ROUTED_SCALE = 2.5
DEPTH = 1
ALPHA = (2 * DEPTH) ** 0.25
LN_EPS = 1e-5
NEG_INF = -1e30

F32 = jnp.float32
BF16 = jnp.bfloat16

VMEM_LIMIT_BYTES = 56 * 1024 * 1024

ROW_TILE = 512
CONV_ROWS = 32
CONV_HALO = 32
ROUTE_TILE = 512
EXPERT_ROWS = 256
RUN_ALIGN = 8
RUN_PIECES = tuple(1 << b for b in range(9, 2, -1))
TILE_CAP = ROUTE_TILE * TOP_K + N_EXPERTS * RUN_ALIGN
SORT_CHUNK = 256
LOW_HALF = 0xFFFF
HIGH_HALF = -0x10000


def _cparams(sem):
    return pltpu.CompilerParams(dimension_semantics=sem, vmem_limit_bytes=VMEM_LIMIT_BYTES)


def _layer_norm(h, g, b):
    mu = jnp.mean(h, axis=-1, keepdims=True)
    d = h - mu
    var = jnp.mean(d * d, axis=-1, keepdims=True)
    return d * lax.rsqrt(var + LN_EPS) * g + b


def _sigmoid(x):
    return 1.0 / (1.0 + jnp.exp(-x))


def _mix_kernel(x_ref, w_ref, b_ref, cw_ref, cb_ref, cg_ref, cbeta_ref,
                conv_ref, q_ref, k_ref, v_ref, u_ext):
    j = pl.program_id(1)
    xb = x_ref[0].astype(BF16)
    proj = jnp.dot(xb, w_ref[...], preferred_element_type=F32) + b_ref[...]
    a = proj[:, :CONV_CH]
    g = proj[:, CONV_CH:2 * CONV_CH]
    q_ref[0] = (proj[:, 2 * CONV_CH:2 * CONV_CH + Q_COLS] * (HEAD_DIM ** -0.5)).astype(BF16)
    k_ref[0] = proj[:, 2 * CONV_CH + Q_COLS:2 * CONV_CH + Q_COLS + KV_COLS].astype(BF16)
    v_ref[0] = proj[:, 2 * CONV_CH + Q_COLS + KV_COLS:].astype(BF16)

    @pl.when(j == 0)
    def _():
        u_ext[0:CONV_HALO, :] = jnp.zeros((CONV_HALO, CONV_CH), F32)

    u_ext[CONV_HALO:CONV_HALO + ROW_TILE, :] = a * _sigmoid(g)

    first_tap = CONV_HALO - (CONV_WIDTH - 1)
    for c in range(ROW_TILE // CONV_ROWS):
        base = c * CONV_ROWS + first_tap
        acc = jnp.zeros((CONV_ROWS, CONV_CH), F32) + cb_ref[...]
        for t in range(CONV_WIDTH):
            acc = acc + u_ext[base + t:base + t + CONV_ROWS, :] * cw_ref[t:t + 1, :]
        y = _layer_norm(acc, cg_ref[...], cbeta_ref[...])
        conv_ref[0, c * CONV_ROWS:(c + 1) * CONV_ROWS, :] = (y * _sigmoid(y)).astype(BF16)

    u_ext[0:CONV_HALO, :] = u_ext[ROW_TILE:ROW_TILE + CONV_HALO, :]


def _mix(x, w_in, b_in, conv_w, conv_b, conv_g, conv_beta):
    bsz, seq, d = x.shape
    nt = seq // ROW_TILE
    row = lambda b, j: (b, j, 0)
    const2 = lambda b, j: (0, 0)
    return pl.pallas_call(
        _mix_kernel,
        grid=(bsz, nt),
        in_specs=[
            pl.BlockSpec((1, ROW_TILE, d), row),
            pl.BlockSpec((d, IN_COLS), const2),
            pl.BlockSpec((1, IN_COLS), const2),
            pl.BlockSpec((CONV_WIDTH, CONV_CH), const2),
            pl.BlockSpec((1, CONV_CH), const2),
            pl.BlockSpec((1, CONV_CH), const2),
            pl.BlockSpec((1, CONV_CH), const2),
        ],
        out_specs=[
            pl.BlockSpec((1, ROW_TILE, CONV_CH), row),
            pl.BlockSpec((1, ROW_TILE, Q_COLS), row),
            pl.BlockSpec((1, ROW_TILE, KV_COLS), row),
            pl.BlockSpec((1, ROW_TILE, KV_COLS), row),
        ],
        out_shape=[
            jax.ShapeDtypeStruct((bsz, seq, CONV_CH), BF16),
            jax.ShapeDtypeStruct((bsz, seq, Q_COLS), BF16),
            jax.ShapeDtypeStruct((bsz, seq, KV_COLS), BF16),
            jax.ShapeDtypeStruct((bsz, seq, KV_COLS), BF16),
        ],
        scratch_shapes=[pltpu.VMEM((ROW_TILE + CONV_HALO, CONV_CH), F32)],
        compiler_params=_cparams(("arbitrary", "arbitrary")),
        name="mix",
    )(x, w_in, b_in, conv_w, conv_b, conv_g, conv_beta)


def _swa_kernel(q_ref, kp_ref, kc_ref, vp_ref, vc_ref, bias_ref, sink_ref, o_ref):
    band = 2 * BLOCK
    lane = lax.broadcasted_iota(jnp.int32, (band, 2 * HEAD_DIM), 1)
    low = lane < HEAD_DIM

    def placements(prev_ref, cur_ref):
        t = jnp.concatenate([prev_ref[0], cur_ref[0]], axis=0).astype(F32)
        tr = pltpu.roll(t, HEAD_DIM, 1)
        zero = jnp.zeros_like(t)
        kv0_low = jnp.where(low, t, zero).astype(BF16)
        kv1_high = jnp.where(low, zero, t).astype(BF16)
        kv1_low = jnp.where(low, tr, zero).astype(BF16)
        kv0_high = jnp.where(low, zero, tr).astype(BF16)
        return (kv0_low, kv0_high, kv1_low, kv1_high)

    ks = placements(kp_ref, kc_ref)
    vs = placements(vp_ref, vc_ref)
    q = q_ref[0]
    slab = 2 * HEAD_DIM
    q_kv0 = jnp.concatenate([q[:, 0:slab], q[:, slab:2 * slab]], axis=0)
    q_kv1 = jnp.concatenate([q[:, 2 * slab:3 * slab], q[:, 3 * slab:4 * slab]], axis=0)
    outs = []
    for s in range(4):
        qs = q_kv0 if s < 2 else q_kv1
        logits = lax.dot_general(qs, ks[s], (((1,), (1,)), ((), ())),
                                 preferred_element_type=F32) + bias_ref[0, s]
        sink = sink_ref[s]
        m = jnp.maximum(jnp.max(logits, axis=-1, keepdims=True), sink)
        p = jnp.exp(logits - m)
        den = jnp.sum(p, axis=-1, keepdims=True) + jnp.exp(sink - m)
        o = jnp.dot(p.astype(BF16), vs[s], preferred_element_type=F32)
        outs.append(o / den)
    o_kv0 = outs[0] + outs[1]
    o_kv1 = outs[2] + outs[3]
    o_ref[0, :, 0:slab] = o_kv0[0:BLOCK].astype(BF16)
    o_ref[0, :, slab:2 * slab] = o_kv0[BLOCK:2 * BLOCK].astype(BF16)
    o_ref[0, :, 2 * slab:3 * slab] = o_kv1[0:BLOCK].astype(BF16)
    o_ref[0, :, 3 * slab:4 * slab] = o_kv1[BLOCK:2 * BLOCK].astype(BF16)


def _swa(q, k, v, bias_tab, sink_tab):
    bsz, seq, _ = q.shape
    nb = seq // BLOCK
    cur = lambda b, n: (b, n, 0)
    prev = lambda b, n: (b, jnp.maximum(n - 1, 0), 0)
    return pl.pallas_call(
        _swa_kernel,
        grid=(bsz, nb),
        in_specs=[
            pl.BlockSpec((1, BLOCK, Q_COLS), cur),
            pl.BlockSpec((1, BLOCK, KV_COLS), prev),
            pl.BlockSpec((1, BLOCK, KV_COLS), cur),
            pl.BlockSpec((1, BLOCK, KV_COLS), prev),
            pl.BlockSpec((1, BLOCK, KV_COLS), cur),
            pl.BlockSpec((1, 4, 2 * BLOCK, 2 * BLOCK), lambda b, n: (jnp.minimum(n, 1), 0, 0, 0)),
            pl.BlockSpec((4, 2 * BLOCK, 1), lambda b, n: (0, 0, 0)),
        ],
        out_specs=pl.BlockSpec((1, BLOCK, Q_COLS), cur),
        out_shape=jax.ShapeDtypeStruct((bsz, seq, Q_COLS), BF16),
        compiler_params=_cparams(("arbitrary", "arbitrary")),
        name="swa",
    )(q, k, k, v, v, bias_tab, sink_tab)


def _t5_bucket(dist):
    n = jnp.maximum(dist, 0)
    exact = REL_BUCKETS // 2
    large = exact + (jnp.log(jnp.maximum(n, 1).astype(F32) / exact)
                     / math.log(REL_MAX_DIST / exact) * (REL_BUCKETS - exact)).astype(jnp.int32)
    large = jnp.minimum(large, REL_BUCKETS - 1)
    return jnp.where(n < exact, n, large)


def _band_tables(rel_bias, sinks):
    qi = jnp.arange(BLOCK)[:, None]
    kj = jnp.arange(2 * BLOCK)[None, :]
    dist = qi + BLOCK - kj
    bucket = _t5_bucket(dist)
    bias = jnp.zeros((ATT_HEADS, BLOCK, 2 * BLOCK), F32)
    for bkt in range(REL_BUCKETS):
        bias = jnp.where(bucket[None] == bkt, rel_bias[bkt].astype(F32)[:, None, None], bias)
    in_window = (dist >= 0) & (dist < WINDOW)
    masks = jnp.stack([in_window & (kj >= BLOCK), in_window])
    masked = jnp.where(masks[:, None], bias[None], NEG_INF)
    pairs = ((0, 2), (1, 3), (4, 6), (5, 7))
    bias_tab = jnp.stack([jnp.concatenate([masked[:, a], masked[:, b]], axis=1) for a, b in pairs], axis=1)
    s = sinks.astype(F32)
    sink_tab = jnp.stack([jnp.concatenate([jnp.full((BLOCK, 1), s[a]), jnp.full((BLOCK, 1), s[b])], axis=0)
                          for a, b in pairs])
    return bias_tab, sink_tab


def _outproj_kernel(x_ref, conv_ref, att_ref, w_ref, b_ref, g_ref, beta_ref, o_ref):
    mix = jnp.dot(conv_ref[...], w_ref[0:CONV_CH, :], preferred_element_type=F32)
    mix = mix + jnp.dot(att_ref[...], w_ref[CONV_CH:, :], preferred_element_type=F32)
    h = ALPHA * x_ref[...] + mix + b_ref[...]
    o_ref[...] = _layer_norm(h, g_ref[...], beta_ref[...])


def _outproj(x2d, conv2d, att2d, w_out, b_out, g, beta):
    t, d = x2d.shape
    row = lambda i: (i, 0)
    const = lambda i: (0, 0)
    return pl.pallas_call(
        _outproj_kernel,
        grid=(t // ROW_TILE,),
        in_specs=[
            pl.BlockSpec((ROW_TILE, d), row),
            pl.BlockSpec((ROW_TILE, CONV_CH), row),
            pl.BlockSpec((ROW_TILE, Q_COLS), row),
            pl.BlockSpec((d, d), const),
            pl.BlockSpec((1, d), const),
            pl.BlockSpec((1, d), const),
            pl.BlockSpec((1, d), const),
        ],
        out_specs=pl.BlockSpec((ROW_TILE, d), row),
        out_shape=jax.ShapeDtypeStruct((t, d), F32),
        compiler_params=_cparams(("arbitrary",)),
        name="outproj",
    )(x2d, conv2d, att2d, w_out, b_out, g, beta)


def _memkv_kernel(mem_ref, w_ref, k_ref, v_ref):
    kv = jnp.dot(mem_ref[0].astype(BF16), w_ref[...], preferred_element_type=F32)
    k_ref[0] = kv[:, :D_MODEL].astype(BF16)
    v_ref[0] = kv[:, D_MODEL:].astype(BF16)


def _memkv(mem, wkv):
    bsz, m, d = mem.shape
    return pl.pallas_call(
        _memkv_kernel,
        grid=(bsz,),
        in_specs=[pl.BlockSpec((1, m, d), lambda b: (b, 0, 0)),
                  pl.BlockSpec((d, 2 * d), lambda b: (0, 0))],
        out_specs=[pl.BlockSpec((1, m, d), lambda b: (b, 0, 0)),
                   pl.BlockSpec((1, m, d), lambda b: (b, 0, 0))],
        out_shape=[jax.ShapeDtypeStruct((bsz, m, d), BF16)] * 2,
        compiler_params=_cparams(("arbitrary",)),
        name="memkv",
    )(mem, wkv)


def _cross_kernel(x_ref, k_ref, v_ref, wq_ref, wo_ref, g_ref, beta_ref, rw_ref, o_ref, lt_ref):
    x = x_ref[0]
    q = jnp.dot(x.astype(BF16), wq_ref[...], preferred_element_type=F32) * (X_HEAD_DIM ** -0.5)
    q = q.astype(BF16)
    heads = []
    for h in range(X_HEADS):
        cols = slice(h * X_HEAD_DIM, (h + 1) * X_HEAD_DIM)
        logits = lax.dot_general(q[:, cols], k_ref[0, :, cols], (((1,), (1,)), ((), ())),
                                 preferred_element_type=F32)
        m = jnp.max(logits, axis=-1, keepdims=True)
        p = jnp.exp(logits - m)
        den = jnp.sum(p, axis=-1, keepdims=True)
        o = jnp.dot(p.astype(BF16), v_ref[0, :, cols], preferred_element_type=F32)
        heads.append((o / den).astype(BF16))
    att = jnp.concatenate(heads, axis=-1)
    cross = jnp.dot(att, wo_ref[...], preferred_element_type=F32)
    y = _layer_norm(ALPHA * x + cross, g_ref[...], beta_ref[...])
    o_ref[0] = y
    lt_ref[...] = lax.dot_general(rw_ref[...], y.astype(BF16), (((1,), (1,)), ((), ())),
                                  preferred_element_type=F32)


def _cross(x1, kmem, vmem, wq, wo, g, beta, rw_t):
    bsz, seq, d = x1.shape
    nt = seq // ROW_TILE
    row = lambda b, j: (b, j, 0)
    mem = lambda b, j: (b, 0, 0)
    const = lambda b, j: (0, 0)
    return pl.pallas_call(
        _cross_kernel,
        grid=(bsz, nt),
        in_specs=[
            pl.BlockSpec((1, ROW_TILE, d), row),
            pl.BlockSpec((1, MEM_LEN, d), mem),
            pl.BlockSpec((1, MEM_LEN, d), mem),
            pl.BlockSpec((d, d), const),
            pl.BlockSpec((d, d), const),
            pl.BlockSpec((1, d), const),
            pl.BlockSpec((1, d), const),
            pl.BlockSpec((N_EXPERTS, d), const),
        ],
        out_specs=[
            pl.BlockSpec((1, ROW_TILE, d), row),
            pl.BlockSpec((N_EXPERTS, ROW_TILE), lambda b, j: (0, b * nt + j)),
        ],
        out_shape=[
            jax.ShapeDtypeStruct((bsz, seq, d), F32),
            jax.ShapeDtypeStruct((N_EXPERTS, bsz * seq), F32),
        ],
        compiler_params=_cparams(("arbitrary", "arbitrary")),
        name="cross",
    )(x1, kmem, vmem, wq, wo, g, beta, rw_t)


def _route_kernel(lt_ref, rb_ref, tri_ref, pos_ref, gate_ref, cnt_ref):
    tn = ROUTE_TILE
    scores = _sigmoid(lt_ref[...])
    choice = scores + rb_ref[...]

    gscore = []
    member = lax.broadcasted_iota(jnp.int32, (GROUP_SIZE, tn), 0).astype(F32)
    for g in range(N_GROUPS):
        c = choice[g * GROUP_SIZE:(g + 1) * GROUP_SIZE, :]
        m1 = jnp.max(c, axis=0, keepdims=True)
        first = jnp.min(jnp.where(c == m1, member, float(GROUP_SIZE)), axis=0, keepdims=True)
        m2 = jnp.max(jnp.where(member == first, -jnp.inf, c), axis=0, keepdims=True)
        gscore.append(m1 + m2)

    keep_rows = []
    for g in range(N_GROUPS):
        beaten = jnp.zeros((1, tn), F32)
        for o in range(N_GROUPS):
            if o == g:
                continue
            ahead = (gscore[o] >= gscore[g]) if o < g else (gscore[o] > gscore[g])
            beaten = beaten + jnp.where(ahead, 1.0, 0.0)
        keep_rows.append(jnp.broadcast_to(beaten, (GROUP_SIZE, tn)))
    beaten_all = jnp.concatenate(keep_rows, axis=0)

    masked = jnp.where(beaten_all < TOPK_GROUPS, choice, -jnp.inf)
    eidx = lax.broadcasted_iota(jnp.int32, (N_EXPERTS, tn), 0).astype(F32)
    sel = jnp.zeros((N_EXPERTS, tn), F32)
    picks, weights = [], []
    for r in range(TOP_K):
        mx = jnp.max(masked, axis=0, keepdims=True)
        first = jnp.min(jnp.where(masked == mx, eidx, float(N_EXPERTS)), axis=0, keepdims=True)
        pick = eidx == first
        picks.append((pick, first))
        weights.append(jnp.sum(jnp.where(pick, scores, 0.0), axis=0, keepdims=True))
        masked = jnp.where(pick, -jnp.inf, masked)
        sel = jnp.where(pick, 1.0, sel)

    wsum = weights[0]
    for r in range(1, TOP_K):
        wsum = wsum + weights[r]

    count = jnp.sum(sel, axis=1, keepdims=True)
    run_len = jnp.floor((count + (RUN_ALIGN - 1.0)) * (1.0 / RUN_ALIGN)) * RUN_ALIGN
    run_len_b = jnp.broadcast_to(run_len, (N_EXPERTS, 128))
    er = lax.broadcasted_iota(jnp.int32, (N_EXPERTS, N_EXPERTS), 0)
    ec = lax.broadcasted_iota(jnp.int32, (N_EXPERTS, N_EXPERTS), 1)
    before = jnp.where(ec < er, 1.0, 0.0).astype(BF16)
    run_start = jnp.dot(before, run_len_b.astype(BF16), preferred_element_type=F32)[:, 0:1]
    incl = jnp.dot(sel.astype(BF16), tri_ref[...], preferred_element_type=F32)
    pos_mat = run_start + incl - sel
    cnt_ref[0] = run_len_b.astype(jnp.int32)

    for r in range(TOP_K):
        pick, _ = picks[r]
        pos_ref[0, r:r + 1, :] = jnp.sum(jnp.where(pick, pos_mat, 0.0), axis=0,
                                         keepdims=True).astype(jnp.int32)
        gate_ref[0, r:r + 1, :] = weights[r] / wsum * ROUTED_SCALE


def _route(logits_t, router_b, tri):
    e, t = logits_t.shape
    nt = t // ROUTE_TILE
    blk = lambda i: (i, 0, 0)
    return pl.pallas_call(
        _route_kernel,
        grid=(nt,),
        in_specs=[
            pl.BlockSpec((e, ROUTE_TILE), lambda i: (0, i)),
            pl.BlockSpec((e, 1), lambda i: (0, 0)),
            pl.BlockSpec((ROUTE_TILE, ROUTE_TILE), lambda i: (0, 0)),
        ],
        out_specs=[
            pl.BlockSpec((1, TOP_K, ROUTE_TILE), blk),
            pl.BlockSpec((1, TOP_K, ROUTE_TILE), blk),
            pl.BlockSpec((1, e, 128), blk),
        ],
        out_shape=[
            jax.ShapeDtypeStruct((nt, TOP_K, ROUTE_TILE), jnp.int32),
            jax.ShapeDtypeStruct((nt, TOP_K, ROUTE_TILE), F32),
            jax.ShapeDtypeStruct((nt, e, 128), jnp.int32),
        ],
        compiler_params=_cparams(("arbitrary",)),
        name="route",
    )(logits_t, router_b, tri)


def _for_each_run_piece(cnt_ref, tile, fn):
    def body(e, carry):
        run = tile * N_EXPERTS + e
        n = cnt_ref[run]
        for piece in RUN_PIECES:
            @pl.when((n & piece) != 0)
            def _():
                fn(run, n & (-2 * piece), piece)
        return carry

    lax.fori_loop(0, N_EXPERTS, body, 0)


def _pack_bf16_pair(lo_f32, hi_f32):
    lo = (lax.bitcast_convert_type(lo_f32, jnp.int32) >> 16) & LOW_HALF
    hi = lax.bitcast_convert_type(hi_f32, jnp.int32) & HIGH_HALF
    return lo | hi


def _unpack_bf16_pair(packed):
    lo = lax.bitcast_convert_type(packed << 16, F32).astype(BF16)
    hi = lax.bitcast_convert_type(packed & HIGH_HALF, F32).astype(BF16)
    return lo, hi


def _dispatch_kernel(gbase_ref, lbase_ref, cnt_ref, rows_ref, total_ref, pos_ref, x_ref, xs_ref,
                     stage, zeros, sem, zsem):
    i = pl.program_id(0)
    nt = pl.num_programs(0)
    slot = i % 2
    tn = ROUTE_TILE
    half = D_MODEL // 2

    def run_copy(tile, buf):
        def build(run, before, n):
            src = pl.multiple_of(lbase_ref[run] + before, RUN_ALIGN)
            dst = pl.multiple_of(gbase_ref[run] + before, RUN_ALIGN)
            return pltpu.make_async_copy(stage.at[buf, pl.ds(src, n), :], xs_ref.at[pl.ds(dst, n), :],
                                         sem.at[buf])
        return build

    @pl.when(i >= 2)
    def _():
        build = run_copy(i - 2, slot)
        _for_each_run_piece(cnt_ref, i - 2, lambda run, before, n: build(run, before, n).wait())

    xb = x_ref[...].astype(BF16)
    pos = pos_ref[0]
    row_iota = lax.broadcasted_iota(jnp.int32, (SORT_CHUNK, tn), 0)

    def chunk(c, carry):
        p0 = pl.multiple_of(c * SORT_CHUNK, SORT_CHUNK)
        target = row_iota + p0
        hit = pos[0:1, :] == target
        for k in range(1, TOP_K):
            hit = hit | (pos[k:k + 1, :] == target)
        onehot = jnp.where(hit, 1.0, 0.0).astype(BF16)
        rows = jnp.dot(onehot, xb, preferred_element_type=F32)
        stage[slot, pl.ds(p0, SORT_CHUNK), :] = _pack_bf16_pair(rows[:, :half], rows[:, half:])
        return carry

    lax.fori_loop(0, (rows_ref[i] + SORT_CHUNK - 1) // SORT_CHUNK, chunk, 0)

    build_now = run_copy(i, slot)
    _for_each_run_piece(cnt_ref, i, lambda run, before, n: build_now(run, before, n).start())

    @pl.when(i == nt - 1)
    def _():
        @pl.when(nt >= 2)
        def _():
            build = run_copy(i - 1, 1 - slot)
            _for_each_run_piece(cnt_ref, i - 1, lambda run, before, n: build(run, before, n).wait())

        _for_each_run_piece(cnt_ref, i, lambda run, before, n: build_now(run, before, n).wait())

        zeros[...] = jnp.zeros_like(zeros)
        total = total_ref[0]
        tail = (-total) & (EXPERT_ROWS - 1)
        for piece in RUN_PIECES:
            if piece >= EXPERT_ROWS:
                continue

            @pl.when((tail & piece) != 0)
            def _():
                dst = pl.multiple_of(total + (tail & (-2 * piece)), RUN_ALIGN)
                cp = pltpu.make_async_copy(zeros.at[pl.ds(0, piece), :], xs_ref.at[pl.ds(dst, piece), :], zsem)
                cp.start()
                cp.wait()

        first_free = (total + tail) // EXPERT_ROWS

        def zero_block(b):
            dst = pl.multiple_of(b * EXPERT_ROWS, EXPERT_ROWS)
            return pltpu.make_async_copy(zeros, xs_ref.at[pl.ds(dst, EXPERT_ROWS), :], zsem)

        n_blocks = xs_ref.shape[0] // EXPERT_ROWS
        lax.fori_loop(first_free, n_blocks, lambda b, c: (zero_block(b).start(), c)[1], 0)
        lax.fori_loop(first_free, n_blocks, lambda b, c: (zero_block(b).wait(), c)[1], 0)


def _dispatch(gbase, lbase, cnt, tile_rows, total, pos, x2d, n_rows):
    t, d = x2d.shape
    nt = t // ROUTE_TILE
    return pl.pallas_call(
        _dispatch_kernel,
        grid_spec=pltpu.PrefetchScalarGridSpec(
            num_scalar_prefetch=5,
            grid=(nt,),
            in_specs=[pl.BlockSpec((1, TOP_K, ROUTE_TILE), lambda i, *_: (i, 0, 0)),
                      pl.BlockSpec((ROUTE_TILE, d), lambda i, *_: (i, 0))],
            out_specs=pl.BlockSpec(memory_space=pl.ANY),
            scratch_shapes=[pltpu.VMEM((2, TILE_CAP, d // 2), jnp.int32),
                            pltpu.VMEM((EXPERT_ROWS, d // 2), jnp.int32),
                            pltpu.SemaphoreType.DMA((2,)),
                            pltpu.SemaphoreType.DMA(())],
        ),
        out_shape=jax.ShapeDtypeStruct((n_rows, d // 2), jnp.int32),
        compiler_params=_cparams(("arbitrary",)),
        name="dispatch",
    )(gbase, lbase, cnt, tile_rows, total, pos, x2d)


def _experts_kernel(ib_ref, ie_ref, flag_ref, seg_ref, xs_ref, wg_ref, wu_ref, wd_ref, y_ref,
                    wg_bf, wu_bf, wd_bf):
    w = pl.program_id(0)
    flags = flag_ref[w]
    valid = (flags & 1) != 0
    first_of_block = (flags & 2) != 0
    new_expert = (flags & 4) != 0

    @pl.when((flags & 8) != 0)
    def _():
        y_ref[...] = jnp.zeros_like(y_ref)

    @pl.when(new_expert)
    def _():
        wg_bf[...] = wg_ref[0].astype(BF16)
        wu_bf[...] = wu_ref[0].astype(BF16)
        wd_bf[...] = wd_ref[0].astype(BF16)

    @pl.when(valid)
    def _():
        e = ie_ref[w]
        half = D_MODEL // 2
        x_lo, x_hi = _unpack_bf16_pair(xs_ref[...])
        gte = (jnp.dot(x_lo, wg_bf[0:half, :], preferred_element_type=F32)
               + jnp.dot(x_hi, wg_bf[half:, :], preferred_element_type=F32))
        up = (jnp.dot(x_lo, wu_bf[0:half, :], preferred_element_type=F32)
              + jnp.dot(x_hi, wu_bf[half:, :], preferred_element_type=F32))
        h = (gte * _sigmoid(gte) * up).astype(BF16)
        y = jnp.dot(h, wd_bf[...], preferred_element_type=F32)
        y_bf = y.astype(BF16).astype(F32)
        packed = _pack_bf16_pair(y_bf[:, :half], y_bf[:, half:])
        rows = ib_ref[w] * EXPERT_ROWS + lax.broadcasted_iota(jnp.int32, (EXPERT_ROWS, 1), 0)
        mine = (rows >= seg_ref[e]) & (rows < seg_ref[e + 1])

        @pl.when(first_of_block)
        def _():
            y_ref[...] = jnp.where(mine, packed, 0)

        @pl.when(jnp.logical_not(first_of_block))
        def _():
            y_ref[...] = jnp.where(mine, packed, y_ref[...])


def _experts(item_block, item_expert, item_flags, seg_bounds, xs, wg, wu, wd):
    n, dh = xs.shape
    n_items = item_block.shape[0]
    _, d, ff = wg.shape
    return pl.pallas_call(
        _experts_kernel,
        grid_spec=pltpu.PrefetchScalarGridSpec(
            num_scalar_prefetch=4,
            grid=(n_items,),
            in_specs=[
                pl.BlockSpec((EXPERT_ROWS, dh), lambda w, ib, ie, fl, sg: (ib[w], 0)),
                pl.BlockSpec((1, d, ff), lambda w, ib, ie, fl, sg: (ie[w], 0, 0)),
                pl.BlockSpec((1, d, ff), lambda w, ib, ie, fl, sg: (ie[w], 0, 0)),
                pl.BlockSpec((1, ff, d), lambda w, ib, ie, fl, sg: (ie[w], 0, 0)),
            ],
            out_specs=pl.BlockSpec((EXPERT_ROWS, dh), lambda w, ib, ie, fl, sg: (ib[w], 0)),
            scratch_shapes=[pltpu.VMEM((d, ff), BF16), pltpu.VMEM((d, ff), BF16), pltpu.VMEM((ff, d), BF16)],
        ),
        out_shape=jax.ShapeDtypeStruct((n, dh), jnp.int32),
        compiler_params=_cparams(("arbitrary",)),
        name="experts",
    )(item_block, item_expert, item_flags, seg_bounds, xs, wg, wu, wd)


def _work_items(counts, n_rows):
    n_items = n_rows // EXPERT_ROWS + N_EXPERTS
    seg_end = jnp.cumsum(counts)
    seg_start = seg_end - counts
    first_blk = seg_start // EXPERT_ROWS
    n_blk = jnp.where(counts > 0, (seg_end - 1) // EXPERT_ROWS - first_blk + 1, 0)
    item_end = jnp.cumsum(n_blk)
    item_start = item_end - n_blk
    total = item_end[-1]
    w = jnp.arange(n_items, dtype=jnp.int32)
    wc = jnp.minimum(w, total - 1)
    e = jnp.minimum(jnp.sum((item_end[None, :] <= wc[:, None]).astype(jnp.int32), axis=1), N_EXPERTS - 1)
    onehot = (e[:, None] == jnp.arange(N_EXPERTS, dtype=jnp.int32)[None, :]).astype(jnp.int32)
    b = jnp.sum(onehot * (first_blk - item_start)[None, :], axis=1) + wc
    valid = w < total
    n_blocks = n_rows // EXPERT_ROWS
    free_blk = (seg_end[-1] + EXPERT_ROWS - 1) // EXPERT_ROWS + (w - total)
    fill = jnp.logical_not(valid) & (free_blk < n_blocks)
    b = jnp.where(valid, b, jnp.minimum(free_blk, n_blocks - 1)).astype(jnp.int32)
    prev_b = jnp.concatenate([jnp.full((1,), -1, jnp.int32), b[:-1]])
    prev_e = jnp.concatenate([jnp.full((1,), -1, jnp.int32), e[:-1]])
    flags = (valid.astype(jnp.int32) + 2 * (valid & (b != prev_b)).astype(jnp.int32)
             + 4 * (valid & (e != prev_e)).astype(jnp.int32) + 8 * fill.astype(jnp.int32))
    seg_bounds = jnp.concatenate([seg_start, seg_end[-1:]]).astype(jnp.int32)
    return b, e, flags, seg_bounds


def _combine_kernel(gbase_ref, lbase_ref, cnt_ref, rows_ref, pos_ref, gate_ref, x_ref,
                    sg_ref, su_ref, sd_ref, g_ref, beta_ref, yb_ref, o_ref, stage, acc, sem):
    i = pl.program_id(0)
    nt = pl.num_programs(0)
    slot = i % 2
    tn = ROUTE_TILE
    half = D_MODEL // 2

    def run_copy(buf):
        def build(run, before, n):
            src = pl.multiple_of(gbase_ref[run] + before, RUN_ALIGN)
            dst = pl.multiple_of(lbase_ref[run] + before, RUN_ALIGN)
            return pltpu.make_async_copy(yb_ref.at[pl.ds(src, n), :], stage.at[buf, pl.ds(dst, n), :],
                                         sem.at[buf])
        return build

    def start_tile(tile, buf):
        build = run_copy(buf)
        _for_each_run_piece(cnt_ref, tile, lambda run, before, n: build(run, before, n).start())

    @pl.when(i == 0)
    def _():
        stage[...] = jnp.zeros_like(stage)
        start_tile(0, 0)

    @pl.when(i + 1 < nt)
    def _():
        start_tile(i + 1, 1 - slot)

    x = x_ref[...]
    xb = x.astype(BF16)
    gte = jnp.dot(xb, sg_ref[...], preferred_element_type=F32)
    up = jnp.dot(xb, su_ref[...], preferred_element_type=F32)
    h = (gte * _sigmoid(gte) * up).astype(BF16)
    acc[...] = jnp.dot(h, sd_ref[...], preferred_element_type=F32)

    build_now = run_copy(slot)
    _for_each_run_piece(cnt_ref, i, lambda run, before, n: build_now(run, before, n).wait())

    pos = pos_ref[...]
    gates = gate_ref[...]
    lane_iota = lax.broadcasted_iota(jnp.int32, (tn, SORT_CHUNK), 1)

    def chunk(c, carry):
        p0 = pl.multiple_of(c * SORT_CHUNK, SORT_CHUNK)
        target = lane_iota + p0
        wts = jnp.zeros((tn, SORT_CHUNK), F32)
        for k in range(TOP_K):
            wts = jnp.where(pos[:, k:k + 1] == target, gates[:, k:k + 1], wts)
        wts = wts.astype(BF16)
        y_lo, y_hi = _unpack_bf16_pair(stage[slot, pl.ds(p0, SORT_CHUNK), :])
        acc[:, 0:half] += jnp.dot(wts, y_lo, preferred_element_type=F32)
        acc[:, half:] += jnp.dot(wts, y_hi, preferred_element_type=F32)
        return carry

    lax.fori_loop(0, (rows_ref[i] + SORT_CHUNK - 1) // SORT_CHUNK, chunk, 0)
    o_ref[...] = _layer_norm(ALPHA * x + acc[...], g_ref[...], beta_ref[...])


def _combine(gbase, lbase, cnt, tile_rows, pos_t, gate_t, x2d, sg, su, sd, g, beta, yb):
    t, d = x2d.shape
    nt = t // ROUTE_TILE
    ff = sg.shape[-1]
    row = lambda i, *_: (i, 0)
    const = lambda i, *_: (0, 0)
    return pl.pallas_call(
        _combine_kernel,
        grid_spec=pltpu.PrefetchScalarGridSpec(
            num_scalar_prefetch=4,
            grid=(nt,),
            in_specs=[
                pl.BlockSpec((ROUTE_TILE, TOP_K), row),
                pl.BlockSpec((ROUTE_TILE, TOP_K), row),
                pl.BlockSpec((ROUTE_TILE, d), row),
                pl.BlockSpec((d, ff), const),
                pl.BlockSpec((d, ff), const),
                pl.BlockSpec((ff, d), const),
                pl.BlockSpec((1, d), const),
                pl.BlockSpec((1, d), const),
                pl.BlockSpec(memory_space=pl.ANY),
            ],
            out_specs=pl.BlockSpec((ROUTE_TILE, d), row),
            scratch_shapes=[pltpu.VMEM((2, TILE_CAP, d // 2), jnp.int32),
                            pltpu.VMEM((ROUTE_TILE, d), F32),
                            pltpu.SemaphoreType.DMA((2,))],
        ),
        out_shape=jax.ShapeDtypeStruct((t, d), F32),
        compiler_params=_cparams(("arbitrary",)),
        name="combine",
    )(gbase, lbase, cnt, tile_rows, pos_t, gate_t, x2d, sg, su, sd, g, beta, yb)


def _run_tables(cnt):
    lbase = jnp.cumsum(cnt, axis=1) - cnt
    per_expert = jnp.sum(cnt, axis=0)
    seg_start = jnp.cumsum(per_expert) - per_expert
    gbase = seg_start[None, :] + jnp.cumsum(cnt, axis=0) - cnt
    flat = lambda a: a.reshape(-1).astype(jnp.int32)
    return flat(gbase), flat(lbase), flat(cnt), jnp.sum(cnt, axis=1).astype(jnp.int32), per_expert


def kernel(x, mem, w_in, b_in, conv_w, conv_b, conv_ln_g, conv_ln_b, attn_sinks, rel_bias, w_out, b_out, ln1_g, ln1_b, xq_w, xkv_w, xo_w, ln2_g, ln2_b, router_w, router_b, exp_gate, exp_up, exp_down, sh_gate, sh_up, sh_down, ln3_g, ln3_b):
    bsz, seq, d = x.shape
    t = bsz * seq
    bias_tab, sink_tab = _band_tables(rel_bias, attn_sinks[0])
    tri = (jnp.arange(ROUTE_TILE)[:, None] <= jnp.arange(ROUTE_TILE)[None, :]).astype(BF16)
    row = lambda p: p.reshape(1, -1)
    for l in range(DEPTH):
        conv_out, q, k, v = _mix(x, w_in[l].astype(BF16), row(b_in[l]), conv_w[l], row(conv_b[l]),
                                 row(conv_ln_g[l]), row(conv_ln_b[l]))
        att = _swa(q, k, v, bias_tab, sink_tab)
        x1 = _outproj(x.reshape(t, d), conv_out.reshape(t, CONV_CH), att.reshape(t, Q_COLS),
                      w_out[l].astype(BF16), row(b_out[l]), row(ln1_g[l]), row(ln1_b[l]))
        kmem, vmem = _memkv(mem, xkv_w[l].astype(BF16))
        x2, logits_t = _cross(x1.reshape(bsz, seq, d), kmem, vmem, xq_w[l].astype(BF16),
                              xo_w[l].astype(BF16), row(ln2_g[l]), row(ln2_b[l]),
                              router_w[l].T.astype(BF16))
        x2 = x2.reshape(t, d)
        pos, gate, cnt = _route(logits_t, router_b[l].reshape(-1, 1), tri)
        nt = t // ROUTE_TILE
        n_rows = -(-(t * TOP_K + nt * N_EXPERTS * (RUN_ALIGN - 1)) // EXPERT_ROWS) * EXPERT_ROWS
        gbase, lbase, cnt_flat, tile_rows, per_expert = _run_tables(cnt[:, :, 0])
        item_block, item_expert, item_flags, seg_bounds = _work_items(per_expert, n_rows)
        xs = _dispatch(gbase, lbase, cnt_flat, tile_rows, seg_bounds[-1:], pos, x2, n_rows)
        yb = _experts(item_block, item_expert, item_flags, seg_bounds, xs,
                      exp_gate[l], exp_up[l], exp_down[l])
        pos_t = jnp.transpose(pos, (0, 2, 1)).reshape(t, TOP_K)
        gate_t = jnp.transpose(gate, (0, 2, 1)).reshape(t, TOP_K)
        x = _combine(gbase, lbase, cnt_flat, tile_rows, pos_t, gate_t, x2, sh_gate[l].astype(BF16),
                     sh_up[l].astype(BF16), sh_down[l].astype(BF16), row(ln3_g[l]), row(ln3_b[l]),
                     yb).reshape(bsz, seq, d)
    return x
```

```python
import functools
import math

import jax
import jax.numpy as jnp
from jax import lax
from jax.experimental import pallas as pl
from jax.experimental.pallas import tpu as pltpu

D_MODEL = 1024
MEM_LEN = 256
HEAD_DIM = 64
CONV_CH = D_MODEL // 2
CONV_WIDTH = 31
ATT_HEADS = 8
KV_HEADS = 2
WINDOW = 128
BLOCK = 128
REL_BUCKETS = 32
REL_MAX_DIST = 128
Q_COLS = ATT_HEADS * HEAD_DIM
KV_COLS = KV_HEADS * HEAD_DIM
IN_COLS = 2 * CONV_CH + Q_COLS + 2 * KV_COLS
X_HEADS = 4
X_HEAD_DIM = D_MODEL // X_HEADS
N_EXPERTS = 64
TOP_K = 8
N_GROUPS = 8
GROUP_SIZE = N_EXPERTS // N_GROUPS
TOPK_GROUPS = 4
EXPERT_FF = D_MODEL // 4
ROUTED_SCALE = 2.5
DEPTH = 1
ALPHA = (2 * DEPTH) ** 0.25
LN_EPS = 1e-5
NEG_INF = -1e30

F32 = jnp.float32
BF16 = jnp.bfloat16

VMEM_LIMIT_BYTES = 56 * 1024 * 1024

ROW_TILE = 512
CONV_ROWS = 32
CONV_HALO = 32
ROUTE_TILE = 512
EXPERT_ROWS = 512
RUN_ALIGN = 8
RUN_PIECES = tuple(1 << b for b in range(9, 2, -1))
TILE_CAP = ROUTE_TILE * TOP_K + N_EXPERTS * RUN_ALIGN
SORT_CHUNK = 512
LOW_HALF = 0xFFFF
HIGH_HALF = -0x10000


def _cparams(sem):
    return pltpu.CompilerParams(dimension_semantics=sem, vmem_limit_bytes=VMEM_LIMIT_BYTES)


def _layer_norm(h, g, b):
    mu = jnp.mean(h, axis=-1, keepdims=True)
    d = h - mu
    var = jnp.mean(d * d, axis=-1, keepdims=True)
    return d * lax.rsqrt(var + LN_EPS) * g + b


def _sigmoid(x):
    return 1.0 / (1.0 + jnp.exp(-x))


def _mix_kernel(x_ref, w_ref, b_ref, cw_ref, cb_ref, cg_ref, cbeta_ref,
                conv_ref, q_ref, k_ref, v_ref, u_ext):
    j = pl.program_id(1)
    xb = x_ref[0].astype(BF16)
    proj = jnp.dot(xb, w_ref[...], preferred_element_type=F32) + b_ref[...]
    a = proj[:, :CONV_CH]
    g = proj[:, CONV_CH:2 * CONV_CH]
    q_ref[0] = (proj[:, 2 * CONV_CH:2 * CONV_CH + Q_COLS] * (HEAD_DIM ** -0.5)).astype(BF16)
    k_ref[0] = proj[:, 2 * CONV_CH + Q_COLS:2 * CONV_CH + Q_COLS + KV_COLS].astype(BF16)
    v_ref[0] = proj[:, 2 * CONV_CH + Q_COLS + KV_COLS:].astype(BF16)

    @pl.when(j == 0)
    def _():
        u_ext[0:CONV_HALO, :] = jnp.zeros((CONV_HALO, CONV_CH), F32)

    u_ext[CONV_HALO:CONV_HALO + ROW_TILE, :] = a * _sigmoid(g)

    first_tap = CONV_HALO - (CONV_WIDTH - 1)
    for c in range(ROW_TILE // CONV_ROWS):
        base = c * CONV_ROWS + first_tap
        acc = jnp.zeros((CONV_ROWS, CONV_CH), F32) + cb_ref[...]
        for t in range(CONV_WIDTH):
            acc = acc + u_ext[base + t:base + t + CONV_ROWS, :] * cw_ref[t:t + 1, :]
        y = _layer_norm(acc, cg_ref[...], cbeta_ref[...])
        conv_ref[0, c * CONV_ROWS:(c + 1) * CONV_ROWS, :] = (y * _sigmoid(y)).astype(BF16)

    u_ext[0:CONV_HALO, :] = u_ext[ROW_TILE:ROW_TILE + CONV_HALO, :]


def _mix(x, w_in, b_in, conv_w, conv_b, conv_g, conv_beta):
    bsz, seq, d = x.shape
    nt = seq // ROW_TILE
    row = lambda b, j: (b, j, 0)
    const2 = lambda b, j: (0, 0)
    return pl.pallas_call(
        _mix_kernel,
        grid=(bsz, nt),
        in_specs=[
            pl.BlockSpec((1, ROW_TILE, d), row),
            pl.BlockSpec((d, IN_COLS), const2),
            pl.BlockSpec((1, IN_COLS), const2),
            pl.BlockSpec((CONV_WIDTH, CONV_CH), const2),
            pl.BlockSpec((1, CONV_CH), const2),
            pl.BlockSpec((1, CONV_CH), const2),
            pl.BlockSpec((1, CONV_CH), const2),
        ],
        out_specs=[
            pl.BlockSpec((1, ROW_TILE, CONV_CH), row),
            pl.BlockSpec((1, ROW_TILE, Q_COLS), row),
            pl.BlockSpec((1, ROW_TILE, KV_COLS), row),
            pl.BlockSpec((1, ROW_TILE, KV_COLS), row),
        ],
        out_shape=[
            jax.ShapeDtypeStruct((bsz, seq, CONV_CH), BF16),
            jax.ShapeDtypeStruct((bsz, seq, Q_COLS), BF16),
            jax.ShapeDtypeStruct((bsz, seq, KV_COLS), BF16),
            jax.ShapeDtypeStruct((bsz, seq, KV_COLS), BF16),
        ],
        scratch_shapes=[pltpu.VMEM((ROW_TILE + CONV_HALO, CONV_CH), F32)],
        compiler_params=_cparams(("arbitrary", "arbitrary")),
        name="mix",
    )(x, w_in, b_in, conv_w, conv_b, conv_g, conv_beta)


def _swa_kernel(q_ref, kp_ref, kc_ref, vp_ref, vc_ref, bias_ref, sink_ref, o_ref):
    band = 2 * BLOCK
    lane = lax.broadcasted_iota(jnp.int32, (band, 2 * HEAD_DIM), 1)
    low = lane < HEAD_DIM

    def placements(prev_ref, cur_ref):
        t = jnp.concatenate([prev_ref[0], cur_ref[0]], axis=0).astype(F32)
        tr = pltpu.roll(t, HEAD_DIM, 1)
        zero = jnp.zeros_like(t)
        kv0_low = jnp.where(low, t, zero).astype(BF16)
        kv1_high = jnp.where(low, zero, t).astype(BF16)
        kv1_low = jnp.where(low, tr, zero).astype(BF16)
        kv0_high = jnp.where(low, zero, tr).astype(BF16)
        return (kv0_low, kv0_high, kv1_low, kv1_high)

    ks = placements(kp_ref, kc_ref)
    vs = placements(vp_ref, vc_ref)
    q = q_ref[0]
    slab = 2 * HEAD_DIM
    q_kv0 = jnp.concatenate([q[:, 0:slab], q[:, slab:2 * slab]], axis=0)
    q_kv1 = jnp.concatenate([q[:, 2 * slab:3 * slab], q[:, 3 * slab:4 * slab]], axis=0)
    outs = []
    for s in range(4):
        qs = q_kv0 if s < 2 else q_kv1
        logits = lax.dot_general(qs, ks[s], (((1,), (1,)), ((), ())),
                                 preferred_element_type=F32) + bias_ref[0, s]
        sink = sink_ref[s]
        m = jnp.maximum(jnp.max(logits, axis=-1, keepdims=True), sink)
        p = jnp.exp(logits - m)
        den = jnp.sum(p, axis=-1, keepdims=True) + jnp.exp(sink - m)
        o = jnp.dot(p.astype(BF16), vs[s], preferred_element_type=F32)
        outs.append(o / den)
    o_kv0 = outs[0] + outs[1]
    o_kv1 = outs[2] + outs[3]
    o_ref[0, :, 0:slab] = o_kv0[0:BLOCK].astype(BF16)
    o_ref[0, :, slab:2 * slab] = o_kv0[BLOCK:2 * BLOCK].astype(BF16)
    o_ref[0, :, 2 * slab:3 * slab] = o_kv1[0:BLOCK].astype(BF16)
    o_ref[0, :, 3 * slab:4 * slab] = o_kv1[BLOCK:2 * BLOCK].astype(BF16)


def _swa(q, k, v, bias_tab, sink_tab):
    bsz, seq, _ = q.shape
    nb = seq // BLOCK
    cur = lambda b, n: (b, n, 0)
    prev = lambda b, n: (b, jnp.maximum(n - 1, 0), 0)
    return pl.pallas_call(
        _swa_kernel,
        grid=(bsz, nb),
        in_specs=[
            pl.BlockSpec((1, BLOCK, Q_COLS), cur),
            pl.BlockSpec((1, BLOCK, KV_COLS), prev),
            pl.BlockSpec((1, BLOCK, KV_COLS), cur),
            pl.BlockSpec((1, BLOCK, KV_COLS), prev),
            pl.BlockSpec((1, BLOCK, KV_COLS), cur),
            pl.BlockSpec((1, 4, 2 * BLOCK, 2 * BLOCK), lambda b, n: (jnp.minimum(n, 1), 0, 0, 0)),
            pl.BlockSpec((4, 2 * BLOCK, 1), lambda b, n: (0, 0, 0)),
        ],
        out_specs=pl.BlockSpec((1, BLOCK, Q_COLS), cur),
        out_shape=jax.ShapeDtypeStruct((bsz, seq, Q_COLS), BF16),
        compiler_params=_cparams(("arbitrary", "arbitrary")),
        name="swa",
    )(q, k, k, v, v, bias_tab, sink_tab)


def _t5_bucket(dist):
    n = jnp.maximum(dist, 0)
    exact = REL_BUCKETS // 2
    large = exact + (jnp.log(jnp.maximum(n, 1).astype(F32) / exact)
                     / math.log(REL_MAX_DIST / exact) * (REL_BUCKETS - exact)).astype(jnp.int32)
    large = jnp.minimum(large, REL_BUCKETS - 1)
    return jnp.where(n < exact, n, large)


def _band_tables(rel_bias, sinks):
    qi = jnp.arange(BLOCK)[:, None]
    kj = jnp.arange(2 * BLOCK)[None, :]
    dist = qi + BLOCK - kj
    bucket = _t5_bucket(dist)
    bias = jnp.zeros((ATT_HEADS, BLOCK, 2 * BLOCK), F32)
    for bkt in range(REL_BUCKETS):
        bias = jnp.where(bucket[None] == bkt, rel_bias[bkt].astype(F32)[:, None, None], bias)
    in_window = (dist >= 0) & (dist < WINDOW)
    masks = jnp.stack([in_window & (kj >= BLOCK), in_window])
    masked = jnp.where(masks[:, None], bias[None], NEG_INF)
    pairs = ((0, 2), (1, 3), (4, 6), (5, 7))
    bias_tab = jnp.stack([jnp.concatenate([masked[:, a], masked[:, b]], axis=1) for a, b in pairs], axis=1)
    s = sinks.astype(F32)
    sink_tab = jnp.stack([jnp.concatenate([jnp.full((BLOCK, 1), s[a]), jnp.full((BLOCK, 1), s[b])], axis=0)
                          for a, b in pairs])
    return bias_tab, sink_tab


def _outproj_kernel(x_ref, conv_ref, att_ref, w_ref, b_ref, g_ref, beta_ref, o_ref):
    mix = jnp.dot(conv_ref[...], w_ref[0:CONV_CH, :], preferred_element_type=F32)
    mix = mix + jnp.dot(att_ref[...], w_ref[CONV_CH:, :], preferred_element_type=F32)
    h = ALPHA * x_ref[...] + mix + b_ref[...]
    o_ref[...] = _layer_norm(h, g_ref[...], beta_ref[...])


def _outproj(x2d, conv2d, att2d, w_out, b_out, g, beta):
    t, d = x2d.shape
    row = lambda i: (i, 0)
    const = lambda i: (0, 0)
    return pl.pallas_call(
        _outproj_kernel,
        grid=(t // ROW_TILE,),
        in_specs=[
            pl.BlockSpec((ROW_TILE, d), row),
            pl.BlockSpec((ROW_TILE, CONV_CH), row),
            pl.BlockSpec((ROW_TILE, Q_COLS), row),
            pl.BlockSpec((d, d), const),
            pl.BlockSpec((1, d), const),
            pl.BlockSpec((1, d), const),
            pl.BlockSpec((1, d), const),
        ],
        out_specs=pl.BlockSpec((ROW_TILE, d), row),
        out_shape=jax.ShapeDtypeStruct((t, d), F32),
        compiler_params=_cparams(("arbitrary",)),
        name="outproj",
    )(x2d, conv2d, att2d, w_out, b_out, g, beta)


def _memkv_kernel(mem_ref, w_ref, k_ref, v_ref):
    kv = jnp.dot(mem_ref[0].astype(BF16), w_ref[...], preferred_element_type=F32)
    k_ref[0] = kv[:, :D_MODEL].astype(BF16)
    v_ref[0] = kv[:, D_MODEL:].astype(BF16)


def _memkv(mem, wkv):
    bsz, m, d = mem.shape
    return pl.pallas_call(
        _memkv_kernel,
        grid=(bsz,),
        in_specs=[pl.BlockSpec((1, m, d), lambda b: (b, 0, 0)),
                  pl.BlockSpec((d, 2 * d), lambda b: (0, 0))],
        out_specs=[pl.BlockSpec((1, m, d), lambda b: (b, 0, 0)),
                   pl.BlockSpec((1, m, d), lambda b: (b, 0, 0))],
        out_shape=[jax.ShapeDtypeStruct((bsz, m, d), BF16)] * 2,
        compiler_params=_cparams(("arbitrary",)),
        name="memkv",
    )(mem, wkv)


def _cross_kernel(x_ref, k_ref, v_ref, wq_ref, wo_ref, g_ref, beta_ref, rw_ref, o_ref, lt_ref):
    x = x_ref[0]
    q = jnp.dot(x.astype(BF16), wq_ref[...], preferred_element_type=F32) * (X_HEAD_DIM ** -0.5)
    q = q.astype(BF16)
    heads = []
    for h in range(X_HEADS):
        cols = slice(h * X_HEAD_DIM, (h + 1) * X_HEAD_DIM)
        logits = lax.dot_general(q[:, cols], k_ref[0, :, cols], (((1,), (1,)), ((), ())),
                                 preferred_element_type=F32)
        m = jnp.max(logits, axis=-1, keepdims=True)
        p = jnp.exp(logits - m)
        den = jnp.sum(p, axis=-1, keepdims=True)
        o = jnp.dot(p.astype(BF16), v_ref[0, :, cols], preferred_element_type=F32)
        heads.append((o / den).astype(BF16))
    att = jnp.concatenate(heads, axis=-1)
    cross = jnp.dot(att, wo_ref[...], preferred_element_type=F32)
    y = _layer_norm(ALPHA * x + cross, g_ref[...], beta_ref[...])
    o_ref[0] = y
    lt_ref[...] = lax.dot_general(rw_ref[...], y.astype(BF16), (((1,), (1,)), ((), ())),
                                  preferred_element_type=F32)


def _cross(x1, kmem, vmem, wq, wo, g, beta, rw_t):
    bsz, seq, d = x1.shape
    nt = seq // ROW_TILE
    row = lambda b, j: (b, j, 0)
    mem = lambda b, j: (b, 0, 0)
    const = lambda b, j: (0, 0)
    return pl.pallas_call(
        _cross_kernel,
        grid=(bsz, nt),
        in_specs=[
            pl.BlockSpec((1, ROW_TILE, d), row),
            pl.BlockSpec((1, MEM_LEN, d), mem),
            pl.BlockSpec((1, MEM_LEN, d), mem),
            pl.BlockSpec((d, d), const),
            pl.BlockSpec((d, d), const),
            pl.BlockSpec((1, d), const),
            pl.BlockSpec((1, d), const),
            pl.BlockSpec((N_EXPERTS, d), const),
        ],
        out_specs=[
            pl.BlockSpec((1, ROW_TILE, d), row),
            pl.BlockSpec((N_EXPERTS, ROW_TILE), lambda b, j: (0, b * nt + j)),
        ],
        out_shape=[
            jax.ShapeDtypeStruct((bsz, seq, d), F32),
            jax.ShapeDtypeStruct((N_EXPERTS, bsz * seq), F32),
        ],
        compiler_params=_cparams(("arbitrary", "arbitrary")),
        name="cross",
    )(x1, kmem, vmem, wq, wo, g, beta, rw_t)


def _route_kernel(lt_ref, rb_ref, tri_ref, pos_ref, gate_ref, cnt_ref):
    tn = ROUTE_TILE
    scores = _sigmoid(lt_ref[...])
    choice = scores + rb_ref[...]

    gscore = []
    member = lax.broadcasted_iota(jnp.int32, (GROUP_SIZE, tn), 0).astype(F32)
    for g in range(N_GROUPS):
        c = choice[g * GROUP_SIZE:(g + 1) * GROUP_SIZE, :]
        m1 = jnp.max(c, axis=0, keepdims=True)
        first = jnp.min(jnp.where(c == m1, member, float(GROUP_SIZE)), axis=0, keepdims=True)
        m2 = jnp.max(jnp.where(member == first, -jnp.inf, c), axis=0, keepdims=True)
        gscore.append(m1 + m2)

    keep_rows = []
    for g in range(N_GROUPS):
        beaten = jnp.zeros((1, tn), F32)
        for o in range(N_GROUPS):
            if o == g:
                continue
            ahead = (gscore[o] >= gscore[g]) if o < g else (gscore[o] > gscore[g])
            beaten = beaten + jnp.where(ahead, 1.0, 0.0)
        keep_rows.append(jnp.broadcast_to(beaten, (GROUP_SIZE, tn)))
    beaten_all = jnp.concatenate(keep_rows, axis=0)

    masked = jnp.where(beaten_all < TOPK_GROUPS, choice, -jnp.inf)
    eidx = lax.broadcasted_iota(jnp.int32, (N_EXPERTS, tn), 0).astype(F32)
    sel = jnp.zeros((N_EXPERTS, tn), F32)
    picks, weights = [], []
    for r in range(TOP_K):
        mx = jnp.max(masked, axis=0, keepdims=True)
        first = jnp.min(jnp.where(masked == mx, eidx, float(N_EXPERTS)), axis=0, keepdims=True)
        pick = eidx == first
        picks.append((pick, first))
        weights.append(jnp.sum(jnp.where(pick, scores, 0.0), axis=0, keepdims=True))
        masked = jnp.where(pick, -jnp.inf, masked)
        sel = jnp.where(pick, 1.0, sel)

    wsum = weights[0]
    for r in range(1, TOP_K):
        wsum = wsum + weights[r]

    count = jnp.sum(sel, axis=1, keepdims=True)
    run_len = jnp.floor((count + (RUN_ALIGN - 1.0)) * (1.0 / RUN_ALIGN)) * RUN_ALIGN
    run_len_b = jnp.broadcast_to(run_len, (N_EXPERTS, 128))
    er = lax.broadcasted_iota(jnp.int32, (N_EXPERTS, N_EXPERTS), 0)
    ec = lax.broadcasted_iota(jnp.int32, (N_EXPERTS, N_EXPERTS), 1)
    before = jnp.where(ec < er, 1.0, 0.0).astype(BF16)
    run_start = jnp.dot(before, run_len_b.astype(BF16), preferred_element_type=F32)[:, 0:1]
    incl = jnp.dot(sel.astype(BF16), tri_ref[...], preferred_element_type=F32)
    pos_mat = run_start + incl - sel
    cnt_ref[0] = run_len_b.astype(jnp.int32)

    for r in range(TOP_K):
        pick, _ = picks[r]
        pos_ref[0, r:r + 1, :] = jnp.sum(jnp.where(pick, pos_mat, 0.0), axis=0,
                                         keepdims=True).astype(jnp.int32)
        gate_ref[0, r:r + 1, :] = weights[r] / wsum * ROUTED_SCALE


def _route(logits_t, router_b, tri):
    e, t = logits_t.shape
    nt = t // ROUTE_TILE
    blk = lambda i: (i, 0, 0)
    return pl.pallas_call(
        _route_kernel,
        grid=(nt,),
        in_specs=[
            pl.BlockSpec((e, ROUTE_TILE), lambda i: (0, i)),
            pl.BlockSpec((e, 1), lambda i: (0, 0)),
            pl.BlockSpec((ROUTE_TILE, ROUTE_TILE), lambda i: (0, 0)),
        ],
        out_specs=[
            pl.BlockSpec((1, TOP_K, ROUTE_TILE), blk),
            pl.BlockSpec((1, TOP_K, ROUTE_TILE), blk),
            pl.BlockSpec((1, e, 128), blk),
        ],
        out_shape=[
            jax.ShapeDtypeStruct((nt, TOP_K, ROUTE_TILE), jnp.int32),
            jax.ShapeDtypeStruct((nt, TOP_K, ROUTE_TILE), F32),
            jax.ShapeDtypeStruct((nt, e, 128), jnp.int32),
        ],
        compiler_params=_cparams(("arbitrary",)),
        name="route",
    )(logits_t, router_b, tri)


def _for_each_run_piece(cnt_ref, tile, fn):
    def body(e, carry):
        run = tile * N_EXPERTS + e
        n = cnt_ref[run]
        for piece in RUN_PIECES:
            @pl.when((n & piece) != 0)
            def _():
                fn(run, n & (-2 * piece), piece)
        return carry

    lax.fori_loop(0, N_EXPERTS, body, 0)


def _pack_bf16_pair(lo_f32, hi_f32):
    lo = (lax.bitcast_convert_type(lo_f32, jnp.int32) >> 16) & LOW_HALF
    hi = lax.bitcast_convert_type(hi_f32, jnp.int32) & HIGH_HALF
    return lo | hi


def _unpack_bf16_pair(packed):
    lo = lax.bitcast_convert_type(packed << 16, F32).astype(BF16)
    hi = lax.bitcast_convert_type(packed & HIGH_HALF, F32).astype(BF16)
    return lo, hi


def _dispatch_kernel(gbase_ref, lbase_ref, cnt_ref, rows_ref, total_ref, pos_ref, x_ref, xs_ref,
                     stage, zeros, sem, zsem):
    i = pl.program_id(0)
    nt = pl.num_programs(0)
    slot = i % 2
    tn = ROUTE_TILE
    half = D_MODEL // 2

    def run_copy(tile, buf):
        def build(run, before, n):
            src = pl.multiple_of(lbase_ref[run] + before, RUN_ALIGN)
            dst = pl.multiple_of(gbase_ref[run] + before, RUN_ALIGN)
            return pltpu.make_async_copy(stage.at[buf, pl.ds(src, n), :], xs_ref.at[pl.ds(dst, n), :],
                                         sem.at[buf])
        return build

    @pl.when(i >= 2)
    def _():
        build = run_copy(i - 2, slot)
        _for_each_run_piece(cnt_ref, i - 2, lambda run, before, n: build(run, before, n).wait())

    xb = x_ref[...].astype(BF16)
    pos = pos_ref[0]
    row_iota = lax.broadcasted_iota(jnp.int32, (SORT_CHUNK, tn), 0)

    def chunk(c, carry):
        p0 = pl.multiple_of(c * SORT_CHUNK, SORT_CHUNK)
        target = row_iota + p0
        hit = pos[0:1, :] == target
        for k in range(1, TOP_K):
            hit = hit | (pos[k:k + 1, :] == target)
        onehot = jnp.where(hit, 1.0, 0.0).astype(BF16)
        rows = jnp.dot(onehot, xb, preferred_element_type=F32)
        stage[slot, pl.ds(p0, SORT_CHUNK), :] = _pack_bf16_pair(rows[:, :half], rows[:, half:])
        return carry

    lax.fori_loop(0, (rows_ref[i] + SORT_CHUNK - 1) // SORT_CHUNK, chunk, 0)

    build_now = run_copy(i, slot)
    _for_each_run_piece(cnt_ref, i, lambda run, before, n: build_now(run, before, n).start())

    @pl.when(i == nt - 1)
    def _():
        @pl.when(nt >= 2)
        def _():
            build = run_copy(i - 1, 1 - slot)
            _for_each_run_piece(cnt_ref, i - 1, lambda run, before, n: build(run, before, n).wait())

        _for_each_run_piece(cnt_ref, i, lambda run, before, n: build_now(run, before, n).wait())

        zeros[...] = jnp.zeros_like(zeros)
        total = total_ref[0]
        tail = (-total) & (EXPERT_ROWS - 1)
        for piece in RUN_PIECES:
            if piece >= EXPERT_ROWS:
                continue

            @pl.when((tail & piece) != 0)
            def _():
                dst = pl.multiple_of(total + (tail & (-2 * piece)), RUN_ALIGN)
                cp = pltpu.make_async_copy(zeros.at[pl.ds(0, piece), :], xs_ref.at[pl.ds(dst, piece), :], zsem)
                cp.start()
                cp.wait()

        first_free = (total + tail) // EXPERT_ROWS

        def zero_block(b):
            dst = pl.multiple_of(b * EXPERT_ROWS, EXPERT_ROWS)
            return pltpu.make_async_copy(zeros, xs_ref.at[pl.ds(dst, EXPERT_ROWS), :], zsem)

        n_blocks = xs_ref.shape[0] // EXPERT_ROWS
        lax.fori_loop(first_free, n_blocks, lambda b, c: (zero_block(b).start(), c)[1], 0)
        lax.fori_loop(first_free, n_blocks, lambda b, c: (zero_block(b).wait(), c)[1], 0)


def _dispatch(gbase, lbase, cnt, tile_rows, total, pos, x2d, n_rows):
    t, d = x2d.shape
    nt = t // ROUTE_TILE
    return pl.pallas_call(
        _dispatch_kernel,
        grid_spec=pltpu.PrefetchScalarGridSpec(
            num_scalar_prefetch=5,
            grid=(nt,),
            in_specs=[pl.BlockSpec((1, TOP_K, ROUTE_TILE), lambda i, *_: (i, 0, 0)),
                      pl.BlockSpec((ROUTE_TILE, d), lambda i, *_: (i, 0))],
            out_specs=pl.BlockSpec(memory_space=pl.ANY),
            scratch_shapes=[pltpu.VMEM((2, TILE_CAP, d // 2), jnp.int32),
                            pltpu.VMEM((EXPERT_ROWS, d // 2), jnp.int32),
                            pltpu.SemaphoreType.DMA((2,)),
                            pltpu.SemaphoreType.DMA(())],
        ),
        out_shape=jax.ShapeDtypeStruct((n_rows, d // 2), jnp.int32),
        compiler_params=_cparams(("arbitrary",)),
        name="dispatch",
    )(gbase, lbase, cnt, tile_rows, total, pos, x2d)


def _experts_kernel(ib_ref, ie_ref, flag_ref, seg_ref, xs_ref, wg_ref, wu_ref, wd_ref, y_ref,
                    wg_bf, wu_bf, wd_bf):
    w = pl.program_id(0)
    flags = flag_ref[w]
    valid = (flags & 1) != 0
    first_of_block = (flags & 2) != 0
    new_expert = (flags & 4) != 0

    @pl.when((flags & 8) != 0)
    def _():
        y_ref[...] = jnp.zeros_like(y_ref)

    @pl.when(new_expert)
    def _():
        wg_bf[...] = wg_ref[0].astype(BF16)
        wu_bf[...] = wu_ref[0].astype(BF16)
        wd_bf[...] = wd_ref[0].astype(BF16)

    @pl.when(valid)
    def _():
        e = ie_ref[w]
        half = D_MODEL // 2
        x_lo, x_hi = _unpack_bf16_pair(xs_ref[...])
        gte = (jnp.dot(x_lo, wg_bf[0:half, :], preferred_element_type=F32)
               + jnp.dot(x_hi, wg_bf[half:, :], preferred_element_type=F32))
        up = (jnp.dot(x_lo, wu_bf[0:half, :], preferred_element_type=F32)
              + jnp.dot(x_hi, wu_bf[half:, :], preferred_element_type=F32))
        h = (gte * _sigmoid(gte) * up).astype(BF16)
        y = jnp.dot(h, wd_bf[...], preferred_element_type=F32)
        y_bf = y.astype(BF16).astype(F32)
        packed = _pack_bf16_pair(y_bf[:, :half], y_bf[:, half:])
        rows = ib_ref[w] * EXPERT_ROWS + lax.broadcasted_iota(jnp.int32, (EXPERT_ROWS, 1), 0)
        mine = (rows >= seg_ref[e]) & (rows < seg_ref[e + 1])

        @pl.when(first_of_block)
        def _():
            y_ref[...] = jnp.where(mine, packed, 0)

        @pl.when(jnp.logical_not(first_of_block))
        def _():
            y_ref[...] = jnp.where(mine, packed, y_ref[...])


def _experts(item_block, item_expert, item_flags, seg_bounds, xs, wg, wu, wd):
    n, dh = xs.shape
    n_items = item_block.shape[0]
    _, d, ff = wg.shape
    return pl.pallas_call(
        _experts_kernel,
        grid_spec=pltpu.PrefetchScalarGridSpec(
            num_scalar_prefetch=4,
            grid=(n_items,),
            in_specs=[
                pl.BlockSpec((EXPERT_ROWS, dh), lambda w, ib, ie, fl, sg: (ib[w], 0)),
                pl.BlockSpec((1, d, ff), lambda w, ib, ie, fl, sg: (ie[w], 0, 0)),
                pl.BlockSpec((1, d, ff), lambda w, ib, ie, fl, sg: (ie[w], 0, 0)),
                pl.BlockSpec((1, ff, d), lambda w, ib, ie, fl, sg: (ie[w], 0, 0)),
            ],
            out_specs=pl.BlockSpec((EXPERT_ROWS, dh), lambda w, ib, ie, fl, sg: (ib[w], 0)),
            scratch_shapes=[pltpu.VMEM((d, ff), BF16), pltpu.VMEM((d, ff), BF16), pltpu.VMEM((ff, d), BF16)],
        ),
        out_shape=jax.ShapeDtypeStruct((n, dh), jnp.int32),
        compiler_params=_cparams(("arbitrary",)),
        name="experts",
    )(item_block, item_expert, item_flags, seg_bounds, xs, wg, wu, wd)


def _work_items(counts, n_rows):
    n_items = n_rows // EXPERT_ROWS + N_EXPERTS
    seg_end = jnp.cumsum(counts)
    seg_start = seg_end - counts
    first_blk = seg_start // EXPERT_ROWS
    n_blk = jnp.where(counts > 0, (seg_end - 1) // EXPERT_ROWS - first_blk + 1, 0)
    item_end = jnp.cumsum(n_blk)
    item_start = item_end - n_blk
    total = item_end[-1]
    w = jnp.arange(n_items, dtype=jnp.int32)
    wc = jnp.minimum(w, total - 1)
    e = jnp.minimum(jnp.sum((item_end[None, :] <= wc[:, None]).astype(jnp.int32), axis=1), N_EXPERTS - 1)
    onehot = (e[:, None] == jnp.arange(N_EXPERTS, dtype=jnp.int32)[None, :]).astype(jnp.int32)
    b = jnp.sum(onehot * (first_blk - item_start)[None, :], axis=1) + wc
    valid = w < total
    n_blocks = n_rows // EXPERT_ROWS
    free_blk = (seg_end[-1] + EXPERT_ROWS - 1) // EXPERT_ROWS + (w - total)
    fill = jnp.logical_not(valid) & (free_blk < n_blocks)
    b = jnp.where(valid, b, jnp.minimum(free_blk, n_blocks - 1)).astype(jnp.int32)
    prev_b = jnp.concatenate([jnp.full((1,), -1, jnp.int32), b[:-1]])
    prev_e = jnp.concatenate([jnp.full((1,), -1, jnp.int32), e[:-1]])
    flags = (valid.astype(jnp.int32) + 2 * (valid & (b != prev_b)).astype(jnp.int32)
             + 4 * (valid & (e != prev_e)).astype(jnp.int32) + 8 * fill.astype(jnp.int32))
    seg_bounds = jnp.concatenate([seg_start, seg_end[-1:]]).astype(jnp.int32)
    return b, e, flags, seg_bounds


def _combine_kernel(gbase_ref, lbase_ref, cnt_ref, rows_ref, pos_ref, gate_ref, x_ref,
                    sg_ref, su_ref, sd_ref, g_ref, beta_ref, yb_ref, o_ref, stage, acc, pos_b, gate_b, sem):
    i = pl.program_id(0)
    nt = pl.num_programs(0)
    slot = i % 2
    tn = ROUTE_TILE
    half = D_MODEL // 2

    def run_copy(buf):
        def build(run, before, n):
            src = pl.multiple_of(gbase_ref[run] + before, RUN_ALIGN)
            dst = pl.multiple_of(lbase_ref[run] + before, RUN_ALIGN)
            return pltpu.make_async_copy(yb_ref.at[pl.ds(src, n), :], stage.at[buf, pl.ds(dst, n), :],
                                         sem.at[buf])
        return build

    def start_tile(tile, buf):
        build = run_copy(buf)
        _for_each_run_piece(cnt_ref, tile, lambda run, before, n: build(run, before, n).start())

    @pl.when(i == 0)
    def _():
        stage[...] = jnp.zeros_like(stage)
        start_tile(0, 0)

    @pl.when(i + 1 < nt)
    def _():
        start_tile(i + 1, 1 - slot)

    x = x_ref[...]
    xb = x.astype(BF16)
    gte = jnp.dot(xb, sg_ref[...], preferred_element_type=F32)
    up = jnp.dot(xb, su_ref[...], preferred_element_type=F32)
    h = (gte * _sigmoid(gte) * up).astype(BF16)
    acc[...] = jnp.dot(h, sd_ref[...], preferred_element_type=F32)

    build_now = run_copy(slot)
    _for_each_run_piece(cnt_ref, i, lambda run, before, n: build_now(run, before, n).wait())

    lanes = 128
    for k in range(TOP_K):
        pos_b[k] = jnp.broadcast_to(pos_ref[:, k:k + 1], (tn, lanes))
        gate_b[k] = jnp.broadcast_to(gate_ref[:, k:k + 1], (tn, lanes))
    lane_iota = lax.broadcasted_iota(jnp.int32, (tn, lanes), 1)

    def chunk(c, carry):
        p0 = pl.multiple_of(c * SORT_CHUNK, SORT_CHUNK)
        cols = []
        for j in range(SORT_CHUNK // lanes):
            target = lane_iota + (p0 + j * lanes)
            w = jnp.zeros((tn, lanes), F32)
            for k in range(TOP_K):
                w = jnp.where(pos_b[k] == target, gate_b[k], w)
            cols.append(w.astype(BF16))
        wts = jnp.concatenate(cols, axis=1)
        y_lo, y_hi = _unpack_bf16_pair(stage[slot, pl.ds(p0, SORT_CHUNK), :])
        acc[:, 0:half] += jnp.dot(wts, y_lo, preferred_element_type=F32)
        acc[:, half:] += jnp.dot(wts, y_hi, preferred_element_type=F32)
        return carry

    lax.fori_loop(0, (rows_ref[i] + SORT_CHUNK - 1) // SORT_CHUNK, chunk, 0)
    o_ref[...] = _layer_norm(ALPHA * x + acc[...], g_ref[...], beta_ref[...])


def _combine(gbase, lbase, cnt, tile_rows, pos_t, gate_t, x2d, sg, su, sd, g, beta, yb):
    t, d = x2d.shape
    nt = t // ROUTE_TILE
    ff = sg.shape[-1]
    row = lambda i, *_: (i, 0)
    const = lambda i, *_: (0, 0)
    return pl.pallas_call(
        _combine_kernel,
        grid_spec=pltpu.PrefetchScalarGridSpec(
            num_scalar_prefetch=4,
            grid=(nt,),
            in_specs=[
                pl.BlockSpec((ROUTE_TILE, TOP_K), row),
                pl.BlockSpec((ROUTE_TILE, TOP_K), row),
                pl.BlockSpec((ROUTE_TILE, d), row),
                pl.BlockSpec((d, ff), const),
                pl.BlockSpec((d, ff), const),
                pl.BlockSpec((ff, d), const),
                pl.BlockSpec((1, d), const),
                pl.BlockSpec((1, d), const),
                pl.BlockSpec(memory_space=pl.ANY),
            ],
            out_specs=pl.BlockSpec((ROUTE_TILE, d), row),
            scratch_shapes=[pltpu.VMEM((2, TILE_CAP, d // 2), jnp.int32),
                            pltpu.VMEM((ROUTE_TILE, d), F32),
                            pltpu.VMEM((TOP_K, ROUTE_TILE, 128), jnp.int32),
                            pltpu.VMEM((TOP_K, ROUTE_TILE, 128), F32),
                            pltpu.SemaphoreType.DMA((2,))],
        ),
        out_shape=jax.ShapeDtypeStruct((t, d), F32),
        compiler_params=_cparams(("arbitrary",)),
        name="combine",
    )(gbase, lbase, cnt, tile_rows, pos_t, gate_t, x2d, sg, su, sd, g, beta, yb)


def _run_tables(cnt):
    lbase = jnp.cumsum(cnt, axis=1) - cnt
    per_expert = jnp.sum(cnt, axis=0)
    seg_start = jnp.cumsum(per_expert) - per_expert
    gbase = seg_start[None, :] + jnp.cumsum(cnt, axis=0) - cnt
    flat = lambda a: a.reshape(-1).astype(jnp.int32)
    return flat(gbase), flat(lbase), flat(cnt), jnp.sum(cnt, axis=1).astype(jnp.int32), per_expert


def kernel(x, mem, w_in, b_in, conv_w, conv_b, conv_ln_g, conv_ln_b, attn_sinks, rel_bias, w_out, b_out, ln1_g, ln1_b, xq_w, xkv_w, xo_w, ln2_g, ln2_b, router_w, router_b, exp_gate, exp_up, exp_down, sh_gate, sh_up, sh_down, ln3_g, ln3_b):
    bsz, seq, d = x.shape
    t = bsz * seq
    bias_tab, sink_tab = _band_tables(rel_bias, attn_sinks[0])
    tri = (jnp.arange(ROUTE_TILE)[:, None] <= jnp.arange(ROUTE_TILE)[None, :]).astype(BF16)
    row = lambda p: p.reshape(1, -1)
    for l in range(DEPTH):
        conv_out, q, k, v = _mix(x, w_in[l].astype(BF16), row(b_in[l]), conv_w[l], row(conv_b[l]),
                                 row(conv_ln_g[l]), row(conv_ln_b[l]))
        att = _swa(q, k, v, bias_tab, sink_tab)
        x1 = _outproj(x.reshape(t, d), conv_out.reshape(t, CONV_CH), att.reshape(t, Q_COLS),
                      w_out[l].astype(BF16), row(b_out[l]), row(ln1_g[l]), row(ln1_b[l]))
        kmem, vmem = _memkv(mem, xkv_w[l].astype(BF16))
        x2, logits_t = _cross(x1.reshape(bsz, seq, d), kmem, vmem, xq_w[l].astype(BF16),
                              xo_w[l].astype(BF16), row(ln2_g[l]), row(ln2_b[l]),
                              router_w[l].T.astype(BF16))
        x2 = x2.reshape(t, d)
        pos, gate, cnt = _route(logits_t, router_b[l].reshape(-1, 1), tri)
        nt = t // ROUTE_TILE
        n_rows = -(-(t * TOP_K + nt * N_EXPERTS * (RUN_ALIGN - 1)) // EXPERT_ROWS) * EXPERT_ROWS
        gbase, lbase, cnt_flat, tile_rows, per_expert = _run_tables(cnt[:, :, 0])
        item_block, item_expert, item_flags, seg_bounds = _work_items(per_expert, n_rows)
        xs = _dispatch(gbase, lbase, cnt_flat, tile_rows, seg_bounds[-1:], pos, x2, n_rows)
        yb = _experts(item_block, item_expert, item_flags, seg_bounds, xs,
                      exp_gate[l], exp_up[l], exp_down[l])
        pos_t = jnp.transpose(pos, (0, 2, 1)).reshape(t, TOP_K)
        gate_t = jnp.transpose(gate, (0, 2, 1)).reshape(t, TOP_K)
        x = _combine(gbase, lbase, cnt_flat, tile_rows, pos_t, gate_t, x2, sh_gate[l].astype(BF16),
                     sh_up[l].astype(BF16), sh_down[l].astype(BF16), row(ln3_g[l]), row(ln3_b[l]),
                     yb).reshape(bsz, seq, d)
    return x
```

```python
import functools
import math

import jax
import jax.numpy as jnp
from jax import lax
from jax.experimental import pallas as pl
from jax.experimental.pallas import tpu as pltpu

D_MODEL = 1024
MEM_LEN = 256
HEAD_DIM = 64
CONV_CH = D_MODEL // 2
CONV_WIDTH = 31
ATT_HEADS = 8
KV_HEADS = 2
WINDOW = 128
BLOCK = 128
REL_BUCKETS = 32
REL_MAX_DIST = 128
Q_COLS = ATT_HEADS * HEAD_DIM
KV_COLS = KV_HEADS * HEAD_DIM
IN_COLS = 2 * CONV_CH + Q_COLS + 2 * KV_COLS
X_HEADS = 4
X_HEAD_DIM = D_MODEL // X_HEADS
N_EXPERTS = 64
TOP_K = 8
N_GROUPS = 8
GROUP_SIZE = N_EXPERTS // N_GROUPS
TOPK_GROUPS = 4
EXPERT_FF = D_MODEL // 4
ROUTED_SCALE = 2.5
DEPTH = 1
ALPHA = (2 * DEPTH) ** 0.25
LN_EPS = 1e-5
NEG_INF = -1e30

F32 = jnp.float32
BF16 = jnp.bfloat16

VMEM_LIMIT_BYTES = 56 * 1024 * 1024

ROW_TILE = 512
CONV_ROWS = 32
SUBLANES = 8
CONV_HALO = 32
SWA_TILE = 512
ROUTE_TILE = 512
EXPERT_ROWS = 512
RUN_ALIGN = 8
RUN_PIECES = tuple(1 << b for b in range(9, 2, -1))
TILE_CAP = ROUTE_TILE * TOP_K + N_EXPERTS * RUN_ALIGN
SORT_CHUNK = 256
COMBINE_CHUNK = 512
LOW_HALF = 0xFFFF
HIGH_HALF = -0x10000


def _cparams(sem):
    return pltpu.CompilerParams(dimension_semantics=sem, vmem_limit_bytes=VMEM_LIMIT_BYTES)


def _layer_norm(h, g, b):
    mu = jnp.mean(h, axis=-1, keepdims=True)
    d = h - mu
    var = jnp.mean(d * d, axis=-1, keepdims=True)
    return d * lax.rsqrt(var + LN_EPS) * g + b


def _sigmoid(x):
    return 1.0 / (1.0 + jnp.exp(-x))


def _mix_kernel(x_ref, w_ref, b_ref, cw_ref, cb_ref, cg_ref, cbeta_ref,
                conv_ref, q_ref, k_ref, v_ref, u_ext, u_sh):
    j = pl.program_id(1)
    xb = x_ref[0].astype(BF16)
    proj = jnp.dot(xb, w_ref[...], preferred_element_type=F32) + b_ref[...]
    a = proj[:, :CONV_CH]
    g = proj[:, CONV_CH:2 * CONV_CH]
    q_ref[0] = (proj[:, 2 * CONV_CH:2 * CONV_CH + Q_COLS] * (HEAD_DIM ** -0.5)).astype(BF16)
    k_ref[0] = proj[:, 2 * CONV_CH + Q_COLS:2 * CONV_CH + Q_COLS + KV_COLS].astype(BF16)
    v_ref[0] = proj[:, 2 * CONV_CH + Q_COLS + KV_COLS:].astype(BF16)

    @pl.when(j == 0)
    def _():
        u_ext[0:CONV_HALO, :] = jnp.zeros((CONV_HALO, CONV_CH), F32)

    u_ext[CONV_HALO:CONV_HALO + ROW_TILE, :] = a * _sigmoid(g)

    first_tap = CONV_HALO - (CONV_WIDTH - 1)
    shifted_rows = u_sh.shape[1]
    for r in range(1, SUBLANES):
        u_sh[r - 1] = u_ext[r:r + shifted_rows, :]

    for c in range(ROW_TILE // CONV_ROWS):
        acc = jnp.zeros((CONV_ROWS, CONV_CH), F32) + cb_ref[...]
        for t in range(CONV_WIDTH):
            r = (first_tap + t) % SUBLANES
            base = c * CONV_ROWS + (first_tap + t) - r
            if r == 0:
                taps = u_ext[base:base + CONV_ROWS, :]
            else:
                taps = u_sh[r - 1, base:base + CONV_ROWS, :]
            acc = acc + taps * cw_ref[t:t + 1, :]
        y = _layer_norm(acc, cg_ref[...], cbeta_ref[...])
        conv_ref[0, c * CONV_ROWS:(c + 1) * CONV_ROWS, :] = (y * _sigmoid(y)).astype(BF16)

    u_ext[0:CONV_HALO, :] = u_ext[ROW_TILE:ROW_TILE + CONV_HALO, :]


def _mix(x, w_in, b_in, conv_w, conv_b, conv_g, conv_beta):
    bsz, seq, d = x.shape
    nt = seq // ROW_TILE
    row = lambda b, j: (b, j, 0)
    const2 = lambda b, j: (0, 0)
    return pl.pallas_call(
        _mix_kernel,
        grid=(bsz, nt),
        in_specs=[
            pl.BlockSpec((1, ROW_TILE, d), row),
            pl.BlockSpec((d, IN_COLS), const2),
            pl.BlockSpec((1, IN_COLS), const2),
            pl.BlockSpec((CONV_WIDTH, CONV_CH), const2),
            pl.BlockSpec((1, CONV_CH), const2),
            pl.BlockSpec((1, CONV_CH), const2),
            pl.BlockSpec((1, CONV_CH), const2),
        ],
        out_specs=[
            pl.BlockSpec((1, ROW_TILE, CONV_CH), row),
            pl.BlockSpec((1, ROW_TILE, Q_COLS), row),
            pl.BlockSpec((1, ROW_TILE, KV_COLS), row),
            pl.BlockSpec((1, ROW_TILE, KV_COLS), row),
        ],
        out_shape=[
            jax.ShapeDtypeStruct((bsz, seq, CONV_CH), BF16),
            jax.ShapeDtypeStruct((bsz, seq, Q_COLS), BF16),
            jax.ShapeDtypeStruct((bsz, seq, KV_COLS), BF16),
            jax.ShapeDtypeStruct((bsz, seq, KV_COLS), BF16),
        ],
        scratch_shapes=[pltpu.VMEM((ROW_TILE + CONV_HALO, CONV_CH), F32),
                        pltpu.VMEM((SUBLANES - 1, ROW_TILE + CONV_HALO - SUBLANES, CONV_CH), F32)],
        compiler_params=_cparams(("arbitrary", "arbitrary")),
        name="mix",
    )(x, w_in, b_in, conv_w, conv_b, conv_g, conv_beta)


def _swa_kernel(q_ref, kp_ref, kc_ref, vp_ref, vc_ref, bias_ref, sink_ref, o_ref):
    j = pl.program_id(1)
    rows = BLOCK + SWA_TILE
    lane = lax.broadcasted_iota(jnp.int32, (rows, 2 * HEAD_DIM), 1)
    low = lane < HEAD_DIM

    def placements(prev_ref, cur_ref):
        t = jnp.concatenate([prev_ref[0], cur_ref[0]], axis=0).astype(F32)
        tr = pltpu.roll(t, HEAD_DIM, 1)
        zero = jnp.zeros_like(t)
        kv0_low = jnp.where(low, t, zero).astype(BF16)
        kv1_high = jnp.where(low, zero, t).astype(BF16)
        kv1_low = jnp.where(low, tr, zero).astype(BF16)
        kv0_high = jnp.where(low, zero, tr).astype(BF16)
        return (kv0_low, kv0_high, kv1_low, kv1_high)

    ks = placements(kp_ref, kc_ref)
    vs = placements(vp_ref, vc_ref)
    slab = 2 * HEAD_DIM
    for i in range(SWA_TILE // BLOCK):
        q = q_ref[0, i * BLOCK:(i + 1) * BLOCK, :]
        q_kv0 = jnp.concatenate([q[:, 0:slab], q[:, slab:2 * slab]], axis=0)
        q_kv1 = jnp.concatenate([q[:, 2 * slab:3 * slab], q[:, 3 * slab:4 * slab]], axis=0)
        band = slice(i * BLOCK, (i + 2) * BLOCK)
        outs = []
        for s in range(4):
            qs = q_kv0 if s < 2 else q_kv1
            bias = bias_ref[1, s]
            if i == 0:
                bias = jnp.where(j == 0, bias_ref[0, s], bias)
            logits = lax.dot_general(qs, ks[s][band], (((1,), (1,)), ((), ())),
                                     preferred_element_type=F32) + bias
            sink = sink_ref[s]
            m = jnp.maximum(jnp.max(logits, axis=-1, keepdims=True), sink)
            p = jnp.exp(logits - m)
            den = jnp.sum(p, axis=-1, keepdims=True) + jnp.exp(sink - m)
            o = jnp.dot(p.astype(BF16), vs[s][band], preferred_element_type=F32)
            outs.append(o / den)
        o_kv0 = outs[0] + outs[1]
        o_kv1 = outs[2] + outs[3]
        blk = slice(i * BLOCK, (i + 1) * BLOCK)
        o_ref[0, blk, 0:slab] = o_kv0[0:BLOCK].astype(BF16)
        o_ref[0, blk, slab:2 * slab] = o_kv0[BLOCK:2 * BLOCK].astype(BF16)
        o_ref[0, blk, 2 * slab:3 * slab] = o_kv1[0:BLOCK].astype(BF16)
        o_ref[0, blk, 3 * slab:4 * slab] = o_kv1[BLOCK:2 * BLOCK].astype(BF16)


def _swa(q, k, v, bias_tab, sink_tab):
    bsz, seq, _ = q.shape
    per = SWA_TILE // BLOCK
    cur = lambda b, n: (b, n, 0)
    prev = lambda b, n: (b, jnp.maximum(n * per - 1, 0), 0)
    whole = lambda b, n: (0, 0, 0, 0)
    return pl.pallas_call(
        _swa_kernel,
        grid=(bsz, seq // SWA_TILE),
        in_specs=[
            pl.BlockSpec((1, SWA_TILE, Q_COLS), cur),
            pl.BlockSpec((1, BLOCK, KV_COLS), prev),
            pl.BlockSpec((1, SWA_TILE, KV_COLS), cur),
            pl.BlockSpec((1, BLOCK, KV_COLS), prev),
            pl.BlockSpec((1, SWA_TILE, KV_COLS), cur),
            pl.BlockSpec((2, 4, 2 * BLOCK, 2 * BLOCK), whole),
            pl.BlockSpec((4, 2 * BLOCK, 1), lambda b, n: (0, 0, 0)),
        ],
        out_specs=pl.BlockSpec((1, SWA_TILE, Q_COLS), cur),
        out_shape=jax.ShapeDtypeStruct((bsz, seq, Q_COLS), BF16),
        compiler_params=_cparams(("arbitrary", "arbitrary")),
        name="swa",
    )(q, k, k, v, v, bias_tab, sink_tab)


def _t5_bucket(dist):
    n = jnp.maximum(dist, 0)
    exact = REL_BUCKETS // 2
    large = exact + (jnp.log(jnp.maximum(n, 1).astype(F32) / exact)
                     / math.log(REL_MAX_DIST / exact) * (REL_BUCKETS - exact)).astype(jnp.int32)
    large = jnp.minimum(large, REL_BUCKETS - 1)
    return jnp.where(n < exact, n, large)


def _band_tables(rel_bias, sinks):
    qi = jnp.arange(BLOCK)[:, None]
    kj = jnp.arange(2 * BLOCK)[None, :]
    dist = qi + BLOCK - kj
    bucket = _t5_bucket(dist)
    bias = jnp.zeros((ATT_HEADS, BLOCK, 2 * BLOCK), F32)
    for bkt in range(REL_BUCKETS):
        bias = jnp.where(bucket[None] == bkt, rel_bias[bkt].astype(F32)[:, None, None], bias)
    in_window = (dist >= 0) & (dist < WINDOW)
    masks = jnp.stack([in_window & (kj >= BLOCK), in_window])
    masked = jnp.where(masks[:, None], bias[None], NEG_INF)
    pairs = ((0, 2), (1, 3), (4, 6), (5, 7))
    bias_tab = jnp.stack([jnp.concatenate([masked[:, a], masked[:, b]], axis=1) for a, b in pairs], axis=1)
    s = sinks.astype(F32)
    sink_tab = jnp.stack([jnp.concatenate([jnp.full((BLOCK, 1), s[a]), jnp.full((BLOCK, 1), s[b])], axis=0)
                          for a, b in pairs])
    return bias_tab, sink_tab


def _outproj_kernel(x_ref, conv_ref, att_ref, w_ref, b_ref, g_ref, beta_ref, o_ref):
    mix = jnp.dot(conv_ref[...], w_ref[0:CONV_CH, :], preferred_element_type=F32)
    mix = mix + jnp.dot(att_ref[...], w_ref[CONV_CH:, :], preferred_element_type=F32)
    h = ALPHA * x_ref[...] + mix + b_ref[...]
    o_ref[...] = _layer_norm(h, g_ref[...], beta_ref[...])


def _outproj(x2d, conv2d, att2d, w_out, b_out, g, beta):
    t, d = x2d.shape
    row = lambda i: (i, 0)
    const = lambda i: (0, 0)
    return pl.pallas_call(
        _outproj_kernel,
        grid=(t // ROW_TILE,),
        in_specs=[
            pl.BlockSpec((ROW_TILE, d), row),
            pl.BlockSpec((ROW_TILE, CONV_CH), row),
            pl.BlockSpec((ROW_TILE, Q_COLS), row),
            pl.BlockSpec((d, d), const),
            pl.BlockSpec((1, d), const),
            pl.BlockSpec((1, d), const),
            pl.BlockSpec((1, d), const),
        ],
        out_specs=pl.BlockSpec((ROW_TILE, d), row),
        out_shape=jax.ShapeDtypeStruct((t, d), F32),
        compiler_params=_cparams(("arbitrary",)),
        name="outproj",
    )(x2d, conv2d, att2d, w_out, b_out, g, beta)


def _memkv_kernel(mem_ref, w_ref, k_ref, v_ref):
    kv = jnp.dot(mem_ref[0].astype(BF16), w_ref[...], preferred_element_type=F32)
    k_ref[0] = kv[:, :D_MODEL].astype(BF16)
    v_ref[0] = kv[:, D_MODEL:].astype(BF16)


def _memkv(mem, wkv):
    bsz, m, d = mem.shape
    return pl.pallas_call(
        _memkv_kernel,
        grid=(bsz,),
        in_specs=[pl.BlockSpec((1, m, d), lambda b: (b, 0, 0)),
                  pl.BlockSpec((d, 2 * d), lambda b: (0, 0))],
        out_specs=[pl.BlockSpec((1, m, d), lambda b: (b, 0, 0)),
                   pl.BlockSpec((1, m, d), lambda b: (b, 0, 0))],
        out_shape=[jax.ShapeDtypeStruct((bsz, m, d), BF16)] * 2,
        compiler_params=_cparams(("arbitrary",)),
        name="memkv",
    )(mem, wkv)


def _cross_kernel(x_ref, k_ref, v_ref, wq_ref, wo_ref, g_ref, beta_ref, rw_ref, o_ref, lt_ref):
    x = x_ref[0]
    q = jnp.dot(x.astype(BF16), wq_ref[...], preferred_element_type=F32) * (X_HEAD_DIM ** -0.5)
    q = q.astype(BF16)
    heads = []
    for h in range(X_HEADS):
        cols = slice(h * X_HEAD_DIM, (h + 1) * X_HEAD_DIM)
        logits = lax.dot_general(q[:, cols], k_ref[0, :, cols], (((1,), (1,)), ((), ())),
                                 preferred_element_type=F32)
        m = jnp.max(logits, axis=-1, keepdims=True)
        p = jnp.exp(logits - m)
        den = jnp.sum(p, axis=-1, keepdims=True)
        o = jnp.dot(p.astype(BF16), v_ref[0, :, cols], preferred_element_type=F32)
        heads.append((o / den).astype(BF16))
    att = jnp.concatenate(heads, axis=-1)
    cross = jnp.dot(att, wo_ref[...], preferred_element_type=F32)
    y = _layer_norm(ALPHA * x + cross, g_ref[...], beta_ref[...])
    o_ref[0] = y
    lt_ref[...] = lax.dot_general(rw_ref[...], y.astype(BF16), (((1,), (1,)), ((), ())),
                                  preferred_element_type=F32)


def _cross(x1, kmem, vmem, wq, wo, g, beta, rw_t):
    bsz, seq, d = x1.shape
    nt = seq // ROW_TILE
    row = lambda b, j: (b, j, 0)
    mem = lambda b, j: (b, 0, 0)
    const = lambda b, j: (0, 0)
    return pl.pallas_call(
        _cross_kernel,
        grid=(bsz, nt),
        in_specs=[
            pl.BlockSpec((1, ROW_TILE, d), row),
            pl.BlockSpec((1, MEM_LEN, d), mem),
            pl.BlockSpec((1, MEM_LEN, d), mem),
            pl.BlockSpec((d, d), const),
            pl.BlockSpec((d, d), const),
            pl.BlockSpec((1, d), const),
            pl.BlockSpec((1, d), const),
            pl.BlockSpec((N_EXPERTS, d), const),
        ],
        out_specs=[
            pl.BlockSpec((1, ROW_TILE, d), row),
            pl.BlockSpec((N_EXPERTS, ROW_TILE), lambda b, j: (0, b * nt + j)),
        ],
        out_shape=[
            jax.ShapeDtypeStruct((bsz, seq, d), F32),
            jax.ShapeDtypeStruct((N_EXPERTS, bsz * seq), F32),
        ],
        compiler_params=_cparams(("arbitrary", "arbitrary")),
        name="cross",
    )(x1, kmem, vmem, wq, wo, g, beta, rw_t)


def _route_kernel(lt_ref, rb_ref, tri_ref, pos_ref, gate_ref, cnt_ref):
    tn = ROUTE_TILE
    scores = _sigmoid(lt_ref[...])
    choice = scores + rb_ref[...]

    gscore = []
    member = lax.broadcasted_iota(jnp.int32, (GROUP_SIZE, tn), 0).astype(F32)
    for g in range(N_GROUPS):
        c = choice[g * GROUP_SIZE:(g + 1) * GROUP_SIZE, :]
        m1 = jnp.max(c, axis=0, keepdims=True)
        first = jnp.min(jnp.where(c == m1, member, float(GROUP_SIZE)), axis=0, keepdims=True)
        m2 = jnp.max(jnp.where(member == first, -jnp.inf, c), axis=0, keepdims=True)
        gscore.append(m1 + m2)

    keep_rows = []
    for g in range(N_GROUPS):
        beaten = jnp.zeros((1, tn), F32)
        for o in range(N_GROUPS):
            if o == g:
                continue
            ahead = (gscore[o] >= gscore[g]) if o < g else (gscore[o] > gscore[g])
            beaten = beaten + jnp.where(ahead, 1.0, 0.0)
        keep_rows.append(jnp.broadcast_to(beaten, (GROUP_SIZE, tn)))
    beaten_all = jnp.concatenate(keep_rows, axis=0)

    masked = jnp.where(beaten_all < TOPK_GROUPS, choice, -jnp.inf)
    eidx = lax.broadcasted_iota(jnp.int32, (N_EXPERTS, tn), 0).astype(F32)
    sel = jnp.zeros((N_EXPERTS, tn), F32)
    picks, weights = [], []
    for r in range(TOP_K):
        mx = jnp.max(masked, axis=0, keepdims=True)
        first = jnp.min(jnp.where(masked == mx, eidx, float(N_EXPERTS)), axis=0, keepdims=True)
        pick = eidx == first
        picks.append((pick, first))
        weights.append(jnp.sum(jnp.where(pick, scores, 0.0), axis=0, keepdims=True))
        masked = jnp.where(pick, -jnp.inf, masked)
        sel = jnp.where(pick, 1.0, sel)

    wsum = weights[0]
    for r in range(1, TOP_K):
        wsum = wsum + weights[r]

    count = jnp.sum(sel, axis=1, keepdims=True)
    run_len = jnp.floor((count + (RUN_ALIGN - 1.0)) * (1.0 / RUN_ALIGN)) * RUN_ALIGN
    run_len_b = jnp.broadcast_to(run_len, (N_EXPERTS, 128))
    er = lax.broadcasted_iota(jnp.int32, (N_EXPERTS, N_EXPERTS), 0)
    ec = lax.broadcasted_iota(jnp.int32, (N_EXPERTS, N_EXPERTS), 1)
    before = jnp.where(ec < er, 1.0, 0.0).astype(BF16)
    run_start = jnp.dot(before, run_len_b.astype(BF16), preferred_element_type=F32)[:, 0:1]
    incl = jnp.dot(sel.astype(BF16), tri_ref[...], preferred_element_type=F32)
    pos_mat = run_start + incl - sel
    cnt_ref[0] = run_len_b.astype(jnp.int32)

    for r in range(TOP_K):
        pick, _ = picks[r]
        pos_ref[0, r:r + 1, :] = jnp.sum(jnp.where(pick, pos_mat, 0.0), axis=0,
                                         keepdims=True).astype(jnp.int32)
        gate_ref[0, r:r + 1, :] = weights[r] / wsum * ROUTED_SCALE


def _route(logits_t, router_b, tri):
    e, t = logits_t.shape
    nt = t // ROUTE_TILE
    blk = lambda i: (i, 0, 0)
    return pl.pallas_call(
        _route_kernel,
        grid=(nt,),
        in_specs=[
            pl.BlockSpec((e, ROUTE_TILE), lambda i: (0, i)),
            pl.BlockSpec((e, 1), lambda i: (0, 0)),
            pl.BlockSpec((ROUTE_TILE, ROUTE_TILE), lambda i: (0, 0)),
        ],
        out_specs=[
            pl.BlockSpec((1, TOP_K, ROUTE_TILE), blk),
            pl.BlockSpec((1, TOP_K, ROUTE_TILE), blk),
            pl.BlockSpec((1, e, 128), blk),
        ],
        out_shape=[
            jax.ShapeDtypeStruct((nt, TOP_K, ROUTE_TILE), jnp.int32),
            jax.ShapeDtypeStruct((nt, TOP_K, ROUTE_TILE), F32),
            jax.ShapeDtypeStruct((nt, e, 128), jnp.int32),
        ],
        compiler_params=_cparams(("arbitrary",)),
        name="route",
    )(logits_t, router_b, tri)


def _for_each_run_piece(cnt_ref, tile, fn):
    def body(e, carry):
        run = tile * N_EXPERTS + e
        n = cnt_ref[run]
        for piece in RUN_PIECES:
            @pl.when((n & piece) != 0)
            def _():
                fn(run, n & (-2 * piece), piece)
        return carry

    lax.fori_loop(0, N_EXPERTS, body, 0)


def _pack_bf16_pair(lo_f32, hi_f32):
    lo = (lax.bitcast_convert_type(lo_f32, jnp.int32) >> 16) & LOW_HALF
    hi = lax.bitcast_convert_type(hi_f32, jnp.int32) & HIGH_HALF
    return lo | hi


def _unpack_bf16_pair(packed):
    lo = lax.bitcast_convert_type(packed << 16, F32).astype(BF16)
    hi = lax.bitcast_convert_type(packed & HIGH_HALF, F32).astype(BF16)
    return lo, hi


def _dispatch_kernel(gbase_ref, lbase_ref, cnt_ref, rows_ref, total_ref, pos_ref, x_ref, xs_ref,
                     stage, zeros, sem, zsem):
    i = pl.program_id(0)
    nt = pl.num_programs(0)
    slot = i % 2
    tn = ROUTE_TILE
    half = D_MODEL // 2

    def run_copy(tile, buf):
        def build(run, before, n):
            src = pl.multiple_of(lbase_ref[run] + before, RUN_ALIGN)
            dst = pl.multiple_of(gbase_ref[run] + before, RUN_ALIGN)
            return pltpu.make_async_copy(stage.at[buf, pl.ds(src, n), :], xs_ref.at[pl.ds(dst, n), :],
                                         sem.at[buf])
        return build

    @pl.when(i >= 2)
    def _():
        build = run_copy(i - 2, slot)
        _for_each_run_piece(cnt_ref, i - 2, lambda run, before, n: build(run, before, n).wait())

    xb = x_ref[...].astype(BF16)
    pos = pos_ref[0]
    row_iota = lax.broadcasted_iota(jnp.int32, (SORT_CHUNK, tn), 0).astype(F32).astype(BF16)
    one = jnp.ones((SORT_CHUNK, tn), BF16)
    zero = jnp.zeros((SORT_CHUNK, tn), BF16)

    def onehot_rows(c):
        rel = jnp.clip(pos - c * SORT_CHUNK, -1, SORT_CHUNK).astype(F32).astype(BF16)
        hit = rel[0:1, :] == row_iota
        for k in range(1, TOP_K):
            hit = hit | (rel[k:k + 1, :] == row_iota)
        return jnp.where(hit, one, zero)

    def sort_chunk(c, carry):
        p0 = pl.multiple_of(c * SORT_CHUNK, SORT_CHUNK)
        rows = jnp.dot(onehot_rows(c), xb, preferred_element_type=F32)
        stage[slot, pl.ds(p0, SORT_CHUNK), :] = _pack_bf16_pair(rows[:, :half], rows[:, half:])
        return carry

    lax.fori_loop(0, (rows_ref[i] + SORT_CHUNK - 1) // SORT_CHUNK, sort_chunk, 0)

    build_now = run_copy(i, slot)
    _for_each_run_piece(cnt_ref, i, lambda run, before, n: build_now(run, before, n).start())

    @pl.when(i == nt - 1)
    def _():
        @pl.when(nt >= 2)
        def _():
            build = run_copy(i - 1, 1 - slot)
            _for_each_run_piece(cnt_ref, i - 1, lambda run, before, n: build(run, before, n).wait())

        _for_each_run_piece(cnt_ref, i, lambda run, before, n: build_now(run, before, n).wait())

        zeros[...] = jnp.zeros_like(zeros)
        total = total_ref[0]
        tail = (-total) & (EXPERT_ROWS - 1)
        for piece in RUN_PIECES:
            if piece >= EXPERT_ROWS:
                continue

            @pl.when((tail & piece) != 0)
            def _():
                dst = pl.multiple_of(total + (tail & (-2 * piece)), RUN_ALIGN)
                cp = pltpu.make_async_copy(zeros.at[pl.ds(0, piece), :], xs_ref.at[pl.ds(dst, piece), :], zsem)
                cp.start()
                cp.wait()

        first_free = (total + tail) // EXPERT_ROWS

        def zero_block(b):
            dst = pl.multiple_of(b * EXPERT_ROWS, EXPERT_ROWS)
            return pltpu.make_async_copy(zeros, xs_ref.at[pl.ds(dst, EXPERT_ROWS), :], zsem)

        n_blocks = xs_ref.shape[0] // EXPERT_ROWS
        lax.fori_loop(first_free, n_blocks, lambda b, c: (zero_block(b).start(), c)[1], 0)
        lax.fori_loop(first_free, n_blocks, lambda b, c: (zero_block(b).wait(), c)[1], 0)


def _dispatch(gbase, lbase, cnt, tile_rows, total, pos, x2d, n_rows):
    t, d = x2d.shape
    nt = t // ROUTE_TILE
    return pl.pallas_call(
        _dispatch_kernel,
        grid_spec=pltpu.PrefetchScalarGridSpec(
            num_scalar_prefetch=5,
            grid=(nt,),
            in_specs=[pl.BlockSpec((1, TOP_K, ROUTE_TILE), lambda i, *_: (i, 0, 0)),
                      pl.BlockSpec((ROUTE_TILE, d), lambda i, *_: (i, 0))],
            out_specs=pl.BlockSpec(memory_space=pl.ANY),
            scratch_shapes=[pltpu.VMEM((2, TILE_CAP, d // 2), jnp.int32),
                            pltpu.VMEM((EXPERT_ROWS, d // 2), jnp.int32),
                            pltpu.SemaphoreType.DMA((2,)),
                            pltpu.SemaphoreType.DMA(())],
        ),
        out_shape=jax.ShapeDtypeStruct((n_rows, d // 2), jnp.int32),
        compiler_params=_cparams(("arbitrary",)),
        name="dispatch",
    )(gbase, lbase, cnt, tile_rows, total, pos, x2d)


def _experts_kernel(ib_ref, ie_ref, flag_ref, seg_ref, xs_ref, wg_ref, wu_ref, wd_ref, y_ref,
                    wg_bf, wu_bf, wd_bf):
    w = pl.program_id(0)
    flags = flag_ref[w]
    valid = (flags & 1) != 0
    first_of_block = (flags & 2) != 0
    new_expert = (flags & 4) != 0

    @pl.when((flags & 8) != 0)
    def _():
        y_ref[...] = jnp.zeros_like(y_ref)

    @pl.when(new_expert)
    def _():
        wg_bf[...] = wg_ref[0].astype(BF16)
        wu_bf[...] = wu_ref[0].astype(BF16)
        wd_bf[...] = wd_ref[0].astype(BF16)

    @pl.when(valid)
    def _():
        e = ie_ref[w]
        half = D_MODEL // 2
        x_lo, x_hi = _unpack_bf16_pair(xs_ref[...])
        gte = (jnp.dot(x_lo, wg_bf[0:half, :], preferred_element_type=F32)
               + jnp.dot(x_hi, wg_bf[half:, :], preferred_element_type=F32))
        up = (jnp.dot(x_lo, wu_bf[0:half, :], preferred_element_type=F32)
              + jnp.dot(x_hi, wu_bf[half:, :], preferred_element_type=F32))
        h = (gte * _sigmoid(gte) * up).astype(BF16)
        y = jnp.dot(h, wd_bf[...], preferred_element_type=F32)
        y_bf = y.astype(BF16).astype(F32)
        packed = _pack_bf16_pair(y_bf[:, :half], y_bf[:, half:])
        rows = ib_ref[w] * EXPERT_ROWS + lax.broadcasted_iota(jnp.int32, (EXPERT_ROWS, 1), 0)
        mine = (rows >= seg_ref[e]) & (rows < seg_ref[e + 1])

        @pl.when(first_of_block)
        def _():
            y_ref[...] = jnp.where(mine, packed, 0)

        @pl.when(jnp.logical_not(first_of_block))
        def _():
            y_ref[...] = jnp.where(mine, packed, y_ref[...])


def _experts(item_block, item_expert, item_flags, seg_bounds, xs, wg, wu, wd):
    n, dh = xs.shape
    n_items = item_block.shape[0]
    _, d, ff = wg.shape
    return pl.pallas_call(
        _experts_kernel,
        grid_spec=pltpu.PrefetchScalarGridSpec(
            num_scalar_prefetch=4,
            grid=(n_items,),
            in_specs=[
                pl.BlockSpec((EXPERT_ROWS, dh), lambda w, ib, ie, fl, sg: (ib[w], 0)),
                pl.BlockSpec((1, d, ff), lambda w, ib, ie, fl, sg: (ie[w], 0, 0)),
                pl.BlockSpec((1, d, ff), lambda w, ib, ie, fl, sg: (ie[w], 0, 0)),
                pl.BlockSpec((1, ff, d), lambda w, ib, ie, fl, sg: (ie[w], 0, 0)),
            ],
            out_specs=pl.BlockSpec((EXPERT_ROWS, dh), lambda w, ib, ie, fl, sg: (ib[w], 0)),
            scratch_shapes=[pltpu.VMEM((d, ff), BF16), pltpu.VMEM((d, ff), BF16), pltpu.VMEM((ff, d), BF16)],
        ),
        out_shape=jax.ShapeDtypeStruct((n, dh), jnp.int32),
        compiler_params=_cparams(("arbitrary",)),
        name="experts",
    )(item_block, item_expert, item_flags, seg_bounds, xs, wg, wu, wd)


def _work_items(counts, n_rows):
    n_items = n_rows // EXPERT_ROWS + N_EXPERTS
    seg_end = jnp.cumsum(counts)
    seg_start = seg_end - counts
    first_blk = seg_start // EXPERT_ROWS
    n_blk = jnp.where(counts > 0, (seg_end - 1) // EXPERT_ROWS - first_blk + 1, 0)
    item_end = jnp.cumsum(n_blk)
    item_start = item_end - n_blk
    total = item_end[-1]
    w = jnp.arange(n_items, dtype=jnp.int32)
    wc = jnp.minimum(w, total - 1)
    e = jnp.minimum(jnp.sum((item_end[None, :] <= wc[:, None]).astype(jnp.int32), axis=1), N_EXPERTS - 1)
    onehot = (e[:, None] == jnp.arange(N_EXPERTS, dtype=jnp.int32)[None, :]).astype(jnp.int32)
    b = jnp.sum(onehot * (first_blk - item_start)[None, :], axis=1) + wc
    valid = w < total
    n_blocks = n_rows // EXPERT_ROWS
    free_blk = (seg_end[-1] + EXPERT_ROWS - 1) // EXPERT_ROWS + (w - total)
    fill = jnp.logical_not(valid) & (free_blk < n_blocks)
    b = jnp.where(valid, b, jnp.minimum(free_blk, n_blocks - 1)).astype(jnp.int32)
    prev_b = jnp.concatenate([jnp.full((1,), -1, jnp.int32), b[:-1]])
    prev_e = jnp.concatenate([jnp.full((1,), -1, jnp.int32), e[:-1]])
    flags = (valid.astype(jnp.int32) + 2 * (valid & (b != prev_b)).astype(jnp.int32)
             + 4 * (valid & (e != prev_e)).astype(jnp.int32) + 8 * fill.astype(jnp.int32))
    seg_bounds = jnp.concatenate([seg_start, seg_end[-1:]]).astype(jnp.int32)
    return b, e, flags, seg_bounds


def _combine_kernel(gbase_ref, lbase_ref, cnt_ref, rows_ref, pos_ref, gate_ref, x_ref,
                    sg_ref, su_ref, sd_ref, g_ref, beta_ref, yb_ref, o_ref, stage, acc, lane_tile, lane_gate,
                    sem):
    i = pl.program_id(0)
    nt = pl.num_programs(0)
    slot = i % 2
    tn = ROUTE_TILE
    half = D_MODEL // 2

    def run_copy(buf):
        def build(run, before, n):
            src = pl.multiple_of(gbase_ref[run] + before, RUN_ALIGN)
            dst = pl.multiple_of(lbase_ref[run] + before, RUN_ALIGN)
            return pltpu.make_async_copy(yb_ref.at[pl.ds(src, n), :], stage.at[buf, pl.ds(dst, n), :],
                                         sem.at[buf])
        return build

    def start_tile(tile, buf):
        build = run_copy(buf)
        _for_each_run_piece(cnt_ref, tile, lambda run, before, n: build(run, before, n).start())

    @pl.when(i == 0)
    def _():
        stage[...] = jnp.zeros_like(stage)
        start_tile(0, 0)

    @pl.when(i + 1 < nt)
    def _():
        start_tile(i + 1, 1 - slot)

    x = x_ref[...]
    xb = x.astype(BF16)
    gte = jnp.dot(xb, sg_ref[...], preferred_element_type=F32)
    up = jnp.dot(xb, su_ref[...], preferred_element_type=F32)
    h = (gte * _sigmoid(gte) * up).astype(BF16)
    acc[...] = jnp.dot(h, sd_ref[...], preferred_element_type=F32)

    build_now = run_copy(slot)
    _for_each_run_piece(cnt_ref, i, lambda run, before, n: build_now(run, before, n).wait())

    lanes = 128
    lane_iota = lax.broadcasted_iota(jnp.int32, (tn, lanes), 1)
    for k in range(TOP_K):
        p = jnp.broadcast_to(pos_ref[:, k:k + 1], (tn, lanes))
        g = jnp.broadcast_to(gate_ref[:, k:k + 1], (tn, lanes))
        lane_tile[k] = (p >> (lanes.bit_length() - 1)).astype(F32).astype(BF16)
        lane_gate[k] = jnp.where((p & (lanes - 1)) == lane_iota, g, 0.0).astype(BF16)
    zero = jnp.zeros((tn, lanes), BF16)

    def gate_matrix(c):
        cols = []
        for j in range(COMBINE_CHUNK // lanes):
            tile = jnp.asarray(c * (COMBINE_CHUNK // lanes) + j, jnp.int32).astype(F32).astype(BF16)
            w = zero
            for k in range(TOP_K):
                w = w + jnp.where(lane_tile[k] == tile, lane_gate[k], zero)
            cols.append(w)
        return jnp.concatenate(cols, axis=1)

    def weigh_chunk(c, carry):
        p0 = pl.multiple_of(c * COMBINE_CHUNK, COMBINE_CHUNK)
        wts = gate_matrix(c)
        y_lo, y_hi = _unpack_bf16_pair(stage[slot, pl.ds(p0, COMBINE_CHUNK), :])
        acc[:, 0:half] += jnp.dot(wts, y_lo, preferred_element_type=F32)
        acc[:, half:] += jnp.dot(wts, y_hi, preferred_element_type=F32)
        return carry

    lax.fori_loop(0, (rows_ref[i] + COMBINE_CHUNK - 1) // COMBINE_CHUNK, weigh_chunk, 0)
    o_ref[...] = _layer_norm(ALPHA * x + acc[...], g_ref[...], beta_ref[...])


def _combine(gbase, lbase, cnt, tile_rows, pos_t, gate_t, x2d, sg, su, sd, g, beta, yb):
    t, d = x2d.shape
    nt = t // ROUTE_TILE
    ff = sg.shape[-1]
    row = lambda i, *_: (i, 0)
    const = lambda i, *_: (0, 0)
    return pl.pallas_call(
        _combine_kernel,
        grid_spec=pltpu.PrefetchScalarGridSpec(
            num_scalar_prefetch=4,
            grid=(nt,),
            in_specs=[
                pl.BlockSpec((ROUTE_TILE, TOP_K), row),
                pl.BlockSpec((ROUTE_TILE, TOP_K), row),
                pl.BlockSpec((ROUTE_TILE, d), row),
                pl.BlockSpec((d, ff), const),
                pl.BlockSpec((d, ff), const),
                pl.BlockSpec((ff, d), const),
                pl.BlockSpec((1, d), const),
                pl.BlockSpec((1, d), const),
                pl.BlockSpec(memory_space=pl.ANY),
            ],
            out_specs=pl.BlockSpec((ROUTE_TILE, d), row),
            scratch_shapes=[pltpu.VMEM((2, TILE_CAP, d // 2), jnp.int32),
                            pltpu.VMEM((ROUTE_TILE, d), F32),
                            pltpu.VMEM((TOP_K, ROUTE_TILE, 128), BF16),
                            pltpu.VMEM((TOP_K, ROUTE_TILE, 128), BF16),
                            pltpu.SemaphoreType.DMA((2,))],
        ),
        out_shape=jax.ShapeDtypeStruct((t, d), F32),
        compiler_params=_cparams(("arbitrary",)),
        name="combine",
    )(gbase, lbase, cnt, tile_rows, pos_t, gate_t, x2d, sg, su, sd, g, beta, yb)


def _run_tables(cnt):
    lbase = jnp.cumsum(cnt, axis=1) - cnt
    per_expert = jnp.sum(cnt, axis=0)
    seg_start = jnp.cumsum(per_expert) - per_expert
    gbase = seg_start[None, :] + jnp.cumsum(cnt, axis=0) - cnt
    flat = lambda a: a.reshape(-1).astype(jnp.int32)
    return flat(gbase), flat(lbase), flat(cnt), jnp.sum(cnt, axis=1).astype(jnp.int32), per_expert


def kernel(x, mem, w_in, b_in, conv_w, conv_b, conv_ln_g, conv_ln_b, attn_sinks, rel_bias, w_out, b_out, ln1_g, ln1_b, xq_w, xkv_w, xo_w, ln2_g, ln2_b, router_w, router_b, exp_gate, exp_up, exp_down, sh_gate, sh_up, sh_down, ln3_g, ln3_b):
    bsz, seq, d = x.shape
    t = bsz * seq
    bias_tab, sink_tab = _band_tables(rel_bias, attn_sinks[0])
    tri = (jnp.arange(ROUTE_TILE)[:, None] <= jnp.arange(ROUTE_TILE)[None, :]).astype(BF16)
    row = lambda p: p.reshape(1, -1)
    for l in range(DEPTH):
        conv_out, q, k, v = _mix(x, w_in[l].astype(BF16), row(b_in[l]), conv_w[l], row(conv_b[l]),
                                 row(conv_ln_g[l]), row(conv_ln_b[l]))
        att = _swa(q, k, v, bias_tab, sink_tab)
        x1 = _outproj(x.reshape(t, d), conv_out.reshape(t, CONV_CH), att.reshape(t, Q_COLS),
                      w_out[l].astype(BF16), row(b_out[l]), row(ln1_g[l]), row(ln1_b[l]))
        kmem, vmem = _memkv(mem, xkv_w[l].astype(BF16))
        x2, logits_t = _cross(x1.reshape(bsz, seq, d), kmem, vmem, xq_w[l].astype(BF16),
                              xo_w[l].astype(BF16), row(ln2_g[l]), row(ln2_b[l]),
                              router_w[l].T.astype(BF16))
        x2 = x2.reshape(t, d)
        pos, gate, cnt = _route(logits_t, router_b[l].reshape(-1, 1), tri)
        nt = t // ROUTE_TILE
        n_rows = -(-(t * TOP_K + nt * N_EXPERTS * (RUN_ALIGN - 1)) // EXPERT_ROWS) * EXPERT_ROWS
        gbase, lbase, cnt_flat, tile_rows, per_expert = _run_tables(cnt[:, :, 0])
        item_block, item_expert, item_flags, seg_bounds = _work_items(per_expert, n_rows)
        xs = _dispatch(gbase, lbase, cnt_flat, tile_rows, seg_bounds[-1:], pos, x2, n_rows)
        yb = _experts(item_block, item_expert, item_flags, seg_bounds, xs,
                      exp_gate[l], exp_up[l], exp_down[l])
        pos_t = jnp.transpose(pos, (0, 2, 1)).reshape(t, TOP_K)
        gate_t = jnp.transpose(gate, (0, 2, 1)).reshape(t, TOP_K)
        x = _combine(gbase, lbase, cnt_flat, tile_rows, pos_t, gate_t, x2, sh_gate[l].astype(BF16),
                     sh_up[l].astype(BF16), sh_down[l].astype(BF16), row(ln3_g[l]), row(ln3_b[l]),
                     yb).reshape(bsz, seq, d)
    return x
```

```python
import functools
import math

import jax
import jax.numpy as jnp
from jax import lax
from jax.experimental import pallas as pl
from jax.experimental.pallas import tpu as pltpu

D_MODEL = 1024
MEM_LEN = 256
HEAD_DIM = 64
CONV_CH = D_MODEL // 2
CONV_WIDTH = 31
ATT_HEADS = 8
KV_HEADS = 2
WINDOW = 128
BLOCK = 128
REL_BUCKETS = 32
REL_MAX_DIST = 128
Q_COLS = ATT_HEADS * HEAD_DIM
KV_COLS = KV_HEADS * HEAD_DIM
IN_COLS = 2 * CONV_CH + Q_COLS + 2 * KV_COLS
X_HEADS = 4
X_HEAD_DIM = D_MODEL // X_HEADS
N_EXPERTS = 64
TOP_K = 8
N_GROUPS = 8
GROUP_SIZE = N_EXPERTS // N_GROUPS
TOPK_GROUPS = 4
EXPERT_FF = D_MODEL // 4
ROUTED_SCALE = 2.5
DEPTH = 1
ALPHA = (2 * DEPTH) ** 0.25
LN_EPS = 1e-5
NEG_INF = -1e30

F32 = jnp.float32
BF16 = jnp.bfloat16

VMEM_LIMIT_BYTES = 56 * 1024 * 1024

ROW_TILE = 512
CONV_ROWS = 32
SUBLANES = 8
CONV_HALO = 32
SWA_TILE = 512
ROUTE_TILE = 512
EXPERT_ROWS = 512
RUN_ALIGN = 16
RUN_ALIGN_LOG2 = RUN_ALIGN.bit_length() - 1
RUN_PIECES = tuple(1 << b for b in range(9, RUN_ALIGN_LOG2 - 1, -1))
RUN_LOOP_ROWS = 32
TILE_CAP = ROUTE_TILE * TOP_K + N_EXPERTS * RUN_ALIGN
TOTAL_PIECES = tuple(1 << b for b in range(TILE_CAP.bit_length() - 1, RUN_ALIGN_LOG2 - 1, -1))
SORT_CHUNK = 256
COMBINE_CHUNK = 512


def _cparams(sem):
    return pltpu.CompilerParams(dimension_semantics=sem, vmem_limit_bytes=VMEM_LIMIT_BYTES)


def _layer_norm(h, g, b):
    mu = jnp.mean(h, axis=-1, keepdims=True)
    d = h - mu
    var = jnp.mean(d * d, axis=-1, keepdims=True)
    return d * lax.rsqrt(var + LN_EPS) * g + b


def _sigmoid(x):
    return 1.0 / (1.0 + jnp.exp(-x))


def _mix_kernel(x_ref, w_ref, b_ref, cw_ref, cb_ref, cg_ref, cbeta_ref,
                conv_ref, q_ref, k_ref, v_ref, u_ext, u_sh):
    j = pl.program_id(1)
    xb = x_ref[0].astype(BF16)
    proj = jnp.dot(xb, w_ref[...], preferred_element_type=F32) + b_ref[...]
    a = proj[:, :CONV_CH]
    g = proj[:, CONV_CH:2 * CONV_CH]
    q_ref[0] = (proj[:, 2 * CONV_CH:2 * CONV_CH + Q_COLS] * (HEAD_DIM ** -0.5)).astype(BF16)
    k_ref[0] = proj[:, 2 * CONV_CH + Q_COLS:2 * CONV_CH + Q_COLS + KV_COLS].astype(BF16)
    v_ref[0] = proj[:, 2 * CONV_CH + Q_COLS + KV_COLS:].astype(BF16)

    @pl.when(j == 0)
    def _():
        u_ext[0:CONV_HALO, :] = jnp.zeros((CONV_HALO, CONV_CH), F32)

    u_ext[CONV_HALO:CONV_HALO + ROW_TILE, :] = a * _sigmoid(g)

    first_tap = CONV_HALO - (CONV_WIDTH - 1)
    shifted_rows = u_sh.shape[1]
    for r in range(1, SUBLANES):
        u_sh[r - 1] = u_ext[r:r + shifted_rows, :]

    for c in range(ROW_TILE // CONV_ROWS):
        acc = jnp.zeros((CONV_ROWS, CONV_CH), F32) + cb_ref[...]
        for t in range(CONV_WIDTH):
            r = (first_tap + t) % SUBLANES
            base = c * CONV_ROWS + (first_tap + t) - r
            if r == 0:
                taps = u_ext[base:base + CONV_ROWS, :]
            else:
                taps = u_sh[r - 1, base:base + CONV_ROWS, :]
            acc = acc + taps * cw_ref[t:t + 1, :]
        y = _layer_norm(acc, cg_ref[...], cbeta_ref[...])
        conv_ref[0, c * CONV_ROWS:(c + 1) * CONV_ROWS, :] = (y * _sigmoid(y)).astype(BF16)

    u_ext[0:CONV_HALO, :] = u_ext[ROW_TILE:ROW_TILE + CONV_HALO, :]


def _mix(x, w_in, b_in, conv_w, conv_b, conv_g, conv_beta):
    bsz, seq, d = x.shape
    nt = seq // ROW_TILE
    row = lambda b, j: (b, j, 0)
    const2 = lambda b, j: (0, 0)
    return pl.pallas_call(
        _mix_kernel,
        grid=(bsz, nt),
        in_specs=[
            pl.BlockSpec((1, ROW_TILE, d), row),
            pl.BlockSpec((d, IN_COLS), const2),
            pl.BlockSpec((1, IN_COLS), const2),
            pl.BlockSpec((CONV_WIDTH, CONV_CH), const2),
            pl.BlockSpec((1, CONV_CH), const2),
            pl.BlockSpec((1, CONV_CH), const2),
            pl.BlockSpec((1, CONV_CH), const2),
        ],
        out_specs=[
            pl.BlockSpec((1, ROW_TILE, CONV_CH), row),
            pl.BlockSpec((1, ROW_TILE, Q_COLS), row),
            pl.BlockSpec((1, ROW_TILE, KV_COLS), row),
            pl.BlockSpec((1, ROW_TILE, KV_COLS), row),
        ],
        out_shape=[
            jax.ShapeDtypeStruct((bsz, seq, CONV_CH), BF16),
            jax.ShapeDtypeStruct((bsz, seq, Q_COLS), BF16),
            jax.ShapeDtypeStruct((bsz, seq, KV_COLS), BF16),
            jax.ShapeDtypeStruct((bsz, seq, KV_COLS), BF16),
        ],
        scratch_shapes=[pltpu.VMEM((ROW_TILE + CONV_HALO, CONV_CH), F32),
                        pltpu.VMEM((SUBLANES - 1, ROW_TILE + CONV_HALO - SUBLANES, CONV_CH), F32)],
        compiler_params=_cparams(("arbitrary", "arbitrary")),
        name="mix",
    )(x, w_in, b_in, conv_w, conv_b, conv_g, conv_beta)


def _swa_kernel(q_ref, kp_ref, kc_ref, vp_ref, vc_ref, bias_ref, sink_ref, o_ref):
    j = pl.program_id(1)
    rows = BLOCK + SWA_TILE
    lane = lax.broadcasted_iota(jnp.int32, (rows, 2 * HEAD_DIM), 1)
    low = lane < HEAD_DIM

    def placements(prev_ref, cur_ref):
        t = jnp.concatenate([prev_ref[0], cur_ref[0]], axis=0).astype(F32)
        tr = pltpu.roll(t, HEAD_DIM, 1)
        zero = jnp.zeros_like(t)
        kv0_low = jnp.where(low, t, zero).astype(BF16)
        kv1_high = jnp.where(low, zero, t).astype(BF16)
        kv1_low = jnp.where(low, tr, zero).astype(BF16)
        kv0_high = jnp.where(low, zero, tr).astype(BF16)
        return (kv0_low, kv0_high, kv1_low, kv1_high)

    ks = placements(kp_ref, kc_ref)
    vs = placements(vp_ref, vc_ref)
    slab = 2 * HEAD_DIM
    for i in range(SWA_TILE // BLOCK):
        q = q_ref[0, i * BLOCK:(i + 1) * BLOCK, :]
        q_kv0 = jnp.concatenate([q[:, 0:slab], q[:, slab:2 * slab]], axis=0)
        q_kv1 = jnp.concatenate([q[:, 2 * slab:3 * slab], q[:, 3 * slab:4 * slab]], axis=0)
        band = slice(i * BLOCK, (i + 2) * BLOCK)
        outs = []
        for s in range(4):
            qs = q_kv0 if s < 2 else q_kv1
            bias = bias_ref[1, s]
            if i == 0:
                bias = jnp.where(j == 0, bias_ref[0, s], bias)
            logits = lax.dot_general(qs, ks[s][band], (((1,), (1,)), ((), ())),
                                     preferred_element_type=F32) + bias
            sink = sink_ref[s]
            m = jnp.maximum(jnp.max(logits, axis=-1, keepdims=True), sink)
            p = jnp.exp(logits - m)
            den = jnp.sum(p, axis=-1, keepdims=True) + jnp.exp(sink - m)
            o = jnp.dot(p.astype(BF16), vs[s][band], preferred_element_type=F32)
            outs.append(o / den)
        o_kv0 = outs[0] + outs[1]
        o_kv1 = outs[2] + outs[3]
        blk = slice(i * BLOCK, (i + 1) * BLOCK)
        o_ref[0, blk, 0:slab] = o_kv0[0:BLOCK].astype(BF16)
        o_ref[0, blk, slab:2 * slab] = o_kv0[BLOCK:2 * BLOCK].astype(BF16)
        o_ref[0, blk, 2 * slab:3 * slab] = o_kv1[0:BLOCK].astype(BF16)
        o_ref[0, blk, 3 * slab:4 * slab] = o_kv1[BLOCK:2 * BLOCK].astype(BF16)


def _swa(q, k, v, bias_tab, sink_tab):
    bsz, seq, _ = q.shape
    per = SWA_TILE // BLOCK
    cur = lambda b, n: (b, n, 0)
    prev = lambda b, n: (b, jnp.maximum(n * per - 1, 0), 0)
    whole = lambda b, n: (0, 0, 0, 0)
    return pl.pallas_call(
        _swa_kernel,
        grid=(bsz, seq // SWA_TILE),
        in_specs=[
            pl.BlockSpec((1, SWA_TILE, Q_COLS), cur),
            pl.BlockSpec((1, BLOCK, KV_COLS), prev),
            pl.BlockSpec((1, SWA_TILE, KV_COLS), cur),
            pl.BlockSpec((1, BLOCK, KV_COLS), prev),
            pl.BlockSpec((1, SWA_TILE, KV_COLS), cur),
            pl.BlockSpec((2, 4, 2 * BLOCK, 2 * BLOCK), whole),
            pl.BlockSpec((4, 2 * BLOCK, 1), lambda b, n: (0, 0, 0)),
        ],
        out_specs=pl.BlockSpec((1, SWA_TILE, Q_COLS), cur),
        out_shape=jax.ShapeDtypeStruct((bsz, seq, Q_COLS), BF16),
        compiler_params=_cparams(("arbitrary", "arbitrary")),
        name="swa",
    )(q, k, k, v, v, bias_tab, sink_tab)


def _t5_bucket(dist):
    n = jnp.maximum(dist, 0)
    exact = REL_BUCKETS // 2
    large = exact + (jnp.log(jnp.maximum(n, 1).astype(F32) / exact)
                     / math.log(REL_MAX_DIST / exact) * (REL_BUCKETS - exact)).astype(jnp.int32)
    large = jnp.minimum(large, REL_BUCKETS - 1)
    return jnp.where(n < exact, n, large)


def _band_tables(rel_bias, sinks):
    qi = jnp.arange(BLOCK)[:, None]
    kj = jnp.arange(2 * BLOCK)[None, :]
    dist = qi + BLOCK - kj
    bucket = _t5_bucket(dist)
    bias = jnp.zeros((ATT_HEADS, BLOCK, 2 * BLOCK), F32)
    for bkt in range(REL_BUCKETS):
        bias = jnp.where(bucket[None] == bkt, rel_bias[bkt].astype(F32)[:, None, None], bias)
    in_window = (dist >= 0) & (dist < WINDOW)
    masks = jnp.stack([in_window & (kj >= BLOCK), in_window])
    masked = jnp.where(masks[:, None], bias[None], NEG_INF)
    pairs = ((0, 2), (1, 3), (4, 6), (5, 7))
    bias_tab = jnp.stack([jnp.concatenate([masked[:, a], masked[:, b]], axis=1) for a, b in pairs], axis=1)
    s = sinks.astype(F32)
    sink_tab = jnp.stack([jnp.concatenate([jnp.full((BLOCK, 1), s[a]), jnp.full((BLOCK, 1), s[b])], axis=0)
                          for a, b in pairs])
    return bias_tab, sink_tab


def _outproj_kernel(x_ref, conv_ref, att_ref, w_ref, b_ref, g_ref, beta_ref, o_ref):
    mix = jnp.dot(conv_ref[...], w_ref[0:CONV_CH, :], preferred_element_type=F32)
    mix = mix + jnp.dot(att_ref[...], w_ref[CONV_CH:, :], preferred_element_type=F32)
    h = ALPHA * x_ref[...] + mix + b_ref[...]
    o_ref[...] = _layer_norm(h, g_ref[...], beta_ref[...])


def _outproj(x2d, conv2d, att2d, w_out, b_out, g, beta):
    t, d = x2d.shape
    row = lambda i: (i, 0)
    const = lambda i: (0, 0)
    return pl.pallas_call(
        _outproj_kernel,
        grid=(t // ROW_TILE,),
        in_specs=[
            pl.BlockSpec((ROW_TILE, d), row),
            pl.BlockSpec((ROW_TILE, CONV_CH), row),
            pl.BlockSpec((ROW_TILE, Q_COLS), row),
            pl.BlockSpec((d, d), const),
            pl.BlockSpec((1, d), const),
            pl.BlockSpec((1, d), const),
            pl.BlockSpec((1, d), const),
        ],
        out_specs=pl.BlockSpec((ROW_TILE, d), row),
        out_shape=jax.ShapeDtypeStruct((t, d), F32),
        compiler_params=_cparams(("arbitrary",)),
        name="outproj",
    )(x2d, conv2d, att2d, w_out, b_out, g, beta)


def _memkv_kernel(mem_ref, w_ref, k_ref, v_ref):
    kv = jnp.dot(mem_ref[0].astype(BF16), w_ref[...], preferred_element_type=F32)
    k_ref[0] = kv[:, :D_MODEL].astype(BF16)
    v_ref[0] = kv[:, D_MODEL:].astype(BF16)


def _memkv(mem, wkv):
    bsz, m, d = mem.shape
    return pl.pallas_call(
        _memkv_kernel,
        grid=(bsz,),
        in_specs=[pl.BlockSpec((1, m, d), lambda b: (b, 0, 0)),
                  pl.BlockSpec((d, 2 * d), lambda b: (0, 0))],
        out_specs=[pl.BlockSpec((1, m, d), lambda b: (b, 0, 0)),
                   pl.BlockSpec((1, m, d), lambda b: (b, 0, 0))],
        out_shape=[jax.ShapeDtypeStruct((bsz, m, d), BF16)] * 2,
        compiler_params=_cparams(("arbitrary",)),
        name="memkv",
    )(mem, wkv)


def _cross_kernel(x_ref, k_ref, v_ref, wq_ref, wo_ref, g_ref, beta_ref, rw_ref, o_ref, lt_ref):
    x = x_ref[0]
    q = jnp.dot(x.astype(BF16), wq_ref[...], preferred_element_type=F32) * (X_HEAD_DIM ** -0.5)
    q = q.astype(BF16)
    heads = []
    for h in range(X_HEADS):
        cols = slice(h * X_HEAD_DIM, (h + 1) * X_HEAD_DIM)
        logits = lax.dot_general(q[:, cols], k_ref[0, :, cols], (((1,), (1,)), ((), ())),
                                 preferred_element_type=F32)
        m = jnp.max(logits, axis=-1, keepdims=True)
        p = jnp.exp(logits - m)
        den = jnp.sum(p, axis=-1, keepdims=True)
        o = jnp.dot(p.astype(BF16), v_ref[0, :, cols], preferred_element_type=F32)
        heads.append((o / den).astype(BF16))
    att = jnp.concatenate(heads, axis=-1)
    cross = jnp.dot(att, wo_ref[...], preferred_element_type=F32)
    y = _layer_norm(ALPHA * x + cross, g_ref[...], beta_ref[...])
    o_ref[0] = y
    lt_ref[...] = lax.dot_general(rw_ref[...], y.astype(BF16), (((1,), (1,)), ((), ())),
                                  preferred_element_type=F32)


def _cross(x1, kmem, vmem, wq, wo, g, beta, rw_t):
    bsz, seq, d = x1.shape
    nt = seq // ROW_TILE
    row = lambda b, j: (b, j, 0)
    mem = lambda b, j: (b, 0, 0)
    const = lambda b, j: (0, 0)
    return pl.pallas_call(
        _cross_kernel,
        grid=(bsz, nt),
        in_specs=[
            pl.BlockSpec((1, ROW_TILE, d), row),
            pl.BlockSpec((1, MEM_LEN, d), mem),
            pl.BlockSpec((1, MEM_LEN, d), mem),
            pl.BlockSpec((d, d), const),
            pl.BlockSpec((d, d), const),
            pl.BlockSpec((1, d), const),
            pl.BlockSpec((1, d), const),
            pl.BlockSpec((N_EXPERTS, d), const),
        ],
        out_specs=[
            pl.BlockSpec((1, ROW_TILE, d), row),
            pl.BlockSpec((N_EXPERTS, ROW_TILE), lambda b, j: (0, b * nt + j)),
        ],
        out_shape=[
            jax.ShapeDtypeStruct((bsz, seq, d), F32),
            jax.ShapeDtypeStruct((N_EXPERTS, bsz * seq), F32),
        ],
        compiler_params=_cparams(("arbitrary", "arbitrary")),
        name="cross",
    )(x1, kmem, vmem, wq, wo, g, beta, rw_t)


def _route_kernel(lt_ref, rb_ref, tri_ref, pos_ref, gate_ref, cnt_ref):
    tn = ROUTE_TILE
    scores = _sigmoid(lt_ref[...])
    choice = scores + rb_ref[...]

    gscore = []
    member = lax.broadcasted_iota(jnp.int32, (GROUP_SIZE, tn), 0).astype(F32)
    for g in range(N_GROUPS):
        c = choice[g * GROUP_SIZE:(g + 1) * GROUP_SIZE, :]
        m1 = jnp.max(c, axis=0, keepdims=True)
        first = jnp.min(jnp.where(c == m1, member, float(GROUP_SIZE)), axis=0, keepdims=True)
        m2 = jnp.max(jnp.where(member == first, -jnp.inf, c), axis=0, keepdims=True)
        gscore.append(m1 + m2)

    keep_rows = []
    for g in range(N_GROUPS):
        beaten = jnp.zeros((1, tn), F32)
        for o in range(N_GROUPS):
            if o == g:
                continue
            ahead = (gscore[o] >= gscore[g]) if o < g else (gscore[o] > gscore[g])
            beaten = beaten + jnp.where(ahead, 1.0, 0.0)
        keep_rows.append(jnp.broadcast_to(beaten, (GROUP_SIZE, tn)))
    beaten_all = jnp.concatenate(keep_rows, axis=0)

    masked = jnp.where(beaten_all < TOPK_GROUPS, choice, -jnp.inf)
    eidx = lax.broadcasted_iota(jnp.int32, (N_EXPERTS, tn), 0).astype(F32)
    sel = jnp.zeros((N_EXPERTS, tn), F32)
    picks, weights = [], []
    for r in range(TOP_K):
        mx = jnp.max(masked, axis=0, keepdims=True)
        first = jnp.min(jnp.where(masked == mx, eidx, float(N_EXPERTS)), axis=0, keepdims=True)
        pick = eidx == first
        picks.append((pick, first))
        weights.append(jnp.sum(jnp.where(pick, scores, 0.0), axis=0, keepdims=True))
        masked = jnp.where(pick, -jnp.inf, masked)
        sel = jnp.where(pick, 1.0, sel)

    wsum = weights[0]
    for r in range(1, TOP_K):
        wsum = wsum + weights[r]

    count = jnp.sum(sel, axis=1, keepdims=True)
    run_len = jnp.floor((count + (RUN_ALIGN - 1.0)) * (1.0 / RUN_ALIGN)) * RUN_ALIGN
    run_len_b = jnp.broadcast_to(run_len, (N_EXPERTS, 128))
    er = lax.broadcasted_iota(jnp.int32, (N_EXPERTS, N_EXPERTS), 0)
    ec = lax.broadcasted_iota(jnp.int32, (N_EXPERTS, N_EXPERTS), 1)
    before = jnp.where(ec < er, 1.0, 0.0).astype(BF16)
    run_start = jnp.dot(before, run_len_b.astype(BF16), preferred_element_type=F32)[:, 0:1]
    incl = jnp.dot(sel.astype(BF16), tri_ref[...], preferred_element_type=F32)
    pos_mat = run_start + incl - sel
    cnt_ref[0] = run_len_b.astype(jnp.int32)

    for r in range(TOP_K):
        pick, _ = picks[r]
        pos_ref[0, r:r + 1, :] = jnp.sum(jnp.where(pick, pos_mat, 0.0), axis=0,
                                         keepdims=True).astype(jnp.int32)
        gate_ref[0, r:r + 1, :] = weights[r] / wsum * ROUTED_SCALE


def _route(logits_t, router_b, tri):
    e, t = logits_t.shape
    nt = t // ROUTE_TILE
    blk = lambda i: (i, 0, 0)
    return pl.pallas_call(
        _route_kernel,
        grid=(nt,),
        in_specs=[
            pl.BlockSpec((e, ROUTE_TILE), lambda i: (0, i)),
            pl.BlockSpec((e, 1), lambda i: (0, 0)),
            pl.BlockSpec((ROUTE_TILE, ROUTE_TILE), lambda i: (0, 0)),
        ],
        out_specs=[
            pl.BlockSpec((1, TOP_K, ROUTE_TILE), blk),
            pl.BlockSpec((1, TOP_K, ROUTE_TILE), blk),
            pl.BlockSpec((1, e, 128), blk),
        ],
        out_shape=[
            jax.ShapeDtypeStruct((nt, TOP_K, ROUTE_TILE), jnp.int32),
            jax.ShapeDtypeStruct((nt, TOP_K, ROUTE_TILE), F32),
            jax.ShapeDtypeStruct((nt, e, 128), jnp.int32),
        ],
        compiler_params=_cparams(("arbitrary",)),
        name="route",
    )(logits_t, router_b, tri)


def _for_each_run_piece(cnt_ref, tile, fn):
    def body(e, carry):
        run = tile * N_EXPERTS + e
        n = cnt_ref[run]

        def whole(q, c):
            fn(run, q * RUN_LOOP_ROWS, RUN_LOOP_ROWS)
            return c

        lax.fori_loop(0, n >> (RUN_LOOP_ROWS.bit_length() - 1), whole, 0)
        for piece in RUN_PIECES:
            if piece >= RUN_LOOP_ROWS:
                continue

            @pl.when((n & piece) != 0)
            def _():
                fn(run, n & (-2 * piece), piece)
        return carry

    lax.fori_loop(0, N_EXPERTS, body, 0)


def _for_each_total_piece(total, fn):
    for piece in TOTAL_PIECES:
        @pl.when((total & piece) != 0)
        def _():
            fn(piece)


def _dispatch_kernel(gbase_ref, lbase_ref, cnt_ref, rows_ref, total_ref, pos_ref, x_ref, xs_ref,
                     stage, zeros, sem, zsem):
    i = pl.program_id(0)
    nt = pl.num_programs(0)
    slot = i % 2
    tn = ROUTE_TILE

    def run_copy(tile, buf):
        def build(run, before, n):
            src = pl.multiple_of(lbase_ref[run] + before, RUN_ALIGN)
            dst = pl.multiple_of(gbase_ref[run] + before, RUN_ALIGN)
            return pltpu.make_async_copy(stage.at[buf, pl.ds(src, n), :], xs_ref.at[pl.ds(dst, n), :],
                                         sem.at[buf])
        return build

    def drain(tile, buf):
        def wait_rows(n):
            pltpu.make_async_copy(stage.at[buf, pl.ds(0, n), :], xs_ref.at[pl.ds(0, n), :],
                                  sem.at[buf]).wait()
        _for_each_total_piece(rows_ref[tile], wait_rows)

    @pl.when(i >= 2)
    def _():
        drain(i - 2, slot)

    xb = x_ref[...].astype(BF16)
    pos = pos_ref[0]
    row_iota = lax.broadcasted_iota(jnp.int32, (SORT_CHUNK, tn), 0).astype(F32).astype(BF16)
    one = jnp.ones((SORT_CHUNK, tn), BF16)
    zero = jnp.zeros((SORT_CHUNK, tn), BF16)

    def onehot_rows(c):
        rel = jnp.clip(pos - c * SORT_CHUNK, -1, SORT_CHUNK).astype(F32).astype(BF16)
        hit = rel[0:1, :] == row_iota
        for k in range(1, TOP_K):
            hit = hit | (rel[k:k + 1, :] == row_iota)
        return jnp.where(hit, one, zero)

    def sort_chunk(c, carry):
        p0 = pl.multiple_of(c * SORT_CHUNK, SORT_CHUNK)
        rows = jnp.dot(onehot_rows(c), xb, preferred_element_type=F32)
        stage[slot, pl.ds(p0, SORT_CHUNK), :] = rows.astype(BF16)
        return carry

    lax.fori_loop(0, (rows_ref[i] + SORT_CHUNK - 1) // SORT_CHUNK, sort_chunk, 0)

    build_now = run_copy(i, slot)
    _for_each_run_piece(cnt_ref, i, lambda run, before, n: build_now(run, before, n).start())

    @pl.when(i == nt - 1)
    def _():
        @pl.when(nt >= 2)
        def _():
            drain(i - 1, 1 - slot)

        drain(i, slot)

        zeros[...] = jnp.zeros_like(zeros)
        total = total_ref[0]
        tail = (-total) & (EXPERT_ROWS - 1)
        for piece in RUN_PIECES:
            if piece >= EXPERT_ROWS:
                continue

            @pl.when((tail & piece) != 0)
            def _():
                dst = pl.multiple_of(total + (tail & (-2 * piece)), RUN_ALIGN)
                cp = pltpu.make_async_copy(zeros.at[pl.ds(0, piece), :], xs_ref.at[pl.ds(dst, piece), :], zsem)
                cp.start()
                cp.wait()

        first_free = (total + tail) // EXPERT_ROWS

        def zero_block(b):
            dst = pl.multiple_of(b * EXPERT_ROWS, EXPERT_ROWS)
            return pltpu.make_async_copy(zeros, xs_ref.at[pl.ds(dst, EXPERT_ROWS), :], zsem)

        n_blocks = xs_ref.shape[0] // EXPERT_ROWS
        lax.fori_loop(first_free, n_blocks, lambda b, c: (zero_block(b).start(), c)[1], 0)
        lax.fori_loop(first_free, n_blocks, lambda b, c: (zero_block(b).wait(), c)[1], 0)


def _dispatch(gbase, lbase, cnt, tile_rows, total, pos, x2d, n_rows):
    t, d = x2d.shape
    nt = t // ROUTE_TILE
    return pl.pallas_call(
        _dispatch_kernel,
        grid_spec=pltpu.PrefetchScalarGridSpec(
            num_scalar_prefetch=5,
            grid=(nt,),
            in_specs=[pl.BlockSpec((1, TOP_K, ROUTE_TILE), lambda i, *_: (i, 0, 0)),
                      pl.BlockSpec((ROUTE_TILE, d), lambda i, *_: (i, 0))],
            out_specs=pl.BlockSpec(memory_space=pl.ANY),
            scratch_shapes=[pltpu.VMEM((2, TILE_CAP, d), BF16),
                            pltpu.VMEM((EXPERT_ROWS, d), BF16),
                            pltpu.SemaphoreType.DMA((2,)),
                            pltpu.SemaphoreType.DMA(())],
        ),
        out_shape=jax.ShapeDtypeStruct((n_rows, d), BF16),
        compiler_params=_cparams(("arbitrary",)),
        name="dispatch",
    )(gbase, lbase, cnt, tile_rows, total, pos, x2d)


def _experts_kernel(ib_ref, ie_ref, flag_ref, seg_ref, xs_ref, wg_ref, wu_ref, wd_ref, y_ref,
                    wg_bf, wu_bf, wd_bf):
    w = pl.program_id(0)
    flags = flag_ref[w]
    valid = (flags & 1) != 0
    first_of_block = (flags & 2) != 0
    new_expert = (flags & 4) != 0

    @pl.when((flags & 8) != 0)
    def _():
        y_ref[...] = jnp.zeros_like(y_ref)

    @pl.when(new_expert)
    def _():
        wg_bf[...] = wg_ref[0].astype(BF16)
        wu_bf[...] = wu_ref[0].astype(BF16)
        wd_bf[...] = wd_ref[0].astype(BF16)

    @pl.when(valid)
    def _():
        e = ie_ref[w]
        x = xs_ref[...]
        gte = jnp.dot(x, wg_bf[...], preferred_element_type=F32)
        up = jnp.dot(x, wu_bf[...], preferred_element_type=F32)
        h = (gte * _sigmoid(gte) * up).astype(BF16)
        y = jnp.dot(h, wd_bf[...], preferred_element_type=F32).astype(BF16)
        rows = ib_ref[w] * EXPERT_ROWS + lax.broadcasted_iota(jnp.int32, (EXPERT_ROWS, 1), 0)
        mine = (rows >= seg_ref[e]) & (rows < seg_ref[e + 1])

        @pl.when(first_of_block)
        def _():
            y_ref[...] = jnp.where(mine, y, jnp.zeros_like(y))

        @pl.when(jnp.logical_not(first_of_block))
        def _():
            y_ref[...] = jnp.where(mine, y, y_ref[...])


def _experts(item_block, item_expert, item_flags, seg_bounds, xs, wg, wu, wd):
    n, d = xs.shape
    n_items = item_block.shape[0]
    ff = wg.shape[-1]
    return pl.pallas_call(
        _experts_kernel,
        grid_spec=pltpu.PrefetchScalarGridSpec(
            num_scalar_prefetch=4,
            grid=(n_items,),
            in_specs=[
                pl.BlockSpec((EXPERT_ROWS, d), lambda w, ib, ie, fl, sg: (ib[w], 0)),
                pl.BlockSpec((1, d, ff), lambda w, ib, ie, fl, sg: (ie[w], 0, 0)),
                pl.BlockSpec((1, d, ff), lambda w, ib, ie, fl, sg: (ie[w], 0, 0)),
                pl.BlockSpec((1, ff, d), lambda w, ib, ie, fl, sg: (ie[w], 0, 0)),
            ],
            out_specs=pl.BlockSpec((EXPERT_ROWS, d), lambda w, ib, ie, fl, sg: (ib[w], 0)),
            scratch_shapes=[pltpu.VMEM((d, ff), BF16), pltpu.VMEM((d, ff), BF16), pltpu.VMEM((ff, d), BF16)],
        ),
        out_shape=jax.ShapeDtypeStruct((n, d), BF16),
        compiler_params=_cparams(("arbitrary",)),
        name="experts",
    )(item_block, item_expert, item_flags, seg_bounds, xs, wg, wu, wd)


def _work_items(counts, n_rows):
    n_items = n_rows // EXPERT_ROWS + N_EXPERTS
    seg_end = jnp.cumsum(counts)
    seg_start = seg_end - counts
    first_blk = seg_start // EXPERT_ROWS
    n_blk = jnp.where(counts > 0, (seg_end - 1) // EXPERT_ROWS - first_blk + 1, 0)
    item_end = jnp.cumsum(n_blk)
    item_start = item_end - n_blk
    total = item_end[-1]
    w = jnp.arange(n_items, dtype=jnp.int32)
    wc = jnp.minimum(w, total - 1)
    e = jnp.minimum(jnp.sum((item_end[None, :] <= wc[:, None]).astype(jnp.int32), axis=1), N_EXPERTS - 1)
    onehot = (e[:, None] == jnp.arange(N_EXPERTS, dtype=jnp.int32)[None, :]).astype(jnp.int32)
    b = jnp.sum(onehot * (first_blk - item_start)[None, :], axis=1) + wc
    valid = w < total
    n_blocks = n_rows // EXPERT_ROWS
    free_blk = (seg_end[-1] + EXPERT_ROWS - 1) // EXPERT_ROWS + (w - total)
    fill = jnp.logical_not(valid) & (free_blk < n_blocks)
    b = jnp.where(valid, b, jnp.minimum(free_blk, n_blocks - 1)).astype(jnp.int32)
    prev_b = jnp.concatenate([jnp.full((1,), -1, jnp.int32), b[:-1]])
    prev_e = jnp.concatenate([jnp.full((1,), -1, jnp.int32), e[:-1]])
    flags = (valid.astype(jnp.int32) + 2 * (valid & (b != prev_b)).astype(jnp.int32)
             + 4 * (valid & (e != prev_e)).astype(jnp.int32) + 8 * fill.astype(jnp.int32))
    seg_bounds = jnp.concatenate([seg_start, seg_end[-1:]]).astype(jnp.int32)
    return b, e, flags, seg_bounds


def _combine_kernel(gbase_ref, lbase_ref, cnt_ref, rows_ref, pos_ref, gate_ref, x_ref,
                    sg_ref, su_ref, sd_ref, g_ref, beta_ref, yb_ref, o_ref, stage, acc, lane_tile, lane_gate,
                    sem):
    i = pl.program_id(0)
    nt = pl.num_programs(0)
    slot = i % 2
    tn = ROUTE_TILE

    def run_copy(buf):
        def build(run, before, n):
            src = pl.multiple_of(gbase_ref[run] + before, RUN_ALIGN)
            dst = pl.multiple_of(lbase_ref[run] + before, RUN_ALIGN)
            return pltpu.make_async_copy(yb_ref.at[pl.ds(src, n), :], stage.at[buf, pl.ds(dst, n), :],
                                         sem.at[buf])
        return build

    def start_tile(tile, buf):
        build = run_copy(buf)
        _for_each_run_piece(cnt_ref, tile, lambda run, before, n: build(run, before, n).start())

    @pl.when(i == 0)
    def _():
        stage[...] = jnp.zeros_like(stage)
        start_tile(0, 0)

    @pl.when(i + 1 < nt)
    def _():
        start_tile(i + 1, 1 - slot)

    x = x_ref[...]
    xb = x.astype(BF16)
    gte = jnp.dot(xb, sg_ref[...], preferred_element_type=F32)
    up = jnp.dot(xb, su_ref[...], preferred_element_type=F32)
    h = (gte * _sigmoid(gte) * up).astype(BF16)
    acc[...] = jnp.dot(h, sd_ref[...], preferred_element_type=F32)

    def wait_rows(n):
        pltpu.make_async_copy(yb_ref.at[pl.ds(0, n), :], stage.at[slot, pl.ds(0, n), :], sem.at[slot]).wait()

    _for_each_total_piece(rows_ref[i], wait_rows)

    lanes = 128
    lane_iota = lax.broadcasted_iota(jnp.int32, (tn, lanes), 1)
    for k in range(TOP_K):
        p = jnp.broadcast_to(pos_ref[:, k:k + 1], (tn, lanes))
        g = jnp.broadcast_to(gate_ref[:, k:k + 1], (tn, lanes))
        lane_tile[k] = (p >> (lanes.bit_length() - 1)).astype(F32).astype(BF16)
        lane_gate[k] = jnp.where((p & (lanes - 1)) == lane_iota, g, 0.0).astype(BF16)
    zero = jnp.zeros((tn, lanes), BF16)

    def gate_matrix(c):
        cols = []
        for j in range(COMBINE_CHUNK // lanes):
            tile = jnp.asarray(c * (COMBINE_CHUNK // lanes) + j, jnp.int32).astype(F32).astype(BF16)
            w = zero
            for k in range(TOP_K):
                w = w + jnp.where(lane_tile[k] == tile, lane_gate[k], zero)
            cols.append(w)
        return jnp.concatenate(cols, axis=1)

    def weigh_chunk(c, carry):
        p0 = pl.multiple_of(c * COMBINE_CHUNK, COMBINE_CHUNK)
        acc[...] += jnp.dot(gate_matrix(c), stage[slot, pl.ds(p0, COMBINE_CHUNK), :],
                            preferred_element_type=F32)
        return carry

    lax.fori_loop(0, (rows_ref[i] + COMBINE_CHUNK - 1) // COMBINE_CHUNK, weigh_chunk, 0)
    o_ref[...] = _layer_norm(ALPHA * x + acc[...], g_ref[...], beta_ref[...])


def _combine(gbase, lbase, cnt, tile_rows, pos_t, gate_t, x2d, sg, su, sd, g, beta, yb):
    t, d = x2d.shape
    nt = t // ROUTE_TILE
    ff = sg.shape[-1]
    row = lambda i, *_: (i, 0)
    const = lambda i, *_: (0, 0)
    return pl.pallas_call(
        _combine_kernel,
        grid_spec=pltpu.PrefetchScalarGridSpec(
            num_scalar_prefetch=4,
            grid=(nt,),
            in_specs=[
                pl.BlockSpec((ROUTE_TILE, TOP_K), row),
                pl.BlockSpec((ROUTE_TILE, TOP_K), row),
                pl.BlockSpec((ROUTE_TILE, d), row),
                pl.BlockSpec((d, ff), const),
                pl.BlockSpec((d, ff), const),
                pl.BlockSpec((ff, d), const),
                pl.BlockSpec((1, d), const),
                pl.BlockSpec((1, d), const),
                pl.BlockSpec(memory_space=pl.ANY),
            ],
            out_specs=pl.BlockSpec((ROUTE_TILE, d), row),
            scratch_shapes=[pltpu.VMEM((2, TILE_CAP, d), BF16),
                            pltpu.VMEM((ROUTE_TILE, d), F32),
                            pltpu.VMEM((TOP_K, ROUTE_TILE, 128), BF16),
                            pltpu.VMEM((TOP_K, ROUTE_TILE, 128), BF16),
                            pltpu.SemaphoreType.DMA((2,))],
        ),
        out_shape=jax.ShapeDtypeStruct((t, d), F32),
        compiler_params=_cparams(("arbitrary",)),
        name="combine",
    )(gbase, lbase, cnt, tile_rows, pos_t, gate_t, x2d, sg, su, sd, g, beta, yb)


def _run_tables(cnt):
    lbase = jnp.cumsum(cnt, axis=1) - cnt
    per_expert = jnp.sum(cnt, axis=0)
    seg_start = jnp.cumsum(per_expert) - per_expert
    gbase = seg_start[None, :] + jnp.cumsum(cnt, axis=0) - cnt
    flat = lambda a: a.reshape(-1).astype(jnp.int32)
    return flat(gbase), flat(lbase), flat(cnt), jnp.sum(cnt, axis=1).astype(jnp.int32), per_expert


def kernel(x, mem, w_in, b_in, conv_w, conv_b, conv_ln_g, conv_ln_b, attn_sinks, rel_bias, w_out, b_out, ln1_g, ln1_b, xq_w, xkv_w, xo_w, ln2_g, ln2_b, router_w, router_b, exp_gate, exp_up, exp_down, sh_gate, sh_up, sh_down, ln3_g, ln3_b):
    bsz, seq, d = x.shape
    t = bsz * seq
    bias_tab, sink_tab = _band_tables(rel_bias, attn_sinks[0])
    tri = (jnp.arange(ROUTE_TILE)[:, None] <= jnp.arange(ROUTE_TILE)[None, :]).astype(BF16)
    row = lambda p: p.reshape(1, -1)
    for l in range(DEPTH):
        conv_out, q, k, v = _mix(x, w_in[l].astype(BF16), row(b_in[l]), conv_w[l], row(conv_b[l]),
                                 row(conv_ln_g[l]), row(conv_ln_b[l]))
        att = _swa(q, k, v, bias_tab, sink_tab)
        x1 = _outproj(x.reshape(t, d), conv_out.reshape(t, CONV_CH), att.reshape(t, Q_COLS),
                      w_out[l].astype(BF16), row(b_out[l]), row(ln1_g[l]), row(ln1_b[l]))
        kmem, vmem = _memkv(mem, xkv_w[l].astype(BF16))
        x2, logits_t = _cross(x1.reshape(bsz, seq, d), kmem, vmem, xq_w[l].astype(BF16),
                              xo_w[l].astype(BF16), row(ln2_g[l]), row(ln2_b[l]),
                              router_w[l].T.astype(BF16))
        x2 = x2.reshape(t, d)
        pos, gate, cnt = _route(logits_t, router_b[l].reshape(-1, 1), tri)
        nt = t // ROUTE_TILE
        n_rows = -(-(t * TOP_K + nt * N_EXPERTS * (RUN_ALIGN - 1)) // EXPERT_ROWS) * EXPERT_ROWS
        gbase, lbase, cnt_flat, tile_rows, per_expert = _run_tables(cnt[:, :, 0])
        item_block, item_expert, item_flags, seg_bounds = _work_items(per_expert, n_rows)
        xs = _dispatch(gbase, lbase, cnt_flat, tile_rows, seg_bounds[-1:], pos, x2, n_rows)
        yb = _experts(item_block, item_expert, item_flags, seg_bounds, xs,
                      exp_gate[l], exp_up[l], exp_down[l])
        pos_t = jnp.transpose(pos, (0, 2, 1)).reshape(t, TOP_K)
        gate_t = jnp.transpose(gate, (0, 2, 1)).reshape(t, TOP_K)
        x = _combine(gbase, lbase, cnt_flat, tile_rows, pos_t, gate_t, x2, sh_gate[l].astype(BF16),
                     sh_up[l].astype(BF16), sh_down[l].astype(BF16), row(ln3_g[l]), row(ln3_b[l]),
                     yb).reshape(bsz, seq, d)
    return x
```

```python
import functools
import math

import jax
import jax.numpy as jnp
from jax import lax
from jax.experimental import pallas as pl
from jax.experimental.pallas import tpu as pltpu

D_MODEL = 1024
MEM_LEN = 256
HEAD_DIM = 64
CONV_CH = D_MODEL // 2
CONV_WIDTH = 31
ATT_HEADS = 8
KV_HEADS = 2
WINDOW = 128
BLOCK = 128
REL_BUCKETS = 32
REL_MAX_DIST = 128
Q_COLS = ATT_HEADS * HEAD_DIM
KV_COLS = KV_HEADS * HEAD_DIM
IN_COLS = 2 * CONV_CH + Q_COLS + 2 * KV_COLS
X_HEADS = 4
X_HEAD_DIM = D_MODEL // X_HEADS
N_EXPERTS = 64
TOP_K = 8
N_GROUPS = 8
GROUP_SIZE = N_EXPERTS // N_GROUPS
TOPK_GROUPS = 4
EXPERT_FF = D_MODEL // 4
ROUTED_SCALE = 2.5
DEPTH = 1
ALPHA = (2 * DEPTH) ** 0.25
LN_EPS = 1e-5
NEG_INF = -1e30

F32 = jnp.float32
BF16 = jnp.bfloat16

VMEM_LIMIT_BYTES = 56 * 1024 * 1024

ROW_TILE = 512
CONV_ROWS = 32
SUBLANES = 8
CONV_HALO = 32
SWA_TILE = 512
ROUTE_TILE = 512
EXPERT_ROWS = 512
RUN_ALIGN = 16
RUN_ALIGN_LOG2 = RUN_ALIGN.bit_length() - 1
WORD_ROWS = 2
TILE_CAP = ROUTE_TILE * TOP_K + N_EXPERTS * RUN_ALIGN
PIECES_PER_TILE = TILE_CAP // RUN_ALIGN
PIECE_UNROLL = 8
TOTAL_PIECES = tuple(1 << b for b in range(TILE_CAP.bit_length() - 1, RUN_ALIGN_LOG2 - 1, -1))
SORT_CHUNK = 256
COMBINE_CHUNK = 512


def _cparams(sem):
    return pltpu.CompilerParams(dimension_semantics=sem, vmem_limit_bytes=VMEM_LIMIT_BYTES)


def _layer_norm(h, g, b):
    mu = jnp.mean(h, axis=-1, keepdims=True)
    d = h - mu
    var = jnp.mean(d * d, axis=-1, keepdims=True)
    return d * lax.rsqrt(var + LN_EPS) * g + b


def _sigmoid(x):
    return 1.0 / (1.0 + jnp.exp(-x))


def _mix_kernel(x_ref, w_ref, b_ref, cw_ref, cb_ref, cg_ref, cbeta_ref,
                conv_ref, q_ref, k_ref, v_ref, u_ext, u_sh):
    j = pl.program_id(1)
    xb = x_ref[0].astype(BF16)
    proj = jnp.dot(xb, w_ref[...], preferred_element_type=F32) + b_ref[...]
    a = proj[:, :CONV_CH]
    g = proj[:, CONV_CH:2 * CONV_CH]
    q_ref[0] = (proj[:, 2 * CONV_CH:2 * CONV_CH + Q_COLS] * (HEAD_DIM ** -0.5)).astype(BF16)
    k_ref[0] = proj[:, 2 * CONV_CH + Q_COLS:2 * CONV_CH + Q_COLS + KV_COLS].astype(BF16)
    v_ref[0] = proj[:, 2 * CONV_CH + Q_COLS + KV_COLS:].astype(BF16)

    @pl.when(j == 0)
    def _():
        u_ext[0:CONV_HALO, :] = jnp.zeros((CONV_HALO, CONV_CH), F32)

    u_ext[CONV_HALO:CONV_HALO + ROW_TILE, :] = a * _sigmoid(g)

    first_tap = CONV_HALO - (CONV_WIDTH - 1)
    shifted_rows = u_sh.shape[1]
    for r in range(1, SUBLANES):
        u_sh[r - 1] = u_ext[r:r + shifted_rows, :]

    for c in range(ROW_TILE // CONV_ROWS):
        acc = jnp.zeros((CONV_ROWS, CONV_CH), F32) + cb_ref[...]
        for t in range(CONV_WIDTH):
            r = (first_tap + t) % SUBLANES
            base = c * CONV_ROWS + (first_tap + t) - r
            if r == 0:
                taps = u_ext[base:base + CONV_ROWS, :]
            else:
                taps = u_sh[r - 1, base:base + CONV_ROWS, :]
            acc = acc + taps * cw_ref[t:t + 1, :]
        y = _layer_norm(acc, cg_ref[...], cbeta_ref[...])
        conv_ref[0, c * CONV_ROWS:(c + 1) * CONV_ROWS, :] = (y * _sigmoid(y)).astype(BF16)

    u_ext[0:CONV_HALO, :] = u_ext[ROW_TILE:ROW_TILE + CONV_HALO, :]


def _mix(x, w_in, b_in, conv_w, conv_b, conv_g, conv_beta):
    bsz, seq, d = x.shape
    nt = seq // ROW_TILE
    row = lambda b, j: (b, j, 0)
    const2 = lambda b, j: (0, 0)
    return pl.pallas_call(
        _mix_kernel,
        grid=(bsz, nt),
        in_specs=[
            pl.BlockSpec((1, ROW_TILE, d), row),
            pl.BlockSpec((d, IN_COLS), const2),
            pl.BlockSpec((1, IN_COLS), const2),
            pl.BlockSpec((CONV_WIDTH, CONV_CH), const2),
            pl.BlockSpec((1, CONV_CH), const2),
            pl.BlockSpec((1, CONV_CH), const2),
            pl.BlockSpec((1, CONV_CH), const2),
        ],
        out_specs=[
            pl.BlockSpec((1, ROW_TILE, CONV_CH), row),
            pl.BlockSpec((1, ROW_TILE, Q_COLS), row),
            pl.BlockSpec((1, ROW_TILE, KV_COLS), row),
            pl.BlockSpec((1, ROW_TILE, KV_COLS), row),
        ],
        out_shape=[
            jax.ShapeDtypeStruct((bsz, seq, CONV_CH), BF16),
            jax.ShapeDtypeStruct((bsz, seq, Q_COLS), BF16),
            jax.ShapeDtypeStruct((bsz, seq, KV_COLS), BF16),
            jax.ShapeDtypeStruct((bsz, seq, KV_COLS), BF16),
        ],
        scratch_shapes=[pltpu.VMEM((ROW_TILE + CONV_HALO, CONV_CH), F32),
                        pltpu.VMEM((SUBLANES - 1, ROW_TILE + CONV_HALO - SUBLANES, CONV_CH), F32)],
        compiler_params=_cparams(("arbitrary", "arbitrary")),
        name="mix",
    )(x, w_in, b_in, conv_w, conv_b, conv_g, conv_beta)


def _swa_kernel(q_ref, kp_ref, kc_ref, vp_ref, vc_ref, bias_ref, sink_ref, o_ref):
    j = pl.program_id(1)
    rows = BLOCK + SWA_TILE
    lane = lax.broadcasted_iota(jnp.int32, (rows, 2 * HEAD_DIM), 1)
    low = lane < HEAD_DIM

    def placements(prev_ref, cur_ref):
        t = jnp.concatenate([prev_ref[0], cur_ref[0]], axis=0).astype(F32)
        tr = pltpu.roll(t, HEAD_DIM, 1)
        zero = jnp.zeros_like(t)
        kv0_low = jnp.where(low, t, zero).astype(BF16)
        kv1_high = jnp.where(low, zero, t).astype(BF16)
        kv1_low = jnp.where(low, tr, zero).astype(BF16)
        kv0_high = jnp.where(low, zero, tr).astype(BF16)
        return (kv0_low, kv0_high, kv1_low, kv1_high)

    ks = placements(kp_ref, kc_ref)
    vs = placements(vp_ref, vc_ref)
    slab = 2 * HEAD_DIM
    for i in range(SWA_TILE // BLOCK):
        q = q_ref[0, i * BLOCK:(i + 1) * BLOCK, :]
        q_kv0 = jnp.concatenate([q[:, 0:slab], q[:, slab:2 * slab]], axis=0)
        q_kv1 = jnp.concatenate([q[:, 2 * slab:3 * slab], q[:, 3 * slab:4 * slab]], axis=0)
        band = slice(i * BLOCK, (i + 2) * BLOCK)
        outs = []
        for s in range(4):
            qs = q_kv0 if s < 2 else q_kv1
            bias = bias_ref[1, s]
            if i == 0:
                bias = jnp.where(j == 0, bias_ref[0, s], bias)
            logits = lax.dot_general(qs, ks[s][band], (((1,), (1,)), ((), ())),
                                     preferred_element_type=F32) + bias
            sink = sink_ref[s]
            m = jnp.maximum(jnp.max(logits, axis=-1, keepdims=True), sink)
            p = jnp.exp(logits - m)
            den = jnp.sum(p, axis=-1, keepdims=True) + jnp.exp(sink - m)
            o = jnp.dot(p.astype(BF16), vs[s][band], preferred_element_type=F32)
            outs.append(o / den)
        o_kv0 = outs[0] + outs[1]
        o_kv1 = outs[2] + outs[3]
        blk = slice(i * BLOCK, (i + 1) * BLOCK)
        o_ref[0, blk, 0:slab] = o_kv0[0:BLOCK].astype(BF16)
        o_ref[0, blk, slab:2 * slab] = o_kv0[BLOCK:2 * BLOCK].astype(BF16)
        o_ref[0, blk, 2 * slab:3 * slab] = o_kv1[0:BLOCK].astype(BF16)
        o_ref[0, blk, 3 * slab:4 * slab] = o_kv1[BLOCK:2 * BLOCK].astype(BF16)


def _swa(q, k, v, bias_tab, sink_tab):
    bsz, seq, _ = q.shape
    per = SWA_TILE // BLOCK
    cur = lambda b, n: (b, n, 0)
    prev = lambda b, n: (b, jnp.maximum(n * per - 1, 0), 0)
    whole = lambda b, n: (0, 0, 0, 0)
    return pl.pallas_call(
        _swa_kernel,
        grid=(bsz, seq // SWA_TILE),
        in_specs=[
            pl.BlockSpec((1, SWA_TILE, Q_COLS), cur),
            pl.BlockSpec((1, BLOCK, KV_COLS), prev),
            pl.BlockSpec((1, SWA_TILE, KV_COLS), cur),
            pl.BlockSpec((1, BLOCK, KV_COLS), prev),
            pl.BlockSpec((1, SWA_TILE, KV_COLS), cur),
            pl.BlockSpec((2, 4, 2 * BLOCK, 2 * BLOCK), whole),
            pl.BlockSpec((4, 2 * BLOCK, 1), lambda b, n: (0, 0, 0)),
        ],
        out_specs=pl.BlockSpec((1, SWA_TILE, Q_COLS), cur),
        out_shape=jax.ShapeDtypeStruct((bsz, seq, Q_COLS), BF16),
        compiler_params=_cparams(("arbitrary", "arbitrary")),
        name="swa",
    )(q, k, k, v, v, bias_tab, sink_tab)


def _t5_bucket(dist):
    n = jnp.maximum(dist, 0)
    exact = REL_BUCKETS // 2
    large = exact + (jnp.log(jnp.maximum(n, 1).astype(F32) / exact)
                     / math.log(REL_MAX_DIST / exact) * (REL_BUCKETS - exact)).astype(jnp.int32)
    large = jnp.minimum(large, REL_BUCKETS - 1)
    return jnp.where(n < exact, n, large)


def _band_tables(rel_bias, sinks):
    qi = jnp.arange(BLOCK)[:, None]
    kj = jnp.arange(2 * BLOCK)[None, :]
    dist = qi + BLOCK - kj
    bucket = _t5_bucket(dist)
    bias = jnp.zeros((ATT_HEADS, BLOCK, 2 * BLOCK), F32)
    for bkt in range(REL_BUCKETS):
        bias = jnp.where(bucket[None] == bkt, rel_bias[bkt].astype(F32)[:, None, None], bias)
    in_window = (dist >= 0) & (dist < WINDOW)
    masks = jnp.stack([in_window & (kj >= BLOCK), in_window])
    masked = jnp.where(masks[:, None], bias[None], NEG_INF)
    pairs = ((0, 2), (1, 3), (4, 6), (5, 7))
    bias_tab = jnp.stack([jnp.concatenate([masked[:, a], masked[:, b]], axis=1) for a, b in pairs], axis=1)
    s = sinks.astype(F32)
    sink_tab = jnp.stack([jnp.concatenate([jnp.full((BLOCK, 1), s[a]), jnp.full((BLOCK, 1), s[b])], axis=0)
                          for a, b in pairs])
    return bias_tab, sink_tab


def _outproj_kernel(x_ref, conv_ref, att_ref, w_ref, b_ref, g_ref, beta_ref, o_ref):
    mix = jnp.dot(conv_ref[...], w_ref[0:CONV_CH, :], preferred_element_type=F32)
    mix = mix + jnp.dot(att_ref[...], w_ref[CONV_CH:, :], preferred_element_type=F32)
    h = ALPHA * x_ref[...] + mix + b_ref[...]
    o_ref[...] = _layer_norm(h, g_ref[...], beta_ref[...])


def _outproj(x2d, conv2d, att2d, w_out, b_out, g, beta):
    t, d = x2d.shape
    row = lambda i: (i, 0)
    const = lambda i: (0, 0)
    return pl.pallas_call(
        _outproj_kernel,
        grid=(t // ROW_TILE,),
        in_specs=[
            pl.BlockSpec((ROW_TILE, d), row),
            pl.BlockSpec((ROW_TILE, CONV_CH), row),
            pl.BlockSpec((ROW_TILE, Q_COLS), row),
            pl.BlockSpec((d, d), const),
            pl.BlockSpec((1, d), const),
            pl.BlockSpec((1, d), const),
            pl.BlockSpec((1, d), const),
        ],
        out_specs=pl.BlockSpec((ROW_TILE, d), row),
        out_shape=jax.ShapeDtypeStruct((t, d), F32),
        compiler_params=_cparams(("arbitrary",)),
        name="outproj",
    )(x2d, conv2d, att2d, w_out, b_out, g, beta)


def _memkv_kernel(mem_ref, w_ref, k_ref, v_ref):
    kv = jnp.dot(mem_ref[0].astype(BF16), w_ref[...], preferred_element_type=F32)
    k_ref[0] = kv[:, :D_MODEL].astype(BF16)
    v_ref[0] = kv[:, D_MODEL:].astype(BF16)


def _memkv(mem, wkv):
    bsz, m, d = mem.shape
    return pl.pallas_call(
        _memkv_kernel,
        grid=(bsz,),
        in_specs=[pl.BlockSpec((1, m, d), lambda b: (b, 0, 0)),
                  pl.BlockSpec((d, 2 * d), lambda b: (0, 0))],
        out_specs=[pl.BlockSpec((1, m, d), lambda b: (b, 0, 0)),
                   pl.BlockSpec((1, m, d), lambda b: (b, 0, 0))],
        out_shape=[jax.ShapeDtypeStruct((bsz, m, d), BF16)] * 2,
        compiler_params=_cparams(("arbitrary",)),
        name="memkv",
    )(mem, wkv)


def _cross_kernel(x_ref, k_ref, v_ref, wq_ref, wo_ref, g_ref, beta_ref, rw_ref, o_ref, lt_ref):
    x = x_ref[0]
    q = jnp.dot(x.astype(BF16), wq_ref[...], preferred_element_type=F32) * (X_HEAD_DIM ** -0.5)
    q = q.astype(BF16)
    heads = []
    for h in range(X_HEADS):
        cols = slice(h * X_HEAD_DIM, (h + 1) * X_HEAD_DIM)
        logits = lax.dot_general(q[:, cols], k_ref[0, :, cols], (((1,), (1,)), ((), ())),
                                 preferred_element_type=F32)
        m = jnp.max(logits, axis=-1, keepdims=True)
        p = jnp.exp(logits - m)
        den = jnp.sum(p, axis=-1, keepdims=True)
        o = jnp.dot(p.astype(BF16), v_ref[0, :, cols], preferred_element_type=F32)
        heads.append((o / den).astype(BF16))
    att = jnp.concatenate(heads, axis=-1)
    cross = jnp.dot(att, wo_ref[...], preferred_element_type=F32)
    y = _layer_norm(ALPHA * x + cross, g_ref[...], beta_ref[...])
    o_ref[0] = y
    lt_ref[...] = lax.dot_general(rw_ref[...], y.astype(BF16), (((1,), (1,)), ((), ())),
                                  preferred_element_type=F32)


def _cross(x1, kmem, vmem, wq, wo, g, beta, rw_t):
    bsz, seq, d = x1.shape
    nt = seq // ROW_TILE
    row = lambda b, j: (b, j, 0)
    mem = lambda b, j: (b, 0, 0)
    const = lambda b, j: (0, 0)
    return pl.pallas_call(
        _cross_kernel,
        grid=(bsz, nt),
        in_specs=[
            pl.BlockSpec((1, ROW_TILE, d), row),
            pl.BlockSpec((1, MEM_LEN, d), mem),
            pl.BlockSpec((1, MEM_LEN, d), mem),
            pl.BlockSpec((d, d), const),
            pl.BlockSpec((d, d), const),
            pl.BlockSpec((1, d), const),
            pl.BlockSpec((1, d), const),
            pl.BlockSpec((N_EXPERTS, d), const),
        ],
        out_specs=[
            pl.BlockSpec((1, ROW_TILE, d), row),
            pl.BlockSpec((N_EXPERTS, ROW_TILE), lambda b, j: (0, b * nt + j)),
        ],
        out_shape=[
            jax.ShapeDtypeStruct((bsz, seq, d), F32),
            jax.ShapeDtypeStruct((N_EXPERTS, bsz * seq), F32),
        ],
        compiler_params=_cparams(("arbitrary", "arbitrary")),
        name="cross",
    )(x1, kmem, vmem, wq, wo, g, beta, rw_t)


def _route_kernel(lt_ref, rb_ref, tri_ref, pos_ref, gate_ref, cnt_ref):
    tn = ROUTE_TILE
    scores = _sigmoid(lt_ref[...])
    choice = scores + rb_ref[...]

    gscore = []
    member = lax.broadcasted_iota(jnp.int32, (GROUP_SIZE, tn), 0).astype(F32)
    for g in range(N_GROUPS):
        c = choice[g * GROUP_SIZE:(g + 1) * GROUP_SIZE, :]
        m1 = jnp.max(c, axis=0, keepdims=True)
        first = jnp.min(jnp.where(c == m1, member, float(GROUP_SIZE)), axis=0, keepdims=True)
        m2 = jnp.max(jnp.where(member == first, -jnp.inf, c), axis=0, keepdims=True)
        gscore.append(m1 + m2)

    keep_rows = []
    for g in range(N_GROUPS):
        beaten = jnp.zeros((1, tn), F32)
        for o in range(N_GROUPS):
            if o == g:
                continue
            ahead = (gscore[o] >= gscore[g]) if o < g else (gscore[o] > gscore[g])
            beaten = beaten + jnp.where(ahead, 1.0, 0.0)
        keep_rows.append(jnp.broadcast_to(beaten, (GROUP_SIZE, tn)))
    beaten_all = jnp.concatenate(keep_rows, axis=0)

    masked = jnp.where(beaten_all < TOPK_GROUPS, choice, -jnp.inf)
    eidx = lax.broadcasted_iota(jnp.int32, (N_EXPERTS, tn), 0).astype(F32)
    sel = jnp.zeros((N_EXPERTS, tn), F32)
    picks, weights = [], []
    for r in range(TOP_K):
        mx = jnp.max(masked, axis=0, keepdims=True)
        first = jnp.min(jnp.where(masked == mx, eidx, float(N_EXPERTS)), axis=0, keepdims=True)
        pick = eidx == first
        picks.append((pick, first))
        weights.append(jnp.sum(jnp.where(pick, scores, 0.0), axis=0, keepdims=True))
        masked = jnp.where(pick, -jnp.inf, masked)
        sel = jnp.where(pick, 1.0, sel)

    wsum = weights[0]
    for r in range(1, TOP_K):
        wsum = wsum + weights[r]

    count = jnp.sum(sel, axis=1, keepdims=True)
    run_len = jnp.floor((count + (RUN_ALIGN - 1.0)) * (1.0 / RUN_ALIGN)) * RUN_ALIGN
    run_len_b = jnp.broadcast_to(run_len, (N_EXPERTS, 128))
    er = lax.broadcasted_iota(jnp.int32, (N_EXPERTS, N_EXPERTS), 0)
    ec = lax.broadcasted_iota(jnp.int32, (N_EXPERTS, N_EXPERTS), 1)
    before = jnp.where(ec < er, 1.0, 0.0).astype(BF16)
    run_start = jnp.dot(before, run_len_b.astype(BF16), preferred_element_type=F32)[:, 0:1]
    incl = jnp.dot(sel.astype(BF16), tri_ref[...], preferred_element_type=F32)
    pos_mat = run_start + incl - sel
    cnt_ref[0] = run_len_b.astype(jnp.int32)

    for r in range(TOP_K):
        pick, _ = picks[r]
        pos_ref[0, r:r + 1, :] = jnp.sum(jnp.where(pick, pos_mat, 0.0), axis=0,
                                         keepdims=True).astype(jnp.int32)
        gate_ref[0, r:r + 1, :] = weights[r] / wsum * ROUTED_SCALE


def _route(logits_t, router_b, tri):
    e, t = logits_t.shape
    nt = t // ROUTE_TILE
    blk = lambda i: (i, 0, 0)
    return pl.pallas_call(
        _route_kernel,
        grid=(nt,),
        in_specs=[
            pl.BlockSpec((e, ROUTE_TILE), lambda i: (0, i)),
            pl.BlockSpec((e, 1), lambda i: (0, 0)),
            pl.BlockSpec((ROUTE_TILE, ROUTE_TILE), lambda i: (0, 0)),
        ],
        out_specs=[
            pl.BlockSpec((1, TOP_K, ROUTE_TILE), blk),
            pl.BlockSpec((1, TOP_K, ROUTE_TILE), blk),
            pl.BlockSpec((1, e, 128), blk),
        ],
        out_shape=[
            jax.ShapeDtypeStruct((nt, TOP_K, ROUTE_TILE), jnp.int32),
            jax.ShapeDtypeStruct((nt, TOP_K, ROUTE_TILE), F32),
            jax.ShapeDtypeStruct((nt, e, 128), jnp.int32),
        ],
        compiler_params=_cparams(("arbitrary",)),
        name="route",
    )(logits_t, router_b, tri)


def _for_each_piece(piece_ref, rows_ref, tile, fn):
    n_rows = rows_ref[tile]
    group_rows = PIECE_UNROLL * RUN_ALIGN

    def body(g, carry):
        for u in range(PIECE_UNROLL):
            j = g * PIECE_UNROLL + u
            local = pl.multiple_of(j * RUN_ALIGN, RUN_ALIGN)

            @pl.when(local < n_rows)
            def _():
                fn(local, pl.multiple_of(piece_ref[tile * PIECES_PER_TILE + j], RUN_ALIGN))
        return carry

    lax.fori_loop(0, (n_rows + group_rows - 1) // group_rows, body, 0)


def _word_rows(start, size):
    if isinstance(start, int):
        first = start // WORD_ROWS
    else:
        first = pl.multiple_of(start >> (WORD_ROWS.bit_length() - 1), RUN_ALIGN // WORD_ROWS)
    return pl.ds(first, size // WORD_ROWS)


def _for_each_total_piece(total, fn):
    for piece in TOTAL_PIECES:
        @pl.when((total & piece) != 0)
        def _():
            fn(piece)


def _dispatch_kernel(piece_ref, rows_ref, total_ref, pos_ref, x_ref, xs_ref, stage, zeros, sem, zsem):
    i = pl.program_id(0)
    nt = pl.num_programs(0)
    slot = i % 2
    tn = ROUTE_TILE

    def drain(tile, buf):
        def wait_rows(n):
            pltpu.make_async_copy(stage.at[buf, _word_rows(0, n), :], xs_ref.at[_word_rows(0, n), :],
                                  sem.at[buf]).wait()
        _for_each_total_piece(rows_ref[tile], wait_rows)

    @pl.when(i >= 2)
    def _():
        drain(i - 2, slot)

    xb = x_ref[...].astype(BF16)
    pos = pos_ref[0]
    row_iota = lax.broadcasted_iota(jnp.int32, (SORT_CHUNK, tn), 0).astype(F32).astype(BF16)
    one = jnp.ones((SORT_CHUNK, tn), BF16)
    zero = jnp.zeros((SORT_CHUNK, tn), BF16)

    def onehot_rows(c):
        rel = jnp.clip(pos - c * SORT_CHUNK, -1, SORT_CHUNK).astype(F32).astype(BF16)
        hit = rel[0:1, :] == row_iota
        for k in range(1, TOP_K):
            hit = hit | (rel[k:k + 1, :] == row_iota)
        return jnp.where(hit, one, zero)

    def sort_chunks(c2, carry):
        for c in (2 * c2, 2 * c2 + 1):
            p0 = pl.multiple_of(c * SORT_CHUNK, SORT_CHUNK)
            rows = jnp.dot(onehot_rows(c), xb, preferred_element_type=F32)
            stage[slot, _word_rows(p0, SORT_CHUNK), :] = pltpu.bitcast(rows.astype(BF16), jnp.int32)
        return carry

    lax.fori_loop(0, (rows_ref[i] + 2 * SORT_CHUNK - 1) // (2 * SORT_CHUNK), sort_chunks, 0)

    def start_piece(local, dst):
        pltpu.make_async_copy(stage.at[slot, _word_rows(local, RUN_ALIGN), :],
                              xs_ref.at[_word_rows(dst, RUN_ALIGN), :], sem.at[slot]).start()

    _for_each_piece(piece_ref, rows_ref, i, start_piece)

    @pl.when(i == nt - 1)
    def _():
        @pl.when(nt >= 2)
        def _():
            drain(i - 1, 1 - slot)

        drain(i, slot)

        zeros[...] = jnp.zeros_like(zeros)
        total = total_ref[0]
        tail = (-total) & (EXPERT_ROWS - 1)
        for piece in TOTAL_PIECES:
            if piece >= EXPERT_ROWS:
                continue

            @pl.when((tail & piece) != 0)
            def _():
                dst = total + (tail & (-2 * piece))
                cp = pltpu.make_async_copy(zeros.at[_word_rows(0, piece), :],
                                           xs_ref.at[_word_rows(dst, piece), :], zsem)
                cp.start()
                cp.wait()

        first_free = (total + tail) // EXPERT_ROWS

        def zero_block(b):
            return pltpu.make_async_copy(zeros, xs_ref.at[_word_rows(b * EXPERT_ROWS, EXPERT_ROWS), :], zsem)

        n_blocks = xs_ref.shape[0] * WORD_ROWS // EXPERT_ROWS
        lax.fori_loop(first_free, n_blocks, lambda b, c: (zero_block(b).start(), c)[1], 0)
        lax.fori_loop(first_free, n_blocks, lambda b, c: (zero_block(b).wait(), c)[1], 0)


def _dispatch(piece_dst, tile_rows, total, pos, x2d, n_rows):
    t, d = x2d.shape
    nt = t // ROUTE_TILE
    return pl.pallas_call(
        _dispatch_kernel,
        grid_spec=pltpu.PrefetchScalarGridSpec(
            num_scalar_prefetch=3,
            grid=(nt,),
            in_specs=[pl.BlockSpec((1, TOP_K, ROUTE_TILE), lambda i, *_: (i, 0, 0)),
                      pl.BlockSpec((ROUTE_TILE, d), lambda i, *_: (i, 0))],
            out_specs=pl.BlockSpec(memory_space=pl.ANY),
            scratch_shapes=[pltpu.VMEM((2, TILE_CAP // WORD_ROWS, d), jnp.int32),
                            pltpu.VMEM((EXPERT_ROWS // WORD_ROWS, d), jnp.int32),
                            pltpu.SemaphoreType.DMA((2,)),
                            pltpu.SemaphoreType.DMA(())],
        ),
        out_shape=jax.ShapeDtypeStruct((n_rows // WORD_ROWS, d), jnp.int32),
        compiler_params=_cparams(("arbitrary",)),
        name="dispatch",
    )(piece_dst, tile_rows, total, pos, x2d)


def _experts_kernel(ib_ref, ie_ref, flag_ref, seg_ref, xs_ref, wg_ref, wu_ref, wd_ref, y_ref,
                    wg_bf, wu_bf, wd_bf):
    w = pl.program_id(0)
    flags = flag_ref[w]
    valid = (flags & 1) != 0
    first_of_block = (flags & 2) != 0
    new_expert = (flags & 4) != 0

    @pl.when((flags & 8) != 0)
    def _():
        y_ref[...] = jnp.zeros_like(y_ref)

    @pl.when(new_expert)
    def _():
        wg_bf[...] = wg_ref[0].astype(BF16)
        wu_bf[...] = wu_ref[0].astype(BF16)
        wd_bf[...] = wd_ref[0].astype(BF16)

    @pl.when(valid)
    def _():
        e = ie_ref[w]
        x = pltpu.bitcast(xs_ref[...], BF16)
        gte = jnp.dot(x, wg_bf[...], preferred_element_type=F32)
        up = jnp.dot(x, wu_bf[...], preferred_element_type=F32)
        h = (gte * _sigmoid(gte) * up).astype(BF16)
        y = jnp.dot(h, wd_bf[...], preferred_element_type=F32).astype(BF16)
        rows = ib_ref[w] * EXPERT_ROWS + lax.broadcasted_iota(jnp.int32, (EXPERT_ROWS, 1), 0)
        mine = (rows >= seg_ref[e]) & (rows < seg_ref[e + 1])

        @pl.when(first_of_block)
        def _():
            y_ref[...] = pltpu.bitcast(jnp.where(mine, y, jnp.zeros_like(y)), jnp.int32)

        @pl.when(jnp.logical_not(first_of_block))
        def _():
            y_ref[...] = pltpu.bitcast(jnp.where(mine, y, pltpu.bitcast(y_ref[...], BF16)), jnp.int32)


def _experts(item_block, item_expert, item_flags, seg_bounds, xs, wg, wu, wd):
    n, d = xs.shape
    n_items = item_block.shape[0]
    ff = wg.shape[-1]
    return pl.pallas_call(
        _experts_kernel,
        grid_spec=pltpu.PrefetchScalarGridSpec(
            num_scalar_prefetch=4,
            grid=(n_items,),
            in_specs=[
                pl.BlockSpec((EXPERT_ROWS // WORD_ROWS, d), lambda w, ib, ie, fl, sg: (ib[w], 0)),
                pl.BlockSpec((1, d, ff), lambda w, ib, ie, fl, sg: (ie[w], 0, 0)),
                pl.BlockSpec((1, d, ff), lambda w, ib, ie, fl, sg: (ie[w], 0, 0)),
                pl.BlockSpec((1, ff, d), lambda w, ib, ie, fl, sg: (ie[w], 0, 0)),
            ],
            out_specs=pl.BlockSpec((EXPERT_ROWS // WORD_ROWS, d), lambda w, ib, ie, fl, sg: (ib[w], 0)),
            scratch_shapes=[pltpu.VMEM((d, ff), BF16), pltpu.VMEM((d, ff), BF16), pltpu.VMEM((ff, d), BF16)],
        ),
        out_shape=jax.ShapeDtypeStruct((n, d), jnp.int32),
        compiler_params=_cparams(("arbitrary",)),
        name="experts",
    )(item_block, item_expert, item_flags, seg_bounds, xs, wg, wu, wd)


def _work_items(counts, n_rows):
    n_items = n_rows // EXPERT_ROWS + N_EXPERTS
    seg_end = jnp.cumsum(counts)
    seg_start = seg_end - counts
    first_blk = seg_start // EXPERT_ROWS
    n_blk = jnp.where(counts > 0, (seg_end - 1) // EXPERT_ROWS - first_blk + 1, 0)
    item_end = jnp.cumsum(n_blk)
    item_start = item_end - n_blk
    total = item_end[-1]
    w = jnp.arange(n_items, dtype=jnp.int32)
    wc = jnp.minimum(w, total - 1)
    e = jnp.minimum(jnp.sum((item_end[None, :] <= wc[:, None]).astype(jnp.int32), axis=1), N_EXPERTS - 1)
    onehot = (e[:, None] == jnp.arange(N_EXPERTS, dtype=jnp.int32)[None, :]).astype(jnp.int32)
    b = jnp.sum(onehot * (first_blk - item_start)[None, :], axis=1) + wc
    valid = w < total
    n_blocks = n_rows // EXPERT_ROWS
    free_blk = (seg_end[-1] + EXPERT_ROWS - 1) // EXPERT_ROWS + (w - total)
    fill = jnp.logical_not(valid) & (free_blk < n_blocks)
    b = jnp.where(valid, b, jnp.minimum(free_blk, n_blocks - 1)).astype(jnp.int32)
    prev_b = jnp.concatenate([jnp.full((1,), -1, jnp.int32), b[:-1]])
    prev_e = jnp.concatenate([jnp.full((1,), -1, jnp.int32), e[:-1]])
    flags = (valid.astype(jnp.int32) + 2 * (valid & (b != prev_b)).astype(jnp.int32)
             + 4 * (valid & (e != prev_e)).astype(jnp.int32) + 8 * fill.astype(jnp.int32))
    seg_bounds = jnp.concatenate([seg_start, seg_end[-1:]]).astype(jnp.int32)
    return b, e, flags, seg_bounds


def _combine_kernel(piece_ref, rows_ref, pos_ref, gate_ref, x_ref, sg_ref, su_ref, sd_ref, g_ref, beta_ref,
                    yb_ref, o_ref, stage, acc, lane_tile, lane_gate, sem):
    i = pl.program_id(0)
    nt = pl.num_programs(0)
    slot = i % 2
    tn = ROUTE_TILE

    def start_tile(tile, buf):
        def start_piece(local, src):
            pltpu.make_async_copy(yb_ref.at[_word_rows(src, RUN_ALIGN), :],
                                  stage.at[buf, _word_rows(local, RUN_ALIGN), :], sem.at[buf]).start()
        _for_each_piece(piece_ref, rows_ref, tile, start_piece)

    @pl.when(i == 0)
    def _():
        stage[...] = jnp.zeros_like(stage)
        start_tile(0, 0)

    @pl.when(i + 1 < nt)
    def _():
        start_tile(i + 1, 1 - slot)

    x = x_ref[...]
    xb = x.astype(BF16)
    gte = jnp.dot(xb, sg_ref[...], preferred_element_type=F32)
    up = jnp.dot(xb, su_ref[...], preferred_element_type=F32)
    h = (gte * _sigmoid(gte) * up).astype(BF16)
    acc[...] = jnp.dot(h, sd_ref[...], preferred_element_type=F32)

    def wait_rows(n):
        pltpu.make_async_copy(yb_ref.at[_word_rows(0, n), :], stage.at[slot, _word_rows(0, n), :],
                              sem.at[slot]).wait()

    _for_each_total_piece(rows_ref[i], wait_rows)

    lanes = 128
    lane_iota = lax.broadcasted_iota(jnp.int32, (tn, lanes), 1)
    for k in range(TOP_K):
        p = jnp.broadcast_to(pos_ref[:, k:k + 1], (tn, lanes))
        g = jnp.broadcast_to(gate_ref[:, k:k + 1], (tn, lanes))
        lane_tile[k] = (p >> (lanes.bit_length() - 1)).astype(F32).astype(BF16)
        lane_gate[k] = jnp.where((p & (lanes - 1)) == lane_iota, g, 0.0).astype(BF16)
    zero = jnp.zeros((tn, lanes), BF16)

    def gate_matrix(c):
        cols = []
        for j in range(COMBINE_CHUNK // lanes):
            tile = jnp.asarray(c * (COMBINE_CHUNK // lanes) + j, jnp.int32).astype(F32).astype(BF16)
            w = zero
            for k in range(TOP_K):
                w = w + jnp.where(lane_tile[k] == tile, lane_gate[k], zero)
            cols.append(w)
        return jnp.concatenate(cols, axis=1)

    def weigh_chunks(c2, carry):
        part = []
        for c in (2 * c2, 2 * c2 + 1):
            p0 = pl.multiple_of(c * COMBINE_CHUNK, COMBINE_CHUNK)
            rows = pltpu.bitcast(stage[slot, _word_rows(p0, COMBINE_CHUNK), :], BF16)
            part.append(jnp.dot(gate_matrix(c), rows, preferred_element_type=F32))
        acc[...] += part[0] + part[1]
        return carry

    lax.fori_loop(0, (rows_ref[i] + 2 * COMBINE_CHUNK - 1) // (2 * COMBINE_CHUNK), weigh_chunks, 0)
    o_ref[...] = _layer_norm(ALPHA * x + acc[...], g_ref[...], beta_ref[...])


def _combine(piece_src, tile_rows, pos_t, gate_t, x2d, sg, su, sd, g, beta, yb):
    t, d = x2d.shape
    nt = t // ROUTE_TILE
    ff = sg.shape[-1]
    row = lambda i, *_: (i, 0)
    const = lambda i, *_: (0, 0)
    return pl.pallas_call(
        _combine_kernel,
        grid_spec=pltpu.PrefetchScalarGridSpec(
            num_scalar_prefetch=2,
            grid=(nt,),
            in_specs=[
                pl.BlockSpec((ROUTE_TILE, TOP_K), row),
                pl.BlockSpec((ROUTE_TILE, TOP_K), row),
                pl.BlockSpec((ROUTE_TILE, d), row),
                pl.BlockSpec((d, ff), const),
                pl.BlockSpec((d, ff), const),
                pl.BlockSpec((ff, d), const),
                pl.BlockSpec((1, d), const),
                pl.BlockSpec((1, d), const),
                pl.BlockSpec(memory_space=pl.ANY),
            ],
            out_specs=pl.BlockSpec((ROUTE_TILE, d), row),
            scratch_shapes=[pltpu.VMEM((2, TILE_CAP // WORD_ROWS, d), jnp.int32),
                            pltpu.VMEM((ROUTE_TILE, d), F32),
                            pltpu.VMEM((TOP_K, ROUTE_TILE, 128), BF16),
                            pltpu.VMEM((TOP_K, ROUTE_TILE, 128), BF16),
                            pltpu.SemaphoreType.DMA((2,))],
        ),
        out_shape=jax.ShapeDtypeStruct((t, d), F32),
        compiler_params=_cparams(("arbitrary",)),
        name="combine",
    )(piece_src, tile_rows, pos_t, gate_t, x2d, sg, su, sd, g, beta, yb)


def _piece_table(cnt):
    lbase = jnp.cumsum(cnt, axis=1) - cnt
    per_expert = jnp.sum(cnt, axis=0)
    seg_start = jnp.cumsum(per_expert) - per_expert
    gbase = seg_start[None, :] + jnp.cumsum(cnt, axis=0) - cnt
    local = (jnp.arange(PIECES_PER_TILE, dtype=jnp.int32) * RUN_ALIGN)[None, :, None]
    in_run = (local >= lbase[:, None, :]) & (local < (lbase + cnt)[:, None, :])
    piece = jnp.sum(jnp.where(in_run, gbase[:, None, :] + local - lbase[:, None, :], 0), axis=-1)
    return piece.reshape(-1).astype(jnp.int32), jnp.sum(cnt, axis=1).astype(jnp.int32), per_expert


def kernel(x, mem, w_in, b_in, conv_w, conv_b, conv_ln_g, conv_ln_b, attn_sinks, rel_bias, w_out, b_out, ln1_g, ln1_b, xq_w, xkv_w, xo_w, ln2_g, ln2_b, router_w, router_b, exp_gate, exp_up, exp_down, sh_gate, sh_up, sh_down, ln3_g, ln3_b):
    bsz, seq, d = x.shape
    t = bsz * seq
    bias_tab, sink_tab = _band_tables(rel_bias, attn_sinks[0])
    tri = (jnp.arange(ROUTE_TILE)[:, None] <= jnp.arange(ROUTE_TILE)[None, :]).astype(BF16)
    row = lambda p: p.reshape(1, -1)
    for l in range(DEPTH):
        conv_out, q, k, v = _mix(x, w_in[l].astype(BF16), row(b_in[l]), conv_w[l], row(conv_b[l]),
                                 row(conv_ln_g[l]), row(conv_ln_b[l]))
        att = _swa(q, k, v, bias_tab, sink_tab)
        x1 = _outproj(x.reshape(t, d), conv_out.reshape(t, CONV_CH), att.reshape(t, Q_COLS),
                      w_out[l].astype(BF16), row(b_out[l]), row(ln1_g[l]), row(ln1_b[l]))
        kmem, vmem = _memkv(mem, xkv_w[l].astype(BF16))
        x2, logits_t = _cross(x1.reshape(bsz, seq, d), kmem, vmem, xq_w[l].astype(BF16),
                              xo_w[l].astype(BF16), row(ln2_g[l]), row(ln2_b[l]),
                              router_w[l].T.astype(BF16))
        x2 = x2.reshape(t, d)
        pos, gate, cnt = _route(logits_t, router_b[l].reshape(-1, 1), tri)
        nt = t // ROUTE_TILE
        n_rows = -(-(t * TOP_K + nt * N_EXPERTS * (RUN_ALIGN - 1)) // EXPERT_ROWS) * EXPERT_ROWS
        pieces, tile_rows, per_expert = _piece_table(cnt[:, :, 0])
        item_block, item_expert, item_flags, seg_bounds = _work_items(per_expert, n_rows)
        xs = _dispatch(pieces, tile_rows, seg_bounds[-1:], pos, x2, n_rows)
        yb = _experts(item_block, item_expert, item_flags, seg_bounds, xs,
                      exp_gate[l], exp_up[l], exp_down[l])
        pos_t = jnp.transpose(pos, (0, 2, 1)).reshape(t, TOP_K)
        gate_t = jnp.transpose(gate, (0, 2, 1)).reshape(t, TOP_K)
        x = _combine(pieces, tile_rows, pos_t, gate_t, x2, sh_gate[l].astype(BF16),
                     sh_up[l].astype(BF16), sh_down[l].astype(BF16), row(ln3_g[l]), row(ln3_b[l]),
                     yb).reshape(bsz, seq, d)
    return x
```

```python
import functools
import math

import jax
import jax.numpy as jnp
from jax import lax
from jax.experimental import pallas as pl
from jax.experimental.pallas import tpu as pltpu

D_MODEL = 1024
MEM_LEN = 256
HEAD_DIM = 64
CONV_CH = D_MODEL // 2
CONV_WIDTH = 31
ATT_HEADS = 8
KV_HEADS = 2
WINDOW = 128
BLOCK = 128
REL_BUCKETS = 32
REL_MAX_DIST = 128
Q_COLS = ATT_HEADS * HEAD_DIM
KV_COLS = KV_HEADS * HEAD_DIM
IN_COLS = 2 * CONV_CH + Q_COLS + 2 * KV_COLS
X_HEADS = 4
X_HEAD_DIM = D_MODEL // X_HEADS
N_EXPERTS = 64
TOP_K = 8
N_GROUPS = 8
GROUP_SIZE = N_EXPERTS // N_GROUPS
TOPK_GROUPS = 4
EXPERT_FF = D_MODEL // 4
ROUTED_SCALE = 2.5
DEPTH = 1
ALPHA = (2 * DEPTH) ** 0.25
LN_EPS = 1e-5
NEG_INF = -1e30

F32 = jnp.float32
BF16 = jnp.bfloat16

VMEM_LIMIT_BYTES = 56 * 1024 * 1024

ROW_TILE = 512
CONV_ROWS = 32
SUBLANES = 8
CONV_HALO = 32
SWA_TILE = 512
ROUTE_TILE = 512
EXPERT_ROWS = 512
RUN_ALIGN = 16
RUN_ALIGN_LOG2 = RUN_ALIGN.bit_length() - 1
WORD_ROWS = 2
TILE_CAP = ROUTE_TILE * TOP_K + N_EXPERTS * RUN_ALIGN
BIG_PIECE = 2 * RUN_ALIGN
BIG_PER_TILE = TILE_CAP // BIG_PIECE
PIECE_UNROLL = 4
TOTAL_PIECES = tuple(1 << b for b in range(TILE_CAP.bit_length() - 1, RUN_ALIGN_LOG2 - 1, -1))
SORT_CHUNK = 256
COMBINE_CHUNK = 512


def _cparams(sem):
    return pltpu.CompilerParams(dimension_semantics=sem, vmem_limit_bytes=VMEM_LIMIT_BYTES)


def _layer_norm(h, g, b):
    mu = jnp.mean(h, axis=-1, keepdims=True)
    d = h - mu
    var = jnp.mean(d * d, axis=-1, keepdims=True)
    return d * lax.rsqrt(var + LN_EPS) * g + b


def _sigmoid(x):
    return 1.0 / (1.0 + jnp.exp(-x))


def _mix_kernel(x_ref, w_ref, b_ref, cw_ref, cb_ref, cg_ref, cbeta_ref,
                conv_ref, q_ref, k_ref, v_ref, u_ext, u_sh):
    j = pl.program_id(1)
    xb = x_ref[0].astype(BF16)
    proj = jnp.dot(xb, w_ref[...], preferred_element_type=F32) + b_ref[...]
    a = proj[:, :CONV_CH]
    g = proj[:, CONV_CH:2 * CONV_CH]
    q_ref[0] = (proj[:, 2 * CONV_CH:2 * CONV_CH + Q_COLS] * (HEAD_DIM ** -0.5)).astype(BF16)
    k_ref[0] = proj[:, 2 * CONV_CH + Q_COLS:2 * CONV_CH + Q_COLS + KV_COLS].astype(BF16)
    v_ref[0] = proj[:, 2 * CONV_CH + Q_COLS + KV_COLS:].astype(BF16)

    @pl.when(j == 0)
    def _():
        u_ext[0:CONV_HALO, :] = jnp.zeros((CONV_HALO, CONV_CH), F32)

    u_ext[CONV_HALO:CONV_HALO + ROW_TILE, :] = a * _sigmoid(g)

    first_tap = CONV_HALO - (CONV_WIDTH - 1)
    shifted_rows = u_sh.shape[1]
    for r in range(1, SUBLANES):
        u_sh[r - 1] = u_ext[r:r + shifted_rows, :]

    for c in range(ROW_TILE // CONV_ROWS):
        acc = jnp.zeros((CONV_ROWS, CONV_CH), F32) + cb_ref[...]
        for t in range(CONV_WIDTH):
            r = (first_tap + t) % SUBLANES
            base = c * CONV_ROWS + (first_tap + t) - r
            if r == 0:
                taps = u_ext[base:base + CONV_ROWS, :]
            else:
                taps = u_sh[r - 1, base:base + CONV_ROWS, :]
            acc = acc + taps * cw_ref[t:t + 1, :]
        y = _layer_norm(acc, cg_ref[...], cbeta_ref[...])
        conv_ref[0, c * CONV_ROWS:(c + 1) * CONV_ROWS, :] = (y * _sigmoid(y)).astype(BF16)

    u_ext[0:CONV_HALO, :] = u_ext[ROW_TILE:ROW_TILE + CONV_HALO, :]


def _mix(x, w_in, b_in, conv_w, conv_b, conv_g, conv_beta):
    bsz, seq, d = x.shape
    nt = seq // ROW_TILE
    row = lambda b, j: (b, j, 0)
    const2 = lambda b, j: (0, 0)
    return pl.pallas_call(
        _mix_kernel,
        grid=(bsz, nt),
        in_specs=[
            pl.BlockSpec((1, ROW_TILE, d), row),
            pl.BlockSpec((d, IN_COLS), const2),
            pl.BlockSpec((1, IN_COLS), const2),
            pl.BlockSpec((CONV_WIDTH, CONV_CH), const2),
            pl.BlockSpec((1, CONV_CH), const2),
            pl.BlockSpec((1, CONV_CH), const2),
            pl.BlockSpec((1, CONV_CH), const2),
        ],
        out_specs=[
            pl.BlockSpec((1, ROW_TILE, CONV_CH), row),
            pl.BlockSpec((1, ROW_TILE, Q_COLS), row),
            pl.BlockSpec((1, ROW_TILE, KV_COLS), row),
            pl.BlockSpec((1, ROW_TILE, KV_COLS), row),
        ],
        out_shape=[
            jax.ShapeDtypeStruct((bsz, seq, CONV_CH), BF16),
            jax.ShapeDtypeStruct((bsz, seq, Q_COLS), BF16),
            jax.ShapeDtypeStruct((bsz, seq, KV_COLS), BF16),
            jax.ShapeDtypeStruct((bsz, seq, KV_COLS), BF16),
        ],
        scratch_shapes=[pltpu.VMEM((ROW_TILE + CONV_HALO, CONV_CH), F32),
                        pltpu.VMEM((SUBLANES - 1, ROW_TILE + CONV_HALO - SUBLANES, CONV_CH), F32)],
        compiler_params=_cparams(("arbitrary", "arbitrary")),
        name="mix",
    )(x, w_in, b_in, conv_w, conv_b, conv_g, conv_beta)


def _swa_kernel(q_ref, kp_ref, kc_ref, vp_ref, vc_ref, bias_ref, sink_ref, o_ref):
    j = pl.program_id(1)
    rows = BLOCK + SWA_TILE
    lane = lax.broadcasted_iota(jnp.int32, (rows, 2 * HEAD_DIM), 1)
    low = lane < HEAD_DIM

    def placements(prev_ref, cur_ref):
        t = jnp.concatenate([prev_ref[0], cur_ref[0]], axis=0).astype(F32)
        tr = pltpu.roll(t, HEAD_DIM, 1)
        zero = jnp.zeros_like(t)
        kv0_low = jnp.where(low, t, zero).astype(BF16)
        kv1_high = jnp.where(low, zero, t).astype(BF16)
        kv1_low = jnp.where(low, tr, zero).astype(BF16)
        kv0_high = jnp.where(low, zero, tr).astype(BF16)
        return (kv0_low, kv0_high, kv1_low, kv1_high)

    ks = placements(kp_ref, kc_ref)
    vs = placements(vp_ref, vc_ref)
    slab = 2 * HEAD_DIM
    for i in range(SWA_TILE // BLOCK):
        q = q_ref[0, i * BLOCK:(i + 1) * BLOCK, :]
        q_kv0 = jnp.concatenate([q[:, 0:slab], q[:, slab:2 * slab]], axis=0)
        q_kv1 = jnp.concatenate([q[:, 2 * slab:3 * slab], q[:, 3 * slab:4 * slab]], axis=0)
        band = slice(i * BLOCK, (i + 2) * BLOCK)
        outs = []
        for s in range(4):
            qs = q_kv0 if s < 2 else q_kv1
            bias = bias_ref[1, s]
            if i == 0:
                bias = jnp.where(j == 0, bias_ref[0, s], bias)
            logits = lax.dot_general(qs, ks[s][band], (((1,), (1,)), ((), ())),
                                     preferred_element_type=F32) + bias
            sink = sink_ref[s]
            m = jnp.maximum(jnp.max(logits, axis=-1, keepdims=True), sink)
            p = jnp.exp(logits - m)
            den = jnp.sum(p, axis=-1, keepdims=True) + jnp.exp(sink - m)
            o = jnp.dot(p.astype(BF16), vs[s][band], preferred_element_type=F32)
            outs.append(o / den)
        o_kv0 = outs[0] + outs[1]
        o_kv1 = outs[2] + outs[3]
        blk = slice(i * BLOCK, (i + 1) * BLOCK)
        o_ref[0, blk, 0:slab] = o_kv0[0:BLOCK].astype(BF16)
        o_ref[0, blk, slab:2 * slab] = o_kv0[BLOCK:2 * BLOCK].astype(BF16)
        o_ref[0, blk, 2 * slab:3 * slab] = o_kv1[0:BLOCK].astype(BF16)
        o_ref[0, blk, 3 * slab:4 * slab] = o_kv1[BLOCK:2 * BLOCK].astype(BF16)


def _swa(q, k, v, bias_tab, sink_tab):
    bsz, seq, _ = q.shape
    per = SWA_TILE // BLOCK
    cur = lambda b, n: (b, n, 0)
    prev = lambda b, n: (b, jnp.maximum(n * per - 1, 0), 0)
    whole = lambda b, n: (0, 0, 0, 0)
    return pl.pallas_call(
        _swa_kernel,
        grid=(bsz, seq // SWA_TILE),
        in_specs=[
            pl.BlockSpec((1, SWA_TILE, Q_COLS), cur),
            pl.BlockSpec((1, BLOCK, KV_COLS), prev),
            pl.BlockSpec((1, SWA_TILE, KV_COLS), cur),
            pl.BlockSpec((1, BLOCK, KV_COLS), prev),
            pl.BlockSpec((1, SWA_TILE, KV_COLS), cur),
            pl.BlockSpec((2, 4, 2 * BLOCK, 2 * BLOCK), whole),
            pl.BlockSpec((4, 2 * BLOCK, 1), lambda b, n: (0, 0, 0)),
        ],
        out_specs=pl.BlockSpec((1, SWA_TILE, Q_COLS), cur),
        out_shape=jax.ShapeDtypeStruct((bsz, seq, Q_COLS), BF16),
        compiler_params=_cparams(("arbitrary", "arbitrary")),
        name="swa",
    )(q, k, k, v, v, bias_tab, sink_tab)


def _t5_bucket(dist):
    n = jnp.maximum(dist, 0)
    exact = REL_BUCKETS // 2
    large = exact + (jnp.log(jnp.maximum(n, 1).astype(F32) / exact)
                     / math.log(REL_MAX_DIST / exact) * (REL_BUCKETS - exact)).astype(jnp.int32)
    large = jnp.minimum(large, REL_BUCKETS - 1)
    return jnp.where(n < exact, n, large)


def _band_tables(rel_bias, sinks):
    qi = jnp.arange(BLOCK)[:, None]
    kj = jnp.arange(2 * BLOCK)[None, :]
    dist = qi + BLOCK - kj
    bucket = _t5_bucket(dist)
    bias = jnp.zeros((ATT_HEADS, BLOCK, 2 * BLOCK), F32)
    for bkt in range(REL_BUCKETS):
        bias = jnp.where(bucket[None] == bkt, rel_bias[bkt].astype(F32)[:, None, None], bias)
    in_window = (dist >= 0) & (dist < WINDOW)
    masks = jnp.stack([in_window & (kj >= BLOCK), in_window])
    masked = jnp.where(masks[:, None], bias[None], NEG_INF)
    pairs = ((0, 2), (1, 3), (4, 6), (5, 7))
    bias_tab = jnp.stack([jnp.concatenate([masked[:, a], masked[:, b]], axis=1) for a, b in pairs], axis=1)
    s = sinks.astype(F32)
    sink_tab = jnp.stack([jnp.concatenate([jnp.full((BLOCK, 1), s[a]), jnp.full((BLOCK, 1), s[b])], axis=0)
                          for a, b in pairs])
    return bias_tab, sink_tab


def _outproj_kernel(x_ref, conv_ref, att_ref, w_ref, b_ref, g_ref, beta_ref, o_ref):
    mix = jnp.dot(conv_ref[...], w_ref[0:CONV_CH, :], preferred_element_type=F32)
    mix = mix + jnp.dot(att_ref[...], w_ref[CONV_CH:, :], preferred_element_type=F32)
    h = ALPHA * x_ref[...] + mix + b_ref[...]
    o_ref[...] = _layer_norm(h, g_ref[...], beta_ref[...])


def _outproj(x2d, conv2d, att2d, w_out, b_out, g, beta):
    t, d = x2d.shape
    row = lambda i: (i, 0)
    const = lambda i: (0, 0)
    return pl.pallas_call(
        _outproj_kernel,
        grid=(t // ROW_TILE,),
        in_specs=[
            pl.BlockSpec((ROW_TILE, d), row),
            pl.BlockSpec((ROW_TILE, CONV_CH), row),
            pl.BlockSpec((ROW_TILE, Q_COLS), row),
            pl.BlockSpec((d, d), const),
            pl.BlockSpec((1, d), const),
            pl.BlockSpec((1, d), const),
            pl.BlockSpec((1, d), const),
        ],
        out_specs=pl.BlockSpec((ROW_TILE, d), row),
        out_shape=jax.ShapeDtypeStruct((t, d), F32),
        compiler_params=_cparams(("arbitrary",)),
        name="outproj",
    )(x2d, conv2d, att2d, w_out, b_out, g, beta)


def _memkv_kernel(mem_ref, w_ref, k_ref, v_ref):
    kv = jnp.dot(mem_ref[0].astype(BF16), w_ref[...], preferred_element_type=F32)
    k_ref[0] = kv[:, :D_MODEL].astype(BF16)
    v_ref[0] = kv[:, D_MODEL:].astype(BF16)


def _memkv(mem, wkv):
    bsz, m, d = mem.shape
    return pl.pallas_call(
        _memkv_kernel,
        grid=(bsz,),
        in_specs=[pl.BlockSpec((1, m, d), lambda b: (b, 0, 0)),
                  pl.BlockSpec((d, 2 * d), lambda b: (0, 0))],
        out_specs=[pl.BlockSpec((1, m, d), lambda b: (b, 0, 0)),
                   pl.BlockSpec((1, m, d), lambda b: (b, 0, 0))],
        out_shape=[jax.ShapeDtypeStruct((bsz, m, d), BF16)] * 2,
        compiler_params=_cparams(("arbitrary",)),
        name="memkv",
    )(mem, wkv)


def _cross_kernel(x_ref, k_ref, v_ref, wq_ref, wo_ref, g_ref, beta_ref, rw_ref, o_ref, lt_ref):
    x = x_ref[0]
    q = jnp.dot(x.astype(BF16), wq_ref[...], preferred_element_type=F32) * (X_HEAD_DIM ** -0.5)
    q = q.astype(BF16)
    heads = []
    for h in range(X_HEADS):
        cols = slice(h * X_HEAD_DIM, (h + 1) * X_HEAD_DIM)
        logits = lax.dot_general(q[:, cols], k_ref[0, :, cols], (((1,), (1,)), ((), ())),
                                 preferred_element_type=F32)
        m = jnp.max(logits, axis=-1, keepdims=True)
        p = jnp.exp(logits - m)
        den = jnp.sum(p, axis=-1, keepdims=True)
        o = jnp.dot(p.astype(BF16), v_ref[0, :, cols], preferred_element_type=F32)
        heads.append((o / den).astype(BF16))
    att = jnp.concatenate(heads, axis=-1)
    cross = jnp.dot(att, wo_ref[...], preferred_element_type=F32)
    y = _layer_norm(ALPHA * x + cross, g_ref[...], beta_ref[...])
    o_ref[0] = y
    lt_ref[...] = lax.dot_general(rw_ref[...], y.astype(BF16), (((1,), (1,)), ((), ())),
                                  preferred_element_type=F32)


def _cross(x1, kmem, vmem, wq, wo, g, beta, rw_t):
    bsz, seq, d = x1.shape
    nt = seq // ROW_TILE
    row = lambda b, j: (b, j, 0)
    mem = lambda b, j: (b, 0, 0)
    const = lambda b, j: (0, 0)
    return pl.pallas_call(
        _cross_kernel,
        grid=(bsz, nt),
        in_specs=[
            pl.BlockSpec((1, ROW_TILE, d), row),
            pl.BlockSpec((1, MEM_LEN, d), mem),
            pl.BlockSpec((1, MEM_LEN, d), mem),
            pl.BlockSpec((d, d), const),
            pl.BlockSpec((d, d), const),
            pl.BlockSpec((1, d), const),
            pl.BlockSpec((1, d), const),
            pl.BlockSpec((N_EXPERTS, d), const),
        ],
        out_specs=[
            pl.BlockSpec((1, ROW_TILE, d), row),
            pl.BlockSpec((N_EXPERTS, ROW_TILE), lambda b, j: (0, b * nt + j)),
        ],
        out_shape=[
            jax.ShapeDtypeStruct((bsz, seq, d), F32),
            jax.ShapeDtypeStruct((N_EXPERTS, bsz * seq), F32),
        ],
        compiler_params=_cparams(("arbitrary", "arbitrary")),
        name="cross",
    )(x1, kmem, vmem, wq, wo, g, beta, rw_t)


def _route_kernel(lt_ref, rb_ref, tri_ref, pos_ref, gate_ref, cnt_ref):
    tn = ROUTE_TILE
    scores = _sigmoid(lt_ref[...])
    choice = scores + rb_ref[...]

    gscore = []
    member = lax.broadcasted_iota(jnp.int32, (GROUP_SIZE, tn), 0).astype(F32)
    for g in range(N_GROUPS):
        c = choice[g * GROUP_SIZE:(g + 1) * GROUP_SIZE, :]
        m1 = jnp.max(c, axis=0, keepdims=True)
        first = jnp.min(jnp.where(c == m1, member, float(GROUP_SIZE)), axis=0, keepdims=True)
        m2 = jnp.max(jnp.where(member == first, -jnp.inf, c), axis=0, keepdims=True)
        gscore.append(m1 + m2)

    keep_rows = []
    for g in range(N_GROUPS):
        beaten = jnp.zeros((1, tn), F32)
        for o in range(N_GROUPS):
            if o == g:
                continue
            ahead = (gscore[o] >= gscore[g]) if o < g else (gscore[o] > gscore[g])
            beaten = beaten + jnp.where(ahead, 1.0, 0.0)
        keep_rows.append(jnp.broadcast_to(beaten, (GROUP_SIZE, tn)))
    beaten_all = jnp.concatenate(keep_rows, axis=0)

    masked = jnp.where(beaten_all < TOPK_GROUPS, choice, -jnp.inf)
    eidx = lax.broadcasted_iota(jnp.int32, (N_EXPERTS, tn), 0).astype(F32)
    sel = jnp.zeros((N_EXPERTS, tn), F32)
    picks, weights = [], []
    for r in range(TOP_K):
        mx = jnp.max(masked, axis=0, keepdims=True)
        first = jnp.min(jnp.where(masked == mx, eidx, float(N_EXPERTS)), axis=0, keepdims=True)
        pick = eidx == first
        picks.append((pick, first))
        weights.append(jnp.sum(jnp.where(pick, scores, 0.0), axis=0, keepdims=True))
        masked = jnp.where(pick, -jnp.inf, masked)
        sel = jnp.where(pick, 1.0, sel)

    wsum = weights[0]
    for r in range(1, TOP_K):
        wsum = wsum + weights[r]

    count = jnp.sum(sel, axis=1, keepdims=True)
    run_len = jnp.floor((count + (RUN_ALIGN - 1.0)) * (1.0 / RUN_ALIGN)) * RUN_ALIGN
    run_len_b = jnp.broadcast_to(run_len, (N_EXPERTS, 128))
    er = lax.broadcasted_iota(jnp.int32, (N_EXPERTS, N_EXPERTS), 0)
    ec = lax.broadcasted_iota(jnp.int32, (N_EXPERTS, N_EXPERTS), 1)
    before = jnp.where(ec < er, 1.0, 0.0).astype(BF16)
    run_start = jnp.dot(before, run_len_b.astype(BF16), preferred_element_type=F32)[:, 0:1]
    incl = jnp.dot(sel.astype(BF16), tri_ref[...], preferred_element_type=F32)
    pos_mat = run_start + incl - sel
    cnt_ref[0] = run_len_b.astype(jnp.int32)

    for r in range(TOP_K):
        pick, _ = picks[r]
        pos_ref[0, r:r + 1, :] = jnp.sum(jnp.where(pick, pos_mat, 0.0), axis=0,
                                         keepdims=True).astype(jnp.int32)
        gate_ref[0, r:r + 1, :] = weights[r] / wsum * ROUTED_SCALE


def _route(logits_t, router_b, tri):
    e, t = logits_t.shape
    nt = t // ROUTE_TILE
    blk = lambda i: (i, 0, 0)
    return pl.pallas_call(
        _route_kernel,
        grid=(nt,),
        in_specs=[
            pl.BlockSpec((e, ROUTE_TILE), lambda i: (0, i)),
            pl.BlockSpec((e, 1), lambda i: (0, 0)),
            pl.BlockSpec((ROUTE_TILE, ROUTE_TILE), lambda i: (0, 0)),
        ],
        out_specs=[
            pl.BlockSpec((1, TOP_K, ROUTE_TILE), blk),
            pl.BlockSpec((1, TOP_K, ROUTE_TILE), blk),
            pl.BlockSpec((1, e, 128), blk),
        ],
        out_shape=[
            jax.ShapeDtypeStruct((nt, TOP_K, ROUTE_TILE), jnp.int32),
            jax.ShapeDtypeStruct((nt, TOP_K, ROUTE_TILE), F32),
            jax.ShapeDtypeStruct((nt, e, 128), jnp.int32),
        ],
        compiler_params=_cparams(("arbitrary",)),
        name="route",
    )(logits_t, router_b, tri)


def _for_each_piece(tables, tile, fn):
    for n_rows, per_tile, (local_ref, global_ref, count_ref) in (
            (BIG_PIECE, BIG_PER_TILE, tables[0:3]), (RUN_ALIGN, N_EXPERTS, tables[3:6])):
        count = count_ref[tile]

        def body(g, carry, n_rows=n_rows, per_tile=per_tile, local_ref=local_ref, global_ref=global_ref,
                 count=count):
            for u in range(PIECE_UNROLL):
                j = g * PIECE_UNROLL + u

                @pl.when(j < count)
                def _():
                    fn(pl.multiple_of(local_ref[tile * per_tile + j], RUN_ALIGN),
                       pl.multiple_of(global_ref[tile * per_tile + j], RUN_ALIGN), n_rows)
            return carry

        lax.fori_loop(0, (count + PIECE_UNROLL - 1) // PIECE_UNROLL, body, 0)


def _word_rows(start, size):
    if isinstance(start, int):
        first = start // WORD_ROWS
    else:
        first = pl.multiple_of(start >> (WORD_ROWS.bit_length() - 1), RUN_ALIGN // WORD_ROWS)
    return pl.ds(first, size // WORD_ROWS)


def _for_each_total_piece(total, fn):
    for piece in TOTAL_PIECES:
        @pl.when((total & piece) != 0)
        def _():
            fn(piece)


def _dispatch_kernel(*refs):
    tables, (rows_ref, total_ref, pos_ref, x_ref, xs_ref, stage, zeros, sem, zsem) = refs[:6], refs[6:]
    i = pl.program_id(0)
    nt = pl.num_programs(0)
    slot = i % 2
    tn = ROUTE_TILE

    def drain(tile, buf):
        def wait_rows(n):
            pltpu.make_async_copy(stage.at[buf, _word_rows(0, n), :], xs_ref.at[_word_rows(0, n), :],
                                  sem.at[buf]).wait()
        _for_each_total_piece(rows_ref[tile], wait_rows)

    @pl.when(i >= 2)
    def _():
        drain(i - 2, slot)

    xb = x_ref[...].astype(BF16)
    pos = pos_ref[0]
    row_iota = lax.broadcasted_iota(jnp.int32, (SORT_CHUNK, tn), 0).astype(F32).astype(BF16)
    one = jnp.ones((SORT_CHUNK, tn), BF16)
    zero = jnp.zeros((SORT_CHUNK, tn), BF16)

    def onehot_rows(c):
        rel = jnp.clip(pos - c * SORT_CHUNK, -1, SORT_CHUNK).astype(F32).astype(BF16)
        hit = rel[0:1, :] == row_iota
        for k in range(1, TOP_K):
            hit = hit | (rel[k:k + 1, :] == row_iota)
        return jnp.where(hit, one, zero)

    def sort_chunks(c2, carry):
        for c in (2 * c2, 2 * c2 + 1):
            p0 = pl.multiple_of(c * SORT_CHUNK, SORT_CHUNK)
            rows = jnp.dot(onehot_rows(c), xb, preferred_element_type=F32)
            stage[slot, _word_rows(p0, SORT_CHUNK), :] = pltpu.bitcast(rows.astype(BF16), jnp.int32)
        return carry

    lax.fori_loop(0, (rows_ref[i] + 2 * SORT_CHUNK - 1) // (2 * SORT_CHUNK), sort_chunks, 0)

    def start_piece(local, dst, n):
        pltpu.make_async_copy(stage.at[slot, _word_rows(local, n), :], xs_ref.at[_word_rows(dst, n), :],
                              sem.at[slot]).start()

    _for_each_piece(tables, i, start_piece)

    @pl.when(i == nt - 1)
    def _():
        @pl.when(nt >= 2)
        def _():
            drain(i - 1, 1 - slot)

        drain(i, slot)

        zeros[...] = jnp.zeros_like(zeros)
        total = total_ref[0]
        tail = (-total) & (EXPERT_ROWS - 1)
        for piece in TOTAL_PIECES:
            if piece >= EXPERT_ROWS:
                continue

            @pl.when((tail & piece) != 0)
            def _():
                dst = total + (tail & (-2 * piece))
                cp = pltpu.make_async_copy(zeros.at[_word_rows(0, piece), :],
                                           xs_ref.at[_word_rows(dst, piece), :], zsem)
                cp.start()
                cp.wait()

        first_free = (total + tail) // EXPERT_ROWS

        def zero_block(b):
            return pltpu.make_async_copy(zeros, xs_ref.at[_word_rows(b * EXPERT_ROWS, EXPERT_ROWS), :], zsem)

        n_blocks = xs_ref.shape[0] * WORD_ROWS // EXPERT_ROWS
        lax.fori_loop(first_free, n_blocks, lambda b, c: (zero_block(b).start(), c)[1], 0)
        lax.fori_loop(first_free, n_blocks, lambda b, c: (zero_block(b).wait(), c)[1], 0)


def _dispatch(pieces, tile_rows, total, pos, x2d, n_rows):
    t, d = x2d.shape
    nt = t // ROUTE_TILE
    return pl.pallas_call(
        _dispatch_kernel,
        grid_spec=pltpu.PrefetchScalarGridSpec(
            num_scalar_prefetch=len(pieces) + 2,
            grid=(nt,),
            in_specs=[pl.BlockSpec((1, TOP_K, ROUTE_TILE), lambda i, *_: (i, 0, 0)),
                      pl.BlockSpec((ROUTE_TILE, d), lambda i, *_: (i, 0))],
            out_specs=pl.BlockSpec(memory_space=pl.ANY),
            scratch_shapes=[pltpu.VMEM((2, TILE_CAP // WORD_ROWS, d), jnp.int32),
                            pltpu.VMEM((EXPERT_ROWS // WORD_ROWS, d), jnp.int32),
                            pltpu.SemaphoreType.DMA((2,)),
                            pltpu.SemaphoreType.DMA(())],
        ),
        out_shape=jax.ShapeDtypeStruct((n_rows // WORD_ROWS, d), jnp.int32),
        compiler_params=_cparams(("arbitrary",)),
        name="dispatch",
    )(*pieces, tile_rows, total, pos, x2d)


def _experts_kernel(ib_ref, ie_ref, flag_ref, seg_ref, xs_ref, wg_ref, wu_ref, wd_ref, y_ref,
                    wg_bf, wu_bf, wd_bf):
    w = pl.program_id(0)
    flags = flag_ref[w]
    valid = (flags & 1) != 0
    first_of_block = (flags & 2) != 0
    new_expert = (flags & 4) != 0

    @pl.when((flags & 8) != 0)
    def _():
        y_ref[...] = jnp.zeros_like(y_ref)

    @pl.when(new_expert)
    def _():
        wg_bf[...] = wg_ref[0].astype(BF16)
        wu_bf[...] = wu_ref[0].astype(BF16)
        wd_bf[...] = wd_ref[0].astype(BF16)

    @pl.when(valid)
    def _():
        e = ie_ref[w]
        x = pltpu.bitcast(xs_ref[...], BF16)
        gte = jnp.dot(x, wg_bf[...], preferred_element_type=F32)
        up = jnp.dot(x, wu_bf[...], preferred_element_type=F32)
        h = (gte * _sigmoid(gte) * up).astype(BF16)
        y = jnp.dot(h, wd_bf[...], preferred_element_type=F32).astype(BF16)
        rows = ib_ref[w] * EXPERT_ROWS + lax.broadcasted_iota(jnp.int32, (EXPERT_ROWS, 1), 0)
        mine = (rows >= seg_ref[e]) & (rows < seg_ref[e + 1])

        @pl.when(first_of_block)
        def _():
            y_ref[...] = pltpu.bitcast(jnp.where(mine, y, jnp.zeros_like(y)), jnp.int32)

        @pl.when(jnp.logical_not(first_of_block))
        def _():
            y_ref[...] = pltpu.bitcast(jnp.where(mine, y, pltpu.bitcast(y_ref[...], BF16)), jnp.int32)


def _experts(item_block, item_expert, item_flags, seg_bounds, xs, wg, wu, wd):
    n, d = xs.shape
    n_items = item_block.shape[0]
    ff = wg.shape[-1]
    return pl.pallas_call(
        _experts_kernel,
        grid_spec=pltpu.PrefetchScalarGridSpec(
            num_scalar_prefetch=4,
            grid=(n_items,),
            in_specs=[
                pl.BlockSpec((EXPERT_ROWS // WORD_ROWS, d), lambda w, ib, ie, fl, sg: (ib[w], 0)),
                pl.BlockSpec((1, d, ff), lambda w, ib, ie, fl, sg: (ie[w], 0, 0)),
                pl.BlockSpec((1, d, ff), lambda w, ib, ie, fl, sg: (ie[w], 0, 0)),
                pl.BlockSpec((1, ff, d), lambda w, ib, ie, fl, sg: (ie[w], 0, 0)),
            ],
            out_specs=pl.BlockSpec((EXPERT_ROWS // WORD_ROWS, d), lambda w, ib, ie, fl, sg: (ib[w], 0)),
            scratch_shapes=[pltpu.VMEM((d, ff), BF16), pltpu.VMEM((d, ff), BF16), pltpu.VMEM((ff, d), BF16)],
        ),
        out_shape=jax.ShapeDtypeStruct((n, d), jnp.int32),
        compiler_params=_cparams(("arbitrary",)),
        name="experts",
    )(item_block, item_expert, item_flags, seg_bounds, xs, wg, wu, wd)


def _work_items(counts, n_rows):
    n_items = n_rows // EXPERT_ROWS + N_EXPERTS
    seg_end = jnp.cumsum(counts)
    seg_start = seg_end - counts
    first_blk = seg_start // EXPERT_ROWS
    n_blk = jnp.where(counts > 0, (seg_end - 1) // EXPERT_ROWS - first_blk + 1, 0)
    item_end = jnp.cumsum(n_blk)
    item_start = item_end - n_blk
    total = item_end[-1]
    w = jnp.arange(n_items, dtype=jnp.int32)
    wc = jnp.minimum(w, total - 1)
    e = jnp.minimum(jnp.sum((item_end[None, :] <= wc[:, None]).astype(jnp.int32), axis=1), N_EXPERTS - 1)
    onehot = (e[:, None] == jnp.arange(N_EXPERTS, dtype=jnp.int32)[None, :]).astype(jnp.int32)
    b = jnp.sum(onehot * (first_blk - item_start)[None, :], axis=1) + wc
    valid = w < total
    n_blocks = n_rows // EXPERT_ROWS
    free_blk = (seg_end[-1] + EXPERT_ROWS - 1) // EXPERT_ROWS + (w - total)
    fill = jnp.logical_not(valid) & (free_blk < n_blocks)
    b = jnp.where(valid, b, jnp.minimum(free_blk, n_blocks - 1)).astype(jnp.int32)
    prev_b = jnp.concatenate([jnp.full((1,), -1, jnp.int32), b[:-1]])
    prev_e = jnp.concatenate([jnp.full((1,), -1, jnp.int32), e[:-1]])
    flags = (valid.astype(jnp.int32) + 2 * (valid & (b != prev_b)).astype(jnp.int32)
             + 4 * (valid & (e != prev_e)).astype(jnp.int32) + 8 * fill.astype(jnp.int32))
    seg_bounds = jnp.concatenate([seg_start, seg_end[-1:]]).astype(jnp.int32)
    return b, e, flags, seg_bounds


def _combine_kernel(*refs):
    tables = refs[:6]
    (rows_ref, pos_ref, gate_ref, x_ref, sg_ref, su_ref, sd_ref, g_ref, beta_ref, yb_ref, o_ref,
     stage, acc, lane_tile, lane_gate, sem) = refs[6:]
    i = pl.program_id(0)
    nt = pl.num_programs(0)
    slot = i % 2
    tn = ROUTE_TILE

    def start_tile(tile, buf):
        def start_piece(local, src, n):
            pltpu.make_async_copy(yb_ref.at[_word_rows(src, n), :], stage.at[buf, _word_rows(local, n), :],
                                  sem.at[buf]).start()
        _for_each_piece(tables, tile, start_piece)

    @pl.when(i == 0)
    def _():
        stage[...] = jnp.zeros_like(stage)
        start_tile(0, 0)

    @pl.when(i + 1 < nt)
    def _():
        start_tile(i + 1, 1 - slot)

    x = x_ref[...]
    xb = x.astype(BF16)
    gte = jnp.dot(xb, sg_ref[...], preferred_element_type=F32)
    up = jnp.dot(xb, su_ref[...], preferred_element_type=F32)
    h = (gte * _sigmoid(gte) * up).astype(BF16)
    acc[...] = jnp.dot(h, sd_ref[...], preferred_element_type=F32)

    def wait_rows(n):
        pltpu.make_async_copy(yb_ref.at[_word_rows(0, n), :], stage.at[slot, _word_rows(0, n), :],
                              sem.at[slot]).wait()

    _for_each_total_piece(rows_ref[i], wait_rows)

    lanes = 128
    lane_iota = lax.broadcasted_iota(jnp.int32, (tn, lanes), 1)
    for k in range(TOP_K):
        p = jnp.broadcast_to(pos_ref[:, k:k + 1], (tn, lanes))
        g = jnp.broadcast_to(gate_ref[:, k:k + 1], (tn, lanes))
        lane_tile[k] = (p >> (lanes.bit_length() - 1)).astype(F32).astype(BF16)
        lane_gate[k] = jnp.where((p & (lanes - 1)) == lane_iota, g, 0.0).astype(BF16)
    zero = jnp.zeros((tn, lanes), BF16)

    def gate_matrix(c):
        cols = []
        for j in range(COMBINE_CHUNK // lanes):
            tile = jnp.asarray(c * (COMBINE_CHUNK // lanes) + j, jnp.int32).astype(F32).astype(BF16)
            w = zero
            for k in range(TOP_K):
                w = w + jnp.where(lane_tile[k] == tile, lane_gate[k], zero)
            cols.append(w)
        return jnp.concatenate(cols, axis=1)

    def weigh_chunks(c2, carry):
        part = []
        for c in (2 * c2, 2 * c2 + 1):
            p0 = pl.multiple_of(c * COMBINE_CHUNK, COMBINE_CHUNK)
            rows = pltpu.bitcast(stage[slot, _word_rows(p0, COMBINE_CHUNK), :], BF16)
            part.append(jnp.dot(gate_matrix(c), rows, preferred_element_type=F32))
        acc[...] += part[0] + part[1]
        return carry

    lax.fori_loop(0, (rows_ref[i] + 2 * COMBINE_CHUNK - 1) // (2 * COMBINE_CHUNK), weigh_chunks, 0)
    o_ref[...] = _layer_norm(ALPHA * x + acc[...], g_ref[...], beta_ref[...])


def _combine(pieces, tile_rows, pos_t, gate_t, x2d, sg, su, sd, g, beta, yb):
    t, d = x2d.shape
    nt = t // ROUTE_TILE
    ff = sg.shape[-1]
    row = lambda i, *_: (i, 0)
    const = lambda i, *_: (0, 0)
    return pl.pallas_call(
        _combine_kernel,
        grid_spec=pltpu.PrefetchScalarGridSpec(
            num_scalar_prefetch=len(pieces) + 1,
            grid=(nt,),
            in_specs=[
                pl.BlockSpec((ROUTE_TILE, TOP_K), row),
                pl.BlockSpec((ROUTE_TILE, TOP_K), row),
                pl.BlockSpec((ROUTE_TILE, d), row),
                pl.BlockSpec((d, ff), const),
                pl.BlockSpec((d, ff), const),
                pl.BlockSpec((ff, d), const),
                pl.BlockSpec((1, d), const),
                pl.BlockSpec((1, d), const),
                pl.BlockSpec(memory_space=pl.ANY),
            ],
            out_specs=pl.BlockSpec((ROUTE_TILE, d), row),
            scratch_shapes=[pltpu.VMEM((2, TILE_CAP // WORD_ROWS, d), jnp.int32),
                            pltpu.VMEM((ROUTE_TILE, d), F32),
                            pltpu.VMEM((TOP_K, ROUTE_TILE, 128), BF16),
                            pltpu.VMEM((TOP_K, ROUTE_TILE, 128), BF16),
                            pltpu.SemaphoreType.DMA((2,))],
        ),
        out_shape=jax.ShapeDtypeStruct((t, d), F32),
        compiler_params=_cparams(("arbitrary",)),
        name="combine",
    )(*pieces, tile_rows, pos_t, gate_t, x2d, sg, su, sd, g, beta, yb)


def _piece_tables(cnt):
    lbase = jnp.cumsum(cnt, axis=1) - cnt
    per_expert = jnp.sum(cnt, axis=0)
    seg_start = jnp.cumsum(per_expert) - per_expert
    gbase = seg_start[None, :] + jnp.cumsum(cnt, axis=0) - cnt

    def listed(n_pieces, start_off, piece_rows, length):
        last = jnp.cumsum(n_pieces, axis=1)
        first = last - n_pieces
        j = jnp.arange(length, dtype=jnp.int32)[None, :, None]
        own = (j >= first[:, None, :]) & (j < last[:, None, :])
        off = (start_off[:, None, :] + (j - first[:, None, :]) * piece_rows)
        pick = lambda base: jnp.sum(jnp.where(own, base[:, None, :] + off, 0), axis=-1).reshape(-1).astype(jnp.int32)
        return pick(lbase), pick(gbase), last[:, -1].astype(jnp.int32)

    n_big = cnt // BIG_PIECE
    big = listed(n_big, jnp.zeros_like(cnt), BIG_PIECE, BIG_PER_TILE)
    small = listed((cnt // RUN_ALIGN) % 2, n_big * BIG_PIECE, RUN_ALIGN, N_EXPERTS)
    return big + small, jnp.sum(cnt, axis=1).astype(jnp.int32), per_expert


def kernel(x, mem, w_in, b_in, conv_w, conv_b, conv_ln_g, conv_ln_b, attn_sinks, rel_bias, w_out, b_out, ln1_g, ln1_b, xq_w, xkv_w, xo_w, ln2_g, ln2_b, router_w, router_b, exp_gate, exp_up, exp_down, sh_gate, sh_up, sh_down, ln3_g, ln3_b):
    bsz, seq, d = x.shape
    t = bsz * seq
    bias_tab, sink_tab = _band_tables(rel_bias, attn_sinks[0])
    tri = (jnp.arange(ROUTE_TILE)[:, None] <= jnp.arange(ROUTE_TILE)[None, :]).astype(BF16)
    row = lambda p: p.reshape(1, -1)
    for l in range(DEPTH):
        conv_out, q, k, v = _mix(x, w_in[l].astype(BF16), row(b_in[l]), conv_w[l], row(conv_b[l]),
                                 row(conv_ln_g[l]), row(conv_ln_b[l]))
        att = _swa(q, k, v, bias_tab, sink_tab)
        x1 = _outproj(x.reshape(t, d), conv_out.reshape(t, CONV_CH), att.reshape(t, Q_COLS),
                      w_out[l].astype(BF16), row(b_out[l]), row(ln1_g[l]), row(ln1_b[l]))
        kmem, vmem = _memkv(mem, xkv_w[l].astype(BF16))
        x2, logits_t = _cross(x1.reshape(bsz, seq, d), kmem, vmem, xq_w[l].astype(BF16),
                              xo_w[l].astype(BF16), row(ln2_g[l]), row(ln2_b[l]),
                              router_w[l].T.astype(BF16))
        x2 = x2.reshape(t, d)
        pos, gate, cnt = _route(logits_t, router_b[l].reshape(-1, 1), tri)
        nt = t // ROUTE_TILE
        n_rows = -(-(t * TOP_K + nt * N_EXPERTS * (RUN_ALIGN - 1)) // EXPERT_ROWS) * EXPERT_ROWS
        pieces, tile_rows, per_expert = _piece_tables(cnt[:, :, 0])
        item_block, item_expert, item_flags, seg_bounds = _work_items(per_expert, n_rows)
        xs = _dispatch(pieces, tile_rows, seg_bounds[-1:], pos, x2, n_rows)
        yb = _experts(item_block, item_expert, item_flags, seg_bounds, xs,
                      exp_gate[l], exp_up[l], exp_down[l])
        pos_t = jnp.transpose(pos, (0, 2, 1)).reshape(t, TOP_K)
        gate_t = jnp.transpose(gate, (0, 2, 1)).reshape(t, TOP_K)
        x = _combine(pieces, tile_rows, pos_t, gate_t, x2, sh_gate[l].astype(BF16),
                     sh_up[l].astype(BF16), sh_down[l].astype(BF16), row(ln3_g[l]), row(ln3_b[l]),
                     yb).reshape(bsz, seq, d)
    return x
```

```python
import functools
import math

import jax
import jax.numpy as jnp
from jax import lax
from jax.experimental import pallas as pl
from jax.experimental.pallas import tpu as pltpu

D_MODEL = 1024
MEM_LEN = 256
HEAD_DIM = 64
CONV_CH = D_MODEL // 2
CONV_WIDTH = 31
ATT_HEADS = 8
KV_HEADS = 2
WINDOW = 128
BLOCK = 128
REL_BUCKETS = 32
REL_MAX_DIST = 128
Q_COLS = ATT_HEADS * HEAD_DIM
KV_COLS = KV_HEADS * HEAD_DIM
IN_COLS = 2 * CONV_CH + Q_COLS + 2 * KV_COLS
X_HEADS = 4
X_HEAD_DIM = D_MODEL // X_HEADS
N_EXPERTS = 64
TOP_K = 8
N_GROUPS = 8
GROUP_SIZE = N_EXPERTS // N_GROUPS
TOPK_GROUPS = 4
EXPERT_FF = D_MODEL // 4
ROUTED_SCALE = 2.5
DEPTH = 1
ALPHA = (2 * DEPTH) ** 0.25
LN_EPS = 1e-5
NEG_INF = -1e30

F32 = jnp.float32
BF16 = jnp.bfloat16

VMEM_LIMIT_BYTES = 56 * 1024 * 1024

ROW_TILE = 512
CONV_ROWS = 32
SUBLANES = 8
CONV_HALO = 32
SWA_TILE = 512
ROUTE_TILE = 512
EXPERT_ROWS = 512
RUN_ALIGN = 16
RUN_ALIGN_LOG2 = RUN_ALIGN.bit_length() - 1
WORD_ROWS = 2
TILE_CAP = ROUTE_TILE * TOP_K + N_EXPERTS * RUN_ALIGN
BIG_PIECE = 2 * RUN_ALIGN
BIG_PER_TILE = TILE_CAP // BIG_PIECE
PIECE_UNROLL = 4
TOTAL_PIECES = tuple(1 << b for b in range(TILE_CAP.bit_length() - 1, RUN_ALIGN_LOG2 - 1, -1))
SORT_CHUNK = 256
COMBINE_CHUNK = 512


def _cparams(sem):
    return pltpu.CompilerParams(dimension_semantics=sem, vmem_limit_bytes=VMEM_LIMIT_BYTES)


def _layer_norm(h, g, b):
    mu = jnp.mean(h, axis=-1, keepdims=True)
    d = h - mu
    var = jnp.mean(d * d, axis=-1, keepdims=True)
    return d * lax.rsqrt(var + LN_EPS) * g + b


def _sigmoid(x):
    return 1.0 / (1.0 + jnp.exp(-x))


def _mix_kernel(x_ref, w_ref, b_ref, cw_ref, cb_ref, cg_ref, cbeta_ref,
                conv_ref, q_ref, k_ref, v_ref, u_ext, u_sh):
    j = pl.program_id(1)
    xb = x_ref[0].astype(BF16)
    proj = jnp.dot(xb, w_ref[...], preferred_element_type=F32) + b_ref[...]
    a = proj[:, :CONV_CH]
    g = proj[:, CONV_CH:2 * CONV_CH]
    q_ref[0] = (proj[:, 2 * CONV_CH:2 * CONV_CH + Q_COLS] * (HEAD_DIM ** -0.5)).astype(BF16)
    k_ref[0] = proj[:, 2 * CONV_CH + Q_COLS:2 * CONV_CH + Q_COLS + KV_COLS].astype(BF16)
    v_ref[0] = proj[:, 2 * CONV_CH + Q_COLS + KV_COLS:].astype(BF16)

    @pl.when(j == 0)
    def _():
        u_ext[0:CONV_HALO, :] = jnp.zeros((CONV_HALO, CONV_CH), F32)

    u_ext[CONV_HALO:CONV_HALO + ROW_TILE, :] = a * _sigmoid(g)

    first_tap = CONV_HALO - (CONV_WIDTH - 1)
    shifted_rows = u_sh.shape[1]
    for r in range(1, SUBLANES):
        u_sh[r - 1] = u_ext[r:r + shifted_rows, :]

    for c in range(ROW_TILE // CONV_ROWS):
        acc = jnp.zeros((CONV_ROWS, CONV_CH), F32) + cb_ref[...]
        for t in range(CONV_WIDTH):
            r = (first_tap + t) % SUBLANES
            base = c * CONV_ROWS + (first_tap + t) - r
            if r == 0:
                taps = u_ext[base:base + CONV_ROWS, :]
            else:
                taps = u_sh[r - 1, base:base + CONV_ROWS, :]
            acc = acc + taps * cw_ref[t:t + 1, :]
        y = _layer_norm(acc, cg_ref[...], cbeta_ref[...])
        conv_ref[0, c * CONV_ROWS:(c + 1) * CONV_ROWS, :] = (y * _sigmoid(y)).astype(BF16)

    u_ext[0:CONV_HALO, :] = u_ext[ROW_TILE:ROW_TILE + CONV_HALO, :]


def _mix(x, w_in, b_in, conv_w, conv_b, conv_g, conv_beta):
    bsz, seq, d = x.shape
    nt = seq // ROW_TILE
    row = lambda b, j: (b, j, 0)
    const2 = lambda b, j: (0, 0)
    return pl.pallas_call(
        _mix_kernel,
        grid=(bsz, nt),
        in_specs=[
            pl.BlockSpec((1, ROW_TILE, d), row),
            pl.BlockSpec((d, IN_COLS), const2),
            pl.BlockSpec((1, IN_COLS), const2),
            pl.BlockSpec((CONV_WIDTH, CONV_CH), const2),
            pl.BlockSpec((1, CONV_CH), const2),
            pl.BlockSpec((1, CONV_CH), const2),
            pl.BlockSpec((1, CONV_CH), const2),
        ],
        out_specs=[
            pl.BlockSpec((1, ROW_TILE, CONV_CH), row),
            pl.BlockSpec((1, ROW_TILE, Q_COLS), row),
            pl.BlockSpec((1, ROW_TILE, KV_COLS), row),
            pl.BlockSpec((1, ROW_TILE, KV_COLS), row),
        ],
        out_shape=[
            jax.ShapeDtypeStruct((bsz, seq, CONV_CH), BF16),
            jax.ShapeDtypeStruct((bsz, seq, Q_COLS), BF16),
            jax.ShapeDtypeStruct((bsz, seq, KV_COLS), BF16),
            jax.ShapeDtypeStruct((bsz, seq, KV_COLS), BF16),
        ],
        scratch_shapes=[pltpu.VMEM((ROW_TILE + CONV_HALO, CONV_CH), F32),
                        pltpu.VMEM((SUBLANES - 1, ROW_TILE + CONV_HALO - SUBLANES, CONV_CH), F32)],
        compiler_params=_cparams(("arbitrary", "arbitrary")),
        name="mix",
    )(x, w_in, b_in, conv_w, conv_b, conv_g, conv_beta)


def _swa_kernel(q_ref, kp_ref, kc_ref, vp_ref, vc_ref, bias_ref, sink_ref, o_ref):
    j = pl.program_id(1)
    rows = BLOCK + SWA_TILE
    lane = lax.broadcasted_iota(jnp.int32, (rows, 2 * HEAD_DIM), 1)
    low = lane < HEAD_DIM

    def placements(prev_ref, cur_ref):
        t = jnp.concatenate([prev_ref[0], cur_ref[0]], axis=0).astype(F32)
        tr = pltpu.roll(t, HEAD_DIM, 1)
        zero = jnp.zeros_like(t)
        kv0_low = jnp.where(low, t, zero).astype(BF16)
        kv1_high = jnp.where(low, zero, t).astype(BF16)
        kv1_low = jnp.where(low, tr, zero).astype(BF16)
        kv0_high = jnp.where(low, zero, tr).astype(BF16)
        return (kv0_low, kv0_high, kv1_low, kv1_high)

    ks = placements(kp_ref, kc_ref)
    vs = placements(vp_ref, vc_ref)
    slab = 2 * HEAD_DIM
    for i in range(SWA_TILE // BLOCK):
        q = q_ref[0, i * BLOCK:(i + 1) * BLOCK, :]
        q_kv0 = jnp.concatenate([q[:, 0:slab], q[:, slab:2 * slab]], axis=0)
        q_kv1 = jnp.concatenate([q[:, 2 * slab:3 * slab], q[:, 3 * slab:4 * slab]], axis=0)
        band = slice(i * BLOCK, (i + 2) * BLOCK)
        outs = []
        for s in range(4):
            qs = q_kv0 if s < 2 else q_kv1
            bias = bias_ref[1, s]
            if i == 0:
                bias = jnp.where(j == 0, bias_ref[0, s], bias)
            logits = lax.dot_general(qs, ks[s][band], (((1,), (1,)), ((), ())),
                                     preferred_element_type=F32) + bias
            sink = sink_ref[s]
            m = jnp.maximum(jnp.max(logits, axis=-1, keepdims=True), sink)
            p = jnp.exp(logits - m)
            den = jnp.sum(p, axis=-1, keepdims=True) + jnp.exp(sink - m)
            o = jnp.dot(p.astype(BF16), vs[s][band], preferred_element_type=F32)
            outs.append(o / den)
        o_kv0 = outs[0] + outs[1]
        o_kv1 = outs[2] + outs[3]
        blk = slice(i * BLOCK, (i + 1) * BLOCK)
        o_ref[0, blk, 0:slab] = o_kv0[0:BLOCK].astype(BF16)
        o_ref[0, blk, slab:2 * slab] = o_kv0[BLOCK:2 * BLOCK].astype(BF16)
        o_ref[0, blk, 2 * slab:3 * slab] = o_kv1[0:BLOCK].astype(BF16)
        o_ref[0, blk, 3 * slab:4 * slab] = o_kv1[BLOCK:2 * BLOCK].astype(BF16)


def _swa(q, k, v, bias_tab, sink_tab):
    bsz, seq, _ = q.shape
    per = SWA_TILE // BLOCK
    cur = lambda b, n: (b, n, 0)
    prev = lambda b, n: (b, jnp.maximum(n * per - 1, 0), 0)
    whole = lambda b, n: (0, 0, 0, 0)
    return pl.pallas_call(
        _swa_kernel,
        grid=(bsz, seq // SWA_TILE),
        in_specs=[
            pl.BlockSpec((1, SWA_TILE, Q_COLS), cur),
            pl.BlockSpec((1, BLOCK, KV_COLS), prev),
            pl.BlockSpec((1, SWA_TILE, KV_COLS), cur),
            pl.BlockSpec((1, BLOCK, KV_COLS), prev),
            pl.BlockSpec((1, SWA_TILE, KV_COLS), cur),
            pl.BlockSpec((2, 4, 2 * BLOCK, 2 * BLOCK), whole),
            pl.BlockSpec((4, 2 * BLOCK, 1), lambda b, n: (0, 0, 0)),
        ],
        out_specs=pl.BlockSpec((1, SWA_TILE, Q_COLS), cur),
        out_shape=jax.ShapeDtypeStruct((bsz, seq, Q_COLS), BF16),
        compiler_params=_cparams(("arbitrary", "arbitrary")),
        name="swa",
    )(q, k, k, v, v, bias_tab, sink_tab)


def _t5_bucket(dist):
    n = jnp.maximum(dist, 0)
    exact = REL_BUCKETS // 2
    large = exact + (jnp.log(jnp.maximum(n, 1).astype(F32) / exact)
                     / math.log(REL_MAX_DIST / exact) * (REL_BUCKETS - exact)).astype(jnp.int32)
    large = jnp.minimum(large, REL_BUCKETS - 1)
    return jnp.where(n < exact, n, large)


def _band_tables(rel_bias, sinks):
    qi = jnp.arange(BLOCK)[:, None]
    kj = jnp.arange(2 * BLOCK)[None, :]
    dist = qi + BLOCK - kj
    bucket = _t5_bucket(dist)
    bias = jnp.zeros((ATT_HEADS, BLOCK, 2 * BLOCK), F32)
    for bkt in range(REL_BUCKETS):
        bias = jnp.where(bucket[None] == bkt, rel_bias[bkt].astype(F32)[:, None, None], bias)
    in_window = (dist >= 0) & (dist < WINDOW)
    masks = jnp.stack([in_window & (kj >= BLOCK), in_window])
    masked = jnp.where(masks[:, None], bias[None], NEG_INF)
    pairs = ((0, 2), (1, 3), (4, 6), (5, 7))
    bias_tab = jnp.stack([jnp.concatenate([masked[:, a], masked[:, b]], axis=1) for a, b in pairs], axis=1)
    s = sinks.astype(F32)
    sink_tab = jnp.stack([jnp.concatenate([jnp.full((BLOCK, 1), s[a]), jnp.full((BLOCK, 1), s[b])], axis=0)
                          for a, b in pairs])
    return bias_tab, sink_tab


def _outproj_kernel(x_ref, conv_ref, att_ref, w_ref, b_ref, g_ref, beta_ref, o_ref):
    mix = jnp.dot(conv_ref[...], w_ref[0:CONV_CH, :], preferred_element_type=F32)
    mix = mix + jnp.dot(att_ref[...], w_ref[CONV_CH:, :], preferred_element_type=F32)
    h = ALPHA * x_ref[...] + mix + b_ref[...]
    o_ref[...] = _layer_norm(h, g_ref[...], beta_ref[...])


def _outproj(x2d, conv2d, att2d, w_out, b_out, g, beta):
    t, d = x2d.shape
    row = lambda i: (i, 0)
    const = lambda i: (0, 0)
    return pl.pallas_call(
        _outproj_kernel,
        grid=(t // ROW_TILE,),
        in_specs=[
            pl.BlockSpec((ROW_TILE, d), row),
            pl.BlockSpec((ROW_TILE, CONV_CH), row),
            pl.BlockSpec((ROW_TILE, Q_COLS), row),
            pl.BlockSpec((d, d), const),
            pl.BlockSpec((1, d), const),
            pl.BlockSpec((1, d), const),
            pl.BlockSpec((1, d), const),
        ],
        out_specs=pl.BlockSpec((ROW_TILE, d), row),
        out_shape=jax.ShapeDtypeStruct((t, d), F32),
        compiler_params=_cparams(("arbitrary",)),
        name="outproj",
    )(x2d, conv2d, att2d, w_out, b_out, g, beta)


def _memkv_kernel(mem_ref, w_ref, k_ref, v_ref):
    kv = jnp.dot(mem_ref[0].astype(BF16), w_ref[...], preferred_element_type=F32)
    k_ref[0] = kv[:, :D_MODEL].astype(BF16)
    v_ref[0] = kv[:, D_MODEL:].astype(BF16)


def _memkv(mem, wkv):
    bsz, m, d = mem.shape
    return pl.pallas_call(
        _memkv_kernel,
        grid=(bsz,),
        in_specs=[pl.BlockSpec((1, m, d), lambda b: (b, 0, 0)),
                  pl.BlockSpec((d, 2 * d), lambda b: (0, 0))],
        out_specs=[pl.BlockSpec((1, m, d), lambda b: (b, 0, 0)),
                   pl.BlockSpec((1, m, d), lambda b: (b, 0, 0))],
        out_shape=[jax.ShapeDtypeStruct((bsz, m, d), BF16)] * 2,
        compiler_params=_cparams(("arbitrary",)),
        name="memkv",
    )(mem, wkv)


def _cross_kernel(x_ref, k_ref, v_ref, wq_ref, wo_ref, g_ref, beta_ref, rw_ref, o_ref, lt_ref):
    x = x_ref[0]
    q = jnp.dot(x.astype(BF16), wq_ref[...], preferred_element_type=F32) * (X_HEAD_DIM ** -0.5)
    q = q.astype(BF16)
    heads = []
    for h in range(X_HEADS):
        cols = slice(h * X_HEAD_DIM, (h + 1) * X_HEAD_DIM)
        logits = lax.dot_general(q[:, cols], k_ref[0, :, cols], (((1,), (1,)), ((), ())),
                                 preferred_element_type=F32)
        m = jnp.max(logits, axis=-1, keepdims=True)
        p = jnp.exp(logits - m)
        den = jnp.sum(p, axis=-1, keepdims=True)
        o = jnp.dot(p.astype(BF16), v_ref[0, :, cols], preferred_element_type=F32)
        heads.append((o / den).astype(BF16))
    att = jnp.concatenate(heads, axis=-1)
    cross = jnp.dot(att, wo_ref[...], preferred_element_type=F32)
    y = _layer_norm(ALPHA * x + cross, g_ref[...], beta_ref[...])
    o_ref[0] = y
    lt_ref[...] = lax.dot_general(rw_ref[...], y.astype(BF16), (((1,), (1,)), ((), ())),
                                  preferred_element_type=F32)


def _cross(x1, kmem, vmem, wq, wo, g, beta, rw_t):
    bsz, seq, d = x1.shape
    nt = seq // ROW_TILE
    row = lambda b, j: (b, j, 0)
    mem = lambda b, j: (b, 0, 0)
    const = lambda b, j: (0, 0)
    return pl.pallas_call(
        _cross_kernel,
        grid=(bsz, nt),
        in_specs=[
            pl.BlockSpec((1, ROW_TILE, d), row),
            pl.BlockSpec((1, MEM_LEN, d), mem),
            pl.BlockSpec((1, MEM_LEN, d), mem),
            pl.BlockSpec((d, d), const),
            pl.BlockSpec((d, d), const),
            pl.BlockSpec((1, d), const),
            pl.BlockSpec((1, d), const),
            pl.BlockSpec((N_EXPERTS, d), const),
        ],
        out_specs=[
            pl.BlockSpec((1, ROW_TILE, d), row),
            pl.BlockSpec((N_EXPERTS, ROW_TILE), lambda b, j: (0, b * nt + j)),
        ],
        out_shape=[
            jax.ShapeDtypeStruct((bsz, seq, d), F32),
            jax.ShapeDtypeStruct((N_EXPERTS, bsz * seq), F32),
        ],
        compiler_params=_cparams(("arbitrary", "arbitrary")),
        name="cross",
    )(x1, kmem, vmem, wq, wo, g, beta, rw_t)


def _route_kernel(lt_ref, rb_ref, tri_ref, pos_ref, gate_ref, cnt_ref):
    tn = ROUTE_TILE
    scores = _sigmoid(lt_ref[...])
    choice = scores + rb_ref[...]

    gscore = []
    member = lax.broadcasted_iota(jnp.int32, (GROUP_SIZE, tn), 0).astype(F32)
    for g in range(N_GROUPS):
        c = choice[g * GROUP_SIZE:(g + 1) * GROUP_SIZE, :]
        m1 = jnp.max(c, axis=0, keepdims=True)
        first = jnp.min(jnp.where(c == m1, member, float(GROUP_SIZE)), axis=0, keepdims=True)
        m2 = jnp.max(jnp.where(member == first, -jnp.inf, c), axis=0, keepdims=True)
        gscore.append(m1 + m2)

    keep_rows = []
    for g in range(N_GROUPS):
        beaten = jnp.zeros((1, tn), F32)
        for o in range(N_GROUPS):
            if o == g:
                continue
            ahead = (gscore[o] >= gscore[g]) if o < g else (gscore[o] > gscore[g])
            beaten = beaten + jnp.where(ahead, 1.0, 0.0)
        keep_rows.append(jnp.broadcast_to(beaten, (GROUP_SIZE, tn)))
    beaten_all = jnp.concatenate(keep_rows, axis=0)

    masked = jnp.where(beaten_all < TOPK_GROUPS, choice, -jnp.inf)
    eidx = lax.broadcasted_iota(jnp.int32, (N_EXPERTS, tn), 0).astype(F32)
    sel = jnp.zeros((N_EXPERTS, tn), F32)
    picks, weights = [], []
    for r in range(TOP_K):
        mx = jnp.max(masked, axis=0, keepdims=True)
        first = jnp.min(jnp.where(masked == mx, eidx, float(N_EXPERTS)), axis=0, keepdims=True)
        pick = eidx == first
        picks.append((pick, first))
        weights.append(jnp.sum(jnp.where(pick, scores, 0.0), axis=0, keepdims=True))
        masked = jnp.where(pick, -jnp.inf, masked)
        sel = jnp.where(pick, 1.0, sel)

    wsum = weights[0]
    for r in range(1, TOP_K):
        wsum = wsum + weights[r]

    count = jnp.sum(sel, axis=1, keepdims=True)
    run_len = jnp.floor((count + (RUN_ALIGN - 1.0)) * (1.0 / RUN_ALIGN)) * RUN_ALIGN
    run_len_b = jnp.broadcast_to(run_len, (N_EXPERTS, 128))
    er = lax.broadcasted_iota(jnp.int32, (N_EXPERTS, N_EXPERTS), 0)
    ec = lax.broadcasted_iota(jnp.int32, (N_EXPERTS, N_EXPERTS), 1)
    before = jnp.where(ec < er, 1.0, 0.0).astype(BF16)
    run_start = jnp.dot(before, run_len_b.astype(BF16), preferred_element_type=F32)[:, 0:1]
    incl = jnp.dot(sel.astype(BF16), tri_ref[...], preferred_element_type=F32)
    pos_mat = run_start + incl - sel
    cnt_ref[0] = run_len_b.astype(jnp.int32)

    for r in range(TOP_K):
        pick, _ = picks[r]
        pos_ref[0, r:r + 1, :] = jnp.sum(jnp.where(pick, pos_mat, 0.0), axis=0,
                                         keepdims=True).astype(jnp.int32)
        gate_ref[0, r:r + 1, :] = weights[r] / wsum * ROUTED_SCALE


def _route(logits_t, router_b, tri):
    e, t = logits_t.shape
    nt = t // ROUTE_TILE
    blk = lambda i: (i, 0, 0)
    return pl.pallas_call(
        _route_kernel,
        grid=(nt,),
        in_specs=[
            pl.BlockSpec((e, ROUTE_TILE), lambda i: (0, i)),
            pl.BlockSpec((e, 1), lambda i: (0, 0)),
            pl.BlockSpec((ROUTE_TILE, ROUTE_TILE), lambda i: (0, 0)),
        ],
        out_specs=[
            pl.BlockSpec((1, TOP_K, ROUTE_TILE), blk),
            pl.BlockSpec((1, TOP_K, ROUTE_TILE), blk),
            pl.BlockSpec((1, e, 128), blk),
        ],
        out_shape=[
            jax.ShapeDtypeStruct((nt, TOP_K, ROUTE_TILE), jnp.int32),
            jax.ShapeDtypeStruct((nt, TOP_K, ROUTE_TILE), F32),
            jax.ShapeDtypeStruct((nt, e, 128), jnp.int32),
        ],
        compiler_params=_cparams(("arbitrary",)),
        name="route",
    )(logits_t, router_b, tri)


def _for_each_piece(tables, tile, fn):
    for n_rows, per_tile, (local_ref, global_ref, count_ref) in (
            (BIG_PIECE, BIG_PER_TILE, tables[0:3]), (RUN_ALIGN, N_EXPERTS, tables[3:6])):
        count = count_ref[tile]

        def body(g, carry, n_rows=n_rows, per_tile=per_tile, local_ref=local_ref, global_ref=global_ref,
                 count=count):
            for u in range(PIECE_UNROLL):
                j = g * PIECE_UNROLL + u

                @pl.when(j < count)
                def _():
                    fn(pl.multiple_of(local_ref[tile * per_tile + j], RUN_ALIGN),
                       pl.multiple_of(global_ref[tile * per_tile + j], RUN_ALIGN), n_rows)
            return carry

        lax.fori_loop(0, (count + PIECE_UNROLL - 1) // PIECE_UNROLL, body, 0)


def _word_rows(start, size):
    if isinstance(start, int):
        first = start // WORD_ROWS
    else:
        first = pl.multiple_of(start >> (WORD_ROWS.bit_length() - 1), RUN_ALIGN // WORD_ROWS)
    return pl.ds(first, size // WORD_ROWS)


def _for_each_total_piece(total, fn):
    for piece in TOTAL_PIECES:
        @pl.when((total & piece) != 0)
        def _():
            fn(piece)


def _dispatch_kernel(*refs):
    tables, (rows_ref, total_ref, gap_ref, pos_ref, x_ref, xs_ref, stage, zeros, sem, zsem) = refs[:6], refs[6:]
    i = pl.program_id(0)
    nt = pl.num_programs(0)
    slot = i % 2
    tn = ROUTE_TILE

    def drain(tile, buf):
        def wait_rows(n):
            pltpu.make_async_copy(stage.at[buf, _word_rows(0, n), :], xs_ref.at[_word_rows(0, n), :],
                                  sem.at[buf]).wait()
        _for_each_total_piece(rows_ref[tile], wait_rows)

    @pl.when(i >= 2)
    def _():
        drain(i - 2, slot)

    xb = x_ref[...].astype(BF16)
    pos = pos_ref[0]
    row_iota = lax.broadcasted_iota(jnp.int32, (SORT_CHUNK, tn), 0).astype(F32).astype(BF16)
    one = jnp.ones((SORT_CHUNK, tn), BF16)
    zero = jnp.zeros((SORT_CHUNK, tn), BF16)

    def onehot_rows(c):
        rel = jnp.clip(pos - c * SORT_CHUNK, -1, SORT_CHUNK).astype(F32).astype(BF16)
        hit = rel[0:1, :] == row_iota
        for k in range(1, TOP_K):
            hit = hit | (rel[k:k + 1, :] == row_iota)
        return jnp.where(hit, one, zero)

    def sort_chunks(c2, carry):
        for c in (2 * c2, 2 * c2 + 1):
            p0 = pl.multiple_of(c * SORT_CHUNK, SORT_CHUNK)
            rows = jnp.dot(onehot_rows(c), xb, preferred_element_type=F32)
            stage[slot, _word_rows(p0, SORT_CHUNK), :] = pltpu.bitcast(rows.astype(BF16), jnp.int32)
        return carry

    lax.fori_loop(0, (rows_ref[i] + 2 * SORT_CHUNK - 1) // (2 * SORT_CHUNK), sort_chunks, 0)

    def start_piece(local, dst, n):
        pltpu.make_async_copy(stage.at[slot, _word_rows(local, n), :], xs_ref.at[_word_rows(dst, n), :],
                              sem.at[slot]).start()

    _for_each_piece(tables, i, start_piece)

    @pl.when(i == nt - 1)
    def _():
        @pl.when(nt >= 2)
        def _():
            drain(i - 1, 1 - slot)

        drain(i, slot)

        zeros[...] = jnp.zeros_like(zeros)

        def gap_copies(e, act):
            gap = gap_ref[N_EXPERTS + e]
            for piece in TOTAL_PIECES:
                if piece >= EXPERT_ROWS:
                    continue

                @pl.when((gap & piece) != 0)
                def _():
                    dst = gap_ref[e] + (gap & (-2 * piece))
                    act(pltpu.make_async_copy(zeros.at[_word_rows(0, piece), :],
                                              xs_ref.at[_word_rows(dst, piece), :], zsem))

        def zero_block(b):
            return pltpu.make_async_copy(zeros, xs_ref.at[_word_rows(b * EXPERT_ROWS, EXPERT_ROWS), :], zsem)

        first_free = total_ref[0] // EXPERT_ROWS
        n_blocks = xs_ref.shape[0] * WORD_ROWS // EXPERT_ROWS
        lax.fori_loop(0, N_EXPERTS, lambda e, c: (gap_copies(e, lambda cp: cp.start()), c)[1], 0)
        lax.fori_loop(first_free, n_blocks, lambda b, c: (zero_block(b).start(), c)[1], 0)
        lax.fori_loop(0, N_EXPERTS, lambda e, c: (gap_copies(e, lambda cp: cp.wait()), c)[1], 0)
        lax.fori_loop(first_free, n_blocks, lambda b, c: (zero_block(b).wait(), c)[1], 0)


def _dispatch(pieces, tile_rows, total, gaps, pos, x2d, n_rows):
    t, d = x2d.shape
    nt = t // ROUTE_TILE
    return pl.pallas_call(
        _dispatch_kernel,
        grid_spec=pltpu.PrefetchScalarGridSpec(
            num_scalar_prefetch=len(pieces) + 3,
            grid=(nt,),
            in_specs=[pl.BlockSpec((1, TOP_K, ROUTE_TILE), lambda i, *_: (i, 0, 0)),
                      pl.BlockSpec((ROUTE_TILE, d), lambda i, *_: (i, 0))],
            out_specs=pl.BlockSpec(memory_space=pl.ANY),
            scratch_shapes=[pltpu.VMEM((2, TILE_CAP // WORD_ROWS, d), jnp.int32),
                            pltpu.VMEM((EXPERT_ROWS // WORD_ROWS, d), jnp.int32),
                            pltpu.SemaphoreType.DMA((2,)),
                            pltpu.SemaphoreType.DMA(())],
        ),
        out_shape=jax.ShapeDtypeStruct((n_rows // WORD_ROWS, d), jnp.int32),
        compiler_params=_cparams(("arbitrary",)),
        name="dispatch",
    )(*pieces, tile_rows, total, gaps, pos, x2d)


def _experts_kernel(ie_ref, flag_ref, xs_ref, wg_ref, wu_ref, wd_ref, y_ref, wg_bf, wu_bf, wd_bf):
    w = pl.program_id(0)
    flags = flag_ref[w]
    valid = (flags & 1) != 0
    new_expert = (flags & 2) != 0

    @pl.when(jnp.logical_not(valid))
    def _():
        y_ref[...] = jnp.zeros_like(y_ref)

    @pl.when(new_expert)
    def _():
        wg_bf[...] = wg_ref[0].astype(BF16)
        wu_bf[...] = wu_ref[0].astype(BF16)
        wd_bf[...] = wd_ref[0].astype(BF16)

    @pl.when(valid)
    def _():
        x = pltpu.bitcast(xs_ref[...], BF16)
        gte = jnp.dot(x, wg_bf[...], preferred_element_type=F32)
        up = jnp.dot(x, wu_bf[...], preferred_element_type=F32)
        h = (gte * _sigmoid(gte) * up).astype(BF16)
        y = jnp.dot(h, wd_bf[...], preferred_element_type=F32)
        y_ref[...] = pltpu.bitcast(y.astype(BF16), jnp.int32)


def _experts(block_expert, block_flags, xs, wg, wu, wd):
    n, d = xs.shape
    ff = wg.shape[-1]
    return pl.pallas_call(
        _experts_kernel,
        grid_spec=pltpu.PrefetchScalarGridSpec(
            num_scalar_prefetch=2,
            grid=(block_expert.shape[0],),
            in_specs=[
                pl.BlockSpec((EXPERT_ROWS // WORD_ROWS, d), lambda w, ie, fl: (w, 0)),
                pl.BlockSpec((1, d, ff), lambda w, ie, fl: (ie[w], 0, 0)),
                pl.BlockSpec((1, d, ff), lambda w, ie, fl: (ie[w], 0, 0)),
                pl.BlockSpec((1, ff, d), lambda w, ie, fl: (ie[w], 0, 0)),
            ],
            out_specs=pl.BlockSpec((EXPERT_ROWS // WORD_ROWS, d), lambda w, ie, fl: (w, 0)),
            scratch_shapes=[pltpu.VMEM((d, ff), BF16), pltpu.VMEM((d, ff), BF16), pltpu.VMEM((ff, d), BF16)],
        ),
        out_shape=jax.ShapeDtypeStruct((n, d), jnp.int32),
        compiler_params=_cparams(("arbitrary",)),
        name="experts",
    )(block_expert, block_flags, xs, wg, wu, wd)


def _block_table(seg_rows, n_rows):
    block_end = jnp.cumsum(seg_rows // EXPERT_ROWS)
    w = jnp.arange(n_rows // EXPERT_ROWS, dtype=jnp.int32)
    valid = w < block_end[-1]
    e = jnp.sum((block_end[None, :] <= jnp.minimum(w, block_end[-1] - 1)[:, None]).astype(jnp.int32), axis=1)
    e = jnp.minimum(e, N_EXPERTS - 1)
    prev_e = jnp.concatenate([jnp.full((1,), -1, jnp.int32), e[:-1]])
    flags = valid.astype(jnp.int32) + 2 * (valid & (e != prev_e)).astype(jnp.int32)
    return e, flags


def _combine_kernel(*refs):
    tables = refs[:6]
    (rows_ref, pos_ref, gate_ref, x_ref, sg_ref, su_ref, sd_ref, g_ref, beta_ref, yb_ref, o_ref,
     stage, acc, lane_tile, lane_gate, sem) = refs[6:]
    i = pl.program_id(0)
    nt = pl.num_programs(0)
    slot = i % 2
    tn = ROUTE_TILE

    def start_tile(tile, buf):
        def start_piece(local, src, n):
            pltpu.make_async_copy(yb_ref.at[_word_rows(src, n), :], stage.at[buf, _word_rows(local, n), :],
                                  sem.at[buf]).start()
        _for_each_piece(tables, tile, start_piece)

    @pl.when(i == 0)
    def _():
        stage[...] = jnp.zeros_like(stage)
        start_tile(0, 0)

    @pl.when(i + 1 < nt)
    def _():
        start_tile(i + 1, 1 - slot)

    x = x_ref[...]
    xb = x.astype(BF16)
    gte = jnp.dot(xb, sg_ref[...], preferred_element_type=F32)
    up = jnp.dot(xb, su_ref[...], preferred_element_type=F32)
    h = (gte * _sigmoid(gte) * up).astype(BF16)
    acc[...] = jnp.dot(h, sd_ref[...], preferred_element_type=F32)

    def wait_rows(n):
        pltpu.make_async_copy(yb_ref.at[_word_rows(0, n), :], stage.at[slot, _word_rows(0, n), :],
                              sem.at[slot]).wait()

    _for_each_total_piece(rows_ref[i], wait_rows)

    lanes = 128
    lane_iota = lax.broadcasted_iota(jnp.int32, (tn, lanes), 1)
    for k in range(TOP_K):
        p = jnp.broadcast_to(pos_ref[:, k:k + 1], (tn, lanes))
        g = jnp.broadcast_to(gate_ref[:, k:k + 1], (tn, lanes))
        lane_tile[k] = (p >> (lanes.bit_length() - 1)).astype(F32).astype(BF16)
        lane_gate[k] = jnp.where((p & (lanes - 1)) == lane_iota, g, 0.0).astype(BF16)
    zero = jnp.zeros((tn, lanes), BF16)

    def gate_matrix(c):
        cols = []
        for j in range(COMBINE_CHUNK // lanes):
            tile = jnp.asarray(c * (COMBINE_CHUNK // lanes) + j, jnp.int32).astype(F32).astype(BF16)
            w = zero
            for k in range(TOP_K):
                w = w + jnp.where(lane_tile[k] == tile, lane_gate[k], zero)
            cols.append(w)
        return jnp.concatenate(cols, axis=1)

    def weigh_chunks(c2, carry):
        part = []
        for c in (2 * c2, 2 * c2 + 1):
            p0 = pl.multiple_of(c * COMBINE_CHUNK, COMBINE_CHUNK)
            rows = pltpu.bitcast(stage[slot, _word_rows(p0, COMBINE_CHUNK), :], BF16)
            part.append(jnp.dot(gate_matrix(c), rows, preferred_element_type=F32))
        acc[...] += part[0] + part[1]
        return carry

    lax.fori_loop(0, (rows_ref[i] + 2 * COMBINE_CHUNK - 1) // (2 * COMBINE_CHUNK), weigh_chunks, 0)
    o_ref[...] = _layer_norm(ALPHA * x + acc[...], g_ref[...], beta_ref[...])


def _combine(pieces, tile_rows, pos_t, gate_t, x2d, sg, su, sd, g, beta, yb):
    t, d = x2d.shape
    nt = t // ROUTE_TILE
    ff = sg.shape[-1]
    row = lambda i, *_: (i, 0)
    const = lambda i, *_: (0, 0)
    return pl.pallas_call(
        _combine_kernel,
        grid_spec=pltpu.PrefetchScalarGridSpec(
            num_scalar_prefetch=len(pieces) + 1,
            grid=(nt,),
            in_specs=[
                pl.BlockSpec((ROUTE_TILE, TOP_K), row),
                pl.BlockSpec((ROUTE_TILE, TOP_K), row),
                pl.BlockSpec((ROUTE_TILE, d), row),
                pl.BlockSpec((d, ff), const),
                pl.BlockSpec((d, ff), const),
                pl.BlockSpec((ff, d), const),
                pl.BlockSpec((1, d), const),
                pl.BlockSpec((1, d), const),
                pl.BlockSpec(memory_space=pl.ANY),
            ],
            out_specs=pl.BlockSpec((ROUTE_TILE, d), row),
            scratch_shapes=[pltpu.VMEM((2, TILE_CAP // WORD_ROWS, d), jnp.int32),
                            pltpu.VMEM((ROUTE_TILE, d), F32),
                            pltpu.VMEM((TOP_K, ROUTE_TILE, 128), BF16),
                            pltpu.VMEM((TOP_K, ROUTE_TILE, 128), BF16),
                            pltpu.SemaphoreType.DMA((2,))],
        ),
        out_shape=jax.ShapeDtypeStruct((t, d), F32),
        compiler_params=_cparams(("arbitrary",)),
        name="combine",
    )(*pieces, tile_rows, pos_t, gate_t, x2d, sg, su, sd, g, beta, yb)


def _piece_tables(cnt):
    lbase = jnp.cumsum(cnt, axis=1) - cnt
    per_expert = jnp.sum(cnt, axis=0)
    seg_rows = -(-per_expert // EXPERT_ROWS) * EXPERT_ROWS
    seg_start = jnp.cumsum(seg_rows) - seg_rows
    gbase = seg_start[None, :] + jnp.cumsum(cnt, axis=0) - cnt
    gaps = jnp.concatenate([seg_start + per_expert, seg_rows - per_expert]).astype(jnp.int32)

    def listed(n_pieces, start_off, piece_rows, length):
        last = jnp.cumsum(n_pieces, axis=1)
        first = last - n_pieces
        j = jnp.arange(length, dtype=jnp.int32)[None, :, None]
        own = (j >= first[:, None, :]) & (j < last[:, None, :])
        off = (start_off[:, None, :] + (j - first[:, None, :]) * piece_rows)
        pick = lambda base: jnp.sum(jnp.where(own, base[:, None, :] + off, 0), axis=-1).reshape(-1).astype(jnp.int32)
        return pick(lbase), pick(gbase), last[:, -1].astype(jnp.int32)

    n_big = cnt // BIG_PIECE
    big = listed(n_big, jnp.zeros_like(cnt), BIG_PIECE, BIG_PER_TILE)
    small = listed((cnt // RUN_ALIGN) % 2, n_big * BIG_PIECE, RUN_ALIGN, N_EXPERTS)
    return big + small, jnp.sum(cnt, axis=1).astype(jnp.int32), seg_rows, gaps


def kernel(x, mem, w_in, b_in, conv_w, conv_b, conv_ln_g, conv_ln_b, attn_sinks, rel_bias, w_out, b_out, ln1_g, ln1_b, xq_w, xkv_w, xo_w, ln2_g, ln2_b, router_w, router_b, exp_gate, exp_up, exp_down, sh_gate, sh_up, sh_down, ln3_g, ln3_b):
    bsz, seq, d = x.shape
    t = bsz * seq
    bias_tab, sink_tab = _band_tables(rel_bias, attn_sinks[0])
    tri = (jnp.arange(ROUTE_TILE)[:, None] <= jnp.arange(ROUTE_TILE)[None, :]).astype(BF16)
    row = lambda p: p.reshape(1, -1)
    for l in range(DEPTH):
        conv_out, q, k, v = _mix(x, w_in[l].astype(BF16), row(b_in[l]), conv_w[l], row(conv_b[l]),
                                 row(conv_ln_g[l]), row(conv_ln_b[l]))
        att = _swa(q, k, v, bias_tab, sink_tab)
        x1 = _outproj(x.reshape(t, d), conv_out.reshape(t, CONV_CH), att.reshape(t, Q_COLS),
                      w_out[l].astype(BF16), row(b_out[l]), row(ln1_g[l]), row(ln1_b[l]))
        kmem, vmem = _memkv(mem, xkv_w[l].astype(BF16))
        x2, logits_t = _cross(x1.reshape(bsz, seq, d), kmem, vmem, xq_w[l].astype(BF16),
                              xo_w[l].astype(BF16), row(ln2_g[l]), row(ln2_b[l]),
                              router_w[l].T.astype(BF16))
        x2 = x2.reshape(t, d)
        pos, gate, cnt = _route(logits_t, router_b[l].reshape(-1, 1), tri)
        nt = t // ROUTE_TILE
        n_rows = (-(-(t * TOP_K + nt * N_EXPERTS * (RUN_ALIGN - 1)) // EXPERT_ROWS) + N_EXPERTS) * EXPERT_ROWS
        pieces, tile_rows, seg_rows, gaps = _piece_tables(cnt[:, :, 0])
        block_expert, block_flags = _block_table(seg_rows, n_rows)
        xs = _dispatch(pieces, tile_rows, jnp.sum(seg_rows).reshape(1).astype(jnp.int32), gaps, pos, x2, n_rows)
        yb = _experts(block_expert, block_flags, xs, exp_gate[l], exp_up[l], exp_down[l])
        pos_t = jnp.transpose(pos, (0, 2, 1)).reshape(t, TOP_K)
        gate_t = jnp.transpose(gate, (0, 2, 1)).reshape(t, TOP_K)
        x = _combine(pieces, tile_rows, pos_t, gate_t, x2, sh_gate[l].astype(BF16),
                     sh_up[l].astype(BF16), sh_down[l].astype(BF16), row(ln3_g[l]), row(ln3_b[l]),
                     yb).reshape(bsz, seq, d)
    return x
```

```python
import functools
import math

import jax
import jax.numpy as jnp
from jax import lax
from jax.experimental import pallas as pl
from jax.experimental.pallas import tpu as pltpu

D_MODEL = 1024
MEM_LEN = 256
HEAD_DIM = 64
CONV_CH = D_MODEL // 2
CONV_WIDTH = 31
ATT_HEADS = 8
KV_HEADS = 2
WINDOW = 128
BLOCK = 128
REL_BUCKETS = 32
REL_MAX_DIST = 128
Q_COLS = ATT_HEADS * HEAD_DIM
KV_COLS = KV_HEADS * HEAD_DIM
IN_COLS = 2 * CONV_CH + Q_COLS + 2 * KV_COLS
X_HEADS = 4
X_HEAD_DIM = D_MODEL // X_HEADS
N_EXPERTS = 64
TOP_K = 8
N_GROUPS = 8
GROUP_SIZE = N_EXPERTS // N_GROUPS
TOPK_GROUPS = 4
EXPERT_FF = D_MODEL // 4
ROUTED_SCALE = 2.5
DEPTH = 1
ALPHA = (2 * DEPTH) ** 0.25
LN_EPS = 1e-5
NEG_INF = -1e30

F32 = jnp.float32
BF16 = jnp.bfloat16

VMEM_LIMIT_BYTES = 56 * 1024 * 1024

ROW_TILE = 512
CONV_ROWS = 32
SUBLANES = 8
CONV_HALO = 32
SWA_TILE = 512
ROUTE_TILE = 512
EXPERT_ROWS = 1024
EXPERT_SPLIT = 2
RUN_ALIGN = 16
RUN_ALIGN_LOG2 = RUN_ALIGN.bit_length() - 1
WORD_ROWS = 2
TILE_CAP = ROUTE_TILE * TOP_K + N_EXPERTS * RUN_ALIGN
BIG_PIECE = 2 * RUN_ALIGN
BIG_PER_TILE = TILE_CAP // BIG_PIECE
PIECE_UNROLL = 4
TOTAL_PIECES = tuple(1 << b for b in range(TILE_CAP.bit_length() - 1, RUN_ALIGN_LOG2 - 1, -1))
SORT_CHUNK = 256
COMBINE_CHUNK = 512


def _cparams(sem):
    return pltpu.CompilerParams(dimension_semantics=sem, vmem_limit_bytes=VMEM_LIMIT_BYTES)


def _layer_norm(h, g, b):
    mu = jnp.mean(h, axis=-1, keepdims=True)
    d = h - mu
    var = jnp.mean(d * d, axis=-1, keepdims=True)
    return d * lax.rsqrt(var + LN_EPS) * g + b


def _sigmoid(x):
    return 1.0 / (1.0 + jnp.exp(-x))


def _mix_kernel(x_ref, w_ref, b_ref, cw_ref, cb_ref, cg_ref, cbeta_ref,
                conv_ref, q_ref, k_ref, v_ref, u_ext, u_sh):
    j = pl.program_id(1)
    xb = x_ref[0].astype(BF16)
    proj = jnp.dot(xb, w_ref[...], preferred_element_type=F32) + b_ref[...]
    a = proj[:, :CONV_CH]
    g = proj[:, CONV_CH:2 * CONV_CH]
    q_ref[0] = (proj[:, 2 * CONV_CH:2 * CONV_CH + Q_COLS] * (HEAD_DIM ** -0.5)).astype(BF16)
    k_ref[0] = proj[:, 2 * CONV_CH + Q_COLS:2 * CONV_CH + Q_COLS + KV_COLS].astype(BF16)
    v_ref[0] = proj[:, 2 * CONV_CH + Q_COLS + KV_COLS:].astype(BF16)

    @pl.when(j == 0)
    def _():
        u_ext[0:CONV_HALO, :] = jnp.zeros((CONV_HALO, CONV_CH), F32)

    u_ext[CONV_HALO:CONV_HALO + ROW_TILE, :] = a * _sigmoid(g)

    first_tap = CONV_HALO - (CONV_WIDTH - 1)
    shifted_rows = u_sh.shape[1]
    for r in range(1, SUBLANES):
        u_sh[r - 1] = u_ext[r:r + shifted_rows, :]

    for c in range(ROW_TILE // CONV_ROWS):
        acc = jnp.zeros((CONV_ROWS, CONV_CH), F32) + cb_ref[...]
        for t in range(CONV_WIDTH):
            r = (first_tap + t) % SUBLANES
            base = c * CONV_ROWS + (first_tap + t) - r
            if r == 0:
                taps = u_ext[base:base + CONV_ROWS, :]
            else:
                taps = u_sh[r - 1, base:base + CONV_ROWS, :]
            acc = acc + taps * cw_ref[t:t + 1, :]
        y = _layer_norm(acc, cg_ref[...], cbeta_ref[...])
        conv_ref[0, c * CONV_ROWS:(c + 1) * CONV_ROWS, :] = (y * _sigmoid(y)).astype(BF16)

    u_ext[0:CONV_HALO, :] = u_ext[ROW_TILE:ROW_TILE + CONV_HALO, :]


def _mix(x, w_in, b_in, conv_w, conv_b, conv_g, conv_beta):
    bsz, seq, d = x.shape
    nt = seq // ROW_TILE
    row = lambda b, j: (b, j, 0)
    const2 = lambda b, j: (0, 0)
    return pl.pallas_call(
        _mix_kernel,
        grid=(bsz, nt),
        in_specs=[
            pl.BlockSpec((1, ROW_TILE, d), row),
            pl.BlockSpec((d, IN_COLS), const2),
            pl.BlockSpec((1, IN_COLS), const2),
            pl.BlockSpec((CONV_WIDTH, CONV_CH), const2),
            pl.BlockSpec((1, CONV_CH), const2),
            pl.BlockSpec((1, CONV_CH), const2),
            pl.BlockSpec((1, CONV_CH), const2),
        ],
        out_specs=[
            pl.BlockSpec((1, ROW_TILE, CONV_CH), row),
            pl.BlockSpec((1, ROW_TILE, Q_COLS), row),
            pl.BlockSpec((1, ROW_TILE, KV_COLS), row),
            pl.BlockSpec((1, ROW_TILE, KV_COLS), row),
        ],
        out_shape=[
            jax.ShapeDtypeStruct((bsz, seq, CONV_CH), BF16),
            jax.ShapeDtypeStruct((bsz, seq, Q_COLS), BF16),
            jax.ShapeDtypeStruct((bsz, seq, KV_COLS), BF16),
            jax.ShapeDtypeStruct((bsz, seq, KV_COLS), BF16),
        ],
        scratch_shapes=[pltpu.VMEM((ROW_TILE + CONV_HALO, CONV_CH), F32),
                        pltpu.VMEM((SUBLANES - 1, ROW_TILE + CONV_HALO - SUBLANES, CONV_CH), F32)],
        compiler_params=_cparams(("arbitrary", "arbitrary")),
        name="mix",
    )(x, w_in, b_in, conv_w, conv_b, conv_g, conv_beta)


def _swa_kernel(q_ref, kp_ref, kc_ref, vp_ref, vc_ref, bias_ref, sink_ref, o_ref):
    j = pl.program_id(1)
    rows = BLOCK + SWA_TILE
    lane = lax.broadcasted_iota(jnp.int32, (rows, 2 * HEAD_DIM), 1)
    low = lane < HEAD_DIM

    def placements(prev_ref, cur_ref):
        t = jnp.concatenate([prev_ref[0], cur_ref[0]], axis=0).astype(F32)
        tr = pltpu.roll(t, HEAD_DIM, 1)
        zero = jnp.zeros_like(t)
        kv0_low = jnp.where(low, t, zero).astype(BF16)
        kv1_high = jnp.where(low, zero, t).astype(BF16)
        kv1_low = jnp.where(low, tr, zero).astype(BF16)
        kv0_high = jnp.where(low, zero, tr).astype(BF16)
        return (kv0_low, kv0_high, kv1_low, kv1_high)

    ks = placements(kp_ref, kc_ref)
    vs = placements(vp_ref, vc_ref)
    slab = 2 * HEAD_DIM
    for i in range(SWA_TILE // BLOCK):
        q = q_ref[0, i * BLOCK:(i + 1) * BLOCK, :]
        q_kv0 = jnp.concatenate([q[:, 0:slab], q[:, slab:2 * slab]], axis=0)
        q_kv1 = jnp.concatenate([q[:, 2 * slab:3 * slab], q[:, 3 * slab:4 * slab]], axis=0)
        band = slice(i * BLOCK, (i + 2) * BLOCK)
        outs = []
        for s in range(4):
            qs = q_kv0 if s < 2 else q_kv1
            bias = bias_ref[1, s]
            if i == 0:
                bias = jnp.where(j == 0, bias_ref[0, s], bias)
            logits = lax.dot_general(qs, ks[s][band], (((1,), (1,)), ((), ())),
                                     preferred_element_type=F32) + bias
            sink = sink_ref[s]
            m = jnp.maximum(jnp.max(logits, axis=-1, keepdims=True), sink)
            p = jnp.exp(logits - m)
            den = jnp.sum(p, axis=-1, keepdims=True) + jnp.exp(sink - m)
            o = jnp.dot(p.astype(BF16), vs[s][band], preferred_element_type=F32)
            outs.append(o / den)
        o_kv0 = outs[0] + outs[1]
        o_kv1 = outs[2] + outs[3]
        blk = slice(i * BLOCK, (i + 1) * BLOCK)
        o_ref[0, blk, 0:slab] = o_kv0[0:BLOCK].astype(BF16)
        o_ref[0, blk, slab:2 * slab] = o_kv0[BLOCK:2 * BLOCK].astype(BF16)
        o_ref[0, blk, 2 * slab:3 * slab] = o_kv1[0:BLOCK].astype(BF16)
        o_ref[0, blk, 3 * slab:4 * slab] = o_kv1[BLOCK:2 * BLOCK].astype(BF16)


def _swa(q, k, v, bias_tab, sink_tab):
    bsz, seq, _ = q.shape
    per = SWA_TILE // BLOCK
    cur = lambda b, n: (b, n, 0)
    prev = lambda b, n: (b, jnp.maximum(n * per - 1, 0), 0)
    whole = lambda b, n: (0, 0, 0, 0)
    return pl.pallas_call(
        _swa_kernel,
        grid=(bsz, seq // SWA_TILE),
        in_specs=[
            pl.BlockSpec((1, SWA_TILE, Q_COLS), cur),
            pl.BlockSpec((1, BLOCK, KV_COLS), prev),
            pl.BlockSpec((1, SWA_TILE, KV_COLS), cur),
            pl.BlockSpec((1, BLOCK, KV_COLS), prev),
            pl.BlockSpec((1, SWA_TILE, KV_COLS), cur),
            pl.BlockSpec((2, 4, 2 * BLOCK, 2 * BLOCK), whole),
            pl.BlockSpec((4, 2 * BLOCK, 1), lambda b, n: (0, 0, 0)),
        ],
        out_specs=pl.BlockSpec((1, SWA_TILE, Q_COLS), cur),
        out_shape=jax.ShapeDtypeStruct((bsz, seq, Q_COLS), BF16),
        compiler_params=_cparams(("arbitrary", "arbitrary")),
        name="swa",
    )(q, k, k, v, v, bias_tab, sink_tab)


def _t5_bucket(dist):
    n = jnp.maximum(dist, 0)
    exact = REL_BUCKETS // 2
    large = exact + (jnp.log(jnp.maximum(n, 1).astype(F32) / exact)
                     / math.log(REL_MAX_DIST / exact) * (REL_BUCKETS - exact)).astype(jnp.int32)
    large = jnp.minimum(large, REL_BUCKETS - 1)
    return jnp.where(n < exact, n, large)


def _band_tables(rel_bias, sinks):
    qi = jnp.arange(BLOCK)[:, None]
    kj = jnp.arange(2 * BLOCK)[None, :]
    dist = qi + BLOCK - kj
    bucket = _t5_bucket(dist)
    bias = jnp.zeros((ATT_HEADS, BLOCK, 2 * BLOCK), F32)
    for bkt in range(REL_BUCKETS):
        bias = jnp.where(bucket[None] == bkt, rel_bias[bkt].astype(F32)[:, None, None], bias)
    in_window = (dist >= 0) & (dist < WINDOW)
    masks = jnp.stack([in_window & (kj >= BLOCK), in_window])
    masked = jnp.where(masks[:, None], bias[None], NEG_INF)
    pairs = ((0, 2), (1, 3), (4, 6), (5, 7))
    bias_tab = jnp.stack([jnp.concatenate([masked[:, a], masked[:, b]], axis=1) for a, b in pairs], axis=1)
    s = sinks.astype(F32)
    sink_tab = jnp.stack([jnp.concatenate([jnp.full((BLOCK, 1), s[a]), jnp.full((BLOCK, 1), s[b])], axis=0)
                          for a, b in pairs])
    return bias_tab, sink_tab


def _outproj_kernel(x_ref, conv_ref, att_ref, w_ref, b_ref, g_ref, beta_ref, o_ref):
    mix = jnp.dot(conv_ref[...], w_ref[0:CONV_CH, :], preferred_element_type=F32)
    mix = mix + jnp.dot(att_ref[...], w_ref[CONV_CH:, :], preferred_element_type=F32)
    h = ALPHA * x_ref[...] + mix + b_ref[...]
    o_ref[...] = _layer_norm(h, g_ref[...], beta_ref[...])


def _outproj(x2d, conv2d, att2d, w_out, b_out, g, beta):
    t, d = x2d.shape
    row = lambda i: (i, 0)
    const = lambda i: (0, 0)
    return pl.pallas_call(
        _outproj_kernel,
        grid=(t // ROW_TILE,),
        in_specs=[
            pl.BlockSpec((ROW_TILE, d), row),
            pl.BlockSpec((ROW_TILE, CONV_CH), row),
            pl.BlockSpec((ROW_TILE, Q_COLS), row),
            pl.BlockSpec((d, d), const),
            pl.BlockSpec((1, d), const),
            pl.BlockSpec((1, d), const),
            pl.BlockSpec((1, d), const),
        ],
        out_specs=pl.BlockSpec((ROW_TILE, d), row),
        out_shape=jax.ShapeDtypeStruct((t, d), F32),
        compiler_params=_cparams(("arbitrary",)),
        name="outproj",
    )(x2d, conv2d, att2d, w_out, b_out, g, beta)


def _memkv_kernel(mem_ref, w_ref, k_ref, v_ref):
    kv = jnp.dot(mem_ref[0].astype(BF16), w_ref[...], preferred_element_type=F32)
    k_ref[0] = kv[:, :D_MODEL].astype(BF16)
    v_ref[0] = kv[:, D_MODEL:].astype(BF16)


def _memkv(mem, wkv):
    bsz, m, d = mem.shape
    return pl.pallas_call(
        _memkv_kernel,
        grid=(bsz,),
        in_specs=[pl.BlockSpec((1, m, d), lambda b: (b, 0, 0)),
                  pl.BlockSpec((d, 2 * d), lambda b: (0, 0))],
        out_specs=[pl.BlockSpec((1, m, d), lambda b: (b, 0, 0)),
                   pl.BlockSpec((1, m, d), lambda b: (b, 0, 0))],
        out_shape=[jax.ShapeDtypeStruct((bsz, m, d), BF16)] * 2,
        compiler_params=_cparams(("arbitrary",)),
        name="memkv",
    )(mem, wkv)


def _cross_kernel(x_ref, k_ref, v_ref, wq_ref, wo_ref, g_ref, beta_ref, rw_ref, o_ref, lt_ref):
    x = x_ref[0]
    q = jnp.dot(x.astype(BF16), wq_ref[...], preferred_element_type=F32) * (X_HEAD_DIM ** -0.5)
    q = q.astype(BF16)
    heads = []
    for h in range(X_HEADS):
        cols = slice(h * X_HEAD_DIM, (h + 1) * X_HEAD_DIM)
        logits = lax.dot_general(q[:, cols], k_ref[0, :, cols], (((1,), (1,)), ((), ())),
                                 preferred_element_type=F32)
        m = jnp.max(logits, axis=-1, keepdims=True)
        p = jnp.exp(logits - m)
        den = jnp.sum(p, axis=-1, keepdims=True)
        o = jnp.dot(p.astype(BF16), v_ref[0, :, cols], preferred_element_type=F32)
        heads.append((o / den).astype(BF16))
    att = jnp.concatenate(heads, axis=-1)
    cross = jnp.dot(att, wo_ref[...], preferred_element_type=F32)
    y = _layer_norm(ALPHA * x + cross, g_ref[...], beta_ref[...])
    o_ref[0] = y
    lt_ref[...] = lax.dot_general(rw_ref[...], y.astype(BF16), (((1,), (1,)), ((), ())),
                                  preferred_element_type=F32)


def _cross(x1, kmem, vmem, wq, wo, g, beta, rw_t):
    bsz, seq, d = x1.shape
    nt = seq // ROW_TILE
    row = lambda b, j: (b, j, 0)
    mem = lambda b, j: (b, 0, 0)
    const = lambda b, j: (0, 0)
    return pl.pallas_call(
        _cross_kernel,
        grid=(bsz, nt),
        in_specs=[
            pl.BlockSpec((1, ROW_TILE, d), row),
            pl.BlockSpec((1, MEM_LEN, d), mem),
            pl.BlockSpec((1, MEM_LEN, d), mem),
            pl.BlockSpec((d, d), const),
            pl.BlockSpec((d, d), const),
            pl.BlockSpec((1, d), const),
            pl.BlockSpec((1, d), const),
            pl.BlockSpec((N_EXPERTS, d), const),
        ],
        out_specs=[
            pl.BlockSpec((1, ROW_TILE, d), row),
            pl.BlockSpec((N_EXPERTS, ROW_TILE), lambda b, j: (0, b * nt + j)),
        ],
        out_shape=[
            jax.ShapeDtypeStruct((bsz, seq, d), F32),
            jax.ShapeDtypeStruct((N_EXPERTS, bsz * seq), F32),
        ],
        compiler_params=_cparams(("arbitrary", "arbitrary")),
        name="cross",
    )(x1, kmem, vmem, wq, wo, g, beta, rw_t)


def _route_kernel(lt_ref, rb_ref, tri_ref, pos_ref, gate_ref, cnt_ref):
    tn = ROUTE_TILE
    scores = _sigmoid(lt_ref[...])
    choice = scores + rb_ref[...]

    gscore = []
    member = lax.broadcasted_iota(jnp.int32, (GROUP_SIZE, tn), 0).astype(F32)
    for g in range(N_GROUPS):
        c = choice[g * GROUP_SIZE:(g + 1) * GROUP_SIZE, :]
        m1 = jnp.max(c, axis=0, keepdims=True)
        first = jnp.min(jnp.where(c == m1, member, float(GROUP_SIZE)), axis=0, keepdims=True)
        m2 = jnp.max(jnp.where(member == first, -jnp.inf, c), axis=0, keepdims=True)
        gscore.append(m1 + m2)

    keep_rows = []
    for g in range(N_GROUPS):
        beaten = jnp.zeros((1, tn), F32)
        for o in range(N_GROUPS):
            if o == g:
                continue
            ahead = (gscore[o] >= gscore[g]) if o < g else (gscore[o] > gscore[g])
            beaten = beaten + jnp.where(ahead, 1.0, 0.0)
        keep_rows.append(jnp.broadcast_to(beaten, (GROUP_SIZE, tn)))
    beaten_all = jnp.concatenate(keep_rows, axis=0)

    masked = jnp.where(beaten_all < TOPK_GROUPS, choice, -jnp.inf)
    eidx = lax.broadcasted_iota(jnp.int32, (N_EXPERTS, tn), 0).astype(F32)
    sel = jnp.zeros((N_EXPERTS, tn), F32)
    picks, weights = [], []
    for r in range(TOP_K):
        mx = jnp.max(masked, axis=0, keepdims=True)
        first = jnp.min(jnp.where(masked == mx, eidx, float(N_EXPERTS)), axis=0, keepdims=True)
        pick = eidx == first
        picks.append((pick, first))
        weights.append(jnp.sum(jnp.where(pick, scores, 0.0), axis=0, keepdims=True))
        masked = jnp.where(pick, -jnp.inf, masked)
        sel = jnp.where(pick, 1.0, sel)

    wsum = weights[0]
    for r in range(1, TOP_K):
        wsum = wsum + weights[r]

    count = jnp.sum(sel, axis=1, keepdims=True)
    run_len = jnp.floor((count + (RUN_ALIGN - 1.0)) * (1.0 / RUN_ALIGN)) * RUN_ALIGN
    run_len_b = jnp.broadcast_to(run_len, (N_EXPERTS, 128))
    er = lax.broadcasted_iota(jnp.int32, (N_EXPERTS, N_EXPERTS), 0)
    ec = lax.broadcasted_iota(jnp.int32, (N_EXPERTS, N_EXPERTS), 1)
    before = jnp.where(ec < er, 1.0, 0.0).astype(BF16)
    run_start = jnp.dot(before, run_len_b.astype(BF16), preferred_element_type=F32)[:, 0:1]
    incl = jnp.dot(sel.astype(BF16), tri_ref[...], preferred_element_type=F32)
    pos_mat = run_start + incl - sel
    cnt_ref[0] = run_len_b.astype(jnp.int32)

    for r in range(TOP_K):
        pick, _ = picks[r]
        pos_ref[0, r:r + 1, :] = jnp.sum(jnp.where(pick, pos_mat, 0.0), axis=0,
                                         keepdims=True).astype(jnp.int32)
        gate_ref[0, r:r + 1, :] = weights[r] / wsum * ROUTED_SCALE


def _route(logits_t, router_b, tri):
    e, t = logits_t.shape
    nt = t // ROUTE_TILE
    blk = lambda i: (i, 0, 0)
    return pl.pallas_call(
        _route_kernel,
        grid=(nt,),
        in_specs=[
            pl.BlockSpec((e, ROUTE_TILE), lambda i: (0, i)),
            pl.BlockSpec((e, 1), lambda i: (0, 0)),
            pl.BlockSpec((ROUTE_TILE, ROUTE_TILE), lambda i: (0, 0)),
        ],
        out_specs=[
            pl.BlockSpec((1, TOP_K, ROUTE_TILE), blk),
            pl.BlockSpec((1, TOP_K, ROUTE_TILE), blk),
            pl.BlockSpec((1, e, 128), blk),
        ],
        out_shape=[
            jax.ShapeDtypeStruct((nt, TOP_K, ROUTE_TILE), jnp.int32),
            jax.ShapeDtypeStruct((nt, TOP_K, ROUTE_TILE), F32),
            jax.ShapeDtypeStruct((nt, e, 128), jnp.int32),
        ],
        compiler_params=_cparams(("arbitrary",)),
        name="route",
    )(logits_t, router_b, tri)


def _for_each_piece(tables, tile, fn):
    for n_rows, per_tile, (local_ref, global_ref, count_ref) in (
            (BIG_PIECE, BIG_PER_TILE, tables[0:3]), (RUN_ALIGN, N_EXPERTS, tables[3:6])):
        count = count_ref[tile]

        def body(g, carry, n_rows=n_rows, per_tile=per_tile, local_ref=local_ref, global_ref=global_ref,
                 count=count):
            for u in range(PIECE_UNROLL):
                j = g * PIECE_UNROLL + u

                @pl.when(j < count)
                def _():
                    fn(pl.multiple_of(local_ref[tile * per_tile + j], RUN_ALIGN),
                       pl.multiple_of(global_ref[tile * per_tile + j], RUN_ALIGN), n_rows)
            return carry

        lax.fori_loop(0, (count + PIECE_UNROLL - 1) // PIECE_UNROLL, body, 0)


def _word_rows(start, size):
    if isinstance(start, int):
        first = start // WORD_ROWS
    else:
        first = pl.multiple_of(start >> (WORD_ROWS.bit_length() - 1), RUN_ALIGN // WORD_ROWS)
    return pl.ds(first, size // WORD_ROWS)


def _for_each_total_piece(total, fn):
    for piece in TOTAL_PIECES:
        @pl.when((total & piece) != 0)
        def _():
            fn(piece)


def _dispatch_kernel(*refs):
    tables, (rows_ref, total_ref, gap_ref, pos_ref, x_ref, xs_ref, stage, zeros, sem, zsem) = refs[:6], refs[6:]
    i = pl.program_id(0)
    nt = pl.num_programs(0)
    slot = i % 2
    tn = ROUTE_TILE

    def drain(tile, buf):
        def wait_rows(n):
            pltpu.make_async_copy(stage.at[buf, _word_rows(0, n), :], xs_ref.at[_word_rows(0, n), :],
                                  sem.at[buf]).wait()
        _for_each_total_piece(rows_ref[tile], wait_rows)

    @pl.when(i >= 2)
    def _():
        drain(i - 2, slot)

    xb = x_ref[...].astype(BF16)
    pos = pos_ref[0]
    row_iota = lax.broadcasted_iota(jnp.int32, (SORT_CHUNK, tn), 0).astype(F32).astype(BF16)
    one = jnp.ones((SORT_CHUNK, tn), BF16)
    zero = jnp.zeros((SORT_CHUNK, tn), BF16)

    def onehot_rows(c):
        rel = jnp.clip(pos - c * SORT_CHUNK, -1, SORT_CHUNK).astype(F32).astype(BF16)
        hit = rel[0:1, :] == row_iota
        for k in range(1, TOP_K):
            hit = hit | (rel[k:k + 1, :] == row_iota)
        return jnp.where(hit, one, zero)

    def sort_chunks(c2, carry):
        for c in (2 * c2, 2 * c2 + 1):
            p0 = pl.multiple_of(c * SORT_CHUNK, SORT_CHUNK)
            rows = jnp.dot(onehot_rows(c), xb, preferred_element_type=F32)
            stage[slot, _word_rows(p0, SORT_CHUNK), :] = pltpu.bitcast(rows.astype(BF16), jnp.int32)
        return carry

    lax.fori_loop(0, (rows_ref[i] + 2 * SORT_CHUNK - 1) // (2 * SORT_CHUNK), sort_chunks, 0)

    def start_piece(local, dst, n):
        pltpu.make_async_copy(stage.at[slot, _word_rows(local, n), :], xs_ref.at[_word_rows(dst, n), :],
                              sem.at[slot]).start()

    _for_each_piece(tables, i, start_piece)

    @pl.when(i == nt - 1)
    def _():
        @pl.when(nt >= 2)
        def _():
            drain(i - 1, 1 - slot)

        drain(i, slot)

        zeros[...] = jnp.zeros_like(zeros)

        def gap_copies(e, act):
            gap = gap_ref[N_EXPERTS + e]
            for piece in TOTAL_PIECES:
                if piece >= EXPERT_ROWS:
                    continue

                @pl.when((gap & piece) != 0)
                def _():
                    dst = gap_ref[e] + (gap & (-2 * piece))
                    act(pltpu.make_async_copy(zeros.at[_word_rows(0, piece), :],
                                              xs_ref.at[_word_rows(dst, piece), :], zsem))

        def zero_block(b):
            return pltpu.make_async_copy(zeros, xs_ref.at[_word_rows(b * EXPERT_ROWS, EXPERT_ROWS), :], zsem)

        first_free = total_ref[0] // EXPERT_ROWS
        n_blocks = xs_ref.shape[0] * WORD_ROWS // EXPERT_ROWS
        lax.fori_loop(0, N_EXPERTS, lambda e, c: (gap_copies(e, lambda cp: cp.start()), c)[1], 0)
        lax.fori_loop(first_free, n_blocks, lambda b, c: (zero_block(b).start(), c)[1], 0)
        lax.fori_loop(0, N_EXPERTS, lambda e, c: (gap_copies(e, lambda cp: cp.wait()), c)[1], 0)
        lax.fori_loop(first_free, n_blocks, lambda b, c: (zero_block(b).wait(), c)[1], 0)


def _dispatch(pieces, tile_rows, total, gaps, pos, x2d, n_rows):
    t, d = x2d.shape
    nt = t // ROUTE_TILE
    return pl.pallas_call(
        _dispatch_kernel,
        grid_spec=pltpu.PrefetchScalarGridSpec(
            num_scalar_prefetch=len(pieces) + 3,
            grid=(nt,),
            in_specs=[pl.BlockSpec((1, TOP_K, ROUTE_TILE), lambda i, *_: (i, 0, 0)),
                      pl.BlockSpec((ROUTE_TILE, d), lambda i, *_: (i, 0))],
            out_specs=pl.BlockSpec(memory_space=pl.ANY),
            scratch_shapes=[pltpu.VMEM((2, TILE_CAP // WORD_ROWS, d), jnp.int32),
                            pltpu.VMEM((EXPERT_ROWS // WORD_ROWS, d), jnp.int32),
                            pltpu.SemaphoreType.DMA((2,)),
                            pltpu.SemaphoreType.DMA(())],
        ),
        out_shape=jax.ShapeDtypeStruct((n_rows // WORD_ROWS, d), jnp.int32),
        compiler_params=_cparams(("arbitrary",)),
        name="dispatch",
    )(*pieces, tile_rows, total, gaps, pos, x2d)


def _experts_kernel(ie_ref, flag_ref, xs_ref, wg_ref, wu_ref, wd_ref, y_ref, wg_bf, wu_bf, wd_bf):
    w = pl.program_id(0)
    flags = flag_ref[w]
    valid = (flags & 1) != 0
    new_expert = (flags & 2) != 0

    @pl.when(jnp.logical_not(valid))
    def _():
        y_ref[...] = jnp.zeros_like(y_ref)

    @pl.when(new_expert)
    def _():
        wg_bf[...] = wg_ref[0].astype(BF16)
        wu_bf[...] = wu_ref[0].astype(BF16)
        wd_bf[...] = wd_ref[0].astype(BF16)

    @pl.when(valid)
    def _():
        sub = EXPERT_ROWS // WORD_ROWS // EXPERT_SPLIT
        groups = [slice(s * sub, (s + 1) * sub) for s in range(EXPERT_SPLIT)]
        xs = [pltpu.bitcast(xs_ref[g, :], BF16) for g in groups]
        gu = [(jnp.dot(x, wg_bf[...], preferred_element_type=F32),
               jnp.dot(x, wu_bf[...], preferred_element_type=F32)) for x in xs]
        hs = [(gte * _sigmoid(gte) * up).astype(BF16) for gte, up in gu]
        for g, h in zip(groups, hs):
            y = jnp.dot(h, wd_bf[...], preferred_element_type=F32)
            y_ref[g, :] = pltpu.bitcast(y.astype(BF16), jnp.int32)


def _experts(block_expert, block_flags, xs, wg, wu, wd):
    n, d = xs.shape
    ff = wg.shape[-1]
    return pl.pallas_call(
        _experts_kernel,
        grid_spec=pltpu.PrefetchScalarGridSpec(
            num_scalar_prefetch=2,
            grid=(block_expert.shape[0],),
            in_specs=[
                pl.BlockSpec((EXPERT_ROWS // WORD_ROWS, d), lambda w, ie, fl: (w, 0)),
                pl.BlockSpec((1, d, ff), lambda w, ie, fl: (ie[w], 0, 0)),
                pl.BlockSpec((1, d, ff), lambda w, ie, fl: (ie[w], 0, 0)),
                pl.BlockSpec((1, ff, d), lambda w, ie, fl: (ie[w], 0, 0)),
            ],
            out_specs=pl.BlockSpec((EXPERT_ROWS // WORD_ROWS, d), lambda w, ie, fl: (w, 0)),
            scratch_shapes=[pltpu.VMEM((d, ff), BF16), pltpu.VMEM((d, ff), BF16), pltpu.VMEM((ff, d), BF16)],
        ),
        out_shape=jax.ShapeDtypeStruct((n, d), jnp.int32),
        compiler_params=_cparams(("arbitrary",)),
        name="experts",
    )(block_expert, block_flags, xs, wg, wu, wd)


def _block_table(seg_rows, n_rows):
    block_end = jnp.cumsum(seg_rows // EXPERT_ROWS)
    w = jnp.arange(n_rows // EXPERT_ROWS, dtype=jnp.int32)
    valid = w < block_end[-1]
    e = jnp.sum((block_end[None, :] <= jnp.minimum(w, block_end[-1] - 1)[:, None]).astype(jnp.int32), axis=1)
    e = jnp.minimum(e, N_EXPERTS - 1)
    prev_e = jnp.concatenate([jnp.full((1,), -1, jnp.int32), e[:-1]])
    flags = valid.astype(jnp.int32) + 2 * (valid & (e != prev_e)).astype(jnp.int32)
    return e, flags


def _combine_kernel(*refs):
    tables = refs[:6]
    (rows_ref, pos_ref, gate_ref, x_ref, sg_ref, su_ref, sd_ref, g_ref, beta_ref, yb_ref, o_ref,
     stage, acc, lane_tile, lane_gate, sem) = refs[6:]
    i = pl.program_id(0)
    nt = pl.num_programs(0)
    slot = i % 2
    tn = ROUTE_TILE

    def start_tile(tile, buf):
        def start_piece(local, src, n):
            pltpu.make_async_copy(yb_ref.at[_word_rows(src, n), :], stage.at[buf, _word_rows(local, n), :],
                                  sem.at[buf]).start()
        _for_each_piece(tables, tile, start_piece)

    @pl.when(i == 0)
    def _():
        stage[...] = jnp.zeros_like(stage)
        start_tile(0, 0)

    @pl.when(i + 1 < nt)
    def _():
        start_tile(i + 1, 1 - slot)

    x = x_ref[...]
    xb = x.astype(BF16)
    gte = jnp.dot(xb, sg_ref[...], preferred_element_type=F32)
    up = jnp.dot(xb, su_ref[...], preferred_element_type=F32)
    h = (gte * _sigmoid(gte) * up).astype(BF16)
    acc[...] = jnp.dot(h, sd_ref[...], preferred_element_type=F32)

    def wait_rows(n):
        pltpu.make_async_copy(yb_ref.at[_word_rows(0, n), :], stage.at[slot, _word_rows(0, n), :],
                              sem.at[slot]).wait()

    _for_each_total_piece(rows_ref[i], wait_rows)

    lanes = 128
    lane_iota = lax.broadcasted_iota(jnp.int32, (tn, lanes), 1)
    for k in range(TOP_K):
        p = jnp.broadcast_to(pos_ref[:, k:k + 1], (tn, lanes))
        g = jnp.broadcast_to(gate_ref[:, k:k + 1], (tn, lanes))
        lane_tile[k] = (p >> (lanes.bit_length() - 1)).astype(F32).astype(BF16)
        lane_gate[k] = jnp.where((p & (lanes - 1)) == lane_iota, g, 0.0).astype(BF16)
    zero = jnp.zeros((tn, lanes), BF16)

    def gate_matrix(c):
        cols = []
        for j in range(COMBINE_CHUNK // lanes):
            tile = jnp.asarray(c * (COMBINE_CHUNK // lanes) + j, jnp.int32).astype(F32).astype(BF16)
            w = zero
            for k in range(TOP_K):
                w = w + jnp.where(lane_tile[k] == tile, lane_gate[k], zero)
            cols.append(w)
        return jnp.concatenate(cols, axis=1)

    def weigh_chunks(c2, carry):
        part = []
        for c in (2 * c2, 2 * c2 + 1):
            p0 = pl.multiple_of(c * COMBINE_CHUNK, COMBINE_CHUNK)
            rows = pltpu.bitcast(stage[slot, _word_rows(p0, COMBINE_CHUNK), :], BF16)
            part.append(jnp.dot(gate_matrix(c), rows, preferred_element_type=F32))
        acc[...] += part[0] + part[1]
        return carry

    lax.fori_loop(0, (rows_ref[i] + 2 * COMBINE_CHUNK - 1) // (2 * COMBINE_CHUNK), weigh_chunks, 0)
    o_ref[...] = _layer_norm(ALPHA * x + acc[...], g_ref[...], beta_ref[...])


def _combine(pieces, tile_rows, pos_t, gate_t, x2d, sg, su, sd, g, beta, yb):
    t, d = x2d.shape
    nt = t // ROUTE_TILE
    ff = sg.shape[-1]
    row = lambda i, *_: (i, 0)
    const = lambda i, *_: (0, 0)
    return pl.pallas_call(
        _combine_kernel,
        grid_spec=pltpu.PrefetchScalarGridSpec(
            num_scalar_prefetch=len(pieces) + 1,
            grid=(nt,),
            in_specs=[
                pl.BlockSpec((ROUTE_TILE, TOP_K), row),
                pl.BlockSpec((ROUTE_TILE, TOP_K), row),
                pl.BlockSpec((ROUTE_TILE, d), row),
                pl.BlockSpec((d, ff), const),
                pl.BlockSpec((d, ff), const),
                pl.BlockSpec((ff, d), const),
                pl.BlockSpec((1, d), const),
                pl.BlockSpec((1, d), const),
                pl.BlockSpec(memory_space=pl.ANY),
            ],
            out_specs=pl.BlockSpec((ROUTE_TILE, d), row),
            scratch_shapes=[pltpu.VMEM((2, TILE_CAP // WORD_ROWS, d), jnp.int32),
                            pltpu.VMEM((ROUTE_TILE, d), F32),
                            pltpu.VMEM((TOP_K, ROUTE_TILE, 128), BF16),
                            pltpu.VMEM((TOP_K, ROUTE_TILE, 128), BF16),
                            pltpu.SemaphoreType.DMA((2,))],
        ),
        out_shape=jax.ShapeDtypeStruct((t, d), F32),
        compiler_params=_cparams(("arbitrary",)),
        name="combine",
    )(*pieces, tile_rows, pos_t, gate_t, x2d, sg, su, sd, g, beta, yb)


def _piece_tables(cnt):
    lbase = jnp.cumsum(cnt, axis=1) - cnt
    per_expert = jnp.sum(cnt, axis=0)
    seg_rows = -(-per_expert // EXPERT_ROWS) * EXPERT_ROWS
    seg_start = jnp.cumsum(seg_rows) - seg_rows
    gbase = seg_start[None, :] + jnp.cumsum(cnt, axis=0) - cnt
    gaps = jnp.concatenate([seg_start + per_expert, seg_rows - per_expert]).astype(jnp.int32)

    def listed(n_pieces, start_off, piece_rows, length):
        last = jnp.cumsum(n_pieces, axis=1)
        first = last - n_pieces
        j = jnp.arange(length, dtype=jnp.int32)[None, :, None]
        own = (j >= first[:, None, :]) & (j < last[:, None, :])
        off = (start_off[:, None, :] + (j - first[:, None, :]) * piece_rows)
        pick = lambda base: jnp.sum(jnp.where(own, base[:, None, :] + off, 0), axis=-1).reshape(-1).astype(jnp.int32)
        return pick(lbase), pick(gbase), last[:, -1].astype(jnp.int32)

    n_big = cnt // BIG_PIECE
    big = listed(n_big, jnp.zeros_like(cnt), BIG_PIECE, BIG_PER_TILE)
    small = listed((cnt // RUN_ALIGN) % 2, n_big * BIG_PIECE, RUN_ALIGN, N_EXPERTS)
    return big + small, jnp.sum(cnt, axis=1).astype(jnp.int32), seg_rows, gaps


def kernel(x, mem, w_in, b_in, conv_w, conv_b, conv_ln_g, conv_ln_b, attn_sinks, rel_bias, w_out, b_out, ln1_g, ln1_b, xq_w, xkv_w, xo_w, ln2_g, ln2_b, router_w, router_b, exp_gate, exp_up, exp_down, sh_gate, sh_up, sh_down, ln3_g, ln3_b):
    bsz, seq, d = x.shape
    t = bsz * seq
    bias_tab, sink_tab = _band_tables(rel_bias, attn_sinks[0])
    tri = (jnp.arange(ROUTE_TILE)[:, None] <= jnp.arange(ROUTE_TILE)[None, :]).astype(BF16)
    row = lambda p: p.reshape(1, -1)
    for l in range(DEPTH):
        conv_out, q, k, v = _mix(x, w_in[l].astype(BF16), row(b_in[l]), conv_w[l], row(conv_b[l]),
                                 row(conv_ln_g[l]), row(conv_ln_b[l]))
        att = _swa(q, k, v, bias_tab, sink_tab)
        x1 = _outproj(x.reshape(t, d), conv_out.reshape(t, CONV_CH), att.reshape(t, Q_COLS),
                      w_out[l].astype(BF16), row(b_out[l]), row(ln1_g[l]), row(ln1_b[l]))
        kmem, vmem = _memkv(mem, xkv_w[l].astype(BF16))
        x2, logits_t = _cross(x1.reshape(bsz, seq, d), kmem, vmem, xq_w[l].astype(BF16),
                              xo_w[l].astype(BF16), row(ln2_g[l]), row(ln2_b[l]),
                              router_w[l].T.astype(BF16))
        x2 = x2.reshape(t, d)
        pos, gate, cnt = _route(logits_t, router_b[l].reshape(-1, 1), tri)
        nt = t // ROUTE_TILE
        n_rows = (-(-(t * TOP_K + nt * N_EXPERTS * (RUN_ALIGN - 1)) // EXPERT_ROWS) + N_EXPERTS) * EXPERT_ROWS
        pieces, tile_rows, seg_rows, gaps = _piece_tables(cnt[:, :, 0])
        block_expert, block_flags = _block_table(seg_rows, n_rows)
        xs = _dispatch(pieces, tile_rows, jnp.sum(seg_rows).reshape(1).astype(jnp.int32), gaps, pos, x2, n_rows)
        yb = _experts(block_expert, block_flags, xs, exp_gate[l], exp_up[l], exp_down[l])
        pos_t = jnp.transpose(pos, (0, 2, 1)).reshape(t, TOP_K)
        gate_t = jnp.transpose(gate, (0, 2, 1)).reshape(t, TOP_K)
        x = _combine(pieces, tile_rows, pos_t, gate_t, x2, sh_gate[l].astype(BF16),
                     sh_up[l].astype(BF16), sh_down[l].astype(BF16), row(ln3_g[l]), row(ln3_b[l]),
                     yb).reshape(bsz, seq, d)
    return x
```

```python
import functools
import math

import jax
import jax.numpy as jnp
from jax import lax
from jax.experimental import pallas as pl
from jax.experimental.pallas import tpu as pltpu

D_MODEL = 1024
MEM_LEN = 256
HEAD_DIM = 64
CONV_CH = D_MODEL // 2
CONV_WIDTH = 31
ATT_HEADS = 8
KV_HEADS = 2
WINDOW = 128
BLOCK = 128
REL_BUCKETS = 32
REL_MAX_DIST = 128
Q_COLS = ATT_HEADS * HEAD_DIM
KV_COLS = KV_HEADS * HEAD_DIM
IN_COLS = 2 * CONV_CH + Q_COLS + 2 * KV_COLS
X_HEADS = 4
X_HEAD_DIM = D_MODEL // X_HEADS
N_EXPERTS = 64
TOP_K = 8
N_GROUPS = 8
GROUP_SIZE = N_EXPERTS // N_GROUPS
TOPK_GROUPS = 4
EXPERT_FF = D_MODEL // 4
ROUTED_SCALE = 2.5
DEPTH = 1
ALPHA = (2 * DEPTH) ** 0.25
LN_EPS = 1e-5
NEG_INF = -1e30

F32 = jnp.float32
BF16 = jnp.bfloat16

VMEM_LIMIT_BYTES = 56 * 1024 * 1024

ROW_TILE = 512
CONV_ROWS = 32
SUBLANES = 8
CONV_HALO = 32
SWA_TILE = 512
ROUTE_TILE = 512
EXPERT_ROWS = 1024
EXPERT_SPLIT = 2
RUN_ALIGN = 16
RUN_ALIGN_LOG2 = RUN_ALIGN.bit_length() - 1
WORD_ROWS = 2
TILE_CAP = ROUTE_TILE * TOP_K + N_EXPERTS * RUN_ALIGN
BIG_PIECE = 2 * RUN_ALIGN
BIG_PER_TILE = TILE_CAP // BIG_PIECE
PIECE_UNROLL = 4
TOTAL_PIECES = tuple(1 << b for b in range(TILE_CAP.bit_length() - 1, RUN_ALIGN_LOG2 - 1, -1))
SORT_CHUNK = 256
COMBINE_CHUNK = 512


def _cparams(sem):
    return pltpu.CompilerParams(dimension_semantics=sem, vmem_limit_bytes=VMEM_LIMIT_BYTES)


def _layer_norm(h, g, b):
    mu = jnp.mean(h, axis=-1, keepdims=True)
    d = h - mu
    var = jnp.mean(d * d, axis=-1, keepdims=True)
    return d * lax.rsqrt(var + LN_EPS) * g + b


def _sigmoid(x):
    return 1.0 / (1.0 + jnp.exp(-x))


def _mix_kernel(x_ref, w_ref, b_ref, cw_ref, cb_ref, cg_ref, cbeta_ref,
                conv_ref, q_ref, k_ref, v_ref, u_ext, u_sh):
    j = pl.program_id(1)
    xb = x_ref[0].astype(BF16)
    proj = jnp.dot(xb, w_ref[...], preferred_element_type=F32) + b_ref[...]
    a = proj[:, :CONV_CH]
    g = proj[:, CONV_CH:2 * CONV_CH]
    q_ref[0] = (proj[:, 2 * CONV_CH:2 * CONV_CH + Q_COLS] * (HEAD_DIM ** -0.5)).astype(BF16)
    k_ref[0] = proj[:, 2 * CONV_CH + Q_COLS:2 * CONV_CH + Q_COLS + KV_COLS].astype(BF16)
    v_ref[0] = proj[:, 2 * CONV_CH + Q_COLS + KV_COLS:].astype(BF16)

    @pl.when(j == 0)
    def _():
        u_ext[0:CONV_HALO, :] = jnp.zeros((CONV_HALO, CONV_CH), F32)

    u_ext[CONV_HALO:CONV_HALO + ROW_TILE, :] = a * _sigmoid(g)

    first_tap = CONV_HALO - (CONV_WIDTH - 1)
    shifted_rows = u_sh.shape[1]
    for r in range(1, SUBLANES):
        u_sh[r - 1] = u_ext[r:r + shifted_rows, :]

    for c in range(ROW_TILE // CONV_ROWS):
        acc = jnp.zeros((CONV_ROWS, CONV_CH), F32) + cb_ref[...]
        for t in range(CONV_WIDTH):
            r = (first_tap + t) % SUBLANES
            base = c * CONV_ROWS + (first_tap + t) - r
            if r == 0:
                taps = u_ext[base:base + CONV_ROWS, :]
            else:
                taps = u_sh[r - 1, base:base + CONV_ROWS, :]
            acc = acc + taps * cw_ref[t:t + 1, :]
        y = _layer_norm(acc, cg_ref[...], cbeta_ref[...])
        conv_ref[0, c * CONV_ROWS:(c + 1) * CONV_ROWS, :] = (y * _sigmoid(y)).astype(BF16)

    u_ext[0:CONV_HALO, :] = u_ext[ROW_TILE:ROW_TILE + CONV_HALO, :]


def _mix(x, w_in, b_in, conv_w, conv_b, conv_g, conv_beta):
    bsz, seq, d = x.shape
    nt = seq // ROW_TILE
    row = lambda b, j: (b, j, 0)
    const2 = lambda b, j: (0, 0)
    return pl.pallas_call(
        _mix_kernel,
        grid=(bsz, nt),
        in_specs=[
            pl.BlockSpec((1, ROW_TILE, d), row),
            pl.BlockSpec((d, IN_COLS), const2),
            pl.BlockSpec((1, IN_COLS), const2),
            pl.BlockSpec((CONV_WIDTH, CONV_CH), const2),
            pl.BlockSpec((1, CONV_CH), const2),
            pl.BlockSpec((1, CONV_CH), const2),
            pl.BlockSpec((1, CONV_CH), const2),
        ],
        out_specs=[
            pl.BlockSpec((1, ROW_TILE, CONV_CH), row),
            pl.BlockSpec((1, ROW_TILE, Q_COLS), row),
            pl.BlockSpec((1, ROW_TILE, KV_COLS), row),
            pl.BlockSpec((1, ROW_TILE, KV_COLS), row),
        ],
        out_shape=[
            jax.ShapeDtypeStruct((bsz, seq, CONV_CH), BF16),
            jax.ShapeDtypeStruct((bsz, seq, Q_COLS), BF16),
            jax.ShapeDtypeStruct((bsz, seq, KV_COLS), BF16),
            jax.ShapeDtypeStruct((bsz, seq, KV_COLS), BF16),
        ],
        scratch_shapes=[pltpu.VMEM((ROW_TILE + CONV_HALO, CONV_CH), F32),
                        pltpu.VMEM((SUBLANES - 1, ROW_TILE + CONV_HALO - SUBLANES, CONV_CH), F32)],
        compiler_params=_cparams(("arbitrary", "arbitrary")),
        name="mix",
    )(x, w_in, b_in, conv_w, conv_b, conv_g, conv_beta)


def _swa_kernel(q_ref, kp_ref, kc_ref, vp_ref, vc_ref, bias_ref, sink_ref, x_ref, conv_ref, w_ref, b_ref,
                g_ref, beta_ref, o_ref):
    j = pl.program_id(1)
    mix = jnp.dot(conv_ref[0], w_ref[0:CONV_CH, :], preferred_element_type=F32)
    rows = BLOCK + SWA_TILE
    lane = lax.broadcasted_iota(jnp.int32, (rows, 2 * HEAD_DIM), 1)
    low = lane < HEAD_DIM

    def placements(prev_ref, cur_ref):
        t = jnp.concatenate([prev_ref[0], cur_ref[0]], axis=0).astype(F32)
        tr = pltpu.roll(t, HEAD_DIM, 1)
        zero = jnp.zeros_like(t)
        kv0_low = jnp.where(low, t, zero).astype(BF16)
        kv1_high = jnp.where(low, zero, t).astype(BF16)
        kv1_low = jnp.where(low, tr, zero).astype(BF16)
        kv0_high = jnp.where(low, zero, tr).astype(BF16)
        return (kv0_low, kv0_high, kv1_low, kv1_high)

    ks = placements(kp_ref, kc_ref)
    vs = placements(vp_ref, vc_ref)
    slab = 2 * HEAD_DIM
    att = []
    for i in range(SWA_TILE // BLOCK):
        q = q_ref[0, i * BLOCK:(i + 1) * BLOCK, :]
        q_kv0 = jnp.concatenate([q[:, 0:slab], q[:, slab:2 * slab]], axis=0)
        q_kv1 = jnp.concatenate([q[:, 2 * slab:3 * slab], q[:, 3 * slab:4 * slab]], axis=0)
        band = slice(i * BLOCK, (i + 2) * BLOCK)
        outs = []
        for s in range(4):
            qs = q_kv0 if s < 2 else q_kv1
            bias = bias_ref[1, s]
            if i == 0:
                bias = jnp.where(j == 0, bias_ref[0, s], bias)
            logits = lax.dot_general(qs, ks[s][band], (((1,), (1,)), ((), ())),
                                     preferred_element_type=F32) + bias
            sink = sink_ref[s]
            m = jnp.maximum(jnp.max(logits, axis=-1, keepdims=True), sink)
            p = jnp.exp(logits - m)
            den = jnp.sum(p, axis=-1, keepdims=True) + jnp.exp(sink - m)
            o = jnp.dot(p.astype(BF16), vs[s][band], preferred_element_type=F32)
            outs.append(o / den)
        o_kv0 = outs[0] + outs[1]
        o_kv1 = outs[2] + outs[3]
        att.append(jnp.concatenate([o_kv0[0:BLOCK], o_kv0[BLOCK:2 * BLOCK], o_kv1[0:BLOCK],
                                    o_kv1[BLOCK:2 * BLOCK]], axis=1).astype(BF16))

    mix = mix + jnp.dot(jnp.concatenate(att, axis=0), w_ref[CONV_CH:, :], preferred_element_type=F32)
    h = ALPHA * x_ref[0] + mix + b_ref[...]
    o_ref[0] = _layer_norm(h, g_ref[...], beta_ref[...])


def _swa(q, k, v, bias_tab, sink_tab, x, conv_out, w_out, b_out, g, beta):
    bsz, seq, d = x.shape
    per = SWA_TILE // BLOCK
    cur = lambda b, n: (b, n, 0)
    prev = lambda b, n: (b, jnp.maximum(n * per - 1, 0), 0)
    whole = lambda b, n: (0, 0, 0, 0)
    const = lambda b, n: (0, 0)
    return pl.pallas_call(
        _swa_kernel,
        grid=(bsz, seq // SWA_TILE),
        in_specs=[
            pl.BlockSpec((1, SWA_TILE, Q_COLS), cur),
            pl.BlockSpec((1, BLOCK, KV_COLS), prev),
            pl.BlockSpec((1, SWA_TILE, KV_COLS), cur),
            pl.BlockSpec((1, BLOCK, KV_COLS), prev),
            pl.BlockSpec((1, SWA_TILE, KV_COLS), cur),
            pl.BlockSpec((2, 4, 2 * BLOCK, 2 * BLOCK), whole),
            pl.BlockSpec((4, 2 * BLOCK, 1), lambda b, n: (0, 0, 0)),
            pl.BlockSpec((1, SWA_TILE, d), cur),
            pl.BlockSpec((1, SWA_TILE, CONV_CH), cur),
            pl.BlockSpec((d, d), const),
            pl.BlockSpec((1, d), const),
            pl.BlockSpec((1, d), const),
            pl.BlockSpec((1, d), const),
        ],
        out_specs=pl.BlockSpec((1, SWA_TILE, d), cur),
        out_shape=jax.ShapeDtypeStruct((bsz, seq, d), F32),
        compiler_params=_cparams(("arbitrary", "arbitrary")),
        name="swa",
    )(q, k, k, v, v, bias_tab, sink_tab, x, conv_out, w_out, b_out, g, beta)


def _t5_bucket(dist):
    n = jnp.maximum(dist, 0)
    exact = REL_BUCKETS // 2
    large = exact + (jnp.log(jnp.maximum(n, 1).astype(F32) / exact)
                     / math.log(REL_MAX_DIST / exact) * (REL_BUCKETS - exact)).astype(jnp.int32)
    large = jnp.minimum(large, REL_BUCKETS - 1)
    return jnp.where(n < exact, n, large)


def _band_tables(rel_bias, sinks):
    qi = jnp.arange(BLOCK)[:, None]
    kj = jnp.arange(2 * BLOCK)[None, :]
    dist = qi + BLOCK - kj
    bucket = _t5_bucket(dist)
    bias = jnp.zeros((ATT_HEADS, BLOCK, 2 * BLOCK), F32)
    for bkt in range(REL_BUCKETS):
        bias = jnp.where(bucket[None] == bkt, rel_bias[bkt].astype(F32)[:, None, None], bias)
    in_window = (dist >= 0) & (dist < WINDOW)
    masks = jnp.stack([in_window & (kj >= BLOCK), in_window])
    masked = jnp.where(masks[:, None], bias[None], NEG_INF)
    pairs = ((0, 2), (1, 3), (4, 6), (5, 7))
    bias_tab = jnp.stack([jnp.concatenate([masked[:, a], masked[:, b]], axis=1) for a, b in pairs], axis=1)
    s = sinks.astype(F32)
    sink_tab = jnp.stack([jnp.concatenate([jnp.full((BLOCK, 1), s[a]), jnp.full((BLOCK, 1), s[b])], axis=0)
                          for a, b in pairs])
    return bias_tab, sink_tab


def _memkv_kernel(mem_ref, w_ref, k_ref, v_ref):
    kv = jnp.dot(mem_ref[0].astype(BF16), w_ref[...], preferred_element_type=F32)
    k_ref[0] = kv[:, :D_MODEL].astype(BF16)
    v_ref[0] = kv[:, D_MODEL:].astype(BF16)


def _memkv(mem, wkv):
    bsz, m, d = mem.shape
    return pl.pallas_call(
        _memkv_kernel,
        grid=(bsz,),
        in_specs=[pl.BlockSpec((1, m, d), lambda b: (b, 0, 0)),
                  pl.BlockSpec((d, 2 * d), lambda b: (0, 0))],
        out_specs=[pl.BlockSpec((1, m, d), lambda b: (b, 0, 0)),
                   pl.BlockSpec((1, m, d), lambda b: (b, 0, 0))],
        out_shape=[jax.ShapeDtypeStruct((bsz, m, d), BF16)] * 2,
        compiler_params=_cparams(("arbitrary",)),
        name="memkv",
    )(mem, wkv)


def _cross_kernel(x_ref, k_ref, v_ref, wq_ref, wo_ref, g_ref, beta_ref, rw_ref, o_ref, lt_ref):
    x = x_ref[0]
    q = jnp.dot(x.astype(BF16), wq_ref[...], preferred_element_type=F32) * (X_HEAD_DIM ** -0.5)
    q = q.astype(BF16)
    heads = []
    for h in range(X_HEADS):
        cols = slice(h * X_HEAD_DIM, (h + 1) * X_HEAD_DIM)
        logits = lax.dot_general(q[:, cols], k_ref[0, :, cols], (((1,), (1,)), ((), ())),
                                 preferred_element_type=F32)
        m = jnp.max(logits, axis=-1, keepdims=True)
        p = jnp.exp(logits - m)
        den = jnp.sum(p, axis=-1, keepdims=True)
        o = jnp.dot(p.astype(BF16), v_ref[0, :, cols], preferred_element_type=F32)
        heads.append((o / den).astype(BF16))
    att = jnp.concatenate(heads, axis=-1)
    cross = jnp.dot(att, wo_ref[...], preferred_element_type=F32)
    y = _layer_norm(ALPHA * x + cross, g_ref[...], beta_ref[...])
    o_ref[0] = y
    lt_ref[...] = lax.dot_general(rw_ref[...], y.astype(BF16), (((1,), (1,)), ((), ())),
                                  preferred_element_type=F32)


def _cross(x1, kmem, vmem, wq, wo, g, beta, rw_t):
    bsz, seq, d = x1.shape
    nt = seq // ROW_TILE
    row = lambda b, j: (b, j, 0)
    mem = lambda b, j: (b, 0, 0)
    const = lambda b, j: (0, 0)
    return pl.pallas_call(
        _cross_kernel,
        grid=(bsz, nt),
        in_specs=[
            pl.BlockSpec((1, ROW_TILE, d), row),
            pl.BlockSpec((1, MEM_LEN, d), mem),
            pl.BlockSpec((1, MEM_LEN, d), mem),
            pl.BlockSpec((d, d), const),
            pl.BlockSpec((d, d), const),
            pl.BlockSpec((1, d), const),
            pl.BlockSpec((1, d), const),
            pl.BlockSpec((N_EXPERTS, d), const),
        ],
        out_specs=[
            pl.BlockSpec((1, ROW_TILE, d), row),
            pl.BlockSpec((N_EXPERTS, ROW_TILE), lambda b, j: (0, b * nt + j)),
        ],
        out_shape=[
            jax.ShapeDtypeStruct((bsz, seq, d), F32),
            jax.ShapeDtypeStruct((N_EXPERTS, bsz * seq), F32),
        ],
        compiler_params=_cparams(("arbitrary", "arbitrary")),
        name="cross",
    )(x1, kmem, vmem, wq, wo, g, beta, rw_t)


def _route_kernel(lt_ref, rb_ref, tri_ref, pos_ref, gate_ref, cnt_ref):
    tn = ROUTE_TILE
    scores = _sigmoid(lt_ref[...])
    choice = scores + rb_ref[...]

    gscore = []
    member = lax.broadcasted_iota(jnp.int32, (GROUP_SIZE, tn), 0).astype(F32)
    for g in range(N_GROUPS):
        c = choice[g * GROUP_SIZE:(g + 1) * GROUP_SIZE, :]
        m1 = jnp.max(c, axis=0, keepdims=True)
        first = jnp.min(jnp.where(c == m1, member, float(GROUP_SIZE)), axis=0, keepdims=True)
        m2 = jnp.max(jnp.where(member == first, -jnp.inf, c), axis=0, keepdims=True)
        gscore.append(m1 + m2)

    keep_rows = []
    for g in range(N_GROUPS):
        beaten = jnp.zeros((1, tn), F32)
        for o in range(N_GROUPS):
            if o == g:
                continue
            ahead = (gscore[o] >= gscore[g]) if o < g else (gscore[o] > gscore[g])
            beaten = beaten + jnp.where(ahead, 1.0, 0.0)
        keep_rows.append(jnp.broadcast_to(beaten, (GROUP_SIZE, tn)))
    beaten_all = jnp.concatenate(keep_rows, axis=0)

    masked = jnp.where(beaten_all < TOPK_GROUPS, choice, -jnp.inf)
    eidx = lax.broadcasted_iota(jnp.int32, (N_EXPERTS, tn), 0).astype(F32)
    sel = jnp.zeros((N_EXPERTS, tn), F32)
    picks, weights = [], []
    for r in range(TOP_K):
        mx = jnp.max(masked, axis=0, keepdims=True)
        first = jnp.min(jnp.where(masked == mx, eidx, float(N_EXPERTS)), axis=0, keepdims=True)
        pick = eidx == first
        picks.append((pick, first))
        weights.append(jnp.sum(jnp.where(pick, scores, 0.0), axis=0, keepdims=True))
        masked = jnp.where(pick, -jnp.inf, masked)
        sel = jnp.where(pick, 1.0, sel)

    wsum = weights[0]
    for r in range(1, TOP_K):
        wsum = wsum + weights[r]

    count = jnp.sum(sel, axis=1, keepdims=True)
    run_len = jnp.floor((count + (RUN_ALIGN - 1.0)) * (1.0 / RUN_ALIGN)) * RUN_ALIGN
    run_len_b = jnp.broadcast_to(run_len, (N_EXPERTS, 128))
    er = lax.broadcasted_iota(jnp.int32, (N_EXPERTS, N_EXPERTS), 0)
    ec = lax.broadcasted_iota(jnp.int32, (N_EXPERTS, N_EXPERTS), 1)
    before = jnp.where(ec < er, 1.0, 0.0).astype(BF16)
    run_start = jnp.dot(before, run_len_b.astype(BF16), preferred_element_type=F32)[:, 0:1]
    incl = jnp.dot(sel.astype(BF16), tri_ref[...], preferred_element_type=F32)
    pos_mat = run_start + incl - sel
    cnt_ref[0] = run_len_b.astype(jnp.int32)

    for r in range(TOP_K):
        pick, _ = picks[r]
        pos_ref[0, r:r + 1, :] = jnp.sum(jnp.where(pick, pos_mat, 0.0), axis=0,
                                         keepdims=True).astype(jnp.int32)
        gate_ref[0, r:r + 1, :] = weights[r] / wsum * ROUTED_SCALE


def _route(logits_t, router_b, tri):
    e, t = logits_t.shape
    nt = t // ROUTE_TILE
    blk = lambda i: (i, 0, 0)
    return pl.pallas_call(
        _route_kernel,
        grid=(nt,),
        in_specs=[
            pl.BlockSpec((e, ROUTE_TILE), lambda i: (0, i)),
            pl.BlockSpec((e, 1), lambda i: (0, 0)),
            pl.BlockSpec((ROUTE_TILE, ROUTE_TILE), lambda i: (0, 0)),
        ],
        out_specs=[
            pl.BlockSpec((1, TOP_K, ROUTE_TILE), blk),
            pl.BlockSpec((1, TOP_K, ROUTE_TILE), blk),
            pl.BlockSpec((1, e, 128), blk),
        ],
        out_shape=[
            jax.ShapeDtypeStruct((nt, TOP_K, ROUTE_TILE), jnp.int32),
            jax.ShapeDtypeStruct((nt, TOP_K, ROUTE_TILE), F32),
            jax.ShapeDtypeStruct((nt, e, 128), jnp.int32),
        ],
        compiler_params=_cparams(("arbitrary",)),
        name="route",
    )(logits_t, router_b, tri)


def _for_each_piece(tables, tile, fn):
    for n_rows, per_tile, (local_ref, global_ref, count_ref) in (
            (BIG_PIECE, BIG_PER_TILE, tables[0:3]), (RUN_ALIGN, N_EXPERTS, tables[3:6])):
        count = count_ref[tile]

        def body(g, carry, n_rows=n_rows, per_tile=per_tile, local_ref=local_ref, global_ref=global_ref,
                 count=count):
            for u in range(PIECE_UNROLL):
                j = g * PIECE_UNROLL + u

                @pl.when(j < count)
                def _():
                    fn(pl.multiple_of(local_ref[tile * per_tile + j], RUN_ALIGN),
                       pl.multiple_of(global_ref[tile * per_tile + j], RUN_ALIGN), n_rows)
            return carry

        lax.fori_loop(0, (count + PIECE_UNROLL - 1) // PIECE_UNROLL, body, 0)


def _word_rows(start, size):
    if isinstance(start, int):
        first = start // WORD_ROWS
    else:
        first = pl.multiple_of(start >> (WORD_ROWS.bit_length() - 1), RUN_ALIGN // WORD_ROWS)
    return pl.ds(first, size // WORD_ROWS)


def _for_each_total_piece(total, fn):
    for piece in TOTAL_PIECES:
        @pl.when((total & piece) != 0)
        def _():
            fn(piece)


def _dispatch_kernel(*refs):
    tables, (rows_ref, total_ref, gap_ref, pos_ref, x_ref, xs_ref, stage, zeros, sem, zsem) = refs[:6], refs[6:]
    i = pl.program_id(0)
    nt = pl.num_programs(0)
    slot = i % 2
    tn = ROUTE_TILE

    def drain(tile, buf):
        def wait_rows(n):
            pltpu.make_async_copy(stage.at[buf, _word_rows(0, n), :], xs_ref.at[_word_rows(0, n), :],
                                  sem.at[buf]).wait()
        _for_each_total_piece(rows_ref[tile], wait_rows)

    @pl.when(i >= 2)
    def _():
        drain(i - 2, slot)

    xb = x_ref[...].astype(BF16)
    pos = pos_ref[0]
    row_iota = lax.broadcasted_iota(jnp.int32, (SORT_CHUNK, tn), 0).astype(F32).astype(BF16)
    one = jnp.ones((SORT_CHUNK, tn), BF16)
    zero = jnp.zeros((SORT_CHUNK, tn), BF16)

    def onehot_rows(c):
        rel = jnp.clip(pos - c * SORT_CHUNK, -1, SORT_CHUNK).astype(F32).astype(BF16)
        hit = rel[0:1, :] == row_iota
        for k in range(1, TOP_K):
            hit = hit | (rel[k:k + 1, :] == row_iota)
        return jnp.where(hit, one, zero)

    def sort_chunks(c2, carry):
        for c in (2 * c2, 2 * c2 + 1):
            p0 = pl.multiple_of(c * SORT_CHUNK, SORT_CHUNK)
            rows = jnp.dot(onehot_rows(c), xb, preferred_element_type=F32)
            stage[slot, _word_rows(p0, SORT_CHUNK), :] = pltpu.bitcast(rows.astype(BF16), jnp.int32)
        return carry

    lax.fori_loop(0, (rows_ref[i] + 2 * SORT_CHUNK - 1) // (2 * SORT_CHUNK), sort_chunks, 0)

    def start_piece(local, dst, n):
        pltpu.make_async_copy(stage.at[slot, _word_rows(local, n), :], xs_ref.at[_word_rows(dst, n), :],
                              sem.at[slot]).start()

    _for_each_piece(tables, i, start_piece)

    @pl.when(i == nt - 1)
    def _():
        @pl.when(nt >= 2)
        def _():
            drain(i - 1, 1 - slot)

        drain(i, slot)

        zeros[...] = jnp.zeros_like(zeros)

        def gap_copies(e, act):
            gap = gap_ref[N_EXPERTS + e]
            for piece in TOTAL_PIECES:
                if piece >= EXPERT_ROWS:
                    continue

                @pl.when((gap & piece) != 0)
                def _():
                    dst = gap_ref[e] + (gap & (-2 * piece))
                    act(pltpu.make_async_copy(zeros.at[_word_rows(0, piece), :],
                                              xs_ref.at[_word_rows(dst, piece), :], zsem))

        def zero_block(b):
            return pltpu.make_async_copy(zeros, xs_ref.at[_word_rows(b * EXPERT_ROWS, EXPERT_ROWS), :], zsem)

        first_free = total_ref[0] // EXPERT_ROWS
        n_blocks = xs_ref.shape[0] * WORD_ROWS // EXPERT_ROWS
        lax.fori_loop(0, N_EXPERTS, lambda e, c: (gap_copies(e, lambda cp: cp.start()), c)[1], 0)
        lax.fori_loop(first_free, n_blocks, lambda b, c: (zero_block(b).start(), c)[1], 0)
        lax.fori_loop(0, N_EXPERTS, lambda e, c: (gap_copies(e, lambda cp: cp.wait()), c)[1], 0)
        lax.fori_loop(first_free, n_blocks, lambda b, c: (zero_block(b).wait(), c)[1], 0)


def _dispatch(pieces, tile_rows, total, gaps, pos, x2d, n_rows):
    t, d = x2d.shape
    nt = t // ROUTE_TILE
    return pl.pallas_call(
        _dispatch_kernel,
        grid_spec=pltpu.PrefetchScalarGridSpec(
            num_scalar_prefetch=len(pieces) + 3,
            grid=(nt,),
            in_specs=[pl.BlockSpec((1, TOP_K, ROUTE_TILE), lambda i, *_: (i, 0, 0)),
                      pl.BlockSpec((ROUTE_TILE, d), lambda i, *_: (i, 0))],
            out_specs=pl.BlockSpec(memory_space=pl.ANY),
            scratch_shapes=[pltpu.VMEM((2, TILE_CAP // WORD_ROWS, d), jnp.int32),
                            pltpu.VMEM((EXPERT_ROWS // WORD_ROWS, d), jnp.int32),
                            pltpu.SemaphoreType.DMA((2,)),
                            pltpu.SemaphoreType.DMA(())],
        ),
        out_shape=jax.ShapeDtypeStruct((n_rows // WORD_ROWS, d), jnp.int32),
        compiler_params=_cparams(("arbitrary",)),
        name="dispatch",
    )(*pieces, tile_rows, total, gaps, pos, x2d)


def _experts_kernel(ie_ref, flag_ref, xs_ref, wg_ref, wu_ref, wd_ref, y_ref, wg_bf, wu_bf, wd_bf):
    w = pl.program_id(0)
    flags = flag_ref[w]
    valid = (flags & 1) != 0
    new_expert = (flags & 2) != 0

    @pl.when(jnp.logical_not(valid))
    def _():
        y_ref[...] = jnp.zeros_like(y_ref)

    @pl.when(new_expert)
    def _():
        wg_bf[...] = wg_ref[0].astype(BF16)
        wu_bf[...] = wu_ref[0].astype(BF16)
        wd_bf[...] = wd_ref[0].astype(BF16)

    @pl.when(valid)
    def _():
        sub = EXPERT_ROWS // WORD_ROWS // EXPERT_SPLIT
        groups = [slice(s * sub, (s + 1) * sub) for s in range(EXPERT_SPLIT)]
        xs = [pltpu.bitcast(xs_ref[g, :], BF16) for g in groups]
        gu = [(jnp.dot(x, wg_bf[...], preferred_element_type=F32),
               jnp.dot(x, wu_bf[...], preferred_element_type=F32)) for x in xs]
        hs = [(gte * _sigmoid(gte) * up).astype(BF16) for gte, up in gu]
        for g, h in zip(groups, hs):
            y = jnp.dot(h, wd_bf[...], preferred_element_type=F32)
            y_ref[g, :] = pltpu.bitcast(y.astype(BF16), jnp.int32)


def _experts(block_expert, block_flags, xs, wg, wu, wd):
    n, d = xs.shape
    ff = wg.shape[-1]
    return pl.pallas_call(
        _experts_kernel,
        grid_spec=pltpu.PrefetchScalarGridSpec(
            num_scalar_prefetch=2,
            grid=(block_expert.shape[0],),
            in_specs=[
                pl.BlockSpec((EXPERT_ROWS // WORD_ROWS, d), lambda w, ie, fl: (jnp.minimum(w, fl[fl.shape[0] - 1]), 0)),
                pl.BlockSpec((1, d, ff), lambda w, ie, fl: (ie[w], 0, 0)),
                pl.BlockSpec((1, d, ff), lambda w, ie, fl: (ie[w], 0, 0)),
                pl.BlockSpec((1, ff, d), lambda w, ie, fl: (ie[w], 0, 0)),
            ],
            out_specs=pl.BlockSpec((EXPERT_ROWS // WORD_ROWS, d), lambda w, ie, fl: (w, 0)),
            scratch_shapes=[pltpu.VMEM((d, ff), BF16), pltpu.VMEM((d, ff), BF16), pltpu.VMEM((ff, d), BF16)],
        ),
        out_shape=jax.ShapeDtypeStruct((n, d), jnp.int32),
        compiler_params=_cparams(("arbitrary",)),
        name="experts",
    )(block_expert, block_flags, xs, wg, wu, wd)


def _block_table(seg_rows, n_rows):
    block_end = jnp.cumsum(seg_rows // EXPERT_ROWS)
    w = jnp.arange(n_rows // EXPERT_ROWS, dtype=jnp.int32)
    valid = w < block_end[-1]
    e = jnp.sum((block_end[None, :] <= jnp.minimum(w, block_end[-1] - 1)[:, None]).astype(jnp.int32), axis=1)
    e = jnp.minimum(e, N_EXPERTS - 1)
    prev_e = jnp.concatenate([jnp.full((1,), -1, jnp.int32), e[:-1]])
    flags = valid.astype(jnp.int32) + 2 * (valid & (e != prev_e)).astype(jnp.int32)
    return e, jnp.concatenate([flags, block_end[-1:].astype(jnp.int32) - 1])


def _combine_kernel(*refs):
    tables = refs[:6]
    (rows_ref, pos_ref, gate_ref, x_ref, sg_ref, su_ref, sd_ref, g_ref, beta_ref, yb_ref, o_ref,
     stage, acc, lane_tile, lane_gate, sem) = refs[6:]
    i = pl.program_id(0)
    nt = pl.num_programs(0)
    slot = i % 2
    tn = ROUTE_TILE

    def start_tile(tile, buf):
        def start_piece(local, src, n):
            pltpu.make_async_copy(yb_ref.at[_word_rows(src, n), :], stage.at[buf, _word_rows(local, n), :],
                                  sem.at[buf]).start()
        _for_each_piece(tables, tile, start_piece)

    @pl.when(i == 0)
    def _():
        stage[...] = jnp.zeros_like(stage)
        start_tile(0, 0)

    @pl.when(i + 1 < nt)
    def _():
        start_tile(i + 1, 1 - slot)

    x = x_ref[...]
    xb = x.astype(BF16)
    gte = jnp.dot(xb, sg_ref[...], preferred_element_type=F32)
    up = jnp.dot(xb, su_ref[...], preferred_element_type=F32)
    h = (gte * _sigmoid(gte) * up).astype(BF16)
    acc[...] = jnp.dot(h, sd_ref[...], preferred_element_type=F32)

    def wait_rows(n):
        pltpu.make_async_copy(yb_ref.at[_word_rows(0, n), :], stage.at[slot, _word_rows(0, n), :],
                              sem.at[slot]).wait()

    _for_each_total_piece(rows_ref[i], wait_rows)

    lanes = 128
    lane_iota = lax.broadcasted_iota(jnp.int32, (tn, lanes), 1)
    for k in range(TOP_K):
        p = jnp.broadcast_to(pos_ref[:, k:k + 1], (tn, lanes))
        g = jnp.broadcast_to(gate_ref[:, k:k + 1], (tn, lanes))
        lane_tile[k] = (p >> (lanes.bit_length() - 1)).astype(F32).astype(BF16)
        lane_gate[k] = jnp.where((p & (lanes - 1)) == lane_iota, g, 0.0).astype(BF16)
    zero = jnp.zeros((tn, lanes), BF16)

    def gate_matrix(c):
        cols = []
        for j in range(COMBINE_CHUNK // lanes):
            tile = jnp.asarray(c * (COMBINE_CHUNK // lanes) + j, jnp.int32).astype(F32).astype(BF16)
            w = zero
            for k in range(TOP_K):
                w = w + jnp.where(lane_tile[k] == tile, lane_gate[k], zero)
            cols.append(w)
        return jnp.concatenate(cols, axis=1)

    def weigh_chunks(c2, carry):
        part = []
        for c in (2 * c2, 2 * c2 + 1):
            p0 = pl.multiple_of(c * COMBINE_CHUNK, COMBINE_CHUNK)
            rows = pltpu.bitcast(stage[slot, _word_rows(p0, COMBINE_CHUNK), :], BF16)
            part.append(jnp.dot(gate_matrix(c), rows, preferred_element_type=F32))
        acc[...] += part[0] + part[1]
        return carry

    lax.fori_loop(0, (rows_ref[i] + 2 * COMBINE_CHUNK - 1) // (2 * COMBINE_CHUNK), weigh_chunks, 0)
    o_ref[...] = _layer_norm(ALPHA * x + acc[...], g_ref[...], beta_ref[...])


def _combine(pieces, tile_rows, pos_t, gate_t, x2d, sg, su, sd, g, beta, yb):
    t, d = x2d.shape
    nt = t // ROUTE_TILE
    ff = sg.shape[-1]
    row = lambda i, *_: (i, 0)
    const = lambda i, *_: (0, 0)
    return pl.pallas_call(
        _combine_kernel,
        grid_spec=pltpu.PrefetchScalarGridSpec(
            num_scalar_prefetch=len(pieces) + 1,
            grid=(nt,),
            in_specs=[
                pl.BlockSpec((ROUTE_TILE, TOP_K), row),
                pl.BlockSpec((ROUTE_TILE, TOP_K), row),
                pl.BlockSpec((ROUTE_TILE, d), row),
                pl.BlockSpec((d, ff), const),
                pl.BlockSpec((d, ff), const),
                pl.BlockSpec((ff, d), const),
                pl.BlockSpec((1, d), const),
                pl.BlockSpec((1, d), const),
                pl.BlockSpec(memory_space=pl.ANY),
            ],
            out_specs=pl.BlockSpec((ROUTE_TILE, d), row),
            scratch_shapes=[pltpu.VMEM((2, TILE_CAP // WORD_ROWS, d), jnp.int32),
                            pltpu.VMEM((ROUTE_TILE, d), F32),
                            pltpu.VMEM((TOP_K, ROUTE_TILE, 128), BF16),
                            pltpu.VMEM((TOP_K, ROUTE_TILE, 128), BF16),
                            pltpu.SemaphoreType.DMA((2,))],
        ),
        out_shape=jax.ShapeDtypeStruct((t, d), F32),
        compiler_params=_cparams(("arbitrary",)),
        name="combine",
    )(*pieces, tile_rows, pos_t, gate_t, x2d, sg, su, sd, g, beta, yb)


def _piece_tables(cnt):
    lbase = jnp.cumsum(cnt, axis=1) - cnt
    per_expert = jnp.sum(cnt, axis=0)
    seg_rows = -(-per_expert // EXPERT_ROWS) * EXPERT_ROWS
    seg_start = jnp.cumsum(seg_rows) - seg_rows
    gbase = seg_start[None, :] + jnp.cumsum(cnt, axis=0) - cnt
    gaps = jnp.concatenate([seg_start + per_expert, seg_rows - per_expert]).astype(jnp.int32)

    def listed(n_pieces, start_off, piece_rows, length):
        last = jnp.cumsum(n_pieces, axis=1)
        first = last - n_pieces
        j = jnp.arange(length, dtype=jnp.int32)[None, :, None]
        own = (j >= first[:, None, :]) & (j < last[:, None, :])
        off = (start_off[:, None, :] + (j - first[:, None, :]) * piece_rows)
        pick = lambda base: jnp.sum(jnp.where(own, base[:, None, :] + off, 0), axis=-1).reshape(-1).astype(jnp.int32)
        return pick(lbase), pick(gbase), last[:, -1].astype(jnp.int32)

    n_big = cnt // BIG_PIECE
    big = listed(n_big, jnp.zeros_like(cnt), BIG_PIECE, BIG_PER_TILE)
    small = listed((cnt // RUN_ALIGN) % 2, n_big * BIG_PIECE, RUN_ALIGN, N_EXPERTS)
    return big + small, jnp.sum(cnt, axis=1).astype(jnp.int32), seg_rows, gaps


def kernel(x, mem, w_in, b_in, conv_w, conv_b, conv_ln_g, conv_ln_b, attn_sinks, rel_bias, w_out, b_out, ln1_g, ln1_b, xq_w, xkv_w, xo_w, ln2_g, ln2_b, router_w, router_b, exp_gate, exp_up, exp_down, sh_gate, sh_up, sh_down, ln3_g, ln3_b):
    bsz, seq, d = x.shape
    t = bsz * seq
    bias_tab, sink_tab = _band_tables(rel_bias, attn_sinks[0])
    tri = (jnp.arange(ROUTE_TILE)[:, None] <= jnp.arange(ROUTE_TILE)[None, :]).astype(BF16)
    row = lambda p: p.reshape(1, -1)
    for l in range(DEPTH):
        conv_out, q, k, v = _mix(x, w_in[l].astype(BF16), row(b_in[l]), conv_w[l], row(conv_b[l]),
                                 row(conv_ln_g[l]), row(conv_ln_b[l]))
        x1 = _swa(q, k, v, bias_tab, sink_tab, x, conv_out, w_out[l].astype(BF16), row(b_out[l]),
                  row(ln1_g[l]), row(ln1_b[l]))
        kmem, vmem = _memkv(mem, xkv_w[l].astype(BF16))
        x2, logits_t = _cross(x1, kmem, vmem, xq_w[l].astype(BF16),
                              xo_w[l].astype(BF16), row(ln2_g[l]), row(ln2_b[l]),
                              router_w[l].T.astype(BF16))
        x2 = x2.reshape(t, d)
        pos, gate, cnt = _route(logits_t, router_b[l].reshape(-1, 1), tri)
        nt = t // ROUTE_TILE
        n_rows = (-(-(t * TOP_K + nt * N_EXPERTS * (RUN_ALIGN - 1)) // EXPERT_ROWS) + N_EXPERTS) * EXPERT_ROWS
        pieces, tile_rows, seg_rows, gaps = _piece_tables(cnt[:, :, 0])
        block_expert, block_flags = _block_table(seg_rows, n_rows)
        xs = _dispatch(pieces, tile_rows, jnp.sum(seg_rows).reshape(1).astype(jnp.int32), gaps, pos, x2, n_rows)
        yb = _experts(block_expert, block_flags, xs, exp_gate[l], exp_up[l], exp_down[l])
        pos_t = jnp.transpose(pos, (0, 2, 1)).reshape(t, TOP_K)
        gate_t = jnp.transpose(gate, (0, 2, 1)).reshape(t, TOP_K)
        x = _combine(pieces, tile_rows, pos_t, gate_t, x2, sh_gate[l].astype(BF16),
                     sh_up[l].astype(BF16), sh_down[l].astype(BF16), row(ln3_g[l]), row(ln3_b[l]),
                     yb).reshape(bsz, seq, d)
    return x
```

```python
import functools
import math

import jax
import jax.numpy as jnp
from jax import lax
from jax.experimental import pallas as pl
from jax.experimental.pallas import tpu as pltpu

D_MODEL = 1024
MEM_LEN = 256
HEAD_DIM = 64
CONV_CH = D_MODEL // 2
CONV_WIDTH = 31
ATT_HEADS = 8
KV_HEADS = 2
WINDOW = 128
BLOCK = 128
REL_BUCKETS = 32
REL_MAX_DIST = 128
Q_COLS = ATT_HEADS * HEAD_DIM
KV_COLS = KV_HEADS * HEAD_DIM
IN_COLS = 2 * CONV_CH + Q_COLS + 2 * KV_COLS
X_HEADS = 4
X_HEAD_DIM = D_MODEL // X_HEADS
N_EXPERTS = 64
TOP_K = 8
N_GROUPS = 8
GROUP_SIZE = N_EXPERTS // N_GROUPS
TOPK_GROUPS = 4
EXPERT_FF = D_MODEL // 4
ROUTED_SCALE = 2.5
DEPTH = 1
ALPHA = (2 * DEPTH) ** 0.25
LN_EPS = 1e-5
NEG_INF = -1e30

F32 = jnp.float32
BF16 = jnp.bfloat16

VMEM_LIMIT_BYTES = 56 * 1024 * 1024

ROW_TILE = 512
CONV_ROWS = 32
SUBLANES = 8
CONV_HALO = 32
CROSS_TILE = 1024
CROSS_SPLIT = 2
SWA_TILE = 512
ROUTE_TILE = 512
EXPERT_ROWS = 1024
EXPERT_SPLIT = 2
RUN_ALIGN = 16
RUN_ALIGN_LOG2 = RUN_ALIGN.bit_length() - 1
WORD_ROWS = 2
TILE_CAP = ROUTE_TILE * TOP_K + N_EXPERTS * RUN_ALIGN
BIG_PIECE = 2 * RUN_ALIGN
BIG_PER_TILE = TILE_CAP // BIG_PIECE
PIECE_UNROLL = 4
TOTAL_PIECES = tuple(1 << b for b in range(TILE_CAP.bit_length() - 1, RUN_ALIGN_LOG2 - 1, -1))
SORT_CHUNK = 256
COMBINE_CHUNK = 512


def _cparams(sem):
    return pltpu.CompilerParams(dimension_semantics=sem, vmem_limit_bytes=VMEM_LIMIT_BYTES)


def _layer_norm(h, g, b):
    mu = jnp.mean(h, axis=-1, keepdims=True)
    d = h - mu
    var = jnp.mean(d * d, axis=-1, keepdims=True)
    return d * lax.rsqrt(var + LN_EPS) * g + b


def _sigmoid(x):
    return 1.0 / (1.0 + jnp.exp(-x))


def _mix_kernel(x_ref, w_ref, b_ref, cw_ref, cb_ref, cg_ref, cbeta_ref,
                conv_ref, q_ref, k_ref, v_ref, u_ext, u_sh):
    j = pl.program_id(1)
    xb = x_ref[0].astype(BF16)
    proj = jnp.dot(xb, w_ref[...], preferred_element_type=F32) + b_ref[...]
    a = proj[:, :CONV_CH]
    g = proj[:, CONV_CH:2 * CONV_CH]
    q_ref[0] = (proj[:, 2 * CONV_CH:2 * CONV_CH + Q_COLS] * (HEAD_DIM ** -0.5)).astype(BF16)
    k_ref[0] = proj[:, 2 * CONV_CH + Q_COLS:2 * CONV_CH + Q_COLS + KV_COLS].astype(BF16)
    v_ref[0] = proj[:, 2 * CONV_CH + Q_COLS + KV_COLS:].astype(BF16)

    @pl.when(j == 0)
    def _():
        u_ext[0:CONV_HALO, :] = jnp.zeros((CONV_HALO, CONV_CH), F32)

    u_ext[CONV_HALO:CONV_HALO + ROW_TILE, :] = a * _sigmoid(g)

    first_tap = CONV_HALO - (CONV_WIDTH - 1)
    shifted_rows = u_sh.shape[1]
    for r in range(1, SUBLANES):
        u_sh[r - 1] = u_ext[r:r + shifted_rows, :]

    for c in range(ROW_TILE // CONV_ROWS):
        acc = jnp.zeros((CONV_ROWS, CONV_CH), F32) + cb_ref[...]
        for t in range(CONV_WIDTH):
            r = (first_tap + t) % SUBLANES
            base = c * CONV_ROWS + (first_tap + t) - r
            if r == 0:
                taps = u_ext[base:base + CONV_ROWS, :]
            else:
                taps = u_sh[r - 1, base:base + CONV_ROWS, :]
            acc = acc + taps * cw_ref[t:t + 1, :]
        y = _layer_norm(acc, cg_ref[...], cbeta_ref[...])
        conv_ref[0, c * CONV_ROWS:(c + 1) * CONV_ROWS, :] = (y * _sigmoid(y)).astype(BF16)

    u_ext[0:CONV_HALO, :] = u_ext[ROW_TILE:ROW_TILE + CONV_HALO, :]


def _mix(x, w_in, b_in, conv_w, conv_b, conv_g, conv_beta):
    bsz, seq, d = x.shape
    nt = seq // ROW_TILE
    row = lambda b, j: (b, j, 0)
    const2 = lambda b, j: (0, 0)
    return pl.pallas_call(
        _mix_kernel,
        grid=(bsz, nt),
        in_specs=[
            pl.BlockSpec((1, ROW_TILE, d), row),
            pl.BlockSpec((d, IN_COLS), const2),
            pl.BlockSpec((1, IN_COLS), const2),
            pl.BlockSpec((CONV_WIDTH, CONV_CH), const2),
            pl.BlockSpec((1, CONV_CH), const2),
            pl.BlockSpec((1, CONV_CH), const2),
            pl.BlockSpec((1, CONV_CH), const2),
        ],
        out_specs=[
            pl.BlockSpec((1, ROW_TILE, CONV_CH), row),
            pl.BlockSpec((1, ROW_TILE, Q_COLS), row),
            pl.BlockSpec((1, ROW_TILE, KV_COLS), row),
            pl.BlockSpec((1, ROW_TILE, KV_COLS), row),
        ],
        out_shape=[
            jax.ShapeDtypeStruct((bsz, seq, CONV_CH), BF16),
            jax.ShapeDtypeStruct((bsz, seq, Q_COLS), BF16),
            jax.ShapeDtypeStruct((bsz, seq, KV_COLS), BF16),
            jax.ShapeDtypeStruct((bsz, seq, KV_COLS), BF16),
        ],
        scratch_shapes=[pltpu.VMEM((ROW_TILE + CONV_HALO, CONV_CH), F32),
                        pltpu.VMEM((SUBLANES - 1, ROW_TILE + CONV_HALO - SUBLANES, CONV_CH), F32)],
        compiler_params=_cparams(("arbitrary", "arbitrary")),
        name="mix",
    )(x, w_in, b_in, conv_w, conv_b, conv_g, conv_beta)


def _swa_kernel(q_ref, kp_ref, kc_ref, vp_ref, vc_ref, bias_ref, sink_ref, x_ref, conv_ref, w_ref, b_ref,
                g_ref, beta_ref, o_ref):
    j = pl.program_id(1)
    mix = jnp.dot(conv_ref[0], w_ref[0:CONV_CH, :], preferred_element_type=F32)
    rows = BLOCK + SWA_TILE
    lane = lax.broadcasted_iota(jnp.int32, (rows, 2 * HEAD_DIM), 1)
    low = lane < HEAD_DIM

    def placements(prev_ref, cur_ref):
        t = jnp.concatenate([prev_ref[0], cur_ref[0]], axis=0).astype(F32)
        tr = pltpu.roll(t, HEAD_DIM, 1)
        zero = jnp.zeros_like(t)
        kv0_low = jnp.where(low, t, zero).astype(BF16)
        kv1_high = jnp.where(low, zero, t).astype(BF16)
        kv1_low = jnp.where(low, tr, zero).astype(BF16)
        kv0_high = jnp.where(low, zero, tr).astype(BF16)
        return (kv0_low, kv0_high, kv1_low, kv1_high)

    ks = placements(kp_ref, kc_ref)
    vs = placements(vp_ref, vc_ref)
    slab = 2 * HEAD_DIM
    att = []
    for i in range(SWA_TILE // BLOCK):
        q = q_ref[0, i * BLOCK:(i + 1) * BLOCK, :]
        q_kv0 = jnp.concatenate([q[:, 0:slab], q[:, slab:2 * slab]], axis=0)
        q_kv1 = jnp.concatenate([q[:, 2 * slab:3 * slab], q[:, 3 * slab:4 * slab]], axis=0)
        band = slice(i * BLOCK, (i + 2) * BLOCK)
        outs = []
        for s in range(4):
            qs = q_kv0 if s < 2 else q_kv1
            bias = bias_ref[1, s]
            if i == 0:
                bias = jnp.where(j == 0, bias_ref[0, s], bias)
            logits = lax.dot_general(qs, ks[s][band], (((1,), (1,)), ((), ())),
                                     preferred_element_type=F32) + bias
            sink = sink_ref[s]
            m = jnp.maximum(jnp.max(logits, axis=-1, keepdims=True), sink)
            p = jnp.exp(logits - m)
            den = jnp.sum(p, axis=-1, keepdims=True) + jnp.exp(sink - m)
            o = jnp.dot(p.astype(BF16), vs[s][band], preferred_element_type=F32)
            outs.append(o / den)
        o_kv0 = outs[0] + outs[1]
        o_kv1 = outs[2] + outs[3]
        att.append(jnp.concatenate([o_kv0[0:BLOCK], o_kv0[BLOCK:2 * BLOCK], o_kv1[0:BLOCK],
                                    o_kv1[BLOCK:2 * BLOCK]], axis=1).astype(BF16))

    mix = mix + jnp.dot(jnp.concatenate(att, axis=0), w_ref[CONV_CH:, :], preferred_element_type=F32)
    h = ALPHA * x_ref[0] + mix + b_ref[...]
    o_ref[0] = _layer_norm(h, g_ref[...], beta_ref[...])


def _swa(q, k, v, bias_tab, sink_tab, x, conv_out, w_out, b_out, g, beta):
    bsz, seq, d = x.shape
    per = SWA_TILE // BLOCK
    cur = lambda b, n: (b, n, 0)
    prev = lambda b, n: (b, jnp.maximum(n * per - 1, 0), 0)
    whole = lambda b, n: (0, 0, 0, 0)
    const = lambda b, n: (0, 0)
    return pl.pallas_call(
        _swa_kernel,
        grid=(bsz, seq // SWA_TILE),
        in_specs=[
            pl.BlockSpec((1, SWA_TILE, Q_COLS), cur),
            pl.BlockSpec((1, BLOCK, KV_COLS), prev),
            pl.BlockSpec((1, SWA_TILE, KV_COLS), cur),
            pl.BlockSpec((1, BLOCK, KV_COLS), prev),
            pl.BlockSpec((1, SWA_TILE, KV_COLS), cur),
            pl.BlockSpec((2, 4, 2 * BLOCK, 2 * BLOCK), whole),
            pl.BlockSpec((4, 2 * BLOCK, 1), lambda b, n: (0, 0, 0)),
            pl.BlockSpec((1, SWA_TILE, d), cur),
            pl.BlockSpec((1, SWA_TILE, CONV_CH), cur),
            pl.BlockSpec((d, d), const),
            pl.BlockSpec((1, d), const),
            pl.BlockSpec((1, d), const),
            pl.BlockSpec((1, d), const),
        ],
        out_specs=pl.BlockSpec((1, SWA_TILE, d), cur),
        out_shape=jax.ShapeDtypeStruct((bsz, seq, d), F32),
        compiler_params=_cparams(("arbitrary", "arbitrary")),
        name="swa",
    )(q, k, k, v, v, bias_tab, sink_tab, x, conv_out, w_out, b_out, g, beta)


def _t5_bucket(dist):
    n = jnp.maximum(dist, 0)
    exact = REL_BUCKETS // 2
    large = exact + (jnp.log(jnp.maximum(n, 1).astype(F32) / exact)
                     / math.log(REL_MAX_DIST / exact) * (REL_BUCKETS - exact)).astype(jnp.int32)
    large = jnp.minimum(large, REL_BUCKETS - 1)
    return jnp.where(n < exact, n, large)


def _band_tables(rel_bias, sinks):
    qi = jnp.arange(BLOCK)[:, None]
    kj = jnp.arange(2 * BLOCK)[None, :]
    dist = qi + BLOCK - kj
    bucket = _t5_bucket(dist)
    bias = jnp.zeros((ATT_HEADS, BLOCK, 2 * BLOCK), F32)
    for bkt in range(REL_BUCKETS):
        bias = jnp.where(bucket[None] == bkt, rel_bias[bkt].astype(F32)[:, None, None], bias)
    in_window = (dist >= 0) & (dist < WINDOW)
    masks = jnp.stack([in_window & (kj >= BLOCK), in_window])
    masked = jnp.where(masks[:, None], bias[None], NEG_INF)
    pairs = ((0, 2), (1, 3), (4, 6), (5, 7))
    bias_tab = jnp.stack([jnp.concatenate([masked[:, a], masked[:, b]], axis=1) for a, b in pairs], axis=1)
    s = sinks.astype(F32)
    sink_tab = jnp.stack([jnp.concatenate([jnp.full((BLOCK, 1), s[a]), jnp.full((BLOCK, 1), s[b])], axis=0)
                          for a, b in pairs])
    return bias_tab, sink_tab


def _memkv_kernel(mem_ref, w_ref, k_ref, v_ref):
    kv = jnp.dot(mem_ref[0].astype(BF16), w_ref[...], preferred_element_type=F32)
    k_ref[0] = kv[:, :D_MODEL].astype(BF16)
    v_ref[0] = kv[:, D_MODEL:].astype(BF16)


def _memkv(mem, wkv):
    bsz, m, d = mem.shape
    return pl.pallas_call(
        _memkv_kernel,
        grid=(bsz,),
        in_specs=[pl.BlockSpec((1, m, d), lambda b: (b, 0, 0)),
                  pl.BlockSpec((d, 2 * d), lambda b: (0, 0))],
        out_specs=[pl.BlockSpec((1, m, d), lambda b: (b, 0, 0)),
                   pl.BlockSpec((1, m, d), lambda b: (b, 0, 0))],
        out_shape=[jax.ShapeDtypeStruct((bsz, m, d), BF16)] * 2,
        compiler_params=_cparams(("arbitrary",)),
        name="memkv",
    )(mem, wkv)


def _cross_kernel(x_ref, k_ref, v_ref, wq_ref, wo_ref, g_ref, beta_ref, rw_ref, o_ref, lt_ref):
    sub = CROSS_TILE // CROSS_SPLIT
    groups = [slice(s * sub, (s + 1) * sub) for s in range(CROSS_SPLIT)]
    xs = [x_ref[0, g, :] for g in groups]
    qs = [(jnp.dot(x.astype(BF16), wq_ref[...], preferred_element_type=F32) * (X_HEAD_DIM ** -0.5)).astype(BF16)
          for x in xs]
    heads = [[] for _ in groups]
    for h in range(X_HEADS):
        cols = slice(h * X_HEAD_DIM, (h + 1) * X_HEAD_DIM)
        logits = [lax.dot_general(q[:, cols], k_ref[0, :, cols], (((1,), (1,)), ((), ())),
                                  preferred_element_type=F32) for q in qs]
        ps, dens = [], []
        for lg in logits:
            p = jnp.exp(lg - jnp.max(lg, axis=-1, keepdims=True))
            ps.append(p)
            dens.append(jnp.sum(p, axis=-1, keepdims=True))
        for s, (p, den) in enumerate(zip(ps, dens)):
            o = jnp.dot(p.astype(BF16), v_ref[0, :, cols], preferred_element_type=F32)
            heads[s].append((o / den).astype(BF16))
    crosses = [jnp.dot(jnp.concatenate(hs, axis=-1), wo_ref[...], preferred_element_type=F32) for hs in heads]
    for g, x, cross in zip(groups, xs, crosses):
        y = _layer_norm(ALPHA * x + cross, g_ref[...], beta_ref[...])
        o_ref[0, g, :] = y
        lt_ref[:, g] = lax.dot_general(rw_ref[...], y.astype(BF16), (((1,), (1,)), ((), ())),
                                       preferred_element_type=F32)


def _cross(x1, kmem, vmem, wq, wo, g, beta, rw_t):
    bsz, seq, d = x1.shape
    nt = seq // CROSS_TILE
    row = lambda b, j: (b, j, 0)
    mem = lambda b, j: (b, 0, 0)
    const = lambda b, j: (0, 0)
    return pl.pallas_call(
        _cross_kernel,
        grid=(bsz, nt),
        in_specs=[
            pl.BlockSpec((1, CROSS_TILE, d), row),
            pl.BlockSpec((1, MEM_LEN, d), mem),
            pl.BlockSpec((1, MEM_LEN, d), mem),
            pl.BlockSpec((d, d), const),
            pl.BlockSpec((d, d), const),
            pl.BlockSpec((1, d), const),
            pl.BlockSpec((1, d), const),
            pl.BlockSpec((N_EXPERTS, d), const),
        ],
        out_specs=[
            pl.BlockSpec((1, CROSS_TILE, d), row),
            pl.BlockSpec((N_EXPERTS, CROSS_TILE), lambda b, j: (0, b * nt + j)),
        ],
        out_shape=[
            jax.ShapeDtypeStruct((bsz, seq, d), F32),
            jax.ShapeDtypeStruct((N_EXPERTS, bsz * seq), F32),
        ],
        compiler_params=_cparams(("arbitrary", "arbitrary")),
        name="cross",
    )(x1, kmem, vmem, wq, wo, g, beta, rw_t)


def _route_kernel(lt_ref, rb_ref, tri_ref, pos_ref, gate_ref, cnt_ref):
    tn = ROUTE_TILE
    scores = _sigmoid(lt_ref[...])
    choice = scores + rb_ref[...]

    gscore = []
    member = lax.broadcasted_iota(jnp.int32, (GROUP_SIZE, tn), 0).astype(F32)
    for g in range(N_GROUPS):
        c = choice[g * GROUP_SIZE:(g + 1) * GROUP_SIZE, :]
        m1 = jnp.max(c, axis=0, keepdims=True)
        first = jnp.min(jnp.where(c == m1, member, float(GROUP_SIZE)), axis=0, keepdims=True)
        m2 = jnp.max(jnp.where(member == first, -jnp.inf, c), axis=0, keepdims=True)
        gscore.append(m1 + m2)

    keep_rows = []
    for g in range(N_GROUPS):
        beaten = jnp.zeros((1, tn), F32)
        for o in range(N_GROUPS):
            if o == g:
                continue
            ahead = (gscore[o] >= gscore[g]) if o < g else (gscore[o] > gscore[g])
            beaten = beaten + jnp.where(ahead, 1.0, 0.0)
        keep_rows.append(jnp.broadcast_to(beaten, (GROUP_SIZE, tn)))
    beaten_all = jnp.concatenate(keep_rows, axis=0)

    masked = jnp.where(beaten_all < TOPK_GROUPS, choice, -jnp.inf)
    eidx = lax.broadcasted_iota(jnp.int32, (N_EXPERTS, tn), 0).astype(F32)
    sel = jnp.zeros((N_EXPERTS, tn), F32)
    picks, weights = [], []
    for r in range(TOP_K):
        mx = jnp.max(masked, axis=0, keepdims=True)
        first = jnp.min(jnp.where(masked == mx, eidx, float(N_EXPERTS)), axis=0, keepdims=True)
        pick = eidx == first
        picks.append((pick, first))
        weights.append(jnp.sum(jnp.where(pick, scores, 0.0), axis=0, keepdims=True))
        masked = jnp.where(pick, -jnp.inf, masked)
        sel = jnp.where(pick, 1.0, sel)

    wsum = weights[0]
    for r in range(1, TOP_K):
        wsum = wsum + weights[r]

    count = jnp.sum(sel, axis=1, keepdims=True)
    run_len = jnp.floor((count + (RUN_ALIGN - 1.0)) * (1.0 / RUN_ALIGN)) * RUN_ALIGN
    run_len_b = jnp.broadcast_to(run_len, (N_EXPERTS, 128))
    er = lax.broadcasted_iota(jnp.int32, (N_EXPERTS, N_EXPERTS), 0)
    ec = lax.broadcasted_iota(jnp.int32, (N_EXPERTS, N_EXPERTS), 1)
    before = jnp.where(ec < er, 1.0, 0.0).astype(BF16)
    run_start = jnp.dot(before, run_len_b.astype(BF16), preferred_element_type=F32)[:, 0:1]
    incl = jnp.dot(sel.astype(BF16), tri_ref[...], preferred_element_type=F32)
    pos_mat = run_start + incl - sel
    cnt_ref[0] = run_len_b.astype(jnp.int32)

    for r in range(TOP_K):
        pick, _ = picks[r]
        pos_ref[0, r:r + 1, :] = jnp.sum(jnp.where(pick, pos_mat, 0.0), axis=0,
                                         keepdims=True).astype(jnp.int32)
        gate_ref[0, r:r + 1, :] = weights[r] / wsum * ROUTED_SCALE


def _route(logits_t, router_b, tri):
    e, t = logits_t.shape
    nt = t // ROUTE_TILE
    blk = lambda i: (i, 0, 0)
    return pl.pallas_call(
        _route_kernel,
        grid=(nt,),
        in_specs=[
            pl.BlockSpec((e, ROUTE_TILE), lambda i: (0, i)),
            pl.BlockSpec((e, 1), lambda i: (0, 0)),
            pl.BlockSpec((ROUTE_TILE, ROUTE_TILE), lambda i: (0, 0)),
        ],
        out_specs=[
            pl.BlockSpec((1, TOP_K, ROUTE_TILE), blk),
            pl.BlockSpec((1, TOP_K, ROUTE_TILE), blk),
            pl.BlockSpec((1, e, 128), blk),
        ],
        out_shape=[
            jax.ShapeDtypeStruct((nt, TOP_K, ROUTE_TILE), jnp.int32),
            jax.ShapeDtypeStruct((nt, TOP_K, ROUTE_TILE), F32),
            jax.ShapeDtypeStruct((nt, e, 128), jnp.int32),
        ],
        compiler_params=_cparams(("arbitrary",)),
        name="route",
    )(logits_t, router_b, tri)


def _for_each_piece(tables, tile, fn):
    for n_rows, per_tile, (local_ref, global_ref, count_ref) in (
            (BIG_PIECE, BIG_PER_TILE, tables[0:3]), (RUN_ALIGN, N_EXPERTS, tables[3:6])):
        count = count_ref[tile]

        def body(g, carry, n_rows=n_rows, per_tile=per_tile, local_ref=local_ref, global_ref=global_ref,
                 count=count):
            for u in range(PIECE_UNROLL):
                j = g * PIECE_UNROLL + u

                @pl.when(j < count)
                def _():
                    fn(pl.multiple_of(local_ref[tile * per_tile + j], RUN_ALIGN),
                       pl.multiple_of(global_ref[tile * per_tile + j], RUN_ALIGN), n_rows)
            return carry

        lax.fori_loop(0, (count + PIECE_UNROLL - 1) // PIECE_UNROLL, body, 0)


def _word_rows(start, size):
    if isinstance(start, int):
        first = start // WORD_ROWS
    else:
        first = pl.multiple_of(start >> (WORD_ROWS.bit_length() - 1), RUN_ALIGN // WORD_ROWS)
    return pl.ds(first, size // WORD_ROWS)


def _for_each_total_piece(total, fn):
    for piece in TOTAL_PIECES:
        @pl.when((total & piece) != 0)
        def _():
            fn(piece)


def _dispatch_kernel(*refs):
    tables, (rows_ref, total_ref, gap_ref, pos_ref, x_ref, xs_ref, stage, zeros, sem, zsem) = refs[:6], refs[6:]
    i = pl.program_id(0)
    nt = pl.num_programs(0)
    slot = i % 2
    tn = ROUTE_TILE

    def drain(tile, buf):
        def wait_rows(n):
            pltpu.make_async_copy(stage.at[buf, _word_rows(0, n), :], xs_ref.at[_word_rows(0, n), :],
                                  sem.at[buf]).wait()
        _for_each_total_piece(rows_ref[tile], wait_rows)

    @pl.when(i >= 2)
    def _():
        drain(i - 2, slot)

    xb = x_ref[...].astype(BF16)
    pos = pos_ref[0]
    row_iota = lax.broadcasted_iota(jnp.int32, (SORT_CHUNK, tn), 0).astype(F32).astype(BF16)
    one = jnp.ones((SORT_CHUNK, tn), BF16)
    zero = jnp.zeros((SORT_CHUNK, tn), BF16)

    def onehot_rows(c):
        rel = jnp.clip(pos - c * SORT_CHUNK, -1, SORT_CHUNK).astype(F32).astype(BF16)
        hit = rel[0:1, :] == row_iota
        for k in range(1, TOP_K):
            hit = hit | (rel[k:k + 1, :] == row_iota)
        return jnp.where(hit, one, zero)

    def sort_chunks(c2, carry):
        for c in (2 * c2, 2 * c2 + 1):
            p0 = pl.multiple_of(c * SORT_CHUNK, SORT_CHUNK)
            rows = jnp.dot(onehot_rows(c), xb, preferred_element_type=F32)
            stage[slot, _word_rows(p0, SORT_CHUNK), :] = pltpu.bitcast(rows.astype(BF16), jnp.int32)
        return carry

    lax.fori_loop(0, (rows_ref[i] + 2 * SORT_CHUNK - 1) // (2 * SORT_CHUNK), sort_chunks, 0)

    def start_piece(local, dst, n):
        pltpu.make_async_copy(stage.at[slot, _word_rows(local, n), :], xs_ref.at[_word_rows(dst, n), :],
                              sem.at[slot]).start()

    _for_each_piece(tables, i, start_piece)

    @pl.when(i == nt - 1)
    def _():
        @pl.when(nt >= 2)
        def _():
            drain(i - 1, 1 - slot)

        drain(i, slot)

        zeros[...] = jnp.zeros_like(zeros)

        def gap_copies(e, act):
            gap = gap_ref[N_EXPERTS + e]
            for piece in TOTAL_PIECES:
                if piece >= EXPERT_ROWS:
                    continue

                @pl.when((gap & piece) != 0)
                def _():
                    dst = gap_ref[e] + (gap & (-2 * piece))
                    act(pltpu.make_async_copy(zeros.at[_word_rows(0, piece), :],
                                              xs_ref.at[_word_rows(dst, piece), :], zsem))

        def zero_block(b):
            return pltpu.make_async_copy(zeros, xs_ref.at[_word_rows(b * EXPERT_ROWS, EXPERT_ROWS), :], zsem)

        first_free = total_ref[0] // EXPERT_ROWS
        n_blocks = xs_ref.shape[0] * WORD_ROWS // EXPERT_ROWS
        lax.fori_loop(0, N_EXPERTS, lambda e, c: (gap_copies(e, lambda cp: cp.start()), c)[1], 0)
        lax.fori_loop(first_free, n_blocks, lambda b, c: (zero_block(b).start(), c)[1], 0)
        lax.fori_loop(0, N_EXPERTS, lambda e, c: (gap_copies(e, lambda cp: cp.wait()), c)[1], 0)
        lax.fori_loop(first_free, n_blocks, lambda b, c: (zero_block(b).wait(), c)[1], 0)


def _dispatch(pieces, tile_rows, total, gaps, pos, x2d, n_rows):
    t, d = x2d.shape
    nt = t // ROUTE_TILE
    return pl.pallas_call(
        _dispatch_kernel,
        grid_spec=pltpu.PrefetchScalarGridSpec(
            num_scalar_prefetch=len(pieces) + 3,
            grid=(nt,),
            in_specs=[pl.BlockSpec((1, TOP_K, ROUTE_TILE), lambda i, *_: (i, 0, 0)),
                      pl.BlockSpec((ROUTE_TILE, d), lambda i, *_: (i, 0))],
            out_specs=pl.BlockSpec(memory_space=pl.ANY),
            scratch_shapes=[pltpu.VMEM((2, TILE_CAP // WORD_ROWS, d), jnp.int32),
                            pltpu.VMEM((EXPERT_ROWS // WORD_ROWS, d), jnp.int32),
                            pltpu.SemaphoreType.DMA((2,)),
                            pltpu.SemaphoreType.DMA(())],
        ),
        out_shape=jax.ShapeDtypeStruct((n_rows // WORD_ROWS, d), jnp.int32),
        compiler_params=_cparams(("arbitrary",)),
        name="dispatch",
    )(*pieces, tile_rows, total, gaps, pos, x2d)


def _experts_kernel(ie_ref, flag_ref, xs_ref, wg_ref, wu_ref, wd_ref, y_ref, wg_bf, wu_bf, wd_bf):
    w = pl.program_id(0)
    flags = flag_ref[w]
    valid = (flags & 1) != 0
    new_expert = (flags & 2) != 0

    @pl.when(jnp.logical_not(valid))
    def _():
        y_ref[...] = jnp.zeros_like(y_ref)

    @pl.when(new_expert)
    def _():
        wg_bf[...] = wg_ref[0].astype(BF16)
        wu_bf[...] = wu_ref[0].astype(BF16)
        wd_bf[...] = wd_ref[0].astype(BF16)

    @pl.when(valid)
    def _():
        sub = EXPERT_ROWS // WORD_ROWS // EXPERT_SPLIT
        groups = [slice(s * sub, (s + 1) * sub) for s in range(EXPERT_SPLIT)]
        xs = [pltpu.bitcast(xs_ref[g, :], BF16) for g in groups]
        gu = [(jnp.dot(x, wg_bf[...], preferred_element_type=F32),
               jnp.dot(x, wu_bf[...], preferred_element_type=F32)) for x in xs]
        hs = [(gte * _sigmoid(gte) * up).astype(BF16) for gte, up in gu]
        for g, h in zip(groups, hs):
            y = jnp.dot(h, wd_bf[...], preferred_element_type=F32)
            y_ref[g, :] = pltpu.bitcast(y.astype(BF16), jnp.int32)


def _experts(block_expert, block_flags, xs, wg, wu, wd):
    n, d = xs.shape
    ff = wg.shape[-1]
    return pl.pallas_call(
        _experts_kernel,
        grid_spec=pltpu.PrefetchScalarGridSpec(
            num_scalar_prefetch=2,
            grid=(block_expert.shape[0],),
            in_specs=[
                pl.BlockSpec((EXPERT_ROWS // WORD_ROWS, d), lambda w, ie, fl: (jnp.minimum(w, fl[fl.shape[0] - 1]), 0)),
                pl.BlockSpec((1, d, ff), lambda w, ie, fl: (ie[w], 0, 0)),
                pl.BlockSpec((1, d, ff), lambda w, ie, fl: (ie[w], 0, 0)),
                pl.BlockSpec((1, ff, d), lambda w, ie, fl: (ie[w], 0, 0)),
            ],
            out_specs=pl.BlockSpec((EXPERT_ROWS // WORD_ROWS, d), lambda w, ie, fl: (w, 0)),
            scratch_shapes=[pltpu.VMEM((d, ff), BF16), pltpu.VMEM((d, ff), BF16), pltpu.VMEM((ff, d), BF16)],
        ),
        out_shape=jax.ShapeDtypeStruct((n, d), jnp.int32),
        compiler_params=_cparams(("arbitrary",)),
        name="experts",
    )(block_expert, block_flags, xs, wg, wu, wd)


def _block_table(seg_rows, n_rows):
    block_end = jnp.cumsum(seg_rows // EXPERT_ROWS)
    w = jnp.arange(n_rows // EXPERT_ROWS, dtype=jnp.int32)
    valid = w < block_end[-1]
    e = jnp.sum((block_end[None, :] <= jnp.minimum(w, block_end[-1] - 1)[:, None]).astype(jnp.int32), axis=1)
    e = jnp.minimum(e, N_EXPERTS - 1)
    prev_e = jnp.concatenate([jnp.full((1,), -1, jnp.int32), e[:-1]])
    flags = valid.astype(jnp.int32) + 2 * (valid & (e != prev_e)).astype(jnp.int32)
    return e, jnp.concatenate([flags, block_end[-1:].astype(jnp.int32) - 1])


def _combine_kernel(*refs):
    tables = refs[:6]
    (rows_ref, pos_ref, gate_ref, x_ref, sg_ref, su_ref, sd_ref, g_ref, beta_ref, yb_ref, o_ref,
     stage, acc, lane_tile, lane_gate, sem) = refs[6:]
    i = pl.program_id(0)
    nt = pl.num_programs(0)
    slot = i % 2
    tn = ROUTE_TILE

    def start_tile(tile, buf):
        def start_piece(local, src, n):
            pltpu.make_async_copy(yb_ref.at[_word_rows(src, n), :], stage.at[buf, _word_rows(local, n), :],
                                  sem.at[buf]).start()
        _for_each_piece(tables, tile, start_piece)

    @pl.when(i == 0)
    def _():
        stage[...] = jnp.zeros_like(stage)
        start_tile(0, 0)

    @pl.when(i + 1 < nt)
    def _():
        start_tile(i + 1, 1 - slot)

    x = x_ref[...]
    xb = x.astype(BF16)
    gte = jnp.dot(xb, sg_ref[...], preferred_element_type=F32)
    up = jnp.dot(xb, su_ref[...], preferred_element_type=F32)
    h = (gte * _sigmoid(gte) * up).astype(BF16)
    acc[...] = jnp.dot(h, sd_ref[...], preferred_element_type=F32)

    def wait_rows(n):
        pltpu.make_async_copy(yb_ref.at[_word_rows(0, n), :], stage.at[slot, _word_rows(0, n), :],
                              sem.at[slot]).wait()

    _for_each_total_piece(rows_ref[i], wait_rows)

    lanes = 128
    lane_iota = lax.broadcasted_iota(jnp.int32, (tn, lanes), 1)
    for k in range(TOP_K):
        p = jnp.broadcast_to(pos_ref[:, k:k + 1], (tn, lanes))
        g = jnp.broadcast_to(gate_ref[:, k:k + 1], (tn, lanes))
        lane_tile[k] = (p >> (lanes.bit_length() - 1)).astype(F32).astype(BF16)
        lane_gate[k] = jnp.where((p & (lanes - 1)) == lane_iota, g, 0.0).astype(BF16)
    zero = jnp.zeros((tn, lanes), BF16)

    def gate_matrix(c):
        cols = []
        for j in range(COMBINE_CHUNK // lanes):
            tile = jnp.asarray(c * (COMBINE_CHUNK // lanes) + j, jnp.int32).astype(F32).astype(BF16)
            w = zero
            for k in range(TOP_K):
                w = w + jnp.where(lane_tile[k] == tile, lane_gate[k], zero)
            cols.append(w)
        return jnp.concatenate(cols, axis=1)

    def weighed(c):
        p0 = pl.multiple_of(c * COMBINE_CHUNK, COMBINE_CHUNK)
        rows = pltpu.bitcast(stage[slot, _word_rows(p0, COMBINE_CHUNK), :], BF16)
        return jnp.dot(gate_matrix(c), rows, preferred_element_type=F32)

    def weigh_chunks(c2, carry):
        acc[...] += weighed(2 * c2) + weighed(2 * c2 + 1)
        return carry

    n_chunks = (rows_ref[i] + COMBINE_CHUNK - 1) // COMBINE_CHUNK
    lax.fori_loop(0, n_chunks // 2, weigh_chunks, 0)

    @pl.when(n_chunks % 2 == 1)
    def _():
        acc[...] += weighed(n_chunks - 1)
    o_ref[...] = _layer_norm(ALPHA * x + acc[...], g_ref[...], beta_ref[...])


def _combine(pieces, tile_rows, pos_t, gate_t, x2d, sg, su, sd, g, beta, yb):
    t, d = x2d.shape
    nt = t // ROUTE_TILE
    ff = sg.shape[-1]
    row = lambda i, *_: (i, 0)
    const = lambda i, *_: (0, 0)
    return pl.pallas_call(
        _combine_kernel,
        grid_spec=pltpu.PrefetchScalarGridSpec(
            num_scalar_prefetch=len(pieces) + 1,
            grid=(nt,),
            in_specs=[
                pl.BlockSpec((ROUTE_TILE, TOP_K), row),
                pl.BlockSpec((ROUTE_TILE, TOP_K), row),
                pl.BlockSpec((ROUTE_TILE, d), row),
                pl.BlockSpec((d, ff), const),
                pl.BlockSpec((d, ff), const),
                pl.BlockSpec((ff, d), const),
                pl.BlockSpec((1, d), const),
                pl.BlockSpec((1, d), const),
                pl.BlockSpec(memory_space=pl.ANY),
            ],
            out_specs=pl.BlockSpec((ROUTE_TILE, d), row),
            scratch_shapes=[pltpu.VMEM((2, TILE_CAP // WORD_ROWS, d), jnp.int32),
                            pltpu.VMEM((ROUTE_TILE, d), F32),
                            pltpu.VMEM((TOP_K, ROUTE_TILE, 128), BF16),
                            pltpu.VMEM((TOP_K, ROUTE_TILE, 128), BF16),
                            pltpu.SemaphoreType.DMA((2,))],
        ),
        out_shape=jax.ShapeDtypeStruct((t, d), F32),
        compiler_params=_cparams(("arbitrary",)),
        name="combine",
    )(*pieces, tile_rows, pos_t, gate_t, x2d, sg, su, sd, g, beta, yb)


def _piece_tables(cnt):
    lbase = jnp.cumsum(cnt, axis=1) - cnt
    per_expert = jnp.sum(cnt, axis=0)
    seg_rows = -(-per_expert // EXPERT_ROWS) * EXPERT_ROWS
    seg_start = jnp.cumsum(seg_rows) - seg_rows
    gbase = seg_start[None, :] + jnp.cumsum(cnt, axis=0) - cnt
    gaps = jnp.concatenate([seg_start + per_expert, seg_rows - per_expert]).astype(jnp.int32)

    def listed(n_pieces, start_off, piece_rows, length):
        last = jnp.cumsum(n_pieces, axis=1)
        first = last - n_pieces
        j = jnp.arange(length, dtype=jnp.int32)[None, :, None]
        own = (j >= first[:, None, :]) & (j < last[:, None, :])
        off = (start_off[:, None, :] + (j - first[:, None, :]) * piece_rows)
        pick = lambda base: jnp.sum(jnp.where(own, base[:, None, :] + off, 0), axis=-1).reshape(-1).astype(jnp.int32)
        return pick(lbase), pick(gbase), last[:, -1].astype(jnp.int32)

    n_big = cnt // BIG_PIECE
    big = listed(n_big, jnp.zeros_like(cnt), BIG_PIECE, BIG_PER_TILE)
    small = listed((cnt // RUN_ALIGN) % 2, n_big * BIG_PIECE, RUN_ALIGN, N_EXPERTS)
    return big + small, jnp.sum(cnt, axis=1).astype(jnp.int32), seg_rows, gaps


def kernel(x, mem, w_in, b_in, conv_w, conv_b, conv_ln_g, conv_ln_b, attn_sinks, rel_bias, w_out, b_out, ln1_g, ln1_b, xq_w, xkv_w, xo_w, ln2_g, ln2_b, router_w, router_b, exp_gate, exp_up, exp_down, sh_gate, sh_up, sh_down, ln3_g, ln3_b):
    bsz, seq, d = x.shape
    t = bsz * seq
    bias_tab, sink_tab = _band_tables(rel_bias, attn_sinks[0])
    tri = (jnp.arange(ROUTE_TILE)[:, None] <= jnp.arange(ROUTE_TILE)[None, :]).astype(BF16)
    row = lambda p: p.reshape(1, -1)
    for l in range(DEPTH):
        conv_out, q, k, v = _mix(x, w_in[l].astype(BF16), row(b_in[l]), conv_w[l], row(conv_b[l]),
                                 row(conv_ln_g[l]), row(conv_ln_b[l]))
        x1 = _swa(q, k, v, bias_tab, sink_tab, x, conv_out, w_out[l].astype(BF16), row(b_out[l]),
                  row(ln1_g[l]), row(ln1_b[l]))
        kmem, vmem = _memkv(mem, xkv_w[l].astype(BF16))
        x2, logits_t = _cross(x1, kmem, vmem, xq_w[l].astype(BF16),
                              xo_w[l].astype(BF16), row(ln2_g[l]), row(ln2_b[l]),
                              router_w[l].T.astype(BF16))
        x2 = x2.reshape(t, d)
        pos, gate, cnt = _route(logits_t, router_b[l].reshape(-1, 1), tri)
        nt = t // ROUTE_TILE
        n_rows = (-(-(t * TOP_K + nt * N_EXPERTS * (RUN_ALIGN - 1)) // EXPERT_ROWS) + N_EXPERTS) * EXPERT_ROWS
        pieces, tile_rows, seg_rows, gaps = _piece_tables(cnt[:, :, 0])
        block_expert, block_flags = _block_table(seg_rows, n_rows)
        xs = _dispatch(pieces, tile_rows, jnp.sum(seg_rows).reshape(1).astype(jnp.int32), gaps, pos, x2, n_rows)
        yb = _experts(block_expert, block_flags, xs, exp_gate[l], exp_up[l], exp_down[l])
        pos_t = jnp.transpose(pos, (0, 2, 1)).reshape(t, TOP_K)
        gate_t = jnp.transpose(gate, (0, 2, 1)).reshape(t, TOP_K)
        x = _combine(pieces, tile_rows, pos_t, gate_t, x2, sh_gate[l].astype(BF16),
                     sh_up[l].astype(BF16), sh_down[l].astype(BF16), row(ln3_g[l]), row(ln3_b[l]),
                     yb).reshape(bsz, seq, d)
    return x
```

```python
import functools
import math

import jax
import jax.numpy as jnp
from jax import lax
from jax.experimental import pallas as pl
from jax.experimental.pallas import tpu as pltpu

D_MODEL = 1024
MEM_LEN = 256
HEAD_DIM = 64
CONV_CH = D_MODEL // 2
CONV_WIDTH = 31
ATT_HEADS = 8
KV_HEADS = 2
WINDOW = 128
BLOCK = 128
REL_BUCKETS = 32
REL_MAX_DIST = 128
Q_COLS = ATT_HEADS * HEAD_DIM
KV_COLS = KV_HEADS * HEAD_DIM
IN_COLS = 2 * CONV_CH + Q_COLS + 2 * KV_COLS
X_HEADS = 4
X_HEAD_DIM = D_MODEL // X_HEADS
N_EXPERTS = 64
TOP_K = 8
N_GROUPS = 8
GROUP_SIZE = N_EXPERTS // N_GROUPS
TOPK_GROUPS = 4
EXPERT_FF = D_MODEL // 4
ROUTED_SCALE = 2.5
DEPTH = 1
ALPHA = (2 * DEPTH) ** 0.25
LN_EPS = 1e-5
NEG_INF = -1e30

F32 = jnp.float32
BF16 = jnp.bfloat16

VMEM_LIMIT_BYTES = 56 * 1024 * 1024

ROW_TILE = 512
CONV_ROWS = 32
SUBLANES = 8
CONV_HALO = 32
CROSS_TILE = 1024
CROSS_SPLIT = 2
SWA_TILE = 512
ROUTE_TILE = 512
EXPERT_ROWS = 1024
EXPERT_SPLIT = 2
RUN_ALIGN = 16
RUN_ALIGN_LOG2 = RUN_ALIGN.bit_length() - 1
WORD_ROWS = 2
TILE_CAP = ROUTE_TILE * TOP_K + N_EXPERTS * RUN_ALIGN
BIG_PIECE = 2 * RUN_ALIGN
BIG_PER_TILE = TILE_CAP // BIG_PIECE
PIECE_UNROLL = 4
TOTAL_PIECES = tuple(1 << b for b in range(TILE_CAP.bit_length() - 1, RUN_ALIGN_LOG2 - 1, -1))
SORT_CHUNK = 256
COMBINE_CHUNK = 512


def _cparams(sem):
    return pltpu.CompilerParams(dimension_semantics=sem, vmem_limit_bytes=VMEM_LIMIT_BYTES)


def _layer_norm(h, g, b):
    mu = jnp.mean(h, axis=-1, keepdims=True)
    d = h - mu
    var = jnp.mean(d * d, axis=-1, keepdims=True)
    return d * lax.rsqrt(var + LN_EPS) * g + b


def _sigmoid(x):
    return 1.0 / (1.0 + jnp.exp(-x))


def _mix_kernel(x_ref, w_ref, b_ref, cw_ref, cb_ref, cg_ref, cbeta_ref,
                conv_ref, q_ref, k_ref, v_ref, u_ext, u_sh):
    j = pl.program_id(1)
    xb = x_ref[0].astype(BF16)
    proj = jnp.dot(xb, w_ref[...], preferred_element_type=F32) + b_ref[...]
    a = proj[:, :CONV_CH]
    g = proj[:, CONV_CH:2 * CONV_CH]
    q_ref[0] = (proj[:, 2 * CONV_CH:2 * CONV_CH + Q_COLS] * (HEAD_DIM ** -0.5)).astype(BF16)
    k_ref[0] = proj[:, 2 * CONV_CH + Q_COLS:2 * CONV_CH + Q_COLS + KV_COLS].astype(BF16)
    v_ref[0] = proj[:, 2 * CONV_CH + Q_COLS + KV_COLS:].astype(BF16)

    @pl.when(j == 0)
    def _():
        u_ext[0:CONV_HALO, :] = jnp.zeros((CONV_HALO, CONV_CH), F32)

    u_ext[CONV_HALO:CONV_HALO + ROW_TILE, :] = a * _sigmoid(g)

    first_tap = CONV_HALO - (CONV_WIDTH - 1)
    shifted_rows = u_sh.shape[1]
    for r in range(1, SUBLANES):
        u_sh[r - 1] = u_ext[r:r + shifted_rows, :]

    for c in range(ROW_TILE // CONV_ROWS):
        acc = jnp.zeros((CONV_ROWS, CONV_CH), F32) + cb_ref[...]
        for t in range(CONV_WIDTH):
            r = (first_tap + t) % SUBLANES
            base = c * CONV_ROWS + (first_tap + t) - r
            if r == 0:
                taps = u_ext[base:base + CONV_ROWS, :]
            else:
                taps = u_sh[r - 1, base:base + CONV_ROWS, :]
            acc = acc + taps * cw_ref[t:t + 1, :]
        y = _layer_norm(acc, cg_ref[...], cbeta_ref[...])
        conv_ref[0, c * CONV_ROWS:(c + 1) * CONV_ROWS, :] = (y * _sigmoid(y)).astype(BF16)

    u_ext[0:CONV_HALO, :] = u_ext[ROW_TILE:ROW_TILE + CONV_HALO, :]


def _mix(x, w_in, b_in, conv_w, conv_b, conv_g, conv_beta):
    bsz, seq, d = x.shape
    nt = seq // ROW_TILE
    row = lambda b, j: (b, j, 0)
    const2 = lambda b, j: (0, 0)
    return pl.pallas_call(
        _mix_kernel,
        grid=(bsz, nt),
        in_specs=[
            pl.BlockSpec((1, ROW_TILE, d), row),
            pl.BlockSpec((d, IN_COLS), const2),
            pl.BlockSpec((1, IN_COLS), const2),
            pl.BlockSpec((CONV_WIDTH, CONV_CH), const2),
            pl.BlockSpec((1, CONV_CH), const2),
            pl.BlockSpec((1, CONV_CH), const2),
            pl.BlockSpec((1, CONV_CH), const2),
        ],
        out_specs=[
            pl.BlockSpec((1, ROW_TILE, CONV_CH), row),
            pl.BlockSpec((1, ROW_TILE, Q_COLS), row),
            pl.BlockSpec((1, ROW_TILE, KV_COLS), row),
            pl.BlockSpec((1, ROW_TILE, KV_COLS), row),
        ],
        out_shape=[
            jax.ShapeDtypeStruct((bsz, seq, CONV_CH), BF16),
            jax.ShapeDtypeStruct((bsz, seq, Q_COLS), BF16),
            jax.ShapeDtypeStruct((bsz, seq, KV_COLS), BF16),
            jax.ShapeDtypeStruct((bsz, seq, KV_COLS), BF16),
        ],
        scratch_shapes=[pltpu.VMEM((ROW_TILE + CONV_HALO, CONV_CH), F32),
                        pltpu.VMEM((SUBLANES - 1, ROW_TILE + CONV_HALO - SUBLANES, CONV_CH), F32)],
        compiler_params=_cparams(("arbitrary", "arbitrary")),
        name="mix",
    )(x, w_in, b_in, conv_w, conv_b, conv_g, conv_beta)


def _swa_kernel(q_ref, kp_ref, kc_ref, vp_ref, vc_ref, bias_ref, x_ref, conv_ref, w_ref, b_ref,
                g_ref, beta_ref, o_ref):
    j = pl.program_id(1)
    mix = jnp.dot(conv_ref[0], w_ref[0:CONV_CH, :], preferred_element_type=F32)
    rows = BLOCK + SWA_TILE
    lane = lax.broadcasted_iota(jnp.int32, (rows, 2 * HEAD_DIM), 1)
    low = lane < HEAD_DIM

    def placements(prev_ref, cur_ref):
        t = jnp.concatenate([prev_ref[0], cur_ref[0]], axis=0).astype(F32)
        tr = pltpu.roll(t, HEAD_DIM, 1)
        zero = jnp.zeros_like(t)
        kv0_low = jnp.where(low, t, zero).astype(BF16)
        kv1_high = jnp.where(low, zero, t).astype(BF16)
        kv1_low = jnp.where(low, tr, zero).astype(BF16)
        kv0_high = jnp.where(low, zero, tr).astype(BF16)
        return (kv0_low, kv0_high, kv1_low, kv1_high)

    ks = placements(kp_ref, kc_ref)
    vs = placements(vp_ref, vc_ref)
    slab = 2 * HEAD_DIM
    first_key = lax.broadcasted_iota(jnp.int32, (2 * BLOCK, 2 * HEAD_DIM), 0) == 0
    no_key = jnp.zeros((2 * BLOCK, 2 * HEAD_DIM), BF16)
    ones = jnp.ones((2 * BLOCK, 2 * HEAD_DIM), BF16)
    att = []
    for i in range(SWA_TILE // BLOCK):
        q = q_ref[0, i * BLOCK:(i + 1) * BLOCK, :]
        q_kv0 = jnp.concatenate([q[:, 0:slab], q[:, slab:2 * slab]], axis=0)
        q_kv1 = jnp.concatenate([q[:, 2 * slab:3 * slab], q[:, 3 * slab:4 * slab]], axis=0)
        band = slice(i * BLOCK, (i + 2) * BLOCK)
        outs = []
        for s in range(4):
            qs = q_kv0 if s < 2 else q_kv1
            bias = bias_ref[1, s]
            if i == 0:
                bias = jnp.where(j == 0, bias_ref[0, s], bias)
            kb = jnp.where(first_key, no_key, ks[s][band])
            vb = jnp.where(first_key, no_key, vs[s][band])
            logits = lax.dot_general(qs, kb, (((1,), (1,)), ((), ())), preferred_element_type=F32) + bias
            p = jnp.exp(logits - jnp.max(logits, axis=-1, keepdims=True)).astype(BF16)
            den = jnp.dot(p, ones, preferred_element_type=F32)
            outs.append(jnp.dot(p, vb, preferred_element_type=F32) / den)
        o_kv0 = outs[0] + outs[1]
        o_kv1 = outs[2] + outs[3]
        att.append(jnp.concatenate([o_kv0[0:BLOCK], o_kv0[BLOCK:2 * BLOCK], o_kv1[0:BLOCK],
                                    o_kv1[BLOCK:2 * BLOCK]], axis=1).astype(BF16))

    mix = mix + jnp.dot(jnp.concatenate(att, axis=0), w_ref[CONV_CH:, :], preferred_element_type=F32)
    h = ALPHA * x_ref[0] + mix + b_ref[...]
    o_ref[0] = _layer_norm(h, g_ref[...], beta_ref[...])


def _swa(q, k, v, bias_tab, x, conv_out, w_out, b_out, g, beta):
    bsz, seq, d = x.shape
    per = SWA_TILE // BLOCK
    cur = lambda b, n: (b, n, 0)
    prev = lambda b, n: (b, jnp.maximum(n * per - 1, 0), 0)
    whole = lambda b, n: (0, 0, 0, 0)
    const = lambda b, n: (0, 0)
    return pl.pallas_call(
        _swa_kernel,
        grid=(bsz, seq // SWA_TILE),
        in_specs=[
            pl.BlockSpec((1, SWA_TILE, Q_COLS), cur),
            pl.BlockSpec((1, BLOCK, KV_COLS), prev),
            pl.BlockSpec((1, SWA_TILE, KV_COLS), cur),
            pl.BlockSpec((1, BLOCK, KV_COLS), prev),
            pl.BlockSpec((1, SWA_TILE, KV_COLS), cur),
            pl.BlockSpec((2, 4, 2 * BLOCK, 2 * BLOCK), whole),
            pl.BlockSpec((1, SWA_TILE, d), cur),
            pl.BlockSpec((1, SWA_TILE, CONV_CH), cur),
            pl.BlockSpec((d, d), const),
            pl.BlockSpec((1, d), const),
            pl.BlockSpec((1, d), const),
            pl.BlockSpec((1, d), const),
        ],
        out_specs=pl.BlockSpec((1, SWA_TILE, d), cur),
        out_shape=jax.ShapeDtypeStruct((bsz, seq, d), F32),
        compiler_params=_cparams(("arbitrary", "arbitrary")),
        name="swa",
    )(q, k, k, v, v, bias_tab, x, conv_out, w_out, b_out, g, beta)


def _t5_bucket(dist):
    n = jnp.maximum(dist, 0)
    exact = REL_BUCKETS // 2
    large = exact + (jnp.log(jnp.maximum(n, 1).astype(F32) / exact)
                     / math.log(REL_MAX_DIST / exact) * (REL_BUCKETS - exact)).astype(jnp.int32)
    large = jnp.minimum(large, REL_BUCKETS - 1)
    return jnp.where(n < exact, n, large)


def _band_tables(rel_bias, sinks):
    qi = jnp.arange(BLOCK)[:, None]
    kj = jnp.arange(2 * BLOCK)[None, :]
    dist = qi + BLOCK - kj
    bucket = _t5_bucket(dist)
    bias = jnp.zeros((ATT_HEADS, BLOCK, 2 * BLOCK), F32)
    for bkt in range(REL_BUCKETS):
        bias = jnp.where(bucket[None] == bkt, rel_bias[bkt].astype(F32)[:, None, None], bias)
    in_window = (dist >= 0) & (dist < WINDOW)
    masks = jnp.stack([in_window & (kj >= BLOCK), in_window])
    masked = jnp.where(masks[:, None], bias[None], NEG_INF)
    masked = jnp.where(kj[None, None] == 0, sinks.astype(F32)[None, :, None, None], masked)
    pairs = ((0, 2), (1, 3), (4, 6), (5, 7))
    return jnp.stack([jnp.concatenate([masked[:, a], masked[:, b]], axis=1) for a, b in pairs], axis=1)


def _memkv_kernel(mem_ref, w_ref, k_ref, v_ref):
    kv = jnp.dot(mem_ref[0].astype(BF16), w_ref[...], preferred_element_type=F32)
    k_ref[0] = kv[:, :D_MODEL].astype(BF16)
    v_ref[0] = kv[:, D_MODEL:].astype(BF16)


def _memkv(mem, wkv):
    bsz, m, d = mem.shape
    return pl.pallas_call(
        _memkv_kernel,
        grid=(bsz,),
        in_specs=[pl.BlockSpec((1, m, d), lambda b: (b, 0, 0)),
                  pl.BlockSpec((d, 2 * d), lambda b: (0, 0))],
        out_specs=[pl.BlockSpec((1, m, d), lambda b: (b, 0, 0)),
                   pl.BlockSpec((1, m, d), lambda b: (b, 0, 0))],
        out_shape=[jax.ShapeDtypeStruct((bsz, m, d), BF16)] * 2,
        compiler_params=_cparams(("arbitrary",)),
        name="memkv",
    )(mem, wkv)


def _cross_kernel(x_ref, k_ref, v_ref, wq_ref, wo_ref, g_ref, beta_ref, rw_ref, rb_ref, tri_ref,
                  o_ref, pos_ref, gate_ref, cnt_ref):
    sub = CROSS_TILE // CROSS_SPLIT
    groups = [slice(s * sub, (s + 1) * sub) for s in range(CROSS_SPLIT)]
    xs = [x_ref[0, g, :] for g in groups]
    qs = [(jnp.dot(x.astype(BF16), wq_ref[...], preferred_element_type=F32) * (X_HEAD_DIM ** -0.5)).astype(BF16)
          for x in xs]
    heads = [[] for _ in groups]
    for h in range(X_HEADS):
        cols = slice(h * X_HEAD_DIM, (h + 1) * X_HEAD_DIM)
        logits = [lax.dot_general(q[:, cols], k_ref[0, :, cols], (((1,), (1,)), ((), ())),
                                  preferred_element_type=F32) for q in qs]
        ps, dens = [], []
        for lg in logits:
            p = jnp.exp(lg - jnp.max(lg, axis=-1, keepdims=True))
            ps.append(p)
            dens.append(jnp.sum(p, axis=-1, keepdims=True))
        for s, (p, den) in enumerate(zip(ps, dens)):
            o = jnp.dot(p.astype(BF16), v_ref[0, :, cols], preferred_element_type=F32)
            heads[s].append((o / den).astype(BF16))
    crosses = [jnp.dot(jnp.concatenate(hs, axis=-1), wo_ref[...], preferred_element_type=F32) for hs in heads]
    for s, (g, x, cross) in enumerate(zip(groups, xs, crosses)):
        y = _layer_norm(ALPHA * x + cross, g_ref[...], beta_ref[...])
        o_ref[0, g, :] = y
        logits_t = lax.dot_general(rw_ref[...], y.astype(BF16), (((1,), (1,)), ((), ())),
                                   preferred_element_type=F32)
        positions, gates, run_len = _route_tile(logits_t, rb_ref[...], tri_ref[...])
        cnt_ref[s] = run_len
        for r in range(TOP_K):
            pos_ref[s, r:r + 1, :] = positions[r]
            gate_ref[s, r:r + 1, :] = gates[r]


def _cross(x1, kmem, vmem, wq, wo, g, beta, rw_t, router_b, tri):
    bsz, seq, d = x1.shape
    nt = seq // CROSS_TILE
    assert CROSS_TILE // CROSS_SPLIT == ROUTE_TILE
    n_tiles = bsz * seq // ROUTE_TILE
    row = lambda b, j: (b, j, 0)
    mem = lambda b, j: (b, 0, 0)
    const = lambda b, j: (0, 0)
    tiles = lambda b, j: (b * nt + j, 0, 0)
    return pl.pallas_call(
        _cross_kernel,
        grid=(bsz, nt),
        in_specs=[
            pl.BlockSpec((1, CROSS_TILE, d), row),
            pl.BlockSpec((1, MEM_LEN, d), mem),
            pl.BlockSpec((1, MEM_LEN, d), mem),
            pl.BlockSpec((d, d), const),
            pl.BlockSpec((d, d), const),
            pl.BlockSpec((1, d), const),
            pl.BlockSpec((1, d), const),
            pl.BlockSpec((N_EXPERTS, d), const),
            pl.BlockSpec((N_EXPERTS, 1), const),
            pl.BlockSpec((ROUTE_TILE, ROUTE_TILE), const),
        ],
        out_specs=[
            pl.BlockSpec((1, CROSS_TILE, d), row),
            pl.BlockSpec((CROSS_SPLIT, TOP_K, ROUTE_TILE), tiles),
            pl.BlockSpec((CROSS_SPLIT, TOP_K, ROUTE_TILE), tiles),
            pl.BlockSpec((CROSS_SPLIT, N_EXPERTS, 128), tiles),
        ],
        out_shape=[
            jax.ShapeDtypeStruct((bsz, seq, d), F32),
            jax.ShapeDtypeStruct((n_tiles, TOP_K, ROUTE_TILE), jnp.int32),
            jax.ShapeDtypeStruct((n_tiles, TOP_K, ROUTE_TILE), F32),
            jax.ShapeDtypeStruct((n_tiles, N_EXPERTS, 128), jnp.int32),
        ],
        compiler_params=_cparams(("arbitrary", "arbitrary")),
        name="cross",
    )(x1, kmem, vmem, wq, wo, g, beta, rw_t, router_b, tri)


def _route_tile(logits_t, router_b, tri):
    tn = ROUTE_TILE
    scores = _sigmoid(logits_t)
    choice = scores + router_b

    gscore = []
    member = lax.broadcasted_iota(jnp.int32, (GROUP_SIZE, tn), 0).astype(F32)
    for g in range(N_GROUPS):
        c = choice[g * GROUP_SIZE:(g + 1) * GROUP_SIZE, :]
        m1 = jnp.max(c, axis=0, keepdims=True)
        first = jnp.min(jnp.where(c == m1, member, float(GROUP_SIZE)), axis=0, keepdims=True)
        m2 = jnp.max(jnp.where(member == first, -jnp.inf, c), axis=0, keepdims=True)
        gscore.append(m1 + m2)

    keep_rows = []
    for g in range(N_GROUPS):
        beaten = jnp.zeros((1, tn), F32)
        for o in range(N_GROUPS):
            if o == g:
                continue
            ahead = (gscore[o] >= gscore[g]) if o < g else (gscore[o] > gscore[g])
            beaten = beaten + jnp.where(ahead, 1.0, 0.0)
        keep_rows.append(jnp.broadcast_to(beaten, (GROUP_SIZE, tn)))
    beaten_all = jnp.concatenate(keep_rows, axis=0)

    masked = jnp.where(beaten_all < TOPK_GROUPS, choice, -jnp.inf)
    eidx = lax.broadcasted_iota(jnp.int32, (N_EXPERTS, tn), 0).astype(F32)
    sel = jnp.zeros((N_EXPERTS, tn), F32)
    picks, weights = [], []
    for r in range(TOP_K):
        mx = jnp.max(masked, axis=0, keepdims=True)
        first = jnp.min(jnp.where(masked == mx, eidx, float(N_EXPERTS)), axis=0, keepdims=True)
        pick = eidx == first
        picks.append((pick, first))
        weights.append(jnp.sum(jnp.where(pick, scores, 0.0), axis=0, keepdims=True))
        masked = jnp.where(pick, -jnp.inf, masked)
        sel = jnp.where(pick, 1.0, sel)

    wsum = weights[0]
    for r in range(1, TOP_K):
        wsum = wsum + weights[r]

    count = jnp.sum(sel, axis=1, keepdims=True)
    run_len = jnp.floor((count + (RUN_ALIGN - 1.0)) * (1.0 / RUN_ALIGN)) * RUN_ALIGN
    run_len_b = jnp.broadcast_to(run_len, (N_EXPERTS, 128))
    er = lax.broadcasted_iota(jnp.int32, (N_EXPERTS, N_EXPERTS), 0)
    ec = lax.broadcasted_iota(jnp.int32, (N_EXPERTS, N_EXPERTS), 1)
    before = jnp.where(ec < er, 1.0, 0.0).astype(BF16)
    run_start = jnp.dot(before, run_len_b.astype(BF16), preferred_element_type=F32)[:, 0:1]
    incl = jnp.dot(sel.astype(BF16), tri, preferred_element_type=F32)
    pos_mat = run_start + incl - sel
    positions = [jnp.sum(jnp.where(pick, pos_mat, 0.0), axis=0, keepdims=True).astype(jnp.int32)
                 for pick, _ in picks]
    gates = [w / wsum * ROUTED_SCALE for w in weights]
    return positions, gates, run_len_b.astype(jnp.int32)


def _for_each_piece(tables, tile, fn):
    for n_rows, per_tile, (local_ref, global_ref, count_ref) in (
            (BIG_PIECE, BIG_PER_TILE, tables[0:3]), (RUN_ALIGN, N_EXPERTS, tables[3:6])):
        count = count_ref[tile]

        def body(g, carry, n_rows=n_rows, per_tile=per_tile, local_ref=local_ref, global_ref=global_ref,
                 count=count):
            for u in range(PIECE_UNROLL):
                j = g * PIECE_UNROLL + u

                @pl.when(j < count)
                def _():
                    fn(pl.multiple_of(local_ref[tile * per_tile + j], RUN_ALIGN),
                       pl.multiple_of(global_ref[tile * per_tile + j], RUN_ALIGN), n_rows)
            return carry

        lax.fori_loop(0, (count + PIECE_UNROLL - 1) // PIECE_UNROLL, body, 0)


def _word_rows(start, size):
    if isinstance(start, int):
        first = start // WORD_ROWS
    else:
        first = pl.multiple_of(start >> (WORD_ROWS.bit_length() - 1), RUN_ALIGN // WORD_ROWS)
    return pl.ds(first, size // WORD_ROWS)


def _for_each_total_piece(total, fn):
    for piece in TOTAL_PIECES:
        @pl.when((total & piece) != 0)
        def _():
            fn(piece)


def _dispatch_kernel(*refs):
    tables, (rows_ref, total_ref, gap_ref, pos_ref, x_ref, xs_ref, stage, zeros, sem, zsem) = refs[:6], refs[6:]
    i = pl.program_id(0)
    nt = pl.num_programs(0)
    slot = i % 2
    tn = ROUTE_TILE

    def drain(tile, buf):
        def wait_rows(n):
            pltpu.make_async_copy(stage.at[buf, _word_rows(0, n), :], xs_ref.at[_word_rows(0, n), :],
                                  sem.at[buf]).wait()
        _for_each_total_piece(rows_ref[tile], wait_rows)

    @pl.when(i >= 2)
    def _():
        drain(i - 2, slot)

    xb = x_ref[...].astype(BF16)
    pos = pos_ref[0]
    row_iota = lax.broadcasted_iota(jnp.int32, (SORT_CHUNK, tn), 0).astype(F32).astype(BF16)
    one = jnp.ones((SORT_CHUNK, tn), BF16)
    zero = jnp.zeros((SORT_CHUNK, tn), BF16)

    def onehot_rows(c):
        rel = jnp.clip(pos - c * SORT_CHUNK, -1, SORT_CHUNK).astype(F32).astype(BF16)
        hit = rel[0:1, :] == row_iota
        for k in range(1, TOP_K):
            hit = hit | (rel[k:k + 1, :] == row_iota)
        return jnp.where(hit, one, zero)

    def sort_chunks(c2, carry):
        for c in (2 * c2, 2 * c2 + 1):
            p0 = pl.multiple_of(c * SORT_CHUNK, SORT_CHUNK)
            rows = jnp.dot(onehot_rows(c), xb, preferred_element_type=F32)
            stage[slot, _word_rows(p0, SORT_CHUNK), :] = pltpu.bitcast(rows.astype(BF16), jnp.int32)
        return carry

    lax.fori_loop(0, (rows_ref[i] + 2 * SORT_CHUNK - 1) // (2 * SORT_CHUNK), sort_chunks, 0)

    def start_piece(local, dst, n):
        pltpu.make_async_copy(stage.at[slot, _word_rows(local, n), :], xs_ref.at[_word_rows(dst, n), :],
                              sem.at[slot]).start()

    _for_each_piece(tables, i, start_piece)

    @pl.when(i == nt - 1)
    def _():
        @pl.when(nt >= 2)
        def _():
            drain(i - 1, 1 - slot)

        drain(i, slot)

        zeros[...] = jnp.zeros_like(zeros)

        def gap_copies(e, act):
            gap = gap_ref[N_EXPERTS + e]
            for piece in TOTAL_PIECES:
                if piece >= EXPERT_ROWS:
                    continue

                @pl.when((gap & piece) != 0)
                def _():
                    dst = gap_ref[e] + (gap & (-2 * piece))
                    act(pltpu.make_async_copy(zeros.at[_word_rows(0, piece), :],
                                              xs_ref.at[_word_rows(dst, piece), :], zsem))

        def zero_block(b):
            return pltpu.make_async_copy(zeros, xs_ref.at[_word_rows(b * EXPERT_ROWS, EXPERT_ROWS), :], zsem)

        first_free = total_ref[0] // EXPERT_ROWS
        n_blocks = xs_ref.shape[0] * WORD_ROWS // EXPERT_ROWS
        lax.fori_loop(0, N_EXPERTS, lambda e, c: (gap_copies(e, lambda cp: cp.start()), c)[1], 0)
        lax.fori_loop(first_free, n_blocks, lambda b, c: (zero_block(b).start(), c)[1], 0)
        lax.fori_loop(0, N_EXPERTS, lambda e, c: (gap_copies(e, lambda cp: cp.wait()), c)[1], 0)
        lax.fori_loop(first_free, n_blocks, lambda b, c: (zero_block(b).wait(), c)[1], 0)


def _dispatch(pieces, tile_rows, total, gaps, pos, x2d, n_rows):
    t, d = x2d.shape
    nt = t // ROUTE_TILE
    return pl.pallas_call(
        _dispatch_kernel,
        grid_spec=pltpu.PrefetchScalarGridSpec(
            num_scalar_prefetch=len(pieces) + 3,
            grid=(nt,),
            in_specs=[pl.BlockSpec((1, TOP_K, ROUTE_TILE), lambda i, *_: (i, 0, 0)),
                      pl.BlockSpec((ROUTE_TILE, d), lambda i, *_: (i, 0))],
            out_specs=pl.BlockSpec(memory_space=pl.ANY),
            scratch_shapes=[pltpu.VMEM((2, TILE_CAP // WORD_ROWS, d), jnp.int32),
                            pltpu.VMEM((EXPERT_ROWS // WORD_ROWS, d), jnp.int32),
                            pltpu.SemaphoreType.DMA((2,)),
                            pltpu.SemaphoreType.DMA(())],
        ),
        out_shape=jax.ShapeDtypeStruct((n_rows // WORD_ROWS, d), jnp.int32),
        compiler_params=_cparams(("arbitrary",)),
        name="dispatch",
    )(*pieces, tile_rows, total, gaps, pos, x2d)


def _experts_kernel(ie_ref, flag_ref, xs_ref, wg_ref, wu_ref, wd_ref, y_ref, wg_bf, wu_bf, wd_bf):
    w = pl.program_id(0)
    flags = flag_ref[w]
    valid = (flags & 1) != 0
    new_expert = (flags & 2) != 0

    @pl.when(jnp.logical_not(valid))
    def _():
        y_ref[...] = jnp.zeros_like(y_ref)

    @pl.when(new_expert)
    def _():
        wg_bf[...] = wg_ref[0].astype(BF16)
        wu_bf[...] = wu_ref[0].astype(BF16)
        wd_bf[...] = wd_ref[0].astype(BF16)

    @pl.when(valid)
    def _():
        sub = EXPERT_ROWS // WORD_ROWS // EXPERT_SPLIT
        groups = [slice(s * sub, (s + 1) * sub) for s in range(EXPERT_SPLIT)]
        xs = [pltpu.bitcast(xs_ref[g, :], BF16) for g in groups]
        gu = [(jnp.dot(x, wg_bf[...], preferred_element_type=F32),
               jnp.dot(x, wu_bf[...], preferred_element_type=F32)) for x in xs]
        hs = [(gte * _sigmoid(gte) * up).astype(BF16) for gte, up in gu]
        for g, h in zip(groups, hs):
            y = jnp.dot(h, wd_bf[...], preferred_element_type=F32)
            y_ref[g, :] = pltpu.bitcast(y.astype(BF16), jnp.int32)


def _experts(block_expert, block_flags, xs, wg, wu, wd):
    n, d = xs.shape
    ff = wg.shape[-1]
    return pl.pallas_call(
        _experts_kernel,
        grid_spec=pltpu.PrefetchScalarGridSpec(
            num_scalar_prefetch=2,
            grid=(block_expert.shape[0],),
            in_specs=[
                pl.BlockSpec((EXPERT_ROWS // WORD_ROWS, d), lambda w, ie, fl: (jnp.minimum(w, fl[fl.shape[0] - 1]), 0)),
                pl.BlockSpec((1, d, ff), lambda w, ie, fl: (ie[w], 0, 0)),
                pl.BlockSpec((1, d, ff), lambda w, ie, fl: (ie[w], 0, 0)),
                pl.BlockSpec((1, ff, d), lambda w, ie, fl: (ie[w], 0, 0)),
            ],
            out_specs=pl.BlockSpec((EXPERT_ROWS // WORD_ROWS, d), lambda w, ie, fl: (w, 0)),
            scratch_shapes=[pltpu.VMEM((d, ff), BF16), pltpu.VMEM((d, ff), BF16), pltpu.VMEM((ff, d), BF16)],
        ),
        out_shape=jax.ShapeDtypeStruct((n, d), jnp.int32),
        compiler_params=_cparams(("arbitrary",)),
        name="experts",
    )(block_expert, block_flags, xs, wg, wu, wd)


def _block_table(seg_rows, n_rows):
    block_end = jnp.cumsum(seg_rows // EXPERT_ROWS)
    w = jnp.arange(n_rows // EXPERT_ROWS, dtype=jnp.int32)
    valid = w < block_end[-1]
    e = jnp.sum((block_end[None, :] <= jnp.minimum(w, block_end[-1] - 1)[:, None]).astype(jnp.int32), axis=1)
    e = jnp.minimum(e, N_EXPERTS - 1)
    prev_e = jnp.concatenate([jnp.full((1,), -1, jnp.int32), e[:-1]])
    flags = valid.astype(jnp.int32) + 2 * (valid & (e != prev_e)).astype(jnp.int32)
    return e, jnp.concatenate([flags, block_end[-1:].astype(jnp.int32) - 1])


def _combine_kernel(*refs):
    tables = refs[:6]
    (rows_ref, pos_ref, gate_ref, x_ref, sg_ref, su_ref, sd_ref, g_ref, beta_ref, yb_ref, o_ref,
     stage, acc, lane_tile, lane_gate, sem) = refs[6:]
    i = pl.program_id(0)
    nt = pl.num_programs(0)
    slot = i % 2
    tn = ROUTE_TILE

    def start_tile(tile, buf):
        def start_piece(local, src, n):
            pltpu.make_async_copy(yb_ref.at[_word_rows(src, n), :], stage.at[buf, _word_rows(local, n), :],
                                  sem.at[buf]).start()
        _for_each_piece(tables, tile, start_piece)

    @pl.when(i == 0)
    def _():
        stage[...] = jnp.zeros_like(stage)
        start_tile(0, 0)

    @pl.when(i + 1 < nt)
    def _():
        start_tile(i + 1, 1 - slot)

    x = x_ref[...]
    xb = x.astype(BF16)
    gte = jnp.dot(xb, sg_ref[...], preferred_element_type=F32)
    up = jnp.dot(xb, su_ref[...], preferred_element_type=F32)
    h = (gte * _sigmoid(gte) * up).astype(BF16)
    acc[...] = jnp.dot(h, sd_ref[...], preferred_element_type=F32)

    def wait_rows(n):
        pltpu.make_async_copy(yb_ref.at[_word_rows(0, n), :], stage.at[slot, _word_rows(0, n), :],
                              sem.at[slot]).wait()

    _for_each_total_piece(rows_ref[i], wait_rows)

    lanes = 128
    lane_iota = lax.broadcasted_iota(jnp.int32, (tn, lanes), 1)
    for k in range(TOP_K):
        p = jnp.broadcast_to(pos_ref[:, k:k + 1], (tn, lanes))
        g = jnp.broadcast_to(gate_ref[:, k:k + 1], (tn, lanes))
        lane_tile[k] = (p >> (lanes.bit_length() - 1)).astype(F32).astype(BF16)
        lane_gate[k] = jnp.where((p & (lanes - 1)) == lane_iota, g, 0.0).astype(BF16)
    zero = jnp.zeros((tn, lanes), BF16)

    def gate_matrix(c):
        cols = []
        for j in range(COMBINE_CHUNK // lanes):
            tile = jnp.asarray(c * (COMBINE_CHUNK // lanes) + j, jnp.int32).astype(F32).astype(BF16)
            w = zero
            for k in range(TOP_K):
                w = w + jnp.where(lane_tile[k] == tile, lane_gate[k], zero)
            cols.append(w)
        return jnp.concatenate(cols, axis=1)

    def weighed(c):
        p0 = pl.multiple_of(c * COMBINE_CHUNK, COMBINE_CHUNK)
        rows = pltpu.bitcast(stage[slot, _word_rows(p0, COMBINE_CHUNK), :], BF16)
        return jnp.dot(gate_matrix(c), rows, preferred_element_type=F32)

    def weigh_chunks(c2, carry):
        acc[...] += weighed(2 * c2) + weighed(2 * c2 + 1)
        return carry

    n_chunks = (rows_ref[i] + COMBINE_CHUNK - 1) // COMBINE_CHUNK
    lax.fori_loop(0, n_chunks // 2, weigh_chunks, 0)

    @pl.when(n_chunks % 2 == 1)
    def _():
        acc[...] += weighed(n_chunks - 1)
    o_ref[...] = _layer_norm(ALPHA * x + acc[...], g_ref[...], beta_ref[...])


def _combine(pieces, tile_rows, pos_t, gate_t, x2d, sg, su, sd, g, beta, yb):
    t, d = x2d.shape
    nt = t // ROUTE_TILE
    ff = sg.shape[-1]
    row = lambda i, *_: (i, 0)
    const = lambda i, *_: (0, 0)
    return pl.pallas_call(
        _combine_kernel,
        grid_spec=pltpu.PrefetchScalarGridSpec(
            num_scalar_prefetch=len(pieces) + 1,
            grid=(nt,),
            in_specs=[
                pl.BlockSpec((ROUTE_TILE, TOP_K), row),
                pl.BlockSpec((ROUTE_TILE, TOP_K), row),
                pl.BlockSpec((ROUTE_TILE, d), row),
                pl.BlockSpec((d, ff), const),
                pl.BlockSpec((d, ff), const),
                pl.BlockSpec((ff, d), const),
                pl.BlockSpec((1, d), const),
                pl.BlockSpec((1, d), const),
                pl.BlockSpec(memory_space=pl.ANY),
            ],
            out_specs=pl.BlockSpec((ROUTE_TILE, d), row),
            scratch_shapes=[pltpu.VMEM((2, TILE_CAP // WORD_ROWS, d), jnp.int32),
                            pltpu.VMEM((ROUTE_TILE, d), F32),
                            pltpu.VMEM((TOP_K, ROUTE_TILE, 128), BF16),
                            pltpu.VMEM((TOP_K, ROUTE_TILE, 128), BF16),
                            pltpu.SemaphoreType.DMA((2,))],
        ),
        out_shape=jax.ShapeDtypeStruct((t, d), F32),
        compiler_params=_cparams(("arbitrary",)),
        name="combine",
    )(*pieces, tile_rows, pos_t, gate_t, x2d, sg, su, sd, g, beta, yb)


def _piece_tables(cnt):
    lbase = jnp.cumsum(cnt, axis=1) - cnt
    per_expert = jnp.sum(cnt, axis=0)
    seg_rows = -(-per_expert // EXPERT_ROWS) * EXPERT_ROWS
    seg_start = jnp.cumsum(seg_rows) - seg_rows
    gbase = seg_start[None, :] + jnp.cumsum(cnt, axis=0) - cnt
    gaps = jnp.concatenate([seg_start + per_expert, seg_rows - per_expert]).astype(jnp.int32)

    def listed(n_pieces, start_off, piece_rows, length):
        last = jnp.cumsum(n_pieces, axis=1)
        first = last - n_pieces
        j = jnp.arange(length, dtype=jnp.int32)[None, :, None]
        own = (j >= first[:, None, :]) & (j < last[:, None, :])
        off = (start_off[:, None, :] + (j - first[:, None, :]) * piece_rows)
        pick = lambda base: jnp.sum(jnp.where(own, base[:, None, :] + off, 0), axis=-1).reshape(-1).astype(jnp.int32)
        return pick(lbase), pick(gbase), last[:, -1].astype(jnp.int32)

    n_big = cnt // BIG_PIECE
    big = listed(n_big, jnp.zeros_like(cnt), BIG_PIECE, BIG_PER_TILE)
    small = listed((cnt // RUN_ALIGN) % 2, n_big * BIG_PIECE, RUN_ALIGN, N_EXPERTS)
    return big + small, jnp.sum(cnt, axis=1).astype(jnp.int32), seg_rows, gaps


def kernel(x, mem, w_in, b_in, conv_w, conv_b, conv_ln_g, conv_ln_b, attn_sinks, rel_bias, w_out, b_out, ln1_g, ln1_b, xq_w, xkv_w, xo_w, ln2_g, ln2_b, router_w, router_b, exp_gate, exp_up, exp_down, sh_gate, sh_up, sh_down, ln3_g, ln3_b):
    bsz, seq, d = x.shape
    t = bsz * seq
    bias_tab = _band_tables(rel_bias, attn_sinks[0])
    tri = (jnp.arange(ROUTE_TILE)[:, None] <= jnp.arange(ROUTE_TILE)[None, :]).astype(BF16)
    row = lambda p: p.reshape(1, -1)
    for l in range(DEPTH):
        conv_out, q, k, v = _mix(x, w_in[l].astype(BF16), row(b_in[l]), conv_w[l], row(conv_b[l]),
                                 row(conv_ln_g[l]), row(conv_ln_b[l]))
        x1 = _swa(q, k, v, bias_tab, x, conv_out, w_out[l].astype(BF16), row(b_out[l]),
                  row(ln1_g[l]), row(ln1_b[l]))
        kmem, vmem = _memkv(mem, xkv_w[l].astype(BF16))
        x2, pos, gate, cnt = _cross(x1, kmem, vmem, xq_w[l].astype(BF16), xo_w[l].astype(BF16),
                                    row(ln2_g[l]), row(ln2_b[l]), router_w[l].T.astype(BF16),
                                    router_b[l].reshape(-1, 1), tri)
        x2 = x2.reshape(t, d)
        nt = t // ROUTE_TILE
        n_rows = (-(-(t * TOP_K + nt * N_EXPERTS * (RUN_ALIGN - 1)) // EXPERT_ROWS) + N_EXPERTS) * EXPERT_ROWS
        pieces, tile_rows, seg_rows, gaps = _piece_tables(cnt[:, :, 0])
        block_expert, block_flags = _block_table(seg_rows, n_rows)
        xs = _dispatch(pieces, tile_rows, jnp.sum(seg_rows).reshape(1).astype(jnp.int32), gaps, pos, x2, n_rows)
        yb = _experts(block_expert, block_flags, xs, exp_gate[l], exp_up[l], exp_down[l])
        pos_t = jnp.transpose(pos, (0, 2, 1)).reshape(t, TOP_K)
        gate_t = jnp.transpose(gate, (0, 2, 1)).reshape(t, TOP_K)
        x = _combine(pieces, tile_rows, pos_t, gate_t, x2, sh_gate[l].astype(BF16),
                     sh_up[l].astype(BF16), sh_down[l].astype(BF16), row(ln3_g[l]), row(ln3_b[l]),
                     yb).reshape(bsz, seq, d)
    return x
```

```python
import functools
import math

import jax
import jax.numpy as jnp
from jax import lax
from jax.experimental import pallas as pl
from jax.experimental.pallas import tpu as pltpu

D_MODEL = 1024
HEAD_DIM = 64
CONV_CH = D_MODEL // 2
CONV_WIDTH = 31
ATT_HEADS = 8
KV_HEADS = 2
WINDOW = 128
BLOCK = 128
REL_BUCKETS = 32
REL_MAX_DIST = 128
Q_COLS = ATT_HEADS * HEAD_DIM
KV_COLS = KV_HEADS * HEAD_DIM
IN_COLS = 2 * CONV_CH + Q_COLS + 2 * KV_COLS
X_HEADS = 4
X_HEAD_DIM = D_MODEL // X_HEADS
N_EXPERTS = 64
TOP_K = 8
N_GROUPS = 8
GROUP_SIZE = N_EXPERTS // N_GROUPS
TOPK_GROUPS = 4
EXPERT_FF = D_MODEL // 4
ROUTED_SCALE = 2.5
DEPTH = 1
ALPHA = (2 * DEPTH) ** 0.25
LN_EPS = 1e-5
NEG_INF = -1e30

F32 = jnp.float32
BF16 = jnp.bfloat16

VMEM_LIMIT_BYTES = 56 * 1024 * 1024

ROW_TILE = 512
CONV_ROWS = 32
SUBLANES = 8
CONV_HALO = 32
CROSS_TILE = 1024
CROSS_SPLIT = 2
SWA_TILE = 512
ROUTE_TILE = 512
EXPERT_ROWS = 1024
EXPERT_SPLIT = 2
RUN_ALIGN = 16
RUN_ALIGN_LOG2 = RUN_ALIGN.bit_length() - 1
WORD_ROWS = 2
TILE_CAP = ROUTE_TILE * TOP_K + N_EXPERTS * RUN_ALIGN
BIG_PIECE = 32
BIG_PER_TILE = TILE_CAP // BIG_PIECE
SMALL_PIECES = tuple(1 << b for b in range(BIG_PIECE.bit_length() - 2, RUN_ALIGN_LOG2 - 1, -1))
N_PIECE_TABLES = 3 * (1 + len(SMALL_PIECES))
PIECE_UNROLL = 4
TOTAL_PIECES = tuple(1 << b for b in range(TILE_CAP.bit_length() - 1, RUN_ALIGN_LOG2 - 1, -1))
SORT_CHUNK = 256
COMBINE_CHUNK = 512


def _cparams(sem):
    return pltpu.CompilerParams(dimension_semantics=sem, vmem_limit_bytes=VMEM_LIMIT_BYTES)


def _layer_norm(h, g, b):
    mu = jnp.mean(h, axis=-1, keepdims=True)
    d = h - mu
    var = jnp.mean(d * d, axis=-1, keepdims=True)
    return d * lax.rsqrt(var + LN_EPS) * g + b


def _sigmoid(x):
    return 1.0 / (1.0 + jnp.exp(-x))


def _mix_kernel(x_ref, w_ref, b_ref, cw_ref, cb_ref, cg_ref, cbeta_ref,
                conv_ref, q_ref, k_ref, v_ref, u_ext, u_sh):
    j = pl.program_id(1)
    xb = x_ref[0].astype(BF16)
    proj = jnp.dot(xb, w_ref[...], preferred_element_type=F32) + b_ref[...]
    a = proj[:, :CONV_CH]
    g = proj[:, CONV_CH:2 * CONV_CH]
    q_ref[0] = (proj[:, 2 * CONV_CH:2 * CONV_CH + Q_COLS] * (HEAD_DIM ** -0.5)).astype(BF16)
    k_ref[0] = proj[:, 2 * CONV_CH + Q_COLS:2 * CONV_CH + Q_COLS + KV_COLS].astype(BF16)
    v_ref[0] = proj[:, 2 * CONV_CH + Q_COLS + KV_COLS:].astype(BF16)

    @pl.when(j == 0)
    def _():
        u_ext[0:CONV_HALO, :] = jnp.zeros((CONV_HALO, CONV_CH), F32)

    u_ext[CONV_HALO:CONV_HALO + ROW_TILE, :] = a * _sigmoid(g)

    first_tap = CONV_HALO - (CONV_WIDTH - 1)
    shifted_rows = u_sh.shape[1]
    for r in range(1, SUBLANES):
        u_sh[r - 1] = u_ext[r:r + shifted_rows, :]

    for c in range(ROW_TILE // CONV_ROWS):
        acc = jnp.zeros((CONV_ROWS, CONV_CH), F32) + cb_ref[...]
        for t in range(CONV_WIDTH):
            r = (first_tap + t) % SUBLANES
            base = c * CONV_ROWS + (first_tap + t) - r
            if r == 0:
                taps = u_ext[base:base + CONV_ROWS, :]
            else:
                taps = u_sh[r - 1, base:base + CONV_ROWS, :]
            acc = acc + taps * cw_ref[t:t + 1, :]
        y = _layer_norm(acc, cg_ref[...], cbeta_ref[...])
        conv_ref[0, c * CONV_ROWS:(c + 1) * CONV_ROWS, :] = (y * _sigmoid(y)).astype(BF16)

    u_ext[0:CONV_HALO, :] = u_ext[ROW_TILE:ROW_TILE + CONV_HALO, :]


def _mix(x, w_in, b_in, conv_w, conv_b, conv_g, conv_beta):
    bsz, seq, d = x.shape
    nt = seq // ROW_TILE
    row = lambda b, j: (b, j, 0)
    const2 = lambda b, j: (0, 0)
    return pl.pallas_call(
        _mix_kernel,
        grid=(bsz, nt),
        in_specs=[
            pl.BlockSpec((1, ROW_TILE, d), row),
            pl.BlockSpec((d, IN_COLS), const2),
            pl.BlockSpec((1, IN_COLS), const2),
            pl.BlockSpec((CONV_WIDTH, CONV_CH), const2),
            pl.BlockSpec((1, CONV_CH), const2),
            pl.BlockSpec((1, CONV_CH), const2),
            pl.BlockSpec((1, CONV_CH), const2),
        ],
        out_specs=[
            pl.BlockSpec((1, ROW_TILE, CONV_CH), row),
            pl.BlockSpec((1, ROW_TILE, Q_COLS), row),
            pl.BlockSpec((1, ROW_TILE, KV_COLS), row),
            pl.BlockSpec((1, ROW_TILE, KV_COLS), row),
        ],
        out_shape=[
            jax.ShapeDtypeStruct((bsz, seq, CONV_CH), BF16),
            jax.ShapeDtypeStruct((bsz, seq, Q_COLS), BF16),
            jax.ShapeDtypeStruct((bsz, seq, KV_COLS), BF16),
            jax.ShapeDtypeStruct((bsz, seq, KV_COLS), BF16),
        ],
        scratch_shapes=[pltpu.VMEM((ROW_TILE + CONV_HALO, CONV_CH), F32),
                        pltpu.VMEM((SUBLANES - 1, ROW_TILE + CONV_HALO - SUBLANES, CONV_CH), F32)],
        compiler_params=_cparams(("arbitrary", "arbitrary")),
        name="mix",
    )(x, w_in, b_in, conv_w, conv_b, conv_g, conv_beta)


def _swa_kernel(q_ref, kp_ref, kc_ref, vp_ref, vc_ref, bias_ref, sink_ref, x_ref, conv_ref, w_ref, b_ref,
                g_ref, beta_ref, o_ref):
    j = pl.program_id(1)
    mix = jnp.dot(conv_ref[0], w_ref[0:CONV_CH, :], preferred_element_type=F32)
    rows = BLOCK + SWA_TILE
    lane = lax.broadcasted_iota(jnp.int32, (rows, 2 * HEAD_DIM), 1)
    low = lane < HEAD_DIM

    def placements(prev_ref, cur_ref):
        t = jnp.concatenate([prev_ref[0], cur_ref[0]], axis=0).astype(F32)
        tr = pltpu.roll(t, HEAD_DIM, 1)
        zero = jnp.zeros_like(t)
        kv0_low = jnp.where(low, t, zero).astype(BF16)
        kv1_high = jnp.where(low, zero, t).astype(BF16)
        kv1_low = jnp.where(low, tr, zero).astype(BF16)
        kv0_high = jnp.where(low, zero, tr).astype(BF16)
        return (kv0_low, kv0_high, kv1_low, kv1_high)

    ks = placements(kp_ref, kc_ref)
    vs = placements(vp_ref, vc_ref)
    slab = 2 * HEAD_DIM
    att = []
    for i in range(SWA_TILE // BLOCK):
        q = q_ref[0, i * BLOCK:(i + 1) * BLOCK, :]
        q_kv0 = jnp.concatenate([q[:, 0:slab], q[:, slab:2 * slab]], axis=0)
        q_kv1 = jnp.concatenate([q[:, 2 * slab:3 * slab], q[:, 3 * slab:4 * slab]], axis=0)
        band = slice(i * BLOCK, (i + 2) * BLOCK)
        outs = []
        for s in range(4):
            qs = q_kv0 if s < 2 else q_kv1
            bias = bias_ref[1, s]
            if i == 0:
                bias = jnp.where(j == 0, bias_ref[0, s], bias)
            logits = lax.dot_general(qs, ks[s][band], (((1,), (1,)), ((), ())),
                                     preferred_element_type=F32) + bias
            sink = sink_ref[s]
            m = jnp.maximum(jnp.max(logits, axis=-1, keepdims=True), sink)
            p = jnp.exp(logits - m)
            den = jnp.sum(p, axis=-1, keepdims=True) + jnp.exp(sink - m)
            o = jnp.dot(p.astype(BF16), vs[s][band], preferred_element_type=F32)
            outs.append(o / den)
        o_kv0 = outs[0] + outs[1]
        o_kv1 = outs[2] + outs[3]
        att.append(jnp.concatenate([o_kv0[0:BLOCK], o_kv0[BLOCK:2 * BLOCK], o_kv1[0:BLOCK],
                                    o_kv1[BLOCK:2 * BLOCK]], axis=1).astype(BF16))

    mix = mix + jnp.dot(jnp.concatenate(att, axis=0), w_ref[CONV_CH:, :], preferred_element_type=F32)
    h = ALPHA * x_ref[0] + mix + b_ref[...]
    o_ref[0] = _layer_norm(h, g_ref[...], beta_ref[...])


def _swa(q, k, v, bias_tab, sink_tab, x, conv_out, w_out, b_out, g, beta):
    bsz, seq, d = x.shape
    per = SWA_TILE // BLOCK
    cur = lambda b, n: (b, n, 0)
    prev = lambda b, n: (b, jnp.maximum(n * per - 1, 0), 0)
    whole = lambda b, n: (0, 0, 0, 0)
    const = lambda b, n: (0, 0)
    return pl.pallas_call(
        _swa_kernel,
        grid=(bsz, seq // SWA_TILE),
        in_specs=[
            pl.BlockSpec((1, SWA_TILE, Q_COLS), cur),
            pl.BlockSpec((1, BLOCK, KV_COLS), prev),
            pl.BlockSpec((1, SWA_TILE, KV_COLS), cur),
            pl.BlockSpec((1, BLOCK, KV_COLS), prev),
            pl.BlockSpec((1, SWA_TILE, KV_COLS), cur),
            pl.BlockSpec((2, 4, 2 * BLOCK, 2 * BLOCK), whole),
            pl.BlockSpec((4, 2 * BLOCK, 1), lambda b, n: (0, 0, 0)),
            pl.BlockSpec((1, SWA_TILE, d), cur),
            pl.BlockSpec((1, SWA_TILE, CONV_CH), cur),
            pl.BlockSpec((d, d), const),
            pl.BlockSpec((1, d), const),
            pl.BlockSpec((1, d), const),
            pl.BlockSpec((1, d), const),
        ],
        out_specs=pl.BlockSpec((1, SWA_TILE, d), cur),
        out_shape=jax.ShapeDtypeStruct((bsz, seq, d), F32),
        compiler_params=_cparams(("arbitrary", "arbitrary")),
        name="swa",
    )(q, k, k, v, v, bias_tab, sink_tab, x, conv_out, w_out, b_out, g, beta)


def _t5_bucket(dist):
    n = jnp.maximum(dist, 0)
    exact = REL_BUCKETS // 2
    large = exact + (jnp.log(jnp.maximum(n, 1).astype(F32) / exact)
                     / math.log(REL_MAX_DIST / exact) * (REL_BUCKETS - exact)).astype(jnp.int32)
    large = jnp.minimum(large, REL_BUCKETS - 1)
    return jnp.where(n < exact, n, large)


def _band_tables(rel_bias, sinks):
    qi = jnp.arange(BLOCK)[:, None]
    kj = jnp.arange(2 * BLOCK)[None, :]
    dist = qi + BLOCK - kj
    bucket = _t5_bucket(dist)
    bias = jnp.zeros((ATT_HEADS, BLOCK, 2 * BLOCK), F32)
    for bkt in range(REL_BUCKETS):
        bias = jnp.where(bucket[None] == bkt, rel_bias[bkt].astype(F32)[:, None, None], bias)
    in_window = (dist >= 0) & (dist < WINDOW)
    masks = jnp.stack([in_window & (kj >= BLOCK), in_window])
    masked = jnp.where(masks[:, None], bias[None], NEG_INF)
    pairs = ((0, 2), (1, 3), (4, 6), (5, 7))
    bias_tab = jnp.stack([jnp.concatenate([masked[:, a], masked[:, b]], axis=1) for a, b in pairs], axis=1)
    s = sinks.astype(F32)
    sink_tab = jnp.stack([jnp.concatenate([jnp.full((BLOCK, 1), s[a]), jnp.full((BLOCK, 1), s[b])], axis=0)
                          for a, b in pairs])
    return bias_tab, sink_tab


def _cross_kernel(x_ref, mem_ref, wkv_ref, wq_ref, wo_ref, g_ref, beta_ref, rw_ref, rb_ref, tri_ref,
                  o_ref, pos_ref, gate_ref, cnt_ref, k_ref, v_ref):
    @pl.when(pl.program_id(1) == 0)
    def _():
        kv = jnp.dot(mem_ref[0].astype(BF16), wkv_ref[...], preferred_element_type=F32)
        k_ref[...] = kv[:, :D_MODEL].astype(BF16)
        v_ref[...] = kv[:, D_MODEL:].astype(BF16)

    sub = CROSS_TILE // CROSS_SPLIT
    groups = [slice(s * sub, (s + 1) * sub) for s in range(CROSS_SPLIT)]
    xs = [x_ref[0, g, :] for g in groups]
    qs = [(jnp.dot(x.astype(BF16), wq_ref[...], preferred_element_type=F32) * (X_HEAD_DIM ** -0.5)).astype(BF16)
          for x in xs]
    heads = [[] for _ in groups]
    for h in range(X_HEADS):
        cols = slice(h * X_HEAD_DIM, (h + 1) * X_HEAD_DIM)
        logits = [lax.dot_general(q[:, cols], k_ref[:, cols], (((1,), (1,)), ((), ())),
                                  preferred_element_type=F32) for q in qs]
        ps, dens = [], []
        for lg in logits:
            p = jnp.exp(lg - jnp.max(lg, axis=-1, keepdims=True))
            ps.append(p)
            dens.append(jnp.sum(p, axis=-1, keepdims=True))
        for s, (p, den) in enumerate(zip(ps, dens)):
            o = jnp.dot(p.astype(BF16), v_ref[:, cols], preferred_element_type=F32)
            heads[s].append((o / den).astype(BF16))
    crosses = [jnp.dot(jnp.concatenate(hs, axis=-1), wo_ref[...], preferred_element_type=F32) for hs in heads]
    for s, (g, x, cross) in enumerate(zip(groups, xs, crosses)):
        y = _layer_norm(ALPHA * x + cross, g_ref[...], beta_ref[...])
        o_ref[0, g, :] = y
        logits_t = lax.dot_general(rw_ref[...], y.astype(BF16), (((1,), (1,)), ((), ())),
                                   preferred_element_type=F32)
        positions, gates, run_len = _route_tile(logits_t, rb_ref[...], tri_ref[...])
        cnt_ref[s] = run_len
        for r in range(TOP_K):
            pos_ref[s, r:r + 1, :] = positions[r]
            gate_ref[s, r:r + 1, :] = gates[r]


def _cross(x1, mem_in, wkv, wq, wo, g, beta, rw_t, router_b, tri):
    bsz, seq, d = x1.shape
    mem_len = mem_in.shape[1]
    nt = seq // CROSS_TILE
    assert CROSS_TILE // CROSS_SPLIT == ROUTE_TILE
    n_tiles = bsz * seq // ROUTE_TILE
    row = lambda b, j: (b, j, 0)
    mem = lambda b, j: (b, 0, 0)
    const = lambda b, j: (0, 0)
    tiles = lambda b, j: (b * nt + j, 0, 0)
    return pl.pallas_call(
        _cross_kernel,
        grid=(bsz, nt),
        in_specs=[
            pl.BlockSpec((1, CROSS_TILE, d), row),
            pl.BlockSpec((1, mem_len, d), mem),
            pl.BlockSpec((d, 2 * d), const),
            pl.BlockSpec((d, d), const),
            pl.BlockSpec((d, d), const),
            pl.BlockSpec((1, d), const),
            pl.BlockSpec((1, d), const),
            pl.BlockSpec((N_EXPERTS, d), const),
            pl.BlockSpec((N_EXPERTS, 1), const),
            pl.BlockSpec((ROUTE_TILE, ROUTE_TILE), const),
        ],
        out_specs=[
            pl.BlockSpec((1, CROSS_TILE, d), row),
            pl.BlockSpec((CROSS_SPLIT, TOP_K, ROUTE_TILE), tiles),
            pl.BlockSpec((CROSS_SPLIT, TOP_K, ROUTE_TILE), tiles),
            pl.BlockSpec((CROSS_SPLIT, N_EXPERTS, 128), tiles),
        ],
        out_shape=[
            jax.ShapeDtypeStruct((bsz, seq, d), F32),
            jax.ShapeDtypeStruct((n_tiles, TOP_K, ROUTE_TILE), jnp.int32),
            jax.ShapeDtypeStruct((n_tiles, TOP_K, ROUTE_TILE), F32),
            jax.ShapeDtypeStruct((n_tiles, N_EXPERTS, 128), jnp.int32),
        ],
        scratch_shapes=[pltpu.VMEM((mem_len, d), BF16), pltpu.VMEM((mem_len, d), BF16)],
        compiler_params=_cparams(("arbitrary", "arbitrary")),
        name="cross",
    )(x1, mem_in, wkv, wq, wo, g, beta, rw_t, router_b, tri)


def _route_tile(logits_t, router_b, tri):
    tn = ROUTE_TILE
    scores = _sigmoid(logits_t)
    choice = scores + router_b

    gscore = []
    member = lax.broadcasted_iota(jnp.int32, (GROUP_SIZE, tn), 0).astype(F32)
    for g in range(N_GROUPS):
        c = choice[g * GROUP_SIZE:(g + 1) * GROUP_SIZE, :]
        m1 = jnp.max(c, axis=0, keepdims=True)
        first = jnp.min(jnp.where(c == m1, member, float(GROUP_SIZE)), axis=0, keepdims=True)
        m2 = jnp.max(jnp.where(member == first, -jnp.inf, c), axis=0, keepdims=True)
        gscore.append(m1 + m2)

    keep_rows = []
    for g in range(N_GROUPS):
        beaten = jnp.zeros((1, tn), F32)
        for o in range(N_GROUPS):
            if o == g:
                continue
            ahead = (gscore[o] >= gscore[g]) if o < g else (gscore[o] > gscore[g])
            beaten = beaten + jnp.where(ahead, 1.0, 0.0)
        keep_rows.append(jnp.broadcast_to(beaten, (GROUP_SIZE, tn)))
    beaten_all = jnp.concatenate(keep_rows, axis=0)

    masked = jnp.where(beaten_all < TOPK_GROUPS, choice, -jnp.inf)
    eidx = lax.broadcasted_iota(jnp.int32, (N_EXPERTS, tn), 0).astype(F32)
    sel = jnp.zeros((N_EXPERTS, tn), F32)
    picks, weights = [], []
    for r in range(TOP_K):
        mx = jnp.max(masked, axis=0, keepdims=True)
        first = jnp.min(jnp.where(masked == mx, eidx, float(N_EXPERTS)), axis=0, keepdims=True)
        pick = eidx == first
        picks.append((pick, first))
        weights.append(jnp.sum(jnp.where(pick, scores, 0.0), axis=0, keepdims=True))
        masked = jnp.where(pick, -jnp.inf, masked)
        sel = jnp.where(pick, 1.0, sel)

    wsum = weights[0]
    for r in range(1, TOP_K):
        wsum = wsum + weights[r]

    count = jnp.sum(sel, axis=1, keepdims=True)
    run_len = jnp.floor((count + (RUN_ALIGN - 1.0)) * (1.0 / RUN_ALIGN)) * RUN_ALIGN
    run_len_b = jnp.broadcast_to(run_len, (N_EXPERTS, 128))
    er = lax.broadcasted_iota(jnp.int32, (N_EXPERTS, N_EXPERTS), 0)
    ec = lax.broadcasted_iota(jnp.int32, (N_EXPERTS, N_EXPERTS), 1)
    before = jnp.where(ec < er, 1.0, 0.0).astype(BF16)
    run_start = jnp.dot(before, run_len_b.astype(BF16), preferred_element_type=F32)[:, 0:1]
    incl = jnp.dot(sel.astype(BF16), tri, preferred_element_type=F32)
    pos_mat = run_start + incl - sel
    positions = [jnp.sum(jnp.where(pick, pos_mat, 0.0), axis=0, keepdims=True).astype(jnp.int32)
                 for pick, _ in picks]
    gates = [w / wsum * ROUTED_SCALE for w in weights]
    return positions, gates, run_len_b.astype(jnp.int32)


def _for_each_piece(tables, tile, fn):
    lists = [(BIG_PIECE, BIG_PER_TILE)] + [(size, N_EXPERTS) for size in SMALL_PIECES]
    for k, (n_rows, per_tile) in enumerate(lists):
        local_ref, global_ref, count_ref = tables[3 * k:3 * k + 3]
        count = count_ref[tile]

        def body(g, carry, n_rows=n_rows, per_tile=per_tile, local_ref=local_ref, global_ref=global_ref,
                 count=count):
            for u in range(PIECE_UNROLL):
                j = g * PIECE_UNROLL + u

                @pl.when(j < count)
                def _():
                    fn(pl.multiple_of(local_ref[tile * per_tile + j], RUN_ALIGN),
                       pl.multiple_of(global_ref[tile * per_tile + j], RUN_ALIGN), n_rows)
            return carry

        lax.fori_loop(0, (count + PIECE_UNROLL - 1) // PIECE_UNROLL, body, 0)


def _word_rows(start, size):
    if isinstance(start, int):
        first = start // WORD_ROWS
    else:
        first = pl.multiple_of(start >> (WORD_ROWS.bit_length() - 1), RUN_ALIGN // WORD_ROWS)
    return pl.ds(first, size // WORD_ROWS)


def _for_each_total_piece(total, fn):
    for piece in TOTAL_PIECES:
        @pl.when((total & piece) != 0)
        def _():
            fn(piece)


def _dispatch_kernel(*refs):
    tables = refs[:N_PIECE_TABLES]
    rows_ref, total_ref, gap_ref, pos_ref, x_ref, xs_ref, stage, zeros, sem, zsem = refs[N_PIECE_TABLES:]
    i = pl.program_id(0)
    nt = pl.num_programs(0)
    slot = i % 2
    tn = ROUTE_TILE

    def drain(tile, buf):
        def wait_rows(n):
            pltpu.make_async_copy(stage.at[buf, _word_rows(0, n), :], xs_ref.at[_word_rows(0, n), :],
                                  sem.at[buf]).wait()
        _for_each_total_piece(rows_ref[tile], wait_rows)

    @pl.when(i >= 2)
    def _():
        drain(i - 2, slot)

    xb = x_ref[...].astype(BF16)
    pos = pos_ref[0]
    row_iota = lax.broadcasted_iota(jnp.int32, (SORT_CHUNK, tn), 0).astype(F32).astype(BF16)
    one = jnp.ones((SORT_CHUNK, tn), BF16)
    zero = jnp.zeros((SORT_CHUNK, tn), BF16)

    def onehot_rows(c):
        rel = jnp.clip(pos - c * SORT_CHUNK, -1, SORT_CHUNK).astype(F32).astype(BF16)
        hit = rel[0:1, :] == row_iota
        for k in range(1, TOP_K):
            hit = hit | (rel[k:k + 1, :] == row_iota)
        return jnp.where(hit, one, zero)

    def sort_chunks(c2, carry):
        for c in (2 * c2, 2 * c2 + 1):
            p0 = pl.multiple_of(c * SORT_CHUNK, SORT_CHUNK)
            rows = jnp.dot(onehot_rows(c), xb, preferred_element_type=F32)
            stage[slot, _word_rows(p0, SORT_CHUNK), :] = pltpu.bitcast(rows.astype(BF16), jnp.int32)
        return carry

    lax.fori_loop(0, (rows_ref[i] + 2 * SORT_CHUNK - 1) // (2 * SORT_CHUNK), sort_chunks, 0)

    def start_piece(local, dst, n):
        pltpu.make_async_copy(stage.at[slot, _word_rows(local, n), :], xs_ref.at[_word_rows(dst, n), :],
                              sem.at[slot]).start()

    _for_each_piece(tables, i, start_piece)

    @pl.when(i == nt - 1)
    def _():
        @pl.when(nt >= 2)
        def _():
            drain(i - 1, 1 - slot)

        drain(i, slot)

        zeros[...] = jnp.zeros_like(zeros)

        def gap_copies(e, act):
            gap = gap_ref[N_EXPERTS + e]
            for piece in TOTAL_PIECES:
                if piece >= EXPERT_ROWS:
                    continue

                @pl.when((gap & piece) != 0)
                def _():
                    dst = gap_ref[e] + (gap & (-2 * piece))
                    act(pltpu.make_async_copy(zeros.at[_word_rows(0, piece), :],
                                              xs_ref.at[_word_rows(dst, piece), :], zsem))

        def zero_block(b):
            return pltpu.make_async_copy(zeros, xs_ref.at[_word_rows(b * EXPERT_ROWS, EXPERT_ROWS), :], zsem)

        first_free = total_ref[0] // EXPERT_ROWS
        n_blocks = xs_ref.shape[0] * WORD_ROWS // EXPERT_ROWS
        lax.fori_loop(0, N_EXPERTS, lambda e, c: (gap_copies(e, lambda cp: cp.start()), c)[1], 0)
        lax.fori_loop(first_free, n_blocks, lambda b, c: (zero_block(b).start(), c)[1], 0)
        lax.fori_loop(0, N_EXPERTS, lambda e, c: (gap_copies(e, lambda cp: cp.wait()), c)[1], 0)
        lax.fori_loop(first_free, n_blocks, lambda b, c: (zero_block(b).wait(), c)[1], 0)


def _dispatch(pieces, tile_rows, total, gaps, pos, x2d, n_rows):
    t, d = x2d.shape
    nt = t // ROUTE_TILE
    return pl.pallas_call(
        _dispatch_kernel,
        grid_spec=pltpu.PrefetchScalarGridSpec(
            num_scalar_prefetch=len(pieces) + 3,
            grid=(nt,),
            in_specs=[pl.BlockSpec((1, TOP_K, ROUTE_TILE), lambda i, *_: (i, 0, 0)),
                      pl.BlockSpec((ROUTE_TILE, d), lambda i, *_: (i, 0))],
            out_specs=pl.BlockSpec(memory_space=pl.ANY),
            scratch_shapes=[pltpu.VMEM((2, TILE_CAP // WORD_ROWS, d), jnp.int32),
                            pltpu.VMEM((EXPERT_ROWS // WORD_ROWS, d), jnp.int32),
                            pltpu.SemaphoreType.DMA((2,)),
                            pltpu.SemaphoreType.DMA(())],
        ),
        out_shape=jax.ShapeDtypeStruct((n_rows // WORD_ROWS, d), jnp.int32),
        compiler_params=_cparams(("arbitrary",)),
        name="dispatch",
    )(*pieces, tile_rows, total, gaps, pos, x2d)


def _experts_kernel(ie_ref, flag_ref, xs_ref, wg_ref, wu_ref, wd_ref, y_ref, wg_bf, wu_bf, wd_bf):
    w = pl.program_id(0)
    flags = flag_ref[w]
    valid = (flags & 1) != 0
    new_expert = (flags & 2) != 0

    @pl.when(jnp.logical_not(valid))
    def _():
        y_ref[...] = jnp.zeros_like(y_ref)

    @pl.when(new_expert)
    def _():
        wg_bf[...] = wg_ref[0].astype(BF16)
        wu_bf[...] = wu_ref[0].astype(BF16)
        wd_bf[...] = wd_ref[0].astype(BF16)

    @pl.when(valid)
    def _():
        sub = EXPERT_ROWS // WORD_ROWS // EXPERT_SPLIT
        groups = [slice(s * sub, (s + 1) * sub) for s in range(EXPERT_SPLIT)]
        xs = [pltpu.bitcast(xs_ref[g, :], BF16) for g in groups]
        gu = [(jnp.dot(x, wg_bf[...], preferred_element_type=F32),
               jnp.dot(x, wu_bf[...], preferred_element_type=F32)) for x in xs]
        hs = [(gte * _sigmoid(gte) * up).astype(BF16) for gte, up in gu]
        for g, h in zip(groups, hs):
            y = jnp.dot(h, wd_bf[...], preferred_element_type=F32)
            y_ref[g, :] = pltpu.bitcast(y.astype(BF16), jnp.int32)


def _experts(block_expert, block_flags, xs, wg, wu, wd):
    n, d = xs.shape
    ff = wg.shape[-1]
    return pl.pallas_call(
        _experts_kernel,
        grid_spec=pltpu.PrefetchScalarGridSpec(
            num_scalar_prefetch=2,
            grid=(block_expert.shape[0],),
            in_specs=[
                pl.BlockSpec((EXPERT_ROWS // WORD_ROWS, d), lambda w, ie, fl: (jnp.minimum(w, fl[fl.shape[0] - 1]), 0)),
                pl.BlockSpec((1, d, ff), lambda w, ie, fl: (ie[w], 0, 0)),
                pl.BlockSpec((1, d, ff), lambda w, ie, fl: (ie[w], 0, 0)),
                pl.BlockSpec((1, ff, d), lambda w, ie, fl: (ie[w], 0, 0)),
            ],
            out_specs=pl.BlockSpec((EXPERT_ROWS // WORD_ROWS, d), lambda w, ie, fl: (w, 0)),
            scratch_shapes=[pltpu.VMEM((d, ff), BF16), pltpu.VMEM((d, ff), BF16), pltpu.VMEM((ff, d), BF16)],
        ),
        out_shape=jax.ShapeDtypeStruct((n, d), jnp.int32),
        compiler_params=_cparams(("arbitrary",)),
        name="experts",
    )(block_expert, block_flags, xs, wg, wu, wd)


def _block_table(seg_rows, n_rows):
    block_end = jnp.cumsum(seg_rows // EXPERT_ROWS)
    w = jnp.arange(n_rows // EXPERT_ROWS, dtype=jnp.int32)
    valid = w < block_end[-1]
    e = jnp.sum((block_end[None, :] <= jnp.minimum(w, block_end[-1] - 1)[:, None]).astype(jnp.int32), axis=1)
    e = jnp.minimum(e, N_EXPERTS - 1)
    prev_e = jnp.concatenate([jnp.full((1,), -1, jnp.int32), e[:-1]])
    flags = valid.astype(jnp.int32) + 2 * (valid & (e != prev_e)).astype(jnp.int32)
    return e, jnp.concatenate([flags, block_end[-1:].astype(jnp.int32) - 1])


def _combine_kernel(*refs):
    tables = refs[:N_PIECE_TABLES]
    (rows_ref, pos_ref, gate_ref, x_ref, sg_ref, su_ref, sd_ref, g_ref, beta_ref, yb_ref, o_ref,
     stage, acc, lane_tile, lane_gate, sem) = refs[N_PIECE_TABLES:]
    i = pl.program_id(0)
    nt = pl.num_programs(0)
    slot = i % 2
    tn = ROUTE_TILE

    def start_tile(tile, buf):
        def start_piece(local, src, n):
            pltpu.make_async_copy(yb_ref.at[_word_rows(src, n), :], stage.at[buf, _word_rows(local, n), :],
                                  sem.at[buf]).start()
        _for_each_piece(tables, tile, start_piece)

    @pl.when(i == 0)
    def _():
        stage[...] = jnp.zeros_like(stage)
        start_tile(0, 0)

    @pl.when(i + 1 < nt)
    def _():
        start_tile(i + 1, 1 - slot)

    x = x_ref[...]
    xb = x.astype(BF16)
    gte = jnp.dot(xb, sg_ref[...], preferred_element_type=F32)
    up = jnp.dot(xb, su_ref[...], preferred_element_type=F32)
    h = (gte * _sigmoid(gte) * up).astype(BF16)
    acc[...] = jnp.dot(h, sd_ref[...], preferred_element_type=F32)

    def wait_rows(n):
        pltpu.make_async_copy(yb_ref.at[_word_rows(0, n), :], stage.at[slot, _word_rows(0, n), :],
                              sem.at[slot]).wait()

    _for_each_total_piece(rows_ref[i], wait_rows)

    lanes = 128
    lane_iota = lax.broadcasted_iota(jnp.int32, (tn, lanes), 1)
    for k in range(TOP_K):
        p = jnp.broadcast_to(pos_ref[:, k:k + 1], (tn, lanes))
        g = jnp.broadcast_to(gate_ref[:, k:k + 1], (tn, lanes))
        lane_tile[k] = (p >> (lanes.bit_length() - 1)).astype(F32).astype(BF16)
        lane_gate[k] = jnp.where((p & (lanes - 1)) == lane_iota, g, 0.0).astype(BF16)
    zero = jnp.zeros((tn, lanes), BF16)

    def gate_matrix(c):
        cols = []
        for j in range(COMBINE_CHUNK // lanes):
            tile = jnp.asarray(c * (COMBINE_CHUNK // lanes) + j, jnp.int32).astype(F32).astype(BF16)
            w = zero
            for k in range(TOP_K):
                w = w + jnp.where(lane_tile[k] == tile, lane_gate[k], zero)
            cols.append(w)
        return jnp.concatenate(cols, axis=1)

    def weighed(c):
        p0 = pl.multiple_of(c * COMBINE_CHUNK, COMBINE_CHUNK)
        rows = pltpu.bitcast(stage[slot, _word_rows(p0, COMBINE_CHUNK), :], BF16)
        return jnp.dot(gate_matrix(c), rows, preferred_element_type=F32)

    def weigh_chunks(c2, carry):
        acc[...] += weighed(2 * c2) + weighed(2 * c2 + 1)
        return carry

    n_chunks = (rows_ref[i] + COMBINE_CHUNK - 1) // COMBINE_CHUNK
    lax.fori_loop(0, n_chunks // 2, weigh_chunks, 0)

    @pl.when(n_chunks % 2 == 1)
    def _():
        acc[...] += weighed(n_chunks - 1)
    o_ref[...] = _layer_norm(ALPHA * x + acc[...], g_ref[...], beta_ref[...])


def _combine(pieces, tile_rows, pos_t, gate_t, x2d, sg, su, sd, g, beta, yb):
    t, d = x2d.shape
    nt = t // ROUTE_TILE
    ff = sg.shape[-1]
    row = lambda i, *_: (i, 0)
    const = lambda i, *_: (0, 0)
    return pl.pallas_call(
        _combine_kernel,
        grid_spec=pltpu.PrefetchScalarGridSpec(
            num_scalar_prefetch=len(pieces) + 1,
            grid=(nt,),
            in_specs=[
                pl.BlockSpec((ROUTE_TILE, TOP_K), row),
                pl.BlockSpec((ROUTE_TILE, TOP_K), row),
                pl.BlockSpec((ROUTE_TILE, d), row),
                pl.BlockSpec((d, ff), const),
                pl.BlockSpec((d, ff), const),
                pl.BlockSpec((ff, d), const),
                pl.BlockSpec((1, d), const),
                pl.BlockSpec((1, d), const),
                pl.BlockSpec(memory_space=pl.ANY),
            ],
            out_specs=pl.BlockSpec((ROUTE_TILE, d), row),
            scratch_shapes=[pltpu.VMEM((2, TILE_CAP // WORD_ROWS, d), jnp.int32),
                            pltpu.VMEM((ROUTE_TILE, d), F32),
                            pltpu.VMEM((TOP_K, ROUTE_TILE, 128), BF16),
                            pltpu.VMEM((TOP_K, ROUTE_TILE, 128), BF16),
                            pltpu.SemaphoreType.DMA((2,))],
        ),
        out_shape=jax.ShapeDtypeStruct((t, d), F32),
        compiler_params=_cparams(("arbitrary",)),
        name="combine",
    )(*pieces, tile_rows, pos_t, gate_t, x2d, sg, su, sd, g, beta, yb)


def _piece_tables(cnt):
    lbase = jnp.cumsum(cnt, axis=1) - cnt
    per_expert = jnp.sum(cnt, axis=0)
    seg_rows = -(-per_expert // EXPERT_ROWS) * EXPERT_ROWS
    seg_start = jnp.cumsum(seg_rows) - seg_rows
    gbase = seg_start[None, :] + jnp.cumsum(cnt, axis=0) - cnt
    gaps = jnp.concatenate([seg_start + per_expert, seg_rows - per_expert]).astype(jnp.int32)

    def listed(n_pieces, start_off, piece_rows, length):
        last = jnp.cumsum(n_pieces, axis=1)
        first = last - n_pieces
        j = jnp.arange(length, dtype=jnp.int32)[None, :, None]
        own = (j >= first[:, None, :]) & (j < last[:, None, :])
        off = (start_off[:, None, :] + (j - first[:, None, :]) * piece_rows)
        pick = lambda base: jnp.sum(jnp.where(own, base[:, None, :] + off, 0), axis=-1).reshape(-1).astype(jnp.int32)
        return pick(lbase), pick(gbase), last[:, -1].astype(jnp.int32)

    tables = listed(cnt // BIG_PIECE, jnp.zeros_like(cnt), BIG_PIECE, BIG_PER_TILE)
    for size in SMALL_PIECES:
        tables += listed((cnt // size) % 2, cnt // (2 * size) * (2 * size), size, N_EXPERTS)
    return tables, jnp.sum(cnt, axis=1).astype(jnp.int32), seg_rows, gaps


def kernel(x, mem, w_in, b_in, conv_w, conv_b, conv_ln_g, conv_ln_b, attn_sinks, rel_bias, w_out, b_out, ln1_g, ln1_b, xq_w, xkv_w, xo_w, ln2_g, ln2_b, router_w, router_b, exp_gate, exp_up, exp_down, sh_gate, sh_up, sh_down, ln3_g, ln3_b):
    bsz, seq, d = x.shape
    t = bsz * seq
    bias_tab, sink_tab = _band_tables(rel_bias, attn_sinks[0])
    tri = (jnp.arange(ROUTE_TILE)[:, None] <= jnp.arange(ROUTE_TILE)[None, :]).astype(BF16)
    row = lambda p: p.reshape(1, -1)
    for l in range(DEPTH):
        conv_out, q, k, v = _mix(x, w_in[l].astype(BF16), row(b_in[l]), conv_w[l], row(conv_b[l]),
                                 row(conv_ln_g[l]), row(conv_ln_b[l]))
        x1 = _swa(q, k, v, bias_tab, sink_tab, x, conv_out, w_out[l].astype(BF16), row(b_out[l]),
                  row(ln1_g[l]), row(ln1_b[l]))
        x2, pos, gate, cnt = _cross(x1, mem, xkv_w[l].astype(BF16), xq_w[l].astype(BF16), xo_w[l].astype(BF16),
                                    row(ln2_g[l]), row(ln2_b[l]), router_w[l].T.astype(BF16),
                                    router_b[l].reshape(-1, 1), tri)
        x2 = x2.reshape(t, d)
        nt = t // ROUTE_TILE
        n_rows = (-(-(t * TOP_K + nt * N_EXPERTS * (RUN_ALIGN - 1)) // EXPERT_ROWS) + N_EXPERTS) * EXPERT_ROWS
        pieces, tile_rows, seg_rows, gaps = _piece_tables(cnt[:, :, 0])
        block_expert, block_flags = _block_table(seg_rows, n_rows)
        xs = _dispatch(pieces, tile_rows, jnp.sum(seg_rows).reshape(1).astype(jnp.int32), gaps, pos, x2, n_rows)
        yb = _experts(block_expert, block_flags, xs, exp_gate[l], exp_up[l], exp_down[l])
        pos_t = jnp.transpose(pos, (0, 2, 1)).reshape(t, TOP_K)
        gate_t = jnp.transpose(gate, (0, 2, 1)).reshape(t, TOP_K)
        x = _combine(pieces, tile_rows, pos_t, gate_t, x2, sh_gate[l].astype(BF16),
                     sh_up[l].astype(BF16), sh_down[l].astype(BF16), row(ln3_g[l]), row(ln3_b[l]),
                     yb).reshape(bsz, seq, d)
    return x
```

```python
import math

import jax
import jax.numpy as jnp
from jax import lax
from jax.experimental import pallas as pl
from jax.experimental.pallas import tpu as pltpu

D_MODEL = 1024
HEAD_DIM = 64
CONV_CH = D_MODEL // 2
CONV_WIDTH = 31
ATT_HEADS = 8
KV_HEADS = 2
WINDOW = 128
BLOCK = 128
REL_BUCKETS = 32
REL_MAX_DIST = 128
Q_COLS = ATT_HEADS * HEAD_DIM
KV_COLS = KV_HEADS * HEAD_DIM
IN_COLS = 2 * CONV_CH + Q_COLS + 2 * KV_COLS
X_HEADS = 4
X_HEAD_DIM = D_MODEL // X_HEADS
N_EXPERTS = 64
TOP_K = 8
N_GROUPS = 8
GROUP_SIZE = N_EXPERTS // N_GROUPS
TOPK_GROUPS = 4
EXPERT_FF = D_MODEL // 4
ROUTED_SCALE = 2.5
DEPTH = 1
ALPHA = (2 * DEPTH) ** 0.25
LN_EPS = 1e-5
NEG_INF = -1e30

F32 = jnp.float32
BF16 = jnp.bfloat16

VMEM_LIMIT_BYTES = 56 * 1024 * 1024

ROW_TILE = 512
CONV_ROWS = 32
SUBLANES = 8
LANES = 128
CONV_HALO = 32
CROSS_TILE = 1024
CROSS_SPLIT = 2
SWA_TILE = 512
ROUTE_TILE = 512
EXPERT_ROWS = 1024
EXPERT_SPLIT = 2
RUN_ALIGN = 16
RUN_ALIGN_LOG2 = RUN_ALIGN.bit_length() - 1
WORD_ROWS = 2
TILE_CAP = ROUTE_TILE * TOP_K + N_EXPERTS * RUN_ALIGN
BIG_PIECE = 32
BIG_PER_TILE = TILE_CAP // BIG_PIECE
SMALL_PIECES = tuple(1 << b for b in range(BIG_PIECE.bit_length() - 2, RUN_ALIGN_LOG2 - 1, -1))
N_PIECE_TABLES = 3 * (1 + len(SMALL_PIECES))
PIECE_UNROLL = 4
TOTAL_PIECES = tuple(1 << b for b in range(TILE_CAP.bit_length() - 1, RUN_ALIGN_LOG2 - 1, -1))
SORT_CHUNK = 256
COMBINE_CHUNK = 512


def _cparams(sem):
    return pltpu.CompilerParams(dimension_semantics=sem, vmem_limit_bytes=VMEM_LIMIT_BYTES)


def _layer_norm(h, g, b):
    mu = jnp.mean(h, axis=-1, keepdims=True)
    d = h - mu
    var = jnp.mean(d * d, axis=-1, keepdims=True)
    return d * lax.rsqrt(var + LN_EPS) * g + b


def _sigmoid(x):
    return 1.0 / (1.0 + jnp.exp(-x))


def _mix_kernel(x_ref, w_ref, b_ref, cw_ref, cb_ref, cg_ref, cbeta_ref,
                conv_ref, q_ref, k_ref, v_ref, u_ext, u_sh):
    j = pl.program_id(1)
    xb = x_ref[0].astype(BF16)
    proj = jnp.dot(xb, w_ref[...], preferred_element_type=F32) + b_ref[...]
    a = proj[:, :CONV_CH]
    g = proj[:, CONV_CH:2 * CONV_CH]
    q_ref[0] = (proj[:, 2 * CONV_CH:2 * CONV_CH + Q_COLS] * (HEAD_DIM ** -0.5)).astype(BF16)
    k_ref[0] = proj[:, 2 * CONV_CH + Q_COLS:2 * CONV_CH + Q_COLS + KV_COLS].astype(BF16)
    v_ref[0] = proj[:, 2 * CONV_CH + Q_COLS + KV_COLS:].astype(BF16)

    @pl.when(j == 0)
    def _():
        u_ext[0:CONV_HALO, :] = jnp.zeros((CONV_HALO, CONV_CH), F32)

    u_ext[CONV_HALO:CONV_HALO + ROW_TILE, :] = a * _sigmoid(g)

    first_tap = CONV_HALO - (CONV_WIDTH - 1)
    shifted_rows = u_sh.shape[1]
    for r in range(1, SUBLANES):
        u_sh[r - 1] = u_ext[r:r + shifted_rows, :]

    for c in range(ROW_TILE // CONV_ROWS):
        acc = jnp.zeros((CONV_ROWS, CONV_CH), F32) + cb_ref[...]
        for t in range(CONV_WIDTH):
            r = (first_tap + t) % SUBLANES
            base = c * CONV_ROWS + (first_tap + t) - r
            if r == 0:
                taps = u_ext[base:base + CONV_ROWS, :]
            else:
                taps = u_sh[r - 1, base:base + CONV_ROWS, :]
            acc = acc + taps * cw_ref[t:t + 1, :]
        y = _layer_norm(acc, cg_ref[...], cbeta_ref[...])
        conv_ref[0, c * CONV_ROWS:(c + 1) * CONV_ROWS, :] = (y * _sigmoid(y)).astype(BF16)

    u_ext[0:CONV_HALO, :] = u_ext[ROW_TILE:ROW_TILE + CONV_HALO, :]


def _mix(x, w_in, b_in, conv_w, conv_b, conv_g, conv_beta):
    bsz, seq, d = x.shape
    nt = seq // ROW_TILE
    row = lambda b, j: (b, j, 0)
    const2 = lambda b, j: (0, 0)
    return pl.pallas_call(
        _mix_kernel,
        grid=(bsz, nt),
        in_specs=[
            pl.BlockSpec((1, ROW_TILE, d), row),
            pl.BlockSpec((d, IN_COLS), const2),
            pl.BlockSpec((1, IN_COLS), const2),
            pl.BlockSpec((CONV_WIDTH, CONV_CH), const2),
            pl.BlockSpec((1, CONV_CH), const2),
            pl.BlockSpec((1, CONV_CH), const2),
            pl.BlockSpec((1, CONV_CH), const2),
        ],
        out_specs=[
            pl.BlockSpec((1, ROW_TILE, CONV_CH), row),
            pl.BlockSpec((1, ROW_TILE, Q_COLS), row),
            pl.BlockSpec((1, ROW_TILE, KV_COLS), row),
            pl.BlockSpec((1, ROW_TILE, KV_COLS), row),
        ],
        out_shape=[
            jax.ShapeDtypeStruct((bsz, seq, CONV_CH), BF16),
            jax.ShapeDtypeStruct((bsz, seq, Q_COLS), BF16),
            jax.ShapeDtypeStruct((bsz, seq, KV_COLS), BF16),
            jax.ShapeDtypeStruct((bsz, seq, KV_COLS), BF16),
        ],
        scratch_shapes=[pltpu.VMEM((ROW_TILE + CONV_HALO, CONV_CH), F32),
                        pltpu.VMEM((SUBLANES - 1, ROW_TILE + CONV_HALO - SUBLANES, CONV_CH), F32)],
        compiler_params=_cparams(("arbitrary", "arbitrary")),
        name="mix",
    )(x, w_in, b_in, conv_w, conv_b, conv_g, conv_beta)


def _swa_kernel(q_ref, kp_ref, kc_ref, vp_ref, vc_ref, bias_ref, sink_ref, x_ref, conv_ref, w_ref, b_ref,
                g_ref, beta_ref, o_ref):
    j = pl.program_id(1)
    mix = jnp.dot(conv_ref[0], w_ref[0:CONV_CH, :], preferred_element_type=F32)
    rows = BLOCK + SWA_TILE
    lane = lax.broadcasted_iota(jnp.int32, (rows, 2 * HEAD_DIM), 1)
    low = lane < HEAD_DIM

    def placements(prev_ref, cur_ref):
        t = jnp.concatenate([prev_ref[0], cur_ref[0]], axis=0).astype(F32)
        tr = pltpu.roll(t, HEAD_DIM, 1)
        zero = jnp.zeros_like(t)
        kv0_low = jnp.where(low, t, zero).astype(BF16)
        kv1_high = jnp.where(low, zero, t).astype(BF16)
        kv1_low = jnp.where(low, tr, zero).astype(BF16)
        kv0_high = jnp.where(low, zero, tr).astype(BF16)
        return (kv0_low, kv0_high, kv1_low, kv1_high)

    ks = placements(kp_ref, kc_ref)
    vs = placements(vp_ref, vc_ref)
    slab = 2 * HEAD_DIM
    att = []
    for i in range(SWA_TILE // BLOCK):
        q = q_ref[0, i * BLOCK:(i + 1) * BLOCK, :]
        q_kv0 = jnp.concatenate([q[:, 0:slab], q[:, slab:2 * slab]], axis=0)
        q_kv1 = jnp.concatenate([q[:, 2 * slab:3 * slab], q[:, 3 * slab:4 * slab]], axis=0)
        band = slice(i * BLOCK, (i + 2) * BLOCK)
        outs = []
        for s in range(4):
            qs = q_kv0 if s < 2 else q_kv1
            bias = bias_ref[1, s]
            if i == 0:
                bias = jnp.where(j == 0, bias_ref[0, s], bias)
            logits = lax.dot_general(qs, ks[s][band], (((1,), (1,)), ((), ())),
                                     preferred_element_type=F32) + bias
            sink = sink_ref[s]
            m = jnp.maximum(jnp.max(logits, axis=-1, keepdims=True), sink)
            p = jnp.exp(logits - m)
            den = jnp.sum(p, axis=-1, keepdims=True) + jnp.exp(sink - m)
            o = jnp.dot(p.astype(BF16), vs[s][band], preferred_element_type=F32)
            outs.append(o / den)
        o_kv0 = outs[0] + outs[1]
        o_kv1 = outs[2] + outs[3]
        att.append(jnp.concatenate([o_kv0[0:BLOCK], o_kv0[BLOCK:2 * BLOCK], o_kv1[0:BLOCK],
                                    o_kv1[BLOCK:2 * BLOCK]], axis=1).astype(BF16))

    mix = mix + jnp.dot(jnp.concatenate(att, axis=0), w_ref[CONV_CH:, :], preferred_element_type=F32)
    h = ALPHA * x_ref[0] + mix + b_ref[...]
    o_ref[0] = _layer_norm(h, g_ref[...], beta_ref[...])


def _swa(q, k, v, bias_tab, sink_tab, x, conv_out, w_out, b_out, g, beta):
    bsz, seq, d = x.shape
    per = SWA_TILE // BLOCK
    cur = lambda b, n: (b, n, 0)
    prev = lambda b, n: (b, jnp.maximum(n * per - 1, 0), 0)
    whole = lambda b, n: (0, 0, 0, 0)
    const = lambda b, n: (0, 0)
    return pl.pallas_call(
        _swa_kernel,
        grid=(bsz, seq // SWA_TILE),
        in_specs=[
            pl.BlockSpec((1, SWA_TILE, Q_COLS), cur),
            pl.BlockSpec((1, BLOCK, KV_COLS), prev),
            pl.BlockSpec((1, SWA_TILE, KV_COLS), cur),
            pl.BlockSpec((1, BLOCK, KV_COLS), prev),
            pl.BlockSpec((1, SWA_TILE, KV_COLS), cur),
            pl.BlockSpec((2, 4, 2 * BLOCK, 2 * BLOCK), whole),
            pl.BlockSpec((4, 2 * BLOCK, 1), lambda b, n: (0, 0, 0)),
            pl.BlockSpec((1, SWA_TILE, d), cur),
            pl.BlockSpec((1, SWA_TILE, CONV_CH), cur),
            pl.BlockSpec((d, d), const),
            pl.BlockSpec((1, d), const),
            pl.BlockSpec((1, d), const),
            pl.BlockSpec((1, d), const),
        ],
        out_specs=pl.BlockSpec((1, SWA_TILE, d), cur),
        out_shape=jax.ShapeDtypeStruct((bsz, seq, d), F32),
        compiler_params=_cparams(("arbitrary", "arbitrary")),
        name="swa",
    )(q, k, k, v, v, bias_tab, sink_tab, x, conv_out, w_out, b_out, g, beta)


def _t5_bucket(dist):
    n = jnp.maximum(dist, 0)
    exact = REL_BUCKETS // 2
    large = exact + (jnp.log(jnp.maximum(n, 1).astype(F32) / exact)
                     / math.log(REL_MAX_DIST / exact) * (REL_BUCKETS - exact)).astype(jnp.int32)
    large = jnp.minimum(large, REL_BUCKETS - 1)
    return jnp.where(n < exact, n, large)


def _band_tables(rel_bias, sinks):
    qi = jnp.arange(BLOCK)[:, None]
    kj = jnp.arange(2 * BLOCK)[None, :]
    dist = qi + BLOCK - kj
    bucket = _t5_bucket(dist)
    bias = jnp.zeros((ATT_HEADS, BLOCK, 2 * BLOCK), F32)
    for bkt in range(REL_BUCKETS):
        bias = jnp.where(bucket[None] == bkt, rel_bias[bkt].astype(F32)[:, None, None], bias)
    in_window = (dist >= 0) & (dist < WINDOW)
    masks = jnp.stack([in_window & (kj >= BLOCK), in_window])
    masked = jnp.where(masks[:, None], bias[None], NEG_INF)
    pairs = ((0, 2), (1, 3), (4, 6), (5, 7))
    bias_tab = jnp.stack([jnp.concatenate([masked[:, a], masked[:, b]], axis=1) for a, b in pairs], axis=1)
    s = sinks.astype(F32)
    sink_tab = jnp.stack([jnp.concatenate([jnp.full((BLOCK, 1), s[a]), jnp.full((BLOCK, 1), s[b])], axis=0)
                          for a, b in pairs])
    return bias_tab, sink_tab


def _cross_kernel(x_ref, mem_ref, wkv_ref, wq_ref, wo_ref, g_ref, beta_ref, rw_ref, rb_ref, tri_ref,
                  o_ref, pos_ref, gate_ref, cnt_ref, k_ref, v_ref):
    @pl.when(pl.program_id(1) == 0)
    def _():
        kv = jnp.dot(mem_ref[0].astype(BF16), wkv_ref[...], preferred_element_type=F32)
        k_ref[...] = kv[:, :D_MODEL].astype(BF16)
        v_ref[...] = kv[:, D_MODEL:].astype(BF16)

    sub = CROSS_TILE // CROSS_SPLIT
    groups = [slice(s * sub, (s + 1) * sub) for s in range(CROSS_SPLIT)]
    xs = [x_ref[0, g, :] for g in groups]
    qs = [(jnp.dot(x.astype(BF16), wq_ref[...], preferred_element_type=F32) * (X_HEAD_DIM ** -0.5)).astype(BF16)
          for x in xs]
    heads = [[] for _ in groups]
    for h in range(X_HEADS):
        cols = slice(h * X_HEAD_DIM, (h + 1) * X_HEAD_DIM)
        logits = [lax.dot_general(q[:, cols], k_ref[:, cols], (((1,), (1,)), ((), ())),
                                  preferred_element_type=F32) for q in qs]
        ps, dens = [], []
        for lg in logits:
            p = jnp.exp(lg - jnp.max(lg, axis=-1, keepdims=True))
            ps.append(p)
            dens.append(jnp.sum(p, axis=-1, keepdims=True))
        for s, (p, den) in enumerate(zip(ps, dens)):
            o = jnp.dot(p.astype(BF16), v_ref[:, cols], preferred_element_type=F32)
            heads[s].append((o / den).astype(BF16))
    crosses = [jnp.dot(jnp.concatenate(hs, axis=-1), wo_ref[...], preferred_element_type=F32) for hs in heads]
    for s, (g, x, cross) in enumerate(zip(groups, xs, crosses)):
        y = _layer_norm(ALPHA * x + cross, g_ref[...], beta_ref[...])
        o_ref[0, g, :] = y
        logits_t = lax.dot_general(rw_ref[...], y.astype(BF16), (((1,), (1,)), ((), ())),
                                   preferred_element_type=F32)
        positions, gates, run_len = _route_tile(logits_t, rb_ref[...], tri_ref[...])
        cnt_ref[s] = run_len
        for r in range(TOP_K):
            pos_ref[s, r:r + 1, :] = positions[r]
            gate_ref[s, r:r + 1, :] = gates[r]


def _cross(x1, mem_in, wkv, wq, wo, g, beta, rw_t, router_b, tri):
    bsz, seq, d = x1.shape
    mem_len = mem_in.shape[1]
    nt = seq // CROSS_TILE
    assert CROSS_TILE // CROSS_SPLIT == ROUTE_TILE
    n_tiles = bsz * seq // ROUTE_TILE
    row = lambda b, j: (b, j, 0)
    mem = lambda b, j: (b, 0, 0)
    const = lambda b, j: (0, 0)
    tiles = lambda b, j: (b * nt + j, 0, 0)
    return pl.pallas_call(
        _cross_kernel,
        grid=(bsz, nt),
        in_specs=[
            pl.BlockSpec((1, CROSS_TILE, d), row),
            pl.BlockSpec((1, mem_len, d), mem),
            pl.BlockSpec((d, 2 * d), const),
            pl.BlockSpec((d, d), const),
            pl.BlockSpec((d, d), const),
            pl.BlockSpec((1, d), const),
            pl.BlockSpec((1, d), const),
            pl.BlockSpec((N_EXPERTS, d), const),
            pl.BlockSpec((N_EXPERTS, 1), const),
            pl.BlockSpec((ROUTE_TILE, ROUTE_TILE), const),
        ],
        out_specs=[
            pl.BlockSpec((1, CROSS_TILE, d), row),
            pl.BlockSpec((CROSS_SPLIT, TOP_K, ROUTE_TILE), tiles),
            pl.BlockSpec((CROSS_SPLIT, TOP_K, ROUTE_TILE), tiles),
            pl.BlockSpec((CROSS_SPLIT, N_EXPERTS, LANES), tiles),
        ],
        out_shape=[
            jax.ShapeDtypeStruct((bsz, seq, d), F32),
            jax.ShapeDtypeStruct((n_tiles, TOP_K, ROUTE_TILE), jnp.int32),
            jax.ShapeDtypeStruct((n_tiles, TOP_K, ROUTE_TILE), F32),
            jax.ShapeDtypeStruct((n_tiles, N_EXPERTS, LANES), jnp.int32),
        ],
        scratch_shapes=[pltpu.VMEM((mem_len, d), BF16), pltpu.VMEM((mem_len, d), BF16)],
        compiler_params=_cparams(("arbitrary", "arbitrary")),
        name="cross",
    )(x1, mem_in, wkv, wq, wo, g, beta, rw_t, router_b, tri)


def _route_tile(logits_t, router_b, tri):
    tn = ROUTE_TILE
    scores = _sigmoid(logits_t)
    choice = scores + router_b

    gscore = []
    member = lax.broadcasted_iota(jnp.int32, (GROUP_SIZE, tn), 0).astype(F32)
    for g in range(N_GROUPS):
        c = choice[g * GROUP_SIZE:(g + 1) * GROUP_SIZE, :]
        m1 = jnp.max(c, axis=0, keepdims=True)
        first = jnp.min(jnp.where(c == m1, member, float(GROUP_SIZE)), axis=0, keepdims=True)
        m2 = jnp.max(jnp.where(member == first, -jnp.inf, c), axis=0, keepdims=True)
        gscore.append(m1 + m2)

    keep_rows = []
    for g in range(N_GROUPS):
        beaten = jnp.zeros((1, tn), F32)
        for o in range(N_GROUPS):
            if o == g:
                continue
            ahead = (gscore[o] >= gscore[g]) if o < g else (gscore[o] > gscore[g])
            beaten = beaten + jnp.where(ahead, 1.0, 0.0)
        keep_rows.append(jnp.broadcast_to(beaten, (GROUP_SIZE, tn)))
    beaten_all = jnp.concatenate(keep_rows, axis=0)

    masked = jnp.where(beaten_all < TOPK_GROUPS, choice, -jnp.inf)
    eidx = lax.broadcasted_iota(jnp.int32, (N_EXPERTS, tn), 0).astype(F32)
    sel = jnp.zeros((N_EXPERTS, tn), F32)
    picks, weights = [], []
    for r in range(TOP_K):
        mx = jnp.max(masked, axis=0, keepdims=True)
        first = jnp.min(jnp.where(masked == mx, eidx, float(N_EXPERTS)), axis=0, keepdims=True)
        pick = eidx == first
        picks.append((pick, first))
        weights.append(jnp.sum(jnp.where(pick, scores, 0.0), axis=0, keepdims=True))
        masked = jnp.where(pick, -jnp.inf, masked)
        sel = jnp.where(pick, 1.0, sel)

    wsum = weights[0]
    for r in range(1, TOP_K):
        wsum = wsum + weights[r]

    count = jnp.sum(sel, axis=1, keepdims=True)
    run_len = jnp.floor((count + (RUN_ALIGN - 1.0)) * (1.0 / RUN_ALIGN)) * RUN_ALIGN
    run_len_b = jnp.broadcast_to(run_len, (N_EXPERTS, LANES))
    er = lax.broadcasted_iota(jnp.int32, (N_EXPERTS, N_EXPERTS), 0)
    ec = lax.broadcasted_iota(jnp.int32, (N_EXPERTS, N_EXPERTS), 1)
    before = jnp.where(ec < er, 1.0, 0.0).astype(BF16)
    run_start = jnp.dot(before, run_len_b.astype(BF16), preferred_element_type=F32)[:, 0:1]
    incl = jnp.dot(sel.astype(BF16), tri, preferred_element_type=F32)
    pos_mat = run_start + incl - sel
    positions = [jnp.sum(jnp.where(pick, pos_mat, 0.0), axis=0, keepdims=True).astype(jnp.int32)
                 for pick, _ in picks]
    gates = [w / wsum * ROUTED_SCALE for w in weights]
    return positions, gates, run_len_b.astype(jnp.int32)


def _for_each_piece(tables, tile, fn):
    lists = [(BIG_PIECE, BIG_PER_TILE)] + [(size, N_EXPERTS) for size in SMALL_PIECES]
    for k, (n_rows, per_tile) in enumerate(lists):
        local_ref, global_ref, count_ref = tables[3 * k:3 * k + 3]
        count = count_ref[tile]

        def body(g, carry, n_rows=n_rows, per_tile=per_tile, local_ref=local_ref, global_ref=global_ref,
                 count=count):
            for u in range(PIECE_UNROLL):
                j = g * PIECE_UNROLL + u

                @pl.when(j < count)
                def _():
                    fn(pl.multiple_of(local_ref[tile * per_tile + j], RUN_ALIGN),
                       pl.multiple_of(global_ref[tile * per_tile + j], RUN_ALIGN), n_rows)
            return carry

        lax.fori_loop(0, (count + PIECE_UNROLL - 1) // PIECE_UNROLL, body, 0)


def _word_rows(start, size):
    if isinstance(start, int):
        first = start // WORD_ROWS
    else:
        first = pl.multiple_of(start >> (WORD_ROWS.bit_length() - 1), RUN_ALIGN // WORD_ROWS)
    return pl.ds(first, size // WORD_ROWS)


def _for_each_total_piece(total, fn):
    for piece in TOTAL_PIECES:
        @pl.when((total & piece) != 0)
        def _():
            fn(piece)


def _dispatch_kernel(*refs):
    tables = refs[:N_PIECE_TABLES]
    rows_ref, total_ref, gap_ref, pos_ref, x_ref, xs_ref, stage, zeros, sem, zsem = refs[N_PIECE_TABLES:]
    i = pl.program_id(0)
    nt = pl.num_programs(0)
    slot = i % 2
    tn = ROUTE_TILE

    def drain(tile, buf):
        def wait_rows(n):
            pltpu.make_async_copy(stage.at[buf, _word_rows(0, n), :], xs_ref.at[_word_rows(0, n), :],
                                  sem.at[buf]).wait()
        _for_each_total_piece(rows_ref[tile], wait_rows)

    @pl.when(i >= 2)
    def _():
        drain(i - 2, slot)

    xb = x_ref[...].astype(BF16)
    pos = pos_ref[0]
    row_iota = lax.broadcasted_iota(jnp.int32, (SORT_CHUNK, tn), 0).astype(F32).astype(BF16)
    one = jnp.ones((SORT_CHUNK, tn), BF16)
    zero = jnp.zeros((SORT_CHUNK, tn), BF16)

    def onehot_rows(c):
        rel = jnp.clip(pos - c * SORT_CHUNK, -1, SORT_CHUNK).astype(F32).astype(BF16)
        hit = rel[0:1, :] == row_iota
        for k in range(1, TOP_K):
            hit = hit | (rel[k:k + 1, :] == row_iota)
        return jnp.where(hit, one, zero)

    def sort_chunks(c2, carry):
        for c in (2 * c2, 2 * c2 + 1):
            p0 = pl.multiple_of(c * SORT_CHUNK, SORT_CHUNK)
            rows = jnp.dot(onehot_rows(c), xb, preferred_element_type=F32)
            stage[slot, _word_rows(p0, SORT_CHUNK), :] = pltpu.bitcast(rows.astype(BF16), jnp.int32)
        return carry

    lax.fori_loop(0, (rows_ref[i] + 2 * SORT_CHUNK - 1) // (2 * SORT_CHUNK), sort_chunks, 0)

    def start_piece(local, dst, n):
        pltpu.make_async_copy(stage.at[slot, _word_rows(local, n), :], xs_ref.at[_word_rows(dst, n), :],
                              sem.at[slot]).start()

    _for_each_piece(tables, i, start_piece)

    @pl.when(i == nt - 1)
    def _():
        @pl.when(nt >= 2)
        def _():
            drain(i - 1, 1 - slot)

        drain(i, slot)

        zeros[...] = jnp.zeros_like(zeros)

        def gap_copies(e, act):
            gap = gap_ref[N_EXPERTS + e]
            for piece in TOTAL_PIECES:
                if piece >= EXPERT_ROWS:
                    continue

                @pl.when((gap & piece) != 0)
                def _():
                    dst = gap_ref[e] + (gap & (-2 * piece))
                    act(pltpu.make_async_copy(zeros.at[_word_rows(0, piece), :],
                                              xs_ref.at[_word_rows(dst, piece), :], zsem))

        def zero_block(b):
            return pltpu.make_async_copy(zeros, xs_ref.at[_word_rows(b * EXPERT_ROWS, EXPERT_ROWS), :], zsem)

        first_free = total_ref[0] // EXPERT_ROWS
        n_blocks = xs_ref.shape[0] * WORD_ROWS // EXPERT_ROWS
        lax.fori_loop(0, N_EXPERTS, lambda e, c: (gap_copies(e, lambda cp: cp.start()), c)[1], 0)
        lax.fori_loop(first_free, n_blocks, lambda b, c: (zero_block(b).start(), c)[1], 0)
        lax.fori_loop(0, N_EXPERTS, lambda e, c: (gap_copies(e, lambda cp: cp.wait()), c)[1], 0)
        lax.fori_loop(first_free, n_blocks, lambda b, c: (zero_block(b).wait(), c)[1], 0)


def _dispatch(pieces, tile_rows, total, gaps, pos, x2d, n_rows):
    t, d = x2d.shape
    nt = t // ROUTE_TILE
    return pl.pallas_call(
        _dispatch_kernel,
        grid_spec=pltpu.PrefetchScalarGridSpec(
            num_scalar_prefetch=len(pieces) + 3,
            grid=(nt,),
            in_specs=[pl.BlockSpec((1, TOP_K, ROUTE_TILE), lambda i, *_: (i, 0, 0)),
                      pl.BlockSpec((ROUTE_TILE, d), lambda i, *_: (i, 0))],
            out_specs=pl.BlockSpec(memory_space=pl.ANY),
            scratch_shapes=[pltpu.VMEM((2, TILE_CAP // WORD_ROWS, d), jnp.int32),
                            pltpu.VMEM((EXPERT_ROWS // WORD_ROWS, d), jnp.int32),
                            pltpu.SemaphoreType.DMA((2,)),
                            pltpu.SemaphoreType.DMA(())],
        ),
        out_shape=jax.ShapeDtypeStruct((n_rows // WORD_ROWS, d), jnp.int32),
        compiler_params=_cparams(("arbitrary",)),
        name="dispatch",
    )(*pieces, tile_rows, total, gaps, pos, x2d)


def _experts_kernel(ie_ref, flag_ref, xs_ref, wg_ref, wu_ref, wd_ref, y_ref, wg_bf, wu_bf, wd_bf):
    w = pl.program_id(0)
    flags = flag_ref[w]
    valid = (flags & 1) != 0
    new_expert = (flags & 2) != 0

    @pl.when(new_expert)
    def _():
        wg_bf[...] = wg_ref[0].astype(BF16)
        wu_bf[...] = wu_ref[0].astype(BF16)
        wd_bf[...] = wd_ref[0].astype(BF16)

    @pl.when(valid)
    def _():
        sub = EXPERT_ROWS // WORD_ROWS // EXPERT_SPLIT
        groups = [slice(s * sub, (s + 1) * sub) for s in range(EXPERT_SPLIT)]
        xs = [pltpu.bitcast(xs_ref[g, :], BF16) for g in groups]
        gu = [(jnp.dot(x, wg_bf[...], preferred_element_type=F32),
               jnp.dot(x, wu_bf[...], preferred_element_type=F32)) for x in xs]
        hs = [(gte * _sigmoid(gte) * up).astype(BF16) for gte, up in gu]
        for g, h in zip(groups, hs):
            y = jnp.dot(h, wd_bf[...], preferred_element_type=F32)
            y_ref[g, :] = pltpu.bitcast(y.astype(BF16), jnp.int32)


def _experts(block_expert, block_flags, xs, wg, wu, wd):
    n, d = xs.shape
    ff = wg.shape[-1]
    rows_block = lambda w, ie, fl: (jnp.minimum(w, fl[fl.shape[0] - 1]), 0)
    return pl.pallas_call(
        _experts_kernel,
        grid_spec=pltpu.PrefetchScalarGridSpec(
            num_scalar_prefetch=2,
            grid=(block_expert.shape[0],),
            in_specs=[
                pl.BlockSpec((EXPERT_ROWS // WORD_ROWS, d), rows_block),
                pl.BlockSpec((1, d, ff), lambda w, ie, fl: (ie[w], 0, 0)),
                pl.BlockSpec((1, d, ff), lambda w, ie, fl: (ie[w], 0, 0)),
                pl.BlockSpec((1, ff, d), lambda w, ie, fl: (ie[w], 0, 0)),
            ],
            out_specs=pl.BlockSpec((EXPERT_ROWS // WORD_ROWS, d), rows_block),
            scratch_shapes=[pltpu.VMEM((d, ff), BF16), pltpu.VMEM((d, ff), BF16), pltpu.VMEM((ff, d), BF16)],
        ),
        out_shape=jax.ShapeDtypeStruct((n, d), jnp.int32),
        input_output_aliases={2: 0},
        compiler_params=_cparams(("arbitrary",)),
        name="experts",
    )(block_expert, block_flags, xs, wg, wu, wd)


def _block_table(seg_rows, n_rows):
    block_end = jnp.cumsum(seg_rows // EXPERT_ROWS)
    w = jnp.arange(n_rows // EXPERT_ROWS, dtype=jnp.int32)
    valid = w < block_end[-1]
    e = jnp.sum((block_end[None, :] <= jnp.minimum(w, block_end[-1] - 1)[:, None]).astype(jnp.int32), axis=1)
    e = jnp.minimum(e, N_EXPERTS - 1)
    prev_e = jnp.concatenate([jnp.full((1,), -1, jnp.int32), e[:-1]])
    flags = valid.astype(jnp.int32) + 2 * (valid & (e != prev_e)).astype(jnp.int32)
    return e, jnp.concatenate([flags, block_end[-1:].astype(jnp.int32) - 1])


def _combine_kernel(*refs):
    tables = refs[:N_PIECE_TABLES]
    (rows_ref, pos_ref, gate_ref, x_ref, sg_ref, su_ref, sd_ref, g_ref, beta_ref, yb_ref, o_ref,
     stage, acc, lane_tile, lane_gate, sem) = refs[N_PIECE_TABLES:]
    i = pl.program_id(0)
    nt = pl.num_programs(0)
    slot = i % 2
    tn = ROUTE_TILE

    def start_tile(tile, buf):
        def start_piece(local, src, n):
            pltpu.make_async_copy(yb_ref.at[_word_rows(src, n), :], stage.at[buf, _word_rows(local, n), :],
                                  sem.at[buf]).start()
        _for_each_piece(tables, tile, start_piece)

    @pl.when(i == 0)
    def _():
        stage[...] = jnp.zeros_like(stage)
        start_tile(0, 0)

    @pl.when(i + 1 < nt)
    def _():
        start_tile(i + 1, 1 - slot)

    x = x_ref[...]
    xb = x.astype(BF16)
    gte = jnp.dot(xb, sg_ref[...], preferred_element_type=F32)
    up = jnp.dot(xb, su_ref[...], preferred_element_type=F32)
    h = (gte * _sigmoid(gte) * up).astype(BF16)
    acc[...] = jnp.dot(h, sd_ref[...], preferred_element_type=F32)

    def wait_rows(n):
        pltpu.make_async_copy(yb_ref.at[_word_rows(0, n), :], stage.at[slot, _word_rows(0, n), :],
                              sem.at[slot]).wait()

    _for_each_total_piece(rows_ref[i], wait_rows)

    lanes = LANES
    lane_iota = lax.broadcasted_iota(jnp.int32, (tn, lanes), 1)
    for k in range(TOP_K):
        p = jnp.broadcast_to(pos_ref[:, k:k + 1], (tn, lanes))
        g = jnp.broadcast_to(gate_ref[:, k:k + 1], (tn, lanes))
        lane_tile[k] = (p >> (lanes.bit_length() - 1)).astype(F32).astype(BF16)
        lane_gate[k] = jnp.where((p & (lanes - 1)) == lane_iota, g, 0.0).astype(BF16)
    zero = jnp.zeros((tn, lanes), BF16)

    def gate_matrix(c):
        cols = []
        for j in range(COMBINE_CHUNK // lanes):
            tile = jnp.asarray(c * (COMBINE_CHUNK // lanes) + j, jnp.int32).astype(F32).astype(BF16)
            w = zero
            for k in range(TOP_K):
                w = w + jnp.where(lane_tile[k] == tile, lane_gate[k], zero)
            cols.append(w)
        return jnp.concatenate(cols, axis=1)

    def weighed(c):
        p0 = pl.multiple_of(c * COMBINE_CHUNK, COMBINE_CHUNK)
        rows = pltpu.bitcast(stage[slot, _word_rows(p0, COMBINE_CHUNK), :], BF16)
        return jnp.dot(gate_matrix(c), rows, preferred_element_type=F32)

    def weigh_chunks(c2, carry):
        acc[...] += weighed(2 * c2) + weighed(2 * c2 + 1)
        return carry

    n_chunks = (rows_ref[i] + COMBINE_CHUNK - 1) // COMBINE_CHUNK
    lax.fori_loop(0, n_chunks // 2, weigh_chunks, 0)

    @pl.when(n_chunks % 2 == 1)
    def _():
        acc[...] += weighed(n_chunks - 1)
    o_ref[...] = _layer_norm(ALPHA * x + acc[...], g_ref[...], beta_ref[...])


def _combine(pieces, tile_rows, pos_t, gate_t, x2d, sg, su, sd, g, beta, yb):
    t, d = x2d.shape
    nt = t // ROUTE_TILE
    ff = sg.shape[-1]
    row = lambda i, *_: (i, 0)
    const = lambda i, *_: (0, 0)
    return pl.pallas_call(
        _combine_kernel,
        grid_spec=pltpu.PrefetchScalarGridSpec(
            num_scalar_prefetch=len(pieces) + 1,
            grid=(nt,),
            in_specs=[
                pl.BlockSpec((ROUTE_TILE, TOP_K), row),
                pl.BlockSpec((ROUTE_TILE, TOP_K), row),
                pl.BlockSpec((ROUTE_TILE, d), row),
                pl.BlockSpec((d, ff), const),
                pl.BlockSpec((d, ff), const),
                pl.BlockSpec((ff, d), const),
                pl.BlockSpec((1, d), const),
                pl.BlockSpec((1, d), const),
                pl.BlockSpec(memory_space=pl.ANY),
            ],
            out_specs=pl.BlockSpec((ROUTE_TILE, d), row),
            scratch_shapes=[pltpu.VMEM((2, TILE_CAP // WORD_ROWS, d), jnp.int32),
                            pltpu.VMEM((ROUTE_TILE, d), F32),
                            pltpu.VMEM((TOP_K, ROUTE_TILE, LANES), BF16),
                            pltpu.VMEM((TOP_K, ROUTE_TILE, LANES), BF16),
                            pltpu.SemaphoreType.DMA((2,))],
        ),
        out_shape=jax.ShapeDtypeStruct((t, d), F32),
        compiler_params=_cparams(("arbitrary",)),
        name="combine",
    )(*pieces, tile_rows, pos_t, gate_t, x2d, sg, su, sd, g, beta, yb)


def _piece_tables(cnt):
    lbase = jnp.cumsum(cnt, axis=1) - cnt
    per_expert = jnp.sum(cnt, axis=0)
    seg_rows = -(-per_expert // EXPERT_ROWS) * EXPERT_ROWS
    seg_start = jnp.cumsum(seg_rows) - seg_rows
    gbase = seg_start[None, :] + jnp.cumsum(cnt, axis=0) - cnt
    gaps = jnp.concatenate([seg_start + per_expert, seg_rows - per_expert]).astype(jnp.int32)

    def listed(n_pieces, start_off, piece_rows, length):
        last = jnp.cumsum(n_pieces, axis=1)
        first = last - n_pieces
        j = jnp.arange(length, dtype=jnp.int32)[None, :, None]
        own = (j >= first[:, None, :]) & (j < last[:, None, :])
        off = (start_off[:, None, :] + (j - first[:, None, :]) * piece_rows)
        pick = lambda base: jnp.sum(jnp.where(own, base[:, None, :] + off, 0), axis=-1).reshape(-1).astype(jnp.int32)
        return pick(lbase), pick(gbase), last[:, -1].astype(jnp.int32)

    tables = listed(cnt // BIG_PIECE, jnp.zeros_like(cnt), BIG_PIECE, BIG_PER_TILE)
    for size in SMALL_PIECES:
        tables += listed((cnt // size) % 2, cnt // (2 * size) * (2 * size), size, N_EXPERTS)
    return tables, jnp.sum(cnt, axis=1).astype(jnp.int32), seg_rows, gaps


def kernel(x, mem, w_in, b_in, conv_w, conv_b, conv_ln_g, conv_ln_b, attn_sinks, rel_bias, w_out, b_out, ln1_g, ln1_b, xq_w, xkv_w, xo_w, ln2_g, ln2_b, router_w, router_b, exp_gate, exp_up, exp_down, sh_gate, sh_up, sh_down, ln3_g, ln3_b):
    bsz, seq, d = x.shape
    t = bsz * seq
    bias_tab, sink_tab = _band_tables(rel_bias, attn_sinks[0])
    tri = (jnp.arange(ROUTE_TILE)[:, None] <= jnp.arange(ROUTE_TILE)[None, :]).astype(BF16)
    row = lambda p: p.reshape(1, -1)
    for l in range(DEPTH):
        conv_out, q, k, v = _mix(x, w_in[l].astype(BF16), row(b_in[l]), conv_w[l], row(conv_b[l]),
                                 row(conv_ln_g[l]), row(conv_ln_b[l]))
        x1 = _swa(q, k, v, bias_tab, sink_tab, x, conv_out, w_out[l].astype(BF16), row(b_out[l]),
                  row(ln1_g[l]), row(ln1_b[l]))
        x2, pos, gate, cnt = _cross(x1, mem, xkv_w[l].astype(BF16), xq_w[l].astype(BF16), xo_w[l].astype(BF16),
                                    row(ln2_g[l]), row(ln2_b[l]), router_w[l].T.astype(BF16),
                                    router_b[l].reshape(-1, 1), tri)
        x2 = x2.reshape(t, d)
        nt = t // ROUTE_TILE
        n_rows = (-(-(t * TOP_K + nt * N_EXPERTS * (RUN_ALIGN - 1)) // EXPERT_ROWS) + N_EXPERTS) * EXPERT_ROWS
        pieces, tile_rows, seg_rows, gaps = _piece_tables(cnt[:, :, 0])
        block_expert, block_flags = _block_table(seg_rows, n_rows)
        xs = _dispatch(pieces, tile_rows, jnp.sum(seg_rows).reshape(1).astype(jnp.int32), gaps, pos, x2, n_rows)
        yb = _experts(block_expert, block_flags, xs, exp_gate[l], exp_up[l], exp_down[l])
        pos_t = jnp.transpose(pos, (0, 2, 1)).reshape(t, TOP_K)
        gate_t = jnp.transpose(gate, (0, 2, 1)).reshape(t, TOP_K)
        x = _combine(pieces, tile_rows, pos_t, gate_t, x2, sh_gate[l].astype(BF16),
                     sh_up[l].astype(BF16), sh_down[l].astype(BF16), row(ln3_g[l]), row(ln3_b[l]),
                     yb).reshape(bsz, seq, d)
    return x
```

```python
import math

import jax
import jax.numpy as jnp
from jax import lax
from jax.experimental import pallas as pl
from jax.experimental.pallas import tpu as pltpu

D_MODEL = 1024
HEAD_DIM = 64
CONV_CH = D_MODEL // 2
CONV_WIDTH = 31
ATT_HEADS = 8
KV_HEADS = 2
WINDOW = 128
BLOCK = 128
REL_BUCKETS = 32
REL_MAX_DIST = 128
Q_COLS = ATT_HEADS * HEAD_DIM
KV_COLS = KV_HEADS * HEAD_DIM
IN_COLS = 2 * CONV_CH + Q_COLS + 2 * KV_COLS
X_HEADS = 4
X_HEAD_DIM = D_MODEL // X_HEADS
N_EXPERTS = 64
TOP_K = 8
N_GROUPS = 8
GROUP_SIZE = N_EXPERTS // N_GROUPS
TOPK_GROUPS = 4
EXPERT_FF = D_MODEL // 4
ROUTED_SCALE = 2.5
DEPTH = 1
ALPHA = (2 * DEPTH) ** 0.25
LN_EPS = 1e-5
NEG_INF = -1e30

F32 = jnp.float32
BF16 = jnp.bfloat16

VMEM_LIMIT_BYTES = 56 * 1024 * 1024

ROW_TILE = 512
CONV_ROWS = 32
SUBLANES = 8
LANES = 128
CONV_HALO = 32
CROSS_TILE = 1024
CROSS_SPLIT = 2
SWA_TILE = 512
ROUTE_TILE = 512
EXPERT_ROWS = 1024
EXPERT_SPLIT = 2
RUN_ALIGN = 16
RUN_ALIGN_LOG2 = RUN_ALIGN.bit_length() - 1
WORD_ROWS = 2
TILE_CAP = ROUTE_TILE * TOP_K + N_EXPERTS * RUN_ALIGN
BIG_PIECE = 32
BIG_PER_TILE = TILE_CAP // BIG_PIECE
SMALL_PIECES = tuple(1 << b for b in range(BIG_PIECE.bit_length() - 2, RUN_ALIGN_LOG2 - 1, -1))
N_PIECE_TABLES = 3 * (1 + len(SMALL_PIECES))
PIECE_UNROLL = 8
TOTAL_PIECES = tuple(1 << b for b in range(TILE_CAP.bit_length() - 1, RUN_ALIGN_LOG2 - 1, -1))
SORT_CHUNK = 256
COMBINE_CHUNK = 512


def _cparams(sem):
    return pltpu.CompilerParams(dimension_semantics=sem, vmem_limit_bytes=VMEM_LIMIT_BYTES)


def _layer_norm(h, g, b):
    mu = jnp.mean(h, axis=-1, keepdims=True)
    d = h - mu
    var = jnp.mean(d * d, axis=-1, keepdims=True)
    return d * lax.rsqrt(var + LN_EPS) * g + b


def _sigmoid(x):
    return 1.0 / (1.0 + jnp.exp(-x))


def _silu(x):
    return x * (0.5 * jnp.tanh(0.5 * x) + 0.5)


def _mix_kernel(x_ref, w_ref, b_ref, cw_ref, cb_ref, cg_ref, cbeta_ref,
                conv_ref, q_ref, k_ref, v_ref, u_ext, u_sh):
    j = pl.program_id(1)
    xb = x_ref[0].astype(BF16)
    proj = jnp.dot(xb, w_ref[...], preferred_element_type=F32) + b_ref[...]
    a = proj[:, :CONV_CH]
    g = proj[:, CONV_CH:2 * CONV_CH]
    q_ref[0] = (proj[:, 2 * CONV_CH:2 * CONV_CH + Q_COLS] * (HEAD_DIM ** -0.5)).astype(BF16)
    k_ref[0] = proj[:, 2 * CONV_CH + Q_COLS:2 * CONV_CH + Q_COLS + KV_COLS].astype(BF16)
    v_ref[0] = proj[:, 2 * CONV_CH + Q_COLS + KV_COLS:].astype(BF16)

    @pl.when(j == 0)
    def _():
        u_ext[0:CONV_HALO, :] = jnp.zeros((CONV_HALO, CONV_CH), F32)

    u_ext[CONV_HALO:CONV_HALO + ROW_TILE, :] = a * _sigmoid(g)

    first_tap = CONV_HALO - (CONV_WIDTH - 1)
    shifted_rows = u_sh.shape[1]
    for r in range(1, SUBLANES):
        u_sh[r - 1] = u_ext[r:r + shifted_rows, :]

    for c in range(ROW_TILE // CONV_ROWS):
        acc = jnp.zeros((CONV_ROWS, CONV_CH), F32) + cb_ref[...]
        for t in range(CONV_WIDTH):
            r = (first_tap + t) % SUBLANES
            base = c * CONV_ROWS + (first_tap + t) - r
            if r == 0:
                taps = u_ext[base:base + CONV_ROWS, :]
            else:
                taps = u_sh[r - 1, base:base + CONV_ROWS, :]
            acc = acc + taps * cw_ref[t:t + 1, :]
        y = _layer_norm(acc, cg_ref[...], cbeta_ref[...])
        conv_ref[0, c * CONV_ROWS:(c + 1) * CONV_ROWS, :] = (y * _sigmoid(y)).astype(BF16)

    u_ext[0:CONV_HALO, :] = u_ext[ROW_TILE:ROW_TILE + CONV_HALO, :]


def _mix(x, w_in, b_in, conv_w, conv_b, conv_g, conv_beta):
    bsz, seq, d = x.shape
    nt = seq // ROW_TILE
    row = lambda b, j: (b, j, 0)
    const2 = lambda b, j: (0, 0)
    return pl.pallas_call(
        _mix_kernel,
        grid=(bsz, nt),
        in_specs=[
            pl.BlockSpec((1, ROW_TILE, d), row),
            pl.BlockSpec((d, IN_COLS), const2),
            pl.BlockSpec((1, IN_COLS), const2),
            pl.BlockSpec((CONV_WIDTH, CONV_CH), const2),
            pl.BlockSpec((1, CONV_CH), const2),
            pl.BlockSpec((1, CONV_CH), const2),
            pl.BlockSpec((1, CONV_CH), const2),
        ],
        out_specs=[
            pl.BlockSpec((1, ROW_TILE, CONV_CH), row),
            pl.BlockSpec((1, ROW_TILE, Q_COLS), row),
            pl.BlockSpec((1, ROW_TILE, KV_COLS), row),
            pl.BlockSpec((1, ROW_TILE, KV_COLS), row),
        ],
        out_shape=[
            jax.ShapeDtypeStruct((bsz, seq, CONV_CH), BF16),
            jax.ShapeDtypeStruct((bsz, seq, Q_COLS), BF16),
            jax.ShapeDtypeStruct((bsz, seq, KV_COLS), BF16),
            jax.ShapeDtypeStruct((bsz, seq, KV_COLS), BF16),
        ],
        scratch_shapes=[pltpu.VMEM((ROW_TILE + CONV_HALO, CONV_CH), F32),
                        pltpu.VMEM((SUBLANES - 1, ROW_TILE + CONV_HALO - SUBLANES, CONV_CH), F32)],
        compiler_params=_cparams(("arbitrary", "arbitrary")),
        name="mix",
    )(x, w_in, b_in, conv_w, conv_b, conv_g, conv_beta)


def _swa_kernel(q_ref, kp_ref, kc_ref, vp_ref, vc_ref, bias_ref, sink_ref, x_ref, conv_ref, w_ref, b_ref,
                g_ref, beta_ref, o_ref):
    j = pl.program_id(1)
    mix = jnp.dot(conv_ref[0], w_ref[0:CONV_CH, :], preferred_element_type=F32)
    rows = BLOCK + SWA_TILE
    lane = lax.broadcasted_iota(jnp.int32, (rows, 2 * HEAD_DIM), 1)
    low = lane < HEAD_DIM

    def placements(prev_ref, cur_ref):
        t = jnp.concatenate([prev_ref[0], cur_ref[0]], axis=0).astype(F32)
        tr = pltpu.roll(t, HEAD_DIM, 1)
        zero = jnp.zeros_like(t)
        kv0_low = jnp.where(low, t, zero).astype(BF16)
        kv1_high = jnp.where(low, zero, t).astype(BF16)
        kv1_low = jnp.where(low, tr, zero).astype(BF16)
        kv0_high = jnp.where(low, zero, tr).astype(BF16)
        return (kv0_low, kv0_high, kv1_low, kv1_high)

    ks = placements(kp_ref, kc_ref)
    vs = placements(vp_ref, vc_ref)
    slab = 2 * HEAD_DIM
    att = []
    for i in range(SWA_TILE // BLOCK):
        q = q_ref[0, i * BLOCK:(i + 1) * BLOCK, :]
        q_kv0 = jnp.concatenate([q[:, 0:slab], q[:, slab:2 * slab]], axis=0)
        q_kv1 = jnp.concatenate([q[:, 2 * slab:3 * slab], q[:, 3 * slab:4 * slab]], axis=0)
        band = slice(i * BLOCK, (i + 2) * BLOCK)
        outs = []
        for s in range(4):
            qs = q_kv0 if s < 2 else q_kv1
            bias = bias_ref[1, s]
            if i == 0:
                bias = jnp.where(j == 0, bias_ref[0, s], bias)
            logits = lax.dot_general(qs, ks[s][band], (((1,), (1,)), ((), ())),
                                     preferred_element_type=F32) + bias
            sink = sink_ref[s]
            m = jnp.maximum(jnp.max(logits, axis=-1, keepdims=True), sink)
            p = jnp.exp(logits - m)
            den = jnp.sum(p, axis=-1, keepdims=True) + jnp.exp(sink - m)
            o = jnp.dot(p.astype(BF16), vs[s][band], preferred_element_type=F32)
            outs.append(o / den)
        o_kv0 = outs[0] + outs[1]
        o_kv1 = outs[2] + outs[3]
        att.append(jnp.concatenate([o_kv0[0:BLOCK], o_kv0[BLOCK:2 * BLOCK], o_kv1[0:BLOCK],
                                    o_kv1[BLOCK:2 * BLOCK]], axis=1).astype(BF16))

    mix = mix + jnp.dot(jnp.concatenate(att, axis=0), w_ref[CONV_CH:, :], preferred_element_type=F32)
    h = ALPHA * x_ref[0] + mix + b_ref[...]
    o_ref[0] = _layer_norm(h, g_ref[...], beta_ref[...])


def _swa(q, k, v, bias_tab, sink_tab, x, conv_out, w_out, b_out, g, beta):
    bsz, seq, d = x.shape
    per = SWA_TILE // BLOCK
    cur = lambda b, n: (b, n, 0)
    prev = lambda b, n: (b, jnp.maximum(n * per - 1, 0), 0)
    whole = lambda b, n: (0, 0, 0, 0)
    const = lambda b, n: (0, 0)
    return pl.pallas_call(
        _swa_kernel,
        grid=(bsz, seq // SWA_TILE),
        in_specs=[
            pl.BlockSpec((1, SWA_TILE, Q_COLS), cur),
            pl.BlockSpec((1, BLOCK, KV_COLS), prev),
            pl.BlockSpec((1, SWA_TILE, KV_COLS), cur),
            pl.BlockSpec((1, BLOCK, KV_COLS), prev),
            pl.BlockSpec((1, SWA_TILE, KV_COLS), cur),
            pl.BlockSpec((2, 4, 2 * BLOCK, 2 * BLOCK), whole),
            pl.BlockSpec((4, 2 * BLOCK, 1), lambda b, n: (0, 0, 0)),
            pl.BlockSpec((1, SWA_TILE, d), cur),
            pl.BlockSpec((1, SWA_TILE, CONV_CH), cur),
            pl.BlockSpec((d, d), const),
            pl.BlockSpec((1, d), const),
            pl.BlockSpec((1, d), const),
            pl.BlockSpec((1, d), const),
        ],
        out_specs=pl.BlockSpec((1, SWA_TILE, d), cur),
        out_shape=jax.ShapeDtypeStruct((bsz, seq, d), F32),
        compiler_params=_cparams(("arbitrary", "arbitrary")),
        name="swa",
    )(q, k, k, v, v, bias_tab, sink_tab, x, conv_out, w_out, b_out, g, beta)


def _t5_bucket(dist):
    n = jnp.maximum(dist, 0)
    exact = REL_BUCKETS // 2
    large = exact + (jnp.log(jnp.maximum(n, 1).astype(F32) / exact)
                     / math.log(REL_MAX_DIST / exact) * (REL_BUCKETS - exact)).astype(jnp.int32)
    large = jnp.minimum(large, REL_BUCKETS - 1)
    return jnp.where(n < exact, n, large)


def _band_tables(rel_bias, sinks):
    qi = jnp.arange(BLOCK)[:, None]
    kj = jnp.arange(2 * BLOCK)[None, :]
    dist = qi + BLOCK - kj
    bucket = _t5_bucket(dist)
    bias = jnp.zeros((ATT_HEADS, BLOCK, 2 * BLOCK), F32)
    for bkt in range(REL_BUCKETS):
        bias = jnp.where(bucket[None] == bkt, rel_bias[bkt].astype(F32)[:, None, None], bias)
    in_window = (dist >= 0) & (dist < WINDOW)
    masks = jnp.stack([in_window & (kj >= BLOCK), in_window])
    masked = jnp.where(masks[:, None], bias[None], NEG_INF)
    pairs = ((0, 2), (1, 3), (4, 6), (5, 7))
    bias_tab = jnp.stack([jnp.concatenate([masked[:, a], masked[:, b]], axis=1) for a, b in pairs], axis=1)
    s = sinks.astype(F32)
    sink_tab = jnp.stack([jnp.concatenate([jnp.full((BLOCK, 1), s[a]), jnp.full((BLOCK, 1), s[b])], axis=0)
                          for a, b in pairs])
    return bias_tab, sink_tab


def _cross_kernel(x_ref, mem_ref, wkv_ref, wq_ref, wo_ref, g_ref, beta_ref, rw_ref, rb_ref, tri_ref,
                  o_ref, pos_ref, gate_ref, cnt_ref, k_ref, v_ref):
    @pl.when(pl.program_id(1) == 0)
    def _():
        kv = jnp.dot(mem_ref[0].astype(BF16), wkv_ref[...], preferred_element_type=F32)
        k_ref[...] = kv[:, :D_MODEL].astype(BF16)
        v_ref[...] = kv[:, D_MODEL:].astype(BF16)

    sub = CROSS_TILE // CROSS_SPLIT
    groups = [slice(s * sub, (s + 1) * sub) for s in range(CROSS_SPLIT)]
    xs = [x_ref[0, g, :] for g in groups]
    qs = [(jnp.dot(x.astype(BF16), wq_ref[...], preferred_element_type=F32) * (X_HEAD_DIM ** -0.5)).astype(BF16)
          for x in xs]
    heads = [[] for _ in groups]
    for h in range(X_HEADS):
        cols = slice(h * X_HEAD_DIM, (h + 1) * X_HEAD_DIM)
        logits = [lax.dot_general(q[:, cols], k_ref[:, cols], (((1,), (1,)), ((), ())),
                                  preferred_element_type=F32) for q in qs]
        ps, dens = [], []
        for lg in logits:
            p = jnp.exp(lg - jnp.max(lg, axis=-1, keepdims=True))
            ps.append(p)
            dens.append(jnp.sum(p, axis=-1, keepdims=True))
        for s, (p, den) in enumerate(zip(ps, dens)):
            o = jnp.dot(p.astype(BF16), v_ref[:, cols], preferred_element_type=F32)
            heads[s].append((o / den).astype(BF16))
    crosses = [jnp.dot(jnp.concatenate(hs, axis=-1), wo_ref[...], preferred_element_type=F32) for hs in heads]
    for s, (g, x, cross) in enumerate(zip(groups, xs, crosses)):
        y = _layer_norm(ALPHA * x + cross, g_ref[...], beta_ref[...])
        o_ref[0, g, :] = y
        logits_t = lax.dot_general(rw_ref[...], y.astype(BF16), (((1,), (1,)), ((), ())),
                                   preferred_element_type=F32)
        positions, gates, run_len = _route_tile(logits_t, rb_ref[...], tri_ref[...])
        cnt_ref[s] = run_len
        for r in range(TOP_K):
            pos_ref[s, r:r + 1, :] = positions[r]
            gate_ref[s, r:r + 1, :] = gates[r]


def _cross(x1, mem_in, wkv, wq, wo, g, beta, rw_t, router_b, tri):
    bsz, seq, d = x1.shape
    mem_len = mem_in.shape[1]
    nt = seq // CROSS_TILE
    assert CROSS_TILE // CROSS_SPLIT == ROUTE_TILE
    n_tiles = bsz * seq // ROUTE_TILE
    row = lambda b, j: (b, j, 0)
    mem = lambda b, j: (b, 0, 0)
    const = lambda b, j: (0, 0)
    tiles = lambda b, j: (b * nt + j, 0, 0)
    return pl.pallas_call(
        _cross_kernel,
        grid=(bsz, nt),
        in_specs=[
            pl.BlockSpec((1, CROSS_TILE, d), row),
            pl.BlockSpec((1, mem_len, d), mem),
            pl.BlockSpec((d, 2 * d), const),
            pl.BlockSpec((d, d), const),
            pl.BlockSpec((d, d), const),
            pl.BlockSpec((1, d), const),
            pl.BlockSpec((1, d), const),
            pl.BlockSpec((N_EXPERTS, d), const),
            pl.BlockSpec((N_EXPERTS, 1), const),
            pl.BlockSpec((ROUTE_TILE, ROUTE_TILE), const),
        ],
        out_specs=[
            pl.BlockSpec((1, CROSS_TILE, d), row),
            pl.BlockSpec((CROSS_SPLIT, TOP_K, ROUTE_TILE), tiles),
            pl.BlockSpec((CROSS_SPLIT, TOP_K, ROUTE_TILE), tiles),
            pl.BlockSpec((CROSS_SPLIT, N_EXPERTS, LANES), tiles),
        ],
        out_shape=[
            jax.ShapeDtypeStruct((bsz, seq, d), F32),
            jax.ShapeDtypeStruct((n_tiles, TOP_K, ROUTE_TILE), jnp.int32),
            jax.ShapeDtypeStruct((n_tiles, TOP_K, ROUTE_TILE), F32),
            jax.ShapeDtypeStruct((n_tiles, N_EXPERTS, LANES), jnp.int32),
        ],
        scratch_shapes=[pltpu.VMEM((mem_len, d), BF16), pltpu.VMEM((mem_len, d), BF16)],
        compiler_params=_cparams(("arbitrary", "arbitrary")),
        name="cross",
    )(x1, mem_in, wkv, wq, wo, g, beta, rw_t, router_b, tri)


def _route_tile(logits_t, router_b, tri):
    tn = ROUTE_TILE
    scores = _sigmoid(logits_t)
    choice = scores + router_b

    gscore = []
    member = lax.broadcasted_iota(jnp.int32, (GROUP_SIZE, tn), 0).astype(F32)
    for g in range(N_GROUPS):
        c = choice[g * GROUP_SIZE:(g + 1) * GROUP_SIZE, :]
        m1 = jnp.max(c, axis=0, keepdims=True)
        first = jnp.min(jnp.where(c == m1, member, float(GROUP_SIZE)), axis=0, keepdims=True)
        m2 = jnp.max(jnp.where(member == first, -jnp.inf, c), axis=0, keepdims=True)
        gscore.append(m1 + m2)

    keep_rows = []
    for g in range(N_GROUPS):
        beaten = jnp.zeros((1, tn), F32)
        for o in range(N_GROUPS):
            if o == g:
                continue
            ahead = (gscore[o] >= gscore[g]) if o < g else (gscore[o] > gscore[g])
            beaten = beaten + jnp.where(ahead, 1.0, 0.0)
        keep_rows.append(jnp.broadcast_to(beaten, (GROUP_SIZE, tn)))
    beaten_all = jnp.concatenate(keep_rows, axis=0)

    masked = jnp.where(beaten_all < TOPK_GROUPS, choice, -jnp.inf)
    eidx = lax.broadcasted_iota(jnp.int32, (N_EXPERTS, tn), 0).astype(F32)
    sel = jnp.zeros((N_EXPERTS, tn), F32)
    picks, weights = [], []
    for r in range(TOP_K):
        mx = jnp.max(masked, axis=0, keepdims=True)
        first = jnp.min(jnp.where(masked == mx, eidx, float(N_EXPERTS)), axis=0, keepdims=True)
        pick = eidx == first
        picks.append((pick, first))
        weights.append(jnp.sum(jnp.where(pick, scores, 0.0), axis=0, keepdims=True))
        masked = jnp.where(pick, -jnp.inf, masked)
        sel = jnp.where(pick, 1.0, sel)

    wsum = weights[0]
    for r in range(1, TOP_K):
        wsum = wsum + weights[r]

    count = jnp.sum(sel, axis=1, keepdims=True)
    run_len = jnp.floor((count + (RUN_ALIGN - 1.0)) * (1.0 / RUN_ALIGN)) * RUN_ALIGN
    run_len_b = jnp.broadcast_to(run_len, (N_EXPERTS, LANES))
    er = lax.broadcasted_iota(jnp.int32, (N_EXPERTS, N_EXPERTS), 0)
    ec = lax.broadcasted_iota(jnp.int32, (N_EXPERTS, N_EXPERTS), 1)
    before = jnp.where(ec < er, 1.0, 0.0).astype(BF16)
    run_start = jnp.dot(before, run_len_b.astype(BF16), preferred_element_type=F32)[:, 0:1]
    incl = jnp.dot(sel.astype(BF16), tri, preferred_element_type=F32)
    pos_mat = run_start + incl - sel
    positions = [jnp.sum(jnp.where(pick, pos_mat, 0.0), axis=0, keepdims=True).astype(jnp.int32)
                 for pick, _ in picks]
    gates = [w / wsum * ROUTED_SCALE for w in weights]
    return positions, gates, run_len_b.astype(jnp.int32)


def _for_each_piece(tables, tile, fn):
    lists = [(BIG_PIECE, BIG_PER_TILE)] + [(size, N_EXPERTS) for size in SMALL_PIECES]
    for k, (n_rows, per_tile) in enumerate(lists):
        local_ref, global_ref, count_ref = tables[3 * k:3 * k + 3]
        count = count_ref[tile]

        def body(g, carry, n_rows=n_rows, per_tile=per_tile, local_ref=local_ref, global_ref=global_ref,
                 count=count):
            for u in range(PIECE_UNROLL):
                j = g * PIECE_UNROLL + u

                @pl.when(j < count)
                def _():
                    fn(pl.multiple_of(local_ref[tile * per_tile + j], RUN_ALIGN),
                       pl.multiple_of(global_ref[tile * per_tile + j], RUN_ALIGN), n_rows)
            return carry

        lax.fori_loop(0, (count + PIECE_UNROLL - 1) // PIECE_UNROLL, body, 0)


def _word_rows(start, size):
    if isinstance(start, int):
        first = start // WORD_ROWS
    else:
        first = pl.multiple_of(start >> (WORD_ROWS.bit_length() - 1), RUN_ALIGN // WORD_ROWS)
    return pl.ds(first, size // WORD_ROWS)


def _for_each_total_piece(total, fn):
    for piece in TOTAL_PIECES:
        @pl.when((total & piece) != 0)
        def _():
            fn(piece)


def _dispatch_kernel(*refs):
    tables = refs[:N_PIECE_TABLES]
    rows_ref, total_ref, gap_ref, pos_ref, x_ref, xs_ref, stage, zeros, sem, zsem = refs[N_PIECE_TABLES:]
    i = pl.program_id(0)
    nt = pl.num_programs(0)
    slot = i % 2
    tn = ROUTE_TILE

    def drain(tile, buf):
        def wait_rows(n):
            pltpu.make_async_copy(stage.at[buf, _word_rows(0, n), :], xs_ref.at[_word_rows(0, n), :],
                                  sem.at[buf]).wait()
        _for_each_total_piece(rows_ref[tile], wait_rows)

    @pl.when(i >= 2)
    def _():
        drain(i - 2, slot)

    xb = x_ref[...].astype(BF16)
    pos = pos_ref[0]
    row_iota = lax.broadcasted_iota(jnp.int32, (SORT_CHUNK, tn), 0).astype(F32).astype(BF16)
    one = jnp.ones((SORT_CHUNK, tn), BF16)
    zero = jnp.zeros((SORT_CHUNK, tn), BF16)

    def onehot_rows(c):
        rel = jnp.clip(pos - c * SORT_CHUNK, -1, SORT_CHUNK).astype(F32).astype(BF16)
        hit = rel[0:1, :] == row_iota
        for k in range(1, TOP_K):
            hit = hit | (rel[k:k + 1, :] == row_iota)
        return jnp.where(hit, one, zero)

    def sort_chunks(c2, carry):
        for c in (2 * c2, 2 * c2 + 1):
            p0 = pl.multiple_of(c * SORT_CHUNK, SORT_CHUNK)
            rows = jnp.dot(onehot_rows(c), xb, preferred_element_type=F32)
            stage[slot, _word_rows(p0, SORT_CHUNK), :] = pltpu.bitcast(rows.astype(BF16), jnp.int32)
        return carry

    lax.fori_loop(0, (rows_ref[i] + 2 * SORT_CHUNK - 1) // (2 * SORT_CHUNK), sort_chunks, 0)

    def start_piece(local, dst, n):
        pltpu.make_async_copy(stage.at[slot, _word_rows(local, n), :], xs_ref.at[_word_rows(dst, n), :],
                              sem.at[slot]).start()

    _for_each_piece(tables, i, start_piece)

    @pl.when(i == nt - 1)
    def _():
        @pl.when(nt >= 2)
        def _():
            drain(i - 1, 1 - slot)

        drain(i, slot)

        zeros[...] = jnp.zeros_like(zeros)

        def gap_copies(e, act):
            gap = gap_ref[N_EXPERTS + e]
            for piece in TOTAL_PIECES:
                if piece >= EXPERT_ROWS:
                    continue

                @pl.when((gap & piece) != 0)
                def _():
                    dst = gap_ref[e] + (gap & (-2 * piece))
                    act(pltpu.make_async_copy(zeros.at[_word_rows(0, piece), :],
                                              xs_ref.at[_word_rows(dst, piece), :], zsem))

        def zero_block(b):
            return pltpu.make_async_copy(zeros, xs_ref.at[_word_rows(b * EXPERT_ROWS, EXPERT_ROWS), :], zsem)

        first_free = total_ref[0] // EXPERT_ROWS
        n_blocks = xs_ref.shape[0] * WORD_ROWS // EXPERT_ROWS
        lax.fori_loop(0, N_EXPERTS, lambda e, c: (gap_copies(e, lambda cp: cp.start()), c)[1], 0)
        lax.fori_loop(first_free, n_blocks, lambda b, c: (zero_block(b).start(), c)[1], 0)
        lax.fori_loop(0, N_EXPERTS, lambda e, c: (gap_copies(e, lambda cp: cp.wait()), c)[1], 0)
        lax.fori_loop(first_free, n_blocks, lambda b, c: (zero_block(b).wait(), c)[1], 0)


def _dispatch(pieces, tile_rows, total, gaps, pos, x2d, n_rows):
    t, d = x2d.shape
    nt = t // ROUTE_TILE
    return pl.pallas_call(
        _dispatch_kernel,
        grid_spec=pltpu.PrefetchScalarGridSpec(
            num_scalar_prefetch=len(pieces) + 3,
            grid=(nt,),
            in_specs=[pl.BlockSpec((1, TOP_K, ROUTE_TILE), lambda i, *_: (i, 0, 0)),
                      pl.BlockSpec((ROUTE_TILE, d), lambda i, *_: (i, 0))],
            out_specs=pl.BlockSpec(memory_space=pl.ANY),
            scratch_shapes=[pltpu.VMEM((2, TILE_CAP // WORD_ROWS, d), jnp.int32),
                            pltpu.VMEM((EXPERT_ROWS // WORD_ROWS, d), jnp.int32),
                            pltpu.SemaphoreType.DMA((2,)),
                            pltpu.SemaphoreType.DMA(())],
        ),
        out_shape=jax.ShapeDtypeStruct((n_rows // WORD_ROWS, d), jnp.int32),
        compiler_params=_cparams(("arbitrary",)),
        name="dispatch",
    )(*pieces, tile_rows, total, gaps, pos, x2d)


def _experts_kernel(ie_ref, flag_ref, xs_ref, wg_ref, wu_ref, wd_ref, y_ref, wg_bf, wu_bf, wd_bf):
    w = pl.program_id(0)
    flags = flag_ref[w]
    valid = (flags & 1) != 0
    new_expert = (flags & 2) != 0

    @pl.when(new_expert)
    def _():
        wg_bf[...] = wg_ref[0].astype(BF16)
        wu_bf[...] = wu_ref[0].astype(BF16)
        wd_bf[...] = wd_ref[0].astype(BF16)

    @pl.when(valid)
    def _():
        sub = EXPERT_ROWS // WORD_ROWS // EXPERT_SPLIT
        groups = [slice(s * sub, (s + 1) * sub) for s in range(EXPERT_SPLIT)]
        xs = [pltpu.bitcast(xs_ref[g, :], BF16) for g in groups]
        gu = [(jnp.dot(x, wg_bf[...], preferred_element_type=F32),
               jnp.dot(x, wu_bf[...], preferred_element_type=F32)) for x in xs]
        hs = [(_silu(gte) * up).astype(BF16) for gte, up in gu]
        for g, h in zip(groups, hs):
            y = jnp.dot(h, wd_bf[...], preferred_element_type=F32)
            y_ref[g, :] = pltpu.bitcast(y.astype(BF16), jnp.int32)


def _experts(block_expert, block_flags, xs, wg, wu, wd):
    n, d = xs.shape
    ff = wg.shape[-1]
    rows_block = lambda w, ie, fl: (jnp.minimum(w, fl[fl.shape[0] - 1]), 0)
    return pl.pallas_call(
        _experts_kernel,
        grid_spec=pltpu.PrefetchScalarGridSpec(
            num_scalar_prefetch=2,
            grid=(block_expert.shape[0],),
            in_specs=[
                pl.BlockSpec((EXPERT_ROWS // WORD_ROWS, d), rows_block),
                pl.BlockSpec((1, d, ff), lambda w, ie, fl: (ie[w], 0, 0)),
                pl.BlockSpec((1, d, ff), lambda w, ie, fl: (ie[w], 0, 0)),
                pl.BlockSpec((1, ff, d), lambda w, ie, fl: (ie[w], 0, 0)),
            ],
            out_specs=pl.BlockSpec((EXPERT_ROWS // WORD_ROWS, d), rows_block),
            scratch_shapes=[pltpu.VMEM((d, ff), BF16), pltpu.VMEM((d, ff), BF16), pltpu.VMEM((ff, d), BF16)],
        ),
        out_shape=jax.ShapeDtypeStruct((n, d), jnp.int32),
        input_output_aliases={2: 0},
        compiler_params=_cparams(("arbitrary",)),
        name="experts",
    )(block_expert, block_flags, xs, wg, wu, wd)


def _block_table(seg_rows, n_rows):
    block_end = jnp.cumsum(seg_rows // EXPERT_ROWS)
    w = jnp.arange(n_rows // EXPERT_ROWS, dtype=jnp.int32)
    valid = w < block_end[-1]
    e = jnp.sum((block_end[None, :] <= jnp.minimum(w, block_end[-1] - 1)[:, None]).astype(jnp.int32), axis=1)
    e = jnp.minimum(e, N_EXPERTS - 1)
    prev_e = jnp.concatenate([jnp.full((1,), -1, jnp.int32), e[:-1]])
    flags = valid.astype(jnp.int32) + 2 * (valid & (e != prev_e)).astype(jnp.int32)
    return e, jnp.concatenate([flags, block_end[-1:].astype(jnp.int32) - 1])


def _combine_kernel(*refs):
    tables = refs[:N_PIECE_TABLES]
    (rows_ref, pos_ref, gate_ref, x_ref, sg_ref, su_ref, sd_ref, g_ref, beta_ref, yb_ref, o_ref,
     stage, acc, lane_tile, lane_gate, sem) = refs[N_PIECE_TABLES:]
    i = pl.program_id(0)
    nt = pl.num_programs(0)
    slot = i % 2
    tn = ROUTE_TILE

    def start_tile(tile, buf):
        def start_piece(local, src, n):
            pltpu.make_async_copy(yb_ref.at[_word_rows(src, n), :], stage.at[buf, _word_rows(local, n), :],
                                  sem.at[buf]).start()
        _for_each_piece(tables, tile, start_piece)

    @pl.when(i == 0)
    def _():
        stage[...] = jnp.zeros_like(stage)
        start_tile(0, 0)

    @pl.when(i + 1 < nt)
    def _():
        start_tile(i + 1, 1 - slot)

    x = x_ref[...]
    xb = x.astype(BF16)
    gte = jnp.dot(xb, sg_ref[...], preferred_element_type=F32)
    up = jnp.dot(xb, su_ref[...], preferred_element_type=F32)
    h = (_silu(gte) * up).astype(BF16)
    acc[...] = jnp.dot(h, sd_ref[...], preferred_element_type=F32)

    def wait_rows(n):
        pltpu.make_async_copy(yb_ref.at[_word_rows(0, n), :], stage.at[slot, _word_rows(0, n), :],
                              sem.at[slot]).wait()

    _for_each_total_piece(rows_ref[i], wait_rows)

    lanes = LANES
    lane_iota = lax.broadcasted_iota(jnp.int32, (tn, lanes), 1)
    for k in range(TOP_K):
        p = jnp.broadcast_to(pos_ref[:, k:k + 1], (tn, lanes))
        g = jnp.broadcast_to(gate_ref[:, k:k + 1], (tn, lanes))
        lane_tile[k] = (p >> (lanes.bit_length() - 1)).astype(F32).astype(BF16)
        lane_gate[k] = jnp.where((p & (lanes - 1)) == lane_iota, g, 0.0).astype(BF16)
    zero = jnp.zeros((tn, lanes), BF16)

    def gate_matrix(c):
        cols = []
        for j in range(COMBINE_CHUNK // lanes):
            tile = jnp.asarray(c * (COMBINE_CHUNK // lanes) + j, jnp.int32).astype(F32).astype(BF16)
            w = zero
            for k in range(TOP_K):
                w = w + jnp.where(lane_tile[k] == tile, lane_gate[k], zero)
            cols.append(w)
        return jnp.concatenate(cols, axis=1)

    def weighed(c):
        p0 = pl.multiple_of(c * COMBINE_CHUNK, COMBINE_CHUNK)
        rows = pltpu.bitcast(stage[slot, _word_rows(p0, COMBINE_CHUNK), :], BF16)
        return jnp.dot(gate_matrix(c), rows, preferred_element_type=F32)

    def weigh_chunks(c2, carry):
        acc[...] += weighed(2 * c2) + weighed(2 * c2 + 1)
        return carry

    n_chunks = (rows_ref[i] + COMBINE_CHUNK - 1) // COMBINE_CHUNK
    lax.fori_loop(0, n_chunks // 2, weigh_chunks, 0)

    @pl.when(n_chunks % 2 == 1)
    def _():
        acc[...] += weighed(n_chunks - 1)
    o_ref[...] = _layer_norm(ALPHA * x + acc[...], g_ref[...], beta_ref[...])


def _combine(pieces, tile_rows, pos_t, gate_t, x2d, sg, su, sd, g, beta, yb):
    t, d = x2d.shape
    nt = t // ROUTE_TILE
    ff = sg.shape[-1]
    row = lambda i, *_: (i, 0)
    const = lambda i, *_: (0, 0)
    return pl.pallas_call(
        _combine_kernel,
        grid_spec=pltpu.PrefetchScalarGridSpec(
            num_scalar_prefetch=len(pieces) + 1,
            grid=(nt,),
            in_specs=[
                pl.BlockSpec((ROUTE_TILE, TOP_K), row),
                pl.BlockSpec((ROUTE_TILE, TOP_K), row),
                pl.BlockSpec((ROUTE_TILE, d), row),
                pl.BlockSpec((d, ff), const),
                pl.BlockSpec((d, ff), const),
                pl.BlockSpec((ff, d), const),
                pl.BlockSpec((1, d), const),
                pl.BlockSpec((1, d), const),
                pl.BlockSpec(memory_space=pl.ANY),
            ],
            out_specs=pl.BlockSpec((ROUTE_TILE, d), row),
            scratch_shapes=[pltpu.VMEM((2, TILE_CAP // WORD_ROWS, d), jnp.int32),
                            pltpu.VMEM((ROUTE_TILE, d), F32),
                            pltpu.VMEM((TOP_K, ROUTE_TILE, LANES), BF16),
                            pltpu.VMEM((TOP_K, ROUTE_TILE, LANES), BF16),
                            pltpu.SemaphoreType.DMA((2,))],
        ),
        out_shape=jax.ShapeDtypeStruct((t, d), F32),
        compiler_params=_cparams(("arbitrary",)),
        name="combine",
    )(*pieces, tile_rows, pos_t, gate_t, x2d, sg, su, sd, g, beta, yb)


def _piece_tables(cnt):
    lbase = jnp.cumsum(cnt, axis=1) - cnt
    per_expert = jnp.sum(cnt, axis=0)
    seg_rows = -(-per_expert // EXPERT_ROWS) * EXPERT_ROWS
    seg_start = jnp.cumsum(seg_rows) - seg_rows
    gbase = seg_start[None, :] + jnp.cumsum(cnt, axis=0) - cnt
    gaps = jnp.concatenate([seg_start + per_expert, seg_rows - per_expert]).astype(jnp.int32)

    def listed(n_pieces, start_off, piece_rows, length):
        last = jnp.cumsum(n_pieces, axis=1)
        first = last - n_pieces
        j = jnp.arange(length, dtype=jnp.int32)[None, :, None]
        own = (j >= first[:, None, :]) & (j < last[:, None, :])
        off = (start_off[:, None, :] + (j - first[:, None, :]) * piece_rows)
        pick = lambda base: jnp.sum(jnp.where(own, base[:, None, :] + off, 0), axis=-1).reshape(-1).astype(jnp.int32)
        return pick(lbase), pick(gbase), last[:, -1].astype(jnp.int32)

    tables = listed(cnt // BIG_PIECE, jnp.zeros_like(cnt), BIG_PIECE, BIG_PER_TILE)
    for size in SMALL_PIECES:
        tables += listed((cnt // size) % 2, cnt // (2 * size) * (2 * size), size, N_EXPERTS)
    return tables, jnp.sum(cnt, axis=1).astype(jnp.int32), seg_rows, gaps


def kernel(x, mem, w_in, b_in, conv_w, conv_b, conv_ln_g, conv_ln_b, attn_sinks, rel_bias, w_out, b_out, ln1_g, ln1_b, xq_w, xkv_w, xo_w, ln2_g, ln2_b, router_w, router_b, exp_gate, exp_up, exp_down, sh_gate, sh_up, sh_down, ln3_g, ln3_b):
    bsz, seq, d = x.shape
    t = bsz * seq
    bias_tab, sink_tab = _band_tables(rel_bias, attn_sinks[0])
    tri = (jnp.arange(ROUTE_TILE)[:, None] <= jnp.arange(ROUTE_TILE)[None, :]).astype(BF16)
    row = lambda p: p.reshape(1, -1)
    for l in range(DEPTH):
        conv_out, q, k, v = _mix(x, w_in[l].astype(BF16), row(b_in[l]), conv_w[l], row(conv_b[l]),
                                 row(conv_ln_g[l]), row(conv_ln_b[l]))
        x1 = _swa(q, k, v, bias_tab, sink_tab, x, conv_out, w_out[l].astype(BF16), row(b_out[l]),
                  row(ln1_g[l]), row(ln1_b[l]))
        x2, pos, gate, cnt = _cross(x1, mem, xkv_w[l].astype(BF16), xq_w[l].astype(BF16), xo_w[l].astype(BF16),
                                    row(ln2_g[l]), row(ln2_b[l]), router_w[l].T.astype(BF16),
                                    router_b[l].reshape(-1, 1), tri)
        x2 = x2.reshape(t, d)
        nt = t // ROUTE_TILE
        n_rows = (-(-(t * TOP_K + nt * N_EXPERTS * (RUN_ALIGN - 1)) // EXPERT_ROWS) + N_EXPERTS) * EXPERT_ROWS
        pieces, tile_rows, seg_rows, gaps = _piece_tables(cnt[:, :, 0])
        block_expert, block_flags = _block_table(seg_rows, n_rows)
        xs = _dispatch(pieces, tile_rows, jnp.sum(seg_rows).reshape(1).astype(jnp.int32), gaps, pos, x2, n_rows)
        yb = _experts(block_expert, block_flags, xs, exp_gate[l], exp_up[l], exp_down[l])
        pos_t = jnp.transpose(pos, (0, 2, 1)).reshape(t, TOP_K)
        gate_t = jnp.transpose(gate, (0, 2, 1)).reshape(t, TOP_K)
        x = _combine(pieces, tile_rows, pos_t, gate_t, x2, sh_gate[l].astype(BF16),
                     sh_up[l].astype(BF16), sh_down[l].astype(BF16), row(ln3_g[l]), row(ln3_b[l]),
                     yb).reshape(bsz, seq, d)
    return x
```

```python
import math

import jax
import jax.numpy as jnp
from jax import lax
from jax.experimental import pallas as pl
from jax.experimental.pallas import tpu as pltpu

D_MODEL = 1024
HEAD_DIM = 64
CONV_CH = D_MODEL // 2
CONV_WIDTH = 31
ATT_HEADS = 8
KV_HEADS = 2
WINDOW = 128
BLOCK = 128
REL_BUCKETS = 32
REL_MAX_DIST = 128
Q_COLS = ATT_HEADS * HEAD_DIM
KV_COLS = KV_HEADS * HEAD_DIM
IN_COLS = 2 * CONV_CH + Q_COLS + 2 * KV_COLS
X_HEADS = 4
X_HEAD_DIM = D_MODEL // X_HEADS
N_EXPERTS = 64
TOP_K = 8
N_GROUPS = 8
GROUP_SIZE = N_EXPERTS // N_GROUPS
TOPK_GROUPS = 4
EXPERT_FF = D_MODEL // 4
ROUTED_SCALE = 2.5
DEPTH = 1
ALPHA = (2 * DEPTH) ** 0.25
LN_EPS = 1e-5
NEG_INF = -1e30

F32 = jnp.float32
BF16 = jnp.bfloat16

VMEM_LIMIT_BYTES = 56 * 1024 * 1024

ROW_TILE = 512
CONV_ROWS = 256
SUBLANES = 8
LANES = 128
CONV_HALO = 32
CROSS_TILE = 1024
CROSS_SPLIT = 2
SWA_TILE = 512
ROUTE_TILE = 512
EXPERT_ROWS = 1024
EXPERT_SPLIT = 2
RUN_ALIGN = 16
RUN_ALIGN_LOG2 = RUN_ALIGN.bit_length() - 1
WORD_ROWS = 2
TILE_CAP = ROUTE_TILE * TOP_K + N_EXPERTS * RUN_ALIGN
BIG_PIECE = 32
BIG_PER_TILE = TILE_CAP // BIG_PIECE
SMALL_PIECES = tuple(1 << b for b in range(BIG_PIECE.bit_length() - 2, RUN_ALIGN_LOG2 - 1, -1))
N_PIECE_TABLES = 3 * (1 + len(SMALL_PIECES))
PIECE_UNROLL = 8
TOTAL_PIECES = tuple(1 << b for b in range(TILE_CAP.bit_length() - 1, RUN_ALIGN_LOG2 - 1, -1))
SORT_CHUNK = 256
COMBINE_CHUNK = 512


def _cparams(sem):
    return pltpu.CompilerParams(dimension_semantics=sem, vmem_limit_bytes=VMEM_LIMIT_BYTES)


def _layer_norm(h, g, b):
    mu = jnp.mean(h, axis=-1, keepdims=True)
    d = h - mu
    var = jnp.mean(d * d, axis=-1, keepdims=True)
    return d * lax.rsqrt(var + LN_EPS) * g + b


def _sigmoid(x):
    return 1.0 / (1.0 + jnp.exp(-x))


def _silu(x):
    return x * (0.5 * jnp.tanh(0.5 * x) + 0.5)


def _mix_kernel(x_ref, w_ref, b_ref, cw_ref, cb_ref, cg_ref, cbeta_ref,
                conv_ref, q_ref, k_ref, v_ref, u_ext, u_sh):
    j = pl.program_id(1)
    xb = x_ref[0].astype(BF16)
    proj = jnp.dot(xb, w_ref[...], preferred_element_type=F32) + b_ref[...]
    a = proj[:, :CONV_CH]
    g = proj[:, CONV_CH:2 * CONV_CH]
    q_ref[0] = (proj[:, 2 * CONV_CH:2 * CONV_CH + Q_COLS] * (HEAD_DIM ** -0.5)).astype(BF16)
    k_ref[0] = proj[:, 2 * CONV_CH + Q_COLS:2 * CONV_CH + Q_COLS + KV_COLS].astype(BF16)
    v_ref[0] = proj[:, 2 * CONV_CH + Q_COLS + KV_COLS:].astype(BF16)

    @pl.when(j == 0)
    def _():
        u_ext[0:CONV_HALO, :] = jnp.zeros((CONV_HALO, CONV_CH), F32)

    u_ext[CONV_HALO:CONV_HALO + ROW_TILE, :] = a * _sigmoid(g)

    first_tap = CONV_HALO - (CONV_WIDTH - 1)
    shifted_rows = u_sh.shape[1]
    for r in range(1, SUBLANES):
        u_sh[r - 1] = u_ext[r:r + shifted_rows, :]

    for c in range(ROW_TILE // CONV_ROWS):
        acc = jnp.zeros((CONV_ROWS, CONV_CH), F32) + cb_ref[...]
        for t in range(CONV_WIDTH):
            r = (first_tap + t) % SUBLANES
            base = c * CONV_ROWS + (first_tap + t) - r
            if r == 0:
                taps = u_ext[base:base + CONV_ROWS, :]
            else:
                taps = u_sh[r - 1, base:base + CONV_ROWS, :]
            acc = acc + taps * cw_ref[t:t + 1, :]
        y = _layer_norm(acc, cg_ref[...], cbeta_ref[...])
        conv_ref[0, c * CONV_ROWS:(c + 1) * CONV_ROWS, :] = (y * _sigmoid(y)).astype(BF16)

    u_ext[0:CONV_HALO, :] = u_ext[ROW_TILE:ROW_TILE + CONV_HALO, :]


def _mix(x, w_in, b_in, conv_w, conv_b, conv_g, conv_beta):
    bsz, seq, d = x.shape
    nt = seq // ROW_TILE
    row = lambda b, j: (b, j, 0)
    const2 = lambda b, j: (0, 0)
    return pl.pallas_call(
        _mix_kernel,
        grid=(bsz, nt),
        in_specs=[
            pl.BlockSpec((1, ROW_TILE, d), row),
            pl.BlockSpec((d, IN_COLS), const2),
            pl.BlockSpec((1, IN_COLS), const2),
            pl.BlockSpec((CONV_WIDTH, CONV_CH), const2),
            pl.BlockSpec((1, CONV_CH), const2),
            pl.BlockSpec((1, CONV_CH), const2),
            pl.BlockSpec((1, CONV_CH), const2),
        ],
        out_specs=[
            pl.BlockSpec((1, ROW_TILE, CONV_CH), row),
            pl.BlockSpec((1, ROW_TILE, Q_COLS), row),
            pl.BlockSpec((1, ROW_TILE, KV_COLS), row),
            pl.BlockSpec((1, ROW_TILE, KV_COLS), row),
        ],
        out_shape=[
            jax.ShapeDtypeStruct((bsz, seq, CONV_CH), BF16),
            jax.ShapeDtypeStruct((bsz, seq, Q_COLS), BF16),
            jax.ShapeDtypeStruct((bsz, seq, KV_COLS), BF16),
            jax.ShapeDtypeStruct((bsz, seq, KV_COLS), BF16),
        ],
        scratch_shapes=[pltpu.VMEM((ROW_TILE + CONV_HALO, CONV_CH), F32),
                        pltpu.VMEM((SUBLANES - 1, ROW_TILE + CONV_HALO - SUBLANES, CONV_CH), F32)],
        compiler_params=_cparams(("arbitrary", "arbitrary")),
        name="mix",
    )(x, w_in, b_in, conv_w, conv_b, conv_g, conv_beta)


def _swa_kernel(q_ref, kp_ref, kc_ref, vp_ref, vc_ref, bias_ref, sink_ref, x_ref, conv_ref, w_ref, b_ref,
                g_ref, beta_ref, o_ref):
    j = pl.program_id(1)
    mix = jnp.dot(conv_ref[0], w_ref[0:CONV_CH, :], preferred_element_type=F32)
    rows = BLOCK + SWA_TILE
    lane = lax.broadcasted_iota(jnp.int32, (rows, 2 * HEAD_DIM), 1)
    low = lane < HEAD_DIM

    def placements(prev_ref, cur_ref):
        t = jnp.concatenate([prev_ref[0], cur_ref[0]], axis=0).astype(F32)
        tr = pltpu.roll(t, HEAD_DIM, 1)
        zero = jnp.zeros_like(t)
        kv0_low = jnp.where(low, t, zero).astype(BF16)
        kv1_high = jnp.where(low, zero, t).astype(BF16)
        kv1_low = jnp.where(low, tr, zero).astype(BF16)
        kv0_high = jnp.where(low, zero, tr).astype(BF16)
        return (kv0_low, kv0_high, kv1_low, kv1_high)

    ks = placements(kp_ref, kc_ref)
    vs = placements(vp_ref, vc_ref)
    slab = 2 * HEAD_DIM
    att = []
    for i in range(SWA_TILE // BLOCK):
        q = q_ref[0, i * BLOCK:(i + 1) * BLOCK, :]
        q_kv0 = jnp.concatenate([q[:, 0:slab], q[:, slab:2 * slab]], axis=0)
        q_kv1 = jnp.concatenate([q[:, 2 * slab:3 * slab], q[:, 3 * slab:4 * slab]], axis=0)
        band = slice(i * BLOCK, (i + 2) * BLOCK)
        outs = []
        for s in range(4):
            qs = q_kv0 if s < 2 else q_kv1
            bias = bias_ref[1, s]
            if i == 0:
                bias = jnp.where(j == 0, bias_ref[0, s], bias)
            logits = lax.dot_general(qs, ks[s][band], (((1,), (1,)), ((), ())),
                                     preferred_element_type=F32) + bias
            sink = sink_ref[s]
            m = jnp.maximum(jnp.max(logits, axis=-1, keepdims=True), sink)
            p = jnp.exp(logits - m)
            den = jnp.sum(p, axis=-1, keepdims=True) + jnp.exp(sink - m)
            o = jnp.dot(p.astype(BF16), vs[s][band], preferred_element_type=F32)
            outs.append(o / den)
        o_kv0 = outs[0] + outs[1]
        o_kv1 = outs[2] + outs[3]
        att.append(jnp.concatenate([o_kv0[0:BLOCK], o_kv0[BLOCK:2 * BLOCK], o_kv1[0:BLOCK],
                                    o_kv1[BLOCK:2 * BLOCK]], axis=1).astype(BF16))

    mix = mix + jnp.dot(jnp.concatenate(att, axis=0), w_ref[CONV_CH:, :], preferred_element_type=F32)
    h = ALPHA * x_ref[0] + mix + b_ref[...]
    o_ref[0] = _layer_norm(h, g_ref[...], beta_ref[...])


def _swa(q, k, v, bias_tab, sink_tab, x, conv_out, w_out, b_out, g, beta):
    bsz, seq, d = x.shape
    per = SWA_TILE // BLOCK
    cur = lambda b, n: (b, n, 0)
    prev = lambda b, n: (b, jnp.maximum(n * per - 1, 0), 0)
    whole = lambda b, n: (0, 0, 0, 0)
    const = lambda b, n: (0, 0)
    return pl.pallas_call(
        _swa_kernel,
        grid=(bsz, seq // SWA_TILE),
        in_specs=[
            pl.BlockSpec((1, SWA_TILE, Q_COLS), cur),
            pl.BlockSpec((1, BLOCK, KV_COLS), prev),
            pl.BlockSpec((1, SWA_TILE, KV_COLS), cur),
            pl.BlockSpec((1, BLOCK, KV_COLS), prev),
            pl.BlockSpec((1, SWA_TILE, KV_COLS), cur),
            pl.BlockSpec((2, 4, 2 * BLOCK, 2 * BLOCK), whole),
            pl.BlockSpec((4, 2 * BLOCK, 1), lambda b, n: (0, 0, 0)),
            pl.BlockSpec((1, SWA_TILE, d), cur),
            pl.BlockSpec((1, SWA_TILE, CONV_CH), cur),
            pl.BlockSpec((d, d), const),
            pl.BlockSpec((1, d), const),
            pl.BlockSpec((1, d), const),
            pl.BlockSpec((1, d), const),
        ],
        out_specs=pl.BlockSpec((1, SWA_TILE, d), cur),
        out_shape=jax.ShapeDtypeStruct((bsz, seq, d), F32),
        compiler_params=_cparams(("arbitrary", "arbitrary")),
        name="swa",
    )(q, k, k, v, v, bias_tab, sink_tab, x, conv_out, w_out, b_out, g, beta)


def _t5_bucket(dist):
    n = jnp.maximum(dist, 0)
    exact = REL_BUCKETS // 2
    large = exact + (jnp.log(jnp.maximum(n, 1).astype(F32) / exact)
                     / math.log(REL_MAX_DIST / exact) * (REL_BUCKETS - exact)).astype(jnp.int32)
    large = jnp.minimum(large, REL_BUCKETS - 1)
    return jnp.where(n < exact, n, large)


def _band_tables(rel_bias, sinks):
    qi = jnp.arange(BLOCK)[:, None]
    kj = jnp.arange(2 * BLOCK)[None, :]
    dist = qi + BLOCK - kj
    bucket = _t5_bucket(dist)
    bias = jnp.zeros((ATT_HEADS, BLOCK, 2 * BLOCK), F32)
    for bkt in range(REL_BUCKETS):
        bias = jnp.where(bucket[None] == bkt, rel_bias[bkt].astype(F32)[:, None, None], bias)
    in_window = (dist >= 0) & (dist < WINDOW)
    masks = jnp.stack([in_window & (kj >= BLOCK), in_window])
    masked = jnp.where(masks[:, None], bias[None], NEG_INF)
    pairs = ((0, 2), (1, 3), (4, 6), (5, 7))
    bias_tab = jnp.stack([jnp.concatenate([masked[:, a], masked[:, b]], axis=1) for a, b in pairs], axis=1)
    s = sinks.astype(F32)
    sink_tab = jnp.stack([jnp.concatenate([jnp.full((BLOCK, 1), s[a]), jnp.full((BLOCK, 1), s[b])], axis=0)
                          for a, b in pairs])
    return bias_tab, sink_tab


def _cross_kernel(x_ref, mem_ref, wkv_ref, wq_ref, wo_ref, g_ref, beta_ref, rw_ref, rb_ref, tri_ref,
                  o_ref, pos_ref, gate_ref, cnt_ref, k_ref, v_ref):
    @pl.when(pl.program_id(1) == 0)
    def _():
        kv = jnp.dot(mem_ref[0].astype(BF16), wkv_ref[...], preferred_element_type=F32)
        k_ref[...] = kv[:, :D_MODEL].astype(BF16)
        v_ref[...] = kv[:, D_MODEL:].astype(BF16)

    sub = CROSS_TILE // CROSS_SPLIT
    groups = [slice(s * sub, (s + 1) * sub) for s in range(CROSS_SPLIT)]
    xs = [x_ref[0, g, :] for g in groups]
    qs = [(jnp.dot(x.astype(BF16), wq_ref[...], preferred_element_type=F32) * (X_HEAD_DIM ** -0.5)).astype(BF16)
          for x in xs]
    heads = [[] for _ in groups]
    for h in range(X_HEADS):
        cols = slice(h * X_HEAD_DIM, (h + 1) * X_HEAD_DIM)
        logits = [lax.dot_general(q[:, cols], k_ref[:, cols], (((1,), (1,)), ((), ())),
                                  preferred_element_type=F32) for q in qs]
        ps, dens = [], []
        for lg in logits:
            p = jnp.exp(lg - jnp.max(lg, axis=-1, keepdims=True))
            ps.append(p)
            dens.append(jnp.sum(p, axis=-1, keepdims=True))
        for s, (p, den) in enumerate(zip(ps, dens)):
            o = jnp.dot(p.astype(BF16), v_ref[:, cols], preferred_element_type=F32)
            heads[s].append((o / den).astype(BF16))
    crosses = [jnp.dot(jnp.concatenate(hs, axis=-1), wo_ref[...], preferred_element_type=F32) for hs in heads]
    for s, (g, x, cross) in enumerate(zip(groups, xs, crosses)):
        y = _layer_norm(ALPHA * x + cross, g_ref[...], beta_ref[...])
        o_ref[0, g, :] = y
        logits_t = lax.dot_general(rw_ref[...], y.astype(BF16), (((1,), (1,)), ((), ())),
                                   preferred_element_type=F32)
        positions, gates, run_len = _route_tile(logits_t, rb_ref[...], tri_ref[...])
        cnt_ref[s] = run_len
        for r in range(TOP_K):
            pos_ref[s, r:r + 1, :] = positions[r]
            gate_ref[s, r:r + 1, :] = gates[r]


def _cross(x1, mem_in, wkv, wq, wo, g, beta, rw_t, router_b, tri):
    bsz, seq, d = x1.shape
    mem_len = mem_in.shape[1]
    nt = seq // CROSS_TILE
    assert CROSS_TILE // CROSS_SPLIT == ROUTE_TILE
    n_tiles = bsz * seq // ROUTE_TILE
    row = lambda b, j: (b, j, 0)
    mem = lambda b, j: (b, 0, 0)
    const = lambda b, j: (0, 0)
    tiles = lambda b, j: (b * nt + j, 0, 0)
    return pl.pallas_call(
        _cross_kernel,
        grid=(bsz, nt),
        in_specs=[
            pl.BlockSpec((1, CROSS_TILE, d), row),
            pl.BlockSpec((1, mem_len, d), mem),
            pl.BlockSpec((d, 2 * d), const),
            pl.BlockSpec((d, d), const),
            pl.BlockSpec((d, d), const),
            pl.BlockSpec((1, d), const),
            pl.BlockSpec((1, d), const),
            pl.BlockSpec((N_EXPERTS, d), const),
            pl.BlockSpec((N_EXPERTS, 1), const),
            pl.BlockSpec((ROUTE_TILE, ROUTE_TILE), const),
        ],
        out_specs=[
            pl.BlockSpec((1, CROSS_TILE, d), row),
            pl.BlockSpec((CROSS_SPLIT, TOP_K, ROUTE_TILE), tiles),
            pl.BlockSpec((CROSS_SPLIT, TOP_K, ROUTE_TILE), tiles),
            pl.BlockSpec((CROSS_SPLIT, N_EXPERTS, LANES), tiles),
        ],
        out_shape=[
            jax.ShapeDtypeStruct((bsz, seq, d), F32),
            jax.ShapeDtypeStruct((n_tiles, TOP_K, ROUTE_TILE), jnp.int32),
            jax.ShapeDtypeStruct((n_tiles, TOP_K, ROUTE_TILE), F32),
            jax.ShapeDtypeStruct((n_tiles, N_EXPERTS, LANES), jnp.int32),
        ],
        scratch_shapes=[pltpu.VMEM((mem_len, d), BF16), pltpu.VMEM((mem_len, d), BF16)],
        compiler_params=_cparams(("arbitrary", "arbitrary")),
        name="cross",
    )(x1, mem_in, wkv, wq, wo, g, beta, rw_t, router_b, tri)


def _route_tile(logits_t, router_b, tri):
    tn = ROUTE_TILE
    scores = _sigmoid(logits_t)
    choice = scores + router_b

    gscore = []
    member = lax.broadcasted_iota(jnp.int32, (GROUP_SIZE, tn), 0).astype(F32)
    for g in range(N_GROUPS):
        c = choice[g * GROUP_SIZE:(g + 1) * GROUP_SIZE, :]
        m1 = jnp.max(c, axis=0, keepdims=True)
        first = jnp.min(jnp.where(c == m1, member, float(GROUP_SIZE)), axis=0, keepdims=True)
        m2 = jnp.max(jnp.where(member == first, -jnp.inf, c), axis=0, keepdims=True)
        gscore.append(m1 + m2)

    keep_rows = []
    for g in range(N_GROUPS):
        beaten = jnp.zeros((1, tn), F32)
        for o in range(N_GROUPS):
            if o == g:
                continue
            ahead = (gscore[o] >= gscore[g]) if o < g else (gscore[o] > gscore[g])
            beaten = beaten + jnp.where(ahead, 1.0, 0.0)
        keep_rows.append(jnp.broadcast_to(beaten, (GROUP_SIZE, tn)))
    beaten_all = jnp.concatenate(keep_rows, axis=0)

    masked = jnp.where(beaten_all < TOPK_GROUPS, choice, -jnp.inf)
    eidx = lax.broadcasted_iota(jnp.int32, (N_EXPERTS, tn), 0).astype(F32)
    sel = jnp.zeros((N_EXPERTS, tn), F32)
    picks, weights = [], []
    for r in range(TOP_K):
        mx = jnp.max(masked, axis=0, keepdims=True)
        first = jnp.min(jnp.where(masked == mx, eidx, float(N_EXPERTS)), axis=0, keepdims=True)
        pick = eidx == first
        picks.append((pick, first))
        weights.append(jnp.sum(jnp.where(pick, scores, 0.0), axis=0, keepdims=True))
        masked = jnp.where(pick, -jnp.inf, masked)
        sel = jnp.where(pick, 1.0, sel)

    wsum = weights[0]
    for r in range(1, TOP_K):
        wsum = wsum + weights[r]

    count = jnp.sum(sel, axis=1, keepdims=True)
    run_len = jnp.floor((count + (RUN_ALIGN - 1.0)) * (1.0 / RUN_ALIGN)) * RUN_ALIGN
    run_len_b = jnp.broadcast_to(run_len, (N_EXPERTS, LANES))
    er = lax.broadcasted_iota(jnp.int32, (N_EXPERTS, N_EXPERTS), 0)
    ec = lax.broadcasted_iota(jnp.int32, (N_EXPERTS, N_EXPERTS), 1)
    before = jnp.where(ec < er, 1.0, 0.0).astype(BF16)
    run_start = jnp.dot(before, run_len_b.astype(BF16), preferred_element_type=F32)[:, 0:1]
    incl = jnp.dot(sel.astype(BF16), tri, preferred_element_type=F32)
    pos_mat = run_start + incl - sel
    positions = [jnp.sum(jnp.where(pick, pos_mat, 0.0), axis=0, keepdims=True).astype(jnp.int32)
                 for pick, _ in picks]
    gates = [w / wsum * ROUTED_SCALE for w in weights]
    return positions, gates, run_len_b.astype(jnp.int32)


def _for_each_piece(tables, tile, fn):
    lists = [(BIG_PIECE, BIG_PER_TILE)] + [(size, N_EXPERTS) for size in SMALL_PIECES]
    for k, (n_rows, per_tile) in enumerate(lists):
        local_ref, global_ref, count_ref = tables[3 * k:3 * k + 3]
        count = count_ref[tile]

        def body(g, carry, n_rows=n_rows, per_tile=per_tile, local_ref=local_ref, global_ref=global_ref,
                 count=count):
            for u in range(PIECE_UNROLL):
                j = g * PIECE_UNROLL + u

                @pl.when(j < count)
                def _():
                    fn(pl.multiple_of(local_ref[tile * per_tile + j], RUN_ALIGN),
                       pl.multiple_of(global_ref[tile * per_tile + j], RUN_ALIGN), n_rows)
            return carry

        lax.fori_loop(0, (count + PIECE_UNROLL - 1) // PIECE_UNROLL, body, 0)


def _word_rows(start, size):
    if isinstance(start, int):
        first = start // WORD_ROWS
    else:
        first = pl.multiple_of(start >> (WORD_ROWS.bit_length() - 1), RUN_ALIGN // WORD_ROWS)
    return pl.ds(first, size // WORD_ROWS)


def _for_each_total_piece(total, fn):
    for piece in TOTAL_PIECES:
        @pl.when((total & piece) != 0)
        def _():
            fn(piece)


def _dispatch_kernel(*refs):
    tables = refs[:N_PIECE_TABLES]
    rows_ref, total_ref, gap_ref, pos_ref, x_ref, xs_ref, stage, zeros, sem, zsem = refs[N_PIECE_TABLES:]
    i = pl.program_id(0)
    nt = pl.num_programs(0)
    slot = i % 2
    tn = ROUTE_TILE

    def drain(tile, buf):
        def wait_rows(n):
            pltpu.make_async_copy(stage.at[buf, _word_rows(0, n), :], xs_ref.at[_word_rows(0, n), :],
                                  sem.at[buf]).wait()
        _for_each_total_piece(rows_ref[tile], wait_rows)

    @pl.when(i >= 2)
    def _():
        drain(i - 2, slot)

    xb = x_ref[...].astype(BF16)
    pos = pos_ref[0]
    row_iota = lax.broadcasted_iota(jnp.int32, (SORT_CHUNK, tn), 0).astype(F32).astype(BF16)
    one = jnp.ones((SORT_CHUNK, tn), BF16)
    zero = jnp.zeros((SORT_CHUNK, tn), BF16)

    def onehot_rows(c):
        rel = jnp.clip(pos - c * SORT_CHUNK, -1, SORT_CHUNK).astype(F32).astype(BF16)
        hit = rel[0:1, :] == row_iota
        for k in range(1, TOP_K):
            hit = hit | (rel[k:k + 1, :] == row_iota)
        return jnp.where(hit, one, zero)

    def sort_chunks(c2, carry):
        for c in (2 * c2, 2 * c2 + 1):
            p0 = pl.multiple_of(c * SORT_CHUNK, SORT_CHUNK)
            rows = jnp.dot(onehot_rows(c), xb, preferred_element_type=F32)
            stage[slot, _word_rows(p0, SORT_CHUNK), :] = pltpu.bitcast(rows.astype(BF16), jnp.int32)
        return carry

    lax.fori_loop(0, (rows_ref[i] + 2 * SORT_CHUNK - 1) // (2 * SORT_CHUNK), sort_chunks, 0)

    def start_piece(local, dst, n):
        pltpu.make_async_copy(stage.at[slot, _word_rows(local, n), :], xs_ref.at[_word_rows(dst, n), :],
                              sem.at[slot]).start()

    _for_each_piece(tables, i, start_piece)

    @pl.when(i == nt - 1)
    def _():
        @pl.when(nt >= 2)
        def _():
            drain(i - 1, 1 - slot)

        drain(i, slot)

        zeros[...] = jnp.zeros_like(zeros)

        def gap_copies(e, act):
            gap = gap_ref[N_EXPERTS + e]
            for piece in TOTAL_PIECES:
                if piece >= EXPERT_ROWS:
                    continue

                @pl.when((gap & piece) != 0)
                def _():
                    dst = gap_ref[e] + (gap & (-2 * piece))
                    act(pltpu.make_async_copy(zeros.at[_word_rows(0, piece), :],
                                              xs_ref.at[_word_rows(dst, piece), :], zsem))

        def zero_block(b):
            return pltpu.make_async_copy(zeros, xs_ref.at[_word_rows(b * EXPERT_ROWS, EXPERT_ROWS), :], zsem)

        first_free = total_ref[0] // EXPERT_ROWS
        n_blocks = xs_ref.shape[0] * WORD_ROWS // EXPERT_ROWS
        lax.fori_loop(0, N_EXPERTS, lambda e, c: (gap_copies(e, lambda cp: cp.start()), c)[1], 0)
        lax.fori_loop(first_free, n_blocks, lambda b, c: (zero_block(b).start(), c)[1], 0)
        lax.fori_loop(0, N_EXPERTS, lambda e, c: (gap_copies(e, lambda cp: cp.wait()), c)[1], 0)
        lax.fori_loop(first_free, n_blocks, lambda b, c: (zero_block(b).wait(), c)[1], 0)


def _dispatch(pieces, tile_rows, total, gaps, pos, x2d, n_rows):
    t, d = x2d.shape
    nt = t // ROUTE_TILE
    return pl.pallas_call(
        _dispatch_kernel,
        grid_spec=pltpu.PrefetchScalarGridSpec(
            num_scalar_prefetch=len(pieces) + 3,
            grid=(nt,),
            in_specs=[pl.BlockSpec((1, TOP_K, ROUTE_TILE), lambda i, *_: (i, 0, 0)),
                      pl.BlockSpec((ROUTE_TILE, d), lambda i, *_: (i, 0))],
            out_specs=pl.BlockSpec(memory_space=pl.ANY),
            scratch_shapes=[pltpu.VMEM((2, TILE_CAP // WORD_ROWS, d), jnp.int32),
                            pltpu.VMEM((EXPERT_ROWS // WORD_ROWS, d), jnp.int32),
                            pltpu.SemaphoreType.DMA((2,)),
                            pltpu.SemaphoreType.DMA(())],
        ),
        out_shape=jax.ShapeDtypeStruct((n_rows // WORD_ROWS, d), jnp.int32),
        compiler_params=_cparams(("arbitrary",)),
        name="dispatch",
    )(*pieces, tile_rows, total, gaps, pos, x2d)


def _experts_kernel(ie_ref, flag_ref, xs_ref, wg_ref, wu_ref, wd_ref, y_ref, wg_bf, wu_bf, wd_bf):
    w = pl.program_id(0)
    flags = flag_ref[w]
    valid = (flags & 1) != 0
    new_expert = (flags & 2) != 0

    @pl.when(new_expert)
    def _():
        wg_bf[...] = wg_ref[0].astype(BF16)
        wu_bf[...] = wu_ref[0].astype(BF16)
        wd_bf[...] = wd_ref[0].astype(BF16)

    @pl.when(valid)
    def _():
        sub = EXPERT_ROWS // WORD_ROWS // EXPERT_SPLIT
        groups = [slice(s * sub, (s + 1) * sub) for s in range(EXPERT_SPLIT)]
        xs = [pltpu.bitcast(xs_ref[g, :], BF16) for g in groups]
        gu = [(jnp.dot(x, wg_bf[...], preferred_element_type=F32),
               jnp.dot(x, wu_bf[...], preferred_element_type=F32)) for x in xs]
        hs = [(_silu(gte) * up).astype(BF16) for gte, up in gu]
        for g, h in zip(groups, hs):
            y = jnp.dot(h, wd_bf[...], preferred_element_type=F32)
            y_ref[g, :] = pltpu.bitcast(y.astype(BF16), jnp.int32)


def _experts(block_expert, block_flags, xs, wg, wu, wd):
    n, d = xs.shape
    ff = wg.shape[-1]
    rows_block = lambda w, ie, fl: (jnp.minimum(w, fl[fl.shape[0] - 1]), 0)
    return pl.pallas_call(
        _experts_kernel,
        grid_spec=pltpu.PrefetchScalarGridSpec(
            num_scalar_prefetch=2,
            grid=(block_expert.shape[0],),
            in_specs=[
                pl.BlockSpec((EXPERT_ROWS // WORD_ROWS, d), rows_block),
                pl.BlockSpec((1, d, ff), lambda w, ie, fl: (ie[w], 0, 0)),
                pl.BlockSpec((1, d, ff), lambda w, ie, fl: (ie[w], 0, 0)),
                pl.BlockSpec((1, ff, d), lambda w, ie, fl: (ie[w], 0, 0)),
            ],
            out_specs=pl.BlockSpec((EXPERT_ROWS // WORD_ROWS, d), rows_block),
            scratch_shapes=[pltpu.VMEM((d, ff), BF16), pltpu.VMEM((d, ff), BF16), pltpu.VMEM((ff, d), BF16)],
        ),
        out_shape=jax.ShapeDtypeStruct((n, d), jnp.int32),
        input_output_aliases={2: 0},
        compiler_params=_cparams(("arbitrary",)),
        name="experts",
    )(block_expert, block_flags, xs, wg, wu, wd)


def _block_table(seg_rows, n_rows):
    block_end = jnp.cumsum(seg_rows // EXPERT_ROWS)
    w = jnp.arange(n_rows // EXPERT_ROWS, dtype=jnp.int32)
    valid = w < block_end[-1]
    e = jnp.sum((block_end[None, :] <= jnp.minimum(w, block_end[-1] - 1)[:, None]).astype(jnp.int32), axis=1)
    e = jnp.minimum(e, N_EXPERTS - 1)
    prev_e = jnp.concatenate([jnp.full((1,), -1, jnp.int32), e[:-1]])
    flags = valid.astype(jnp.int32) + 2 * (valid & (e != prev_e)).astype(jnp.int32)
    return e, jnp.concatenate([flags, block_end[-1:].astype(jnp.int32) - 1])


def _combine_kernel(*refs):
    tables = refs[:N_PIECE_TABLES]
    (rows_ref, pos_ref, gate_ref, x_ref, sg_ref, su_ref, sd_ref, g_ref, beta_ref, yb_ref, o_ref,
     stage, acc, lane_tile, lane_gate, sem) = refs[N_PIECE_TABLES:]
    i = pl.program_id(0)
    nt = pl.num_programs(0)
    slot = i % 2
    tn = ROUTE_TILE

    def start_tile(tile, buf):
        def start_piece(local, src, n):
            pltpu.make_async_copy(yb_ref.at[_word_rows(src, n), :], stage.at[buf, _word_rows(local, n), :],
                                  sem.at[buf]).start()
        _for_each_piece(tables, tile, start_piece)

    @pl.when(i == 0)
    def _():
        stage[...] = jnp.zeros_like(stage)
        start_tile(0, 0)

    @pl.when(i + 1 < nt)
    def _():
        start_tile(i + 1, 1 - slot)

    x = x_ref[...]
    xb = x.astype(BF16)
    gte = jnp.dot(xb, sg_ref[...], preferred_element_type=F32)
    up = jnp.dot(xb, su_ref[...], preferred_element_type=F32)
    h = (_silu(gte) * up).astype(BF16)
    acc[...] = jnp.dot(h, sd_ref[...], preferred_element_type=F32)

    def wait_rows(n):
        pltpu.make_async_copy(yb_ref.at[_word_rows(0, n), :], stage.at[slot, _word_rows(0, n), :],
                              sem.at[slot]).wait()

    _for_each_total_piece(rows_ref[i], wait_rows)

    lanes = LANES
    lane_iota = lax.broadcasted_iota(jnp.int32, (tn, lanes), 1)
    for k in range(TOP_K):
        p = jnp.broadcast_to(pos_ref[:, k:k + 1], (tn, lanes))
        g = jnp.broadcast_to(gate_ref[:, k:k + 1], (tn, lanes))
        lane_tile[k] = (p >> (lanes.bit_length() - 1)).astype(F32).astype(BF16)
        lane_gate[k] = jnp.where((p & (lanes - 1)) == lane_iota, g, 0.0).astype(BF16)
    zero = jnp.zeros((tn, lanes), BF16)

    def gate_matrix(c):
        cols = []
        for j in range(COMBINE_CHUNK // lanes):
            tile = jnp.asarray(c * (COMBINE_CHUNK // lanes) + j, jnp.int32).astype(F32).astype(BF16)
            w = zero
            for k in range(TOP_K):
                w = w + jnp.where(lane_tile[k] == tile, lane_gate[k], zero)
            cols.append(w)
        return jnp.concatenate(cols, axis=1)

    def weighed(c):
        p0 = pl.multiple_of(c * COMBINE_CHUNK, COMBINE_CHUNK)
        rows = pltpu.bitcast(stage[slot, _word_rows(p0, COMBINE_CHUNK), :], BF16)
        return jnp.dot(gate_matrix(c), rows, preferred_element_type=F32)

    def weigh_chunks(c2, carry):
        acc[...] += weighed(2 * c2) + weighed(2 * c2 + 1)
        return carry

    n_chunks = (rows_ref[i] + COMBINE_CHUNK - 1) // COMBINE_CHUNK
    lax.fori_loop(0, n_chunks // 2, weigh_chunks, 0)

    @pl.when(n_chunks % 2 == 1)
    def _():
        acc[...] += weighed(n_chunks - 1)
    o_ref[...] = _layer_norm(ALPHA * x + acc[...], g_ref[...], beta_ref[...])


def _combine(pieces, tile_rows, pos_t, gate_t, x2d, sg, su, sd, g, beta, yb):
    t, d = x2d.shape
    nt = t // ROUTE_TILE
    ff = sg.shape[-1]
    row = lambda i, *_: (i, 0)
    const = lambda i, *_: (0, 0)
    return pl.pallas_call(
        _combine_kernel,
        grid_spec=pltpu.PrefetchScalarGridSpec(
            num_scalar_prefetch=len(pieces) + 1,
            grid=(nt,),
            in_specs=[
                pl.BlockSpec((ROUTE_TILE, TOP_K), row),
                pl.BlockSpec((ROUTE_TILE, TOP_K), row),
                pl.BlockSpec((ROUTE_TILE, d), row),
                pl.BlockSpec((d, ff), const),
                pl.BlockSpec((d, ff), const),
                pl.BlockSpec((ff, d), const),
                pl.BlockSpec((1, d), const),
                pl.BlockSpec((1, d), const),
                pl.BlockSpec(memory_space=pl.ANY),
            ],
            out_specs=pl.BlockSpec((ROUTE_TILE, d), row),
            scratch_shapes=[pltpu.VMEM((2, TILE_CAP // WORD_ROWS, d), jnp.int32),
                            pltpu.VMEM((ROUTE_TILE, d), F32),
                            pltpu.VMEM((TOP_K, ROUTE_TILE, LANES), BF16),
                            pltpu.VMEM((TOP_K, ROUTE_TILE, LANES), BF16),
                            pltpu.SemaphoreType.DMA((2,))],
        ),
        out_shape=jax.ShapeDtypeStruct((t, d), F32),
        compiler_params=_cparams(("arbitrary",)),
        name="combine",
    )(*pieces, tile_rows, pos_t, gate_t, x2d, sg, su, sd, g, beta, yb)


def _piece_tables(cnt):
    lbase = jnp.cumsum(cnt, axis=1) - cnt
    per_expert = jnp.sum(cnt, axis=0)
    seg_rows = -(-per_expert // EXPERT_ROWS) * EXPERT_ROWS
    seg_start = jnp.cumsum(seg_rows) - seg_rows
    gbase = seg_start[None, :] + jnp.cumsum(cnt, axis=0) - cnt
    gaps = jnp.concatenate([seg_start + per_expert, seg_rows - per_expert]).astype(jnp.int32)

    def listed(n_pieces, start_off, piece_rows, length):
        last = jnp.cumsum(n_pieces, axis=1)
        first = last - n_pieces
        j = jnp.arange(length, dtype=jnp.int32)[None, :, None]
        own = (j >= first[:, None, :]) & (j < last[:, None, :])
        off = (start_off[:, None, :] + (j - first[:, None, :]) * piece_rows)
        pick = lambda base: jnp.sum(jnp.where(own, base[:, None, :] + off, 0), axis=-1).reshape(-1).astype(jnp.int32)
        return pick(lbase), pick(gbase), last[:, -1].astype(jnp.int32)

    tables = listed(cnt // BIG_PIECE, jnp.zeros_like(cnt), BIG_PIECE, BIG_PER_TILE)
    for size in SMALL_PIECES:
        tables += listed((cnt // size) % 2, cnt // (2 * size) * (2 * size), size, N_EXPERTS)
    return tables, jnp.sum(cnt, axis=1).astype(jnp.int32), seg_rows, gaps


def kernel(x, mem, w_in, b_in, conv_w, conv_b, conv_ln_g, conv_ln_b, attn_sinks, rel_bias, w_out, b_out, ln1_g, ln1_b, xq_w, xkv_w, xo_w, ln2_g, ln2_b, router_w, router_b, exp_gate, exp_up, exp_down, sh_gate, sh_up, sh_down, ln3_g, ln3_b):
    bsz, seq, d = x.shape
    t = bsz * seq
    bias_tab, sink_tab = _band_tables(rel_bias, attn_sinks[0])
    tri = (jnp.arange(ROUTE_TILE)[:, None] <= jnp.arange(ROUTE_TILE)[None, :]).astype(BF16)
    row = lambda p: p.reshape(1, -1)
    for l in range(DEPTH):
        conv_out, q, k, v = _mix(x, w_in[l].astype(BF16), row(b_in[l]), conv_w[l], row(conv_b[l]),
                                 row(conv_ln_g[l]), row(conv_ln_b[l]))
        x1 = _swa(q, k, v, bias_tab, sink_tab, x, conv_out, w_out[l].astype(BF16), row(b_out[l]),
                  row(ln1_g[l]), row(ln1_b[l]))
        x2, pos, gate, cnt = _cross(x1, mem, xkv_w[l].astype(BF16), xq_w[l].astype(BF16), xo_w[l].astype(BF16),
                                    row(ln2_g[l]), row(ln2_b[l]), router_w[l].T.astype(BF16),
                                    router_b[l].reshape(-1, 1), tri)
        x2 = x2.reshape(t, d)
        nt = t // ROUTE_TILE
        n_rows = (-(-(t * TOP_K + nt * N_EXPERTS * (RUN_ALIGN - 1)) // EXPERT_ROWS) + N_EXPERTS) * EXPERT_ROWS
        pieces, tile_rows, seg_rows, gaps = _piece_tables(cnt[:, :, 0])
        block_expert, block_flags = _block_table(seg_rows, n_rows)
        xs = _dispatch(pieces, tile_rows, jnp.sum(seg_rows).reshape(1).astype(jnp.int32), gaps, pos, x2, n_rows)
        yb = _experts(block_expert, block_flags, xs, exp_gate[l], exp_up[l], exp_down[l])
        pos_t = jnp.transpose(pos, (0, 2, 1)).reshape(t, TOP_K)
        gate_t = jnp.transpose(gate, (0, 2, 1)).reshape(t, TOP_K)
        x = _combine(pieces, tile_rows, pos_t, gate_t, x2, sh_gate[l].astype(BF16),
                     sh_up[l].astype(BF16), sh_down[l].astype(BF16), row(ln3_g[l]), row(ln3_b[l]),
                     yb).reshape(bsz, seq, d)
    return x
```

```python
import math

import jax
import jax.numpy as jnp
from jax import lax
from jax.experimental import pallas as pl
from jax.experimental.pallas import tpu as pltpu

D_MODEL = 1024
HEAD_DIM = 64
CONV_CH = D_MODEL // 2
CONV_WIDTH = 31
ATT_HEADS = 8
KV_HEADS = 2
WINDOW = 128
BLOCK = 128
REL_BUCKETS = 32
REL_MAX_DIST = 128
Q_COLS = ATT_HEADS * HEAD_DIM
KV_COLS = KV_HEADS * HEAD_DIM
IN_COLS = 2 * CONV_CH + Q_COLS + 2 * KV_COLS
X_HEADS = 4
X_HEAD_DIM = D_MODEL // X_HEADS
N_EXPERTS = 64
TOP_K = 8
N_GROUPS = 8
GROUP_SIZE = N_EXPERTS // N_GROUPS
TOPK_GROUPS = 4
EXPERT_FF = D_MODEL // 4
ROUTED_SCALE = 2.5
DEPTH = 1
ALPHA = (2 * DEPTH) ** 0.25
LN_EPS = 1e-5
NEG_INF = -1e30

F32 = jnp.float32
BF16 = jnp.bfloat16

VMEM_LIMIT_BYTES = 56 * 1024 * 1024

ROW_TILE = 512
CONV_ROWS = 256
SUBLANES = 8
LANES = 128
CONV_HALO = 32
CROSS_TILE = 1024
CROSS_SPLIT = 2
SWA_TILE = 512
ROUTE_TILE = 512
EXPERT_ROWS = 1024
EXPERT_SPLIT = 2
RUN_ALIGN = 16
RUN_ALIGN_LOG2 = RUN_ALIGN.bit_length() - 1
WORD_ROWS = 2
TILE_CAP = ROUTE_TILE * TOP_K + N_EXPERTS * RUN_ALIGN
BIG_PIECE = 32
BIG_PER_TILE = TILE_CAP // BIG_PIECE
SMALL_PIECES = tuple(1 << b for b in range(BIG_PIECE.bit_length() - 2, RUN_ALIGN_LOG2 - 1, -1))
N_PIECE_TABLES = 3 * (1 + len(SMALL_PIECES))
PIECE_UNROLL = 8
TOTAL_PIECES = tuple(1 << b for b in range(TILE_CAP.bit_length() - 1, RUN_ALIGN_LOG2 - 1, -1))
SORT_CHUNK = 256
COMBINE_CHUNK = 512


def _cparams(sem):
    return pltpu.CompilerParams(dimension_semantics=sem, vmem_limit_bytes=VMEM_LIMIT_BYTES)


def _layer_norm(h, g, b):
    mu = jnp.mean(h, axis=-1, keepdims=True)
    d = h - mu
    var = jnp.mean(d * d, axis=-1, keepdims=True)
    return d * lax.rsqrt(var + LN_EPS) * g + b


def _sigmoid(x):
    return 1.0 / (1.0 + jnp.exp(-x))


def _silu(x):
    return x * (0.5 * jnp.tanh(0.5 * x) + 0.5)


def _mix_kernel(x_ref, w_ref, b_ref, cw_ref, cb_ref, cg_ref, cbeta_ref,
                conv_ref, q_ref, k_ref, v_ref, u_ext, u_sh):
    j = pl.program_id(1)
    xb = x_ref[0].astype(BF16)
    proj = jnp.dot(xb, w_ref[...], preferred_element_type=F32) + b_ref[...]
    a = proj[:, :CONV_CH]
    g = proj[:, CONV_CH:2 * CONV_CH]
    q_ref[0] = (proj[:, 2 * CONV_CH:2 * CONV_CH + Q_COLS] * (HEAD_DIM ** -0.5)).astype(BF16)
    k_ref[0] = proj[:, 2 * CONV_CH + Q_COLS:2 * CONV_CH + Q_COLS + KV_COLS].astype(BF16)
    v_ref[0] = proj[:, 2 * CONV_CH + Q_COLS + KV_COLS:].astype(BF16)

    @pl.when(j == 0)
    def _():
        u_ext[0:CONV_HALO, :] = jnp.zeros((CONV_HALO, CONV_CH), F32)

    u_ext[CONV_HALO:CONV_HALO + ROW_TILE, :] = a * _sigmoid(g)

    first_tap = CONV_HALO - (CONV_WIDTH - 1)
    shifted_rows = u_sh.shape[1]
    for r in range(1, SUBLANES):
        u_sh[r - 1] = u_ext[r:r + shifted_rows, :]

    for c in range(ROW_TILE // CONV_ROWS):
        acc = jnp.zeros((CONV_ROWS, CONV_CH), F32) + cb_ref[...]
        for t in range(CONV_WIDTH):
            r = (first_tap + t) % SUBLANES
            base = c * CONV_ROWS + (first_tap + t) - r
            if r == 0:
                taps = u_ext[base:base + CONV_ROWS, :]
            else:
                taps = u_sh[r - 1, base:base + CONV_ROWS, :]
            acc = acc + taps * cw_ref[t:t + 1, :]
        y = _layer_norm(acc, cg_ref[...], cbeta_ref[...])
        conv_ref[0, c * CONV_ROWS:(c + 1) * CONV_ROWS, :] = (y * _sigmoid(y)).astype(BF16)

    u_ext[0:CONV_HALO, :] = u_ext[ROW_TILE:ROW_TILE + CONV_HALO, :]


def _mix(x, w_in, b_in, conv_w, conv_b, conv_g, conv_beta):
    bsz, seq, d = x.shape
    nt = seq // ROW_TILE
    row = lambda b, j: (b, j, 0)
    const2 = lambda b, j: (0, 0)
    return pl.pallas_call(
        _mix_kernel,
        grid=(bsz, nt),
        in_specs=[
            pl.BlockSpec((1, ROW_TILE, d), row),
            pl.BlockSpec((d, IN_COLS), const2),
            pl.BlockSpec((1, IN_COLS), const2),
            pl.BlockSpec((CONV_WIDTH, CONV_CH), const2),
            pl.BlockSpec((1, CONV_CH), const2),
            pl.BlockSpec((1, CONV_CH), const2),
            pl.BlockSpec((1, CONV_CH), const2),
        ],
        out_specs=[
            pl.BlockSpec((1, ROW_TILE, CONV_CH), row),
            pl.BlockSpec((1, ROW_TILE, Q_COLS), row),
            pl.BlockSpec((1, ROW_TILE, KV_COLS), row),
            pl.BlockSpec((1, ROW_TILE, KV_COLS), row),
        ],
        out_shape=[
            jax.ShapeDtypeStruct((bsz, seq, CONV_CH), BF16),
            jax.ShapeDtypeStruct((bsz, seq, Q_COLS), BF16),
            jax.ShapeDtypeStruct((bsz, seq, KV_COLS), BF16),
            jax.ShapeDtypeStruct((bsz, seq, KV_COLS), BF16),
        ],
        scratch_shapes=[pltpu.VMEM((ROW_TILE + CONV_HALO, CONV_CH), F32),
                        pltpu.VMEM((SUBLANES - 1, ROW_TILE + CONV_HALO - SUBLANES, CONV_CH), F32)],
        compiler_params=_cparams(("arbitrary", "arbitrary")),
        name="mix",
    )(x, w_in, b_in, conv_w, conv_b, conv_g, conv_beta)


def _swa_kernel(q_ref, kp_ref, kc_ref, vp_ref, vc_ref, bias_ref, sink_ref, x_ref, conv_ref, w_ref, b_ref,
                g_ref, beta_ref, o_ref):
    j = pl.program_id(1)
    mix = jnp.dot(conv_ref[0], w_ref[0:CONV_CH, :], preferred_element_type=F32)
    rows = BLOCK + SWA_TILE
    lane = lax.broadcasted_iota(jnp.int32, (rows, 2 * HEAD_DIM), 1)
    low = lane < HEAD_DIM

    def placements(prev_ref, cur_ref):
        t = jnp.concatenate([prev_ref[0], cur_ref[0]], axis=0).astype(F32)
        tr = pltpu.roll(t, HEAD_DIM, 1)
        zero = jnp.zeros_like(t)
        kv0_low = jnp.where(low, t, zero).astype(BF16)
        kv1_high = jnp.where(low, zero, t).astype(BF16)
        kv1_low = jnp.where(low, tr, zero).astype(BF16)
        kv0_high = jnp.where(low, zero, tr).astype(BF16)
        return (kv0_low, kv0_high, kv1_low, kv1_high)

    ks = placements(kp_ref, kc_ref)
    vs = placements(vp_ref, vc_ref)
    slab = 2 * HEAD_DIM
    att = []
    for i in range(SWA_TILE // BLOCK):
        q = q_ref[0, i * BLOCK:(i + 1) * BLOCK, :]
        q_kv0 = jnp.concatenate([q[:, 0:slab], q[:, slab:2 * slab]], axis=0)
        q_kv1 = jnp.concatenate([q[:, 2 * slab:3 * slab], q[:, 3 * slab:4 * slab]], axis=0)
        band = slice(i * BLOCK, (i + 2) * BLOCK)
        logits = []
        for s in range(4):
            qs = q_kv0 if s < 2 else q_kv1
            bias = bias_ref[1, s]
            if i == 0:
                bias = jnp.where(j == 0, bias_ref[0, s], bias)
            logits.append(lax.dot_general(qs, ks[s][band], (((1,), (1,)), ((), ())),
                                          preferred_element_type=F32) + bias)
        ps, dens = [], []
        for s in range(4):
            sink = sink_ref[s]
            m = jnp.maximum(jnp.max(logits[s], axis=-1, keepdims=True), sink)
            p = jnp.exp(logits[s] - m)
            ps.append(p.astype(BF16))
            dens.append(jnp.sum(p, axis=-1, keepdims=True) + jnp.exp(sink - m))
        outs = [jnp.dot(ps[s], vs[s][band], preferred_element_type=F32) / dens[s] for s in range(4)]
        o_kv0 = outs[0] + outs[1]
        o_kv1 = outs[2] + outs[3]
        att.append(jnp.concatenate([o_kv0[0:BLOCK], o_kv0[BLOCK:2 * BLOCK], o_kv1[0:BLOCK],
                                    o_kv1[BLOCK:2 * BLOCK]], axis=1).astype(BF16))

    mix = mix + jnp.dot(jnp.concatenate(att, axis=0), w_ref[CONV_CH:, :], preferred_element_type=F32)
    h = ALPHA * x_ref[0] + mix + b_ref[...]
    o_ref[0] = _layer_norm(h, g_ref[...], beta_ref[...])


def _swa(q, k, v, bias_tab, sink_tab, x, conv_out, w_out, b_out, g, beta):
    bsz, seq, d = x.shape
    per = SWA_TILE // BLOCK
    cur = lambda b, n: (b, n, 0)
    prev = lambda b, n: (b, jnp.maximum(n * per - 1, 0), 0)
    whole = lambda b, n: (0, 0, 0, 0)
    const = lambda b, n: (0, 0)
    return pl.pallas_call(
        _swa_kernel,
        grid=(bsz, seq // SWA_TILE),
        in_specs=[
            pl.BlockSpec((1, SWA_TILE, Q_COLS), cur),
            pl.BlockSpec((1, BLOCK, KV_COLS), prev),
            pl.BlockSpec((1, SWA_TILE, KV_COLS), cur),
            pl.BlockSpec((1, BLOCK, KV_COLS), prev),
            pl.BlockSpec((1, SWA_TILE, KV_COLS), cur),
            pl.BlockSpec((2, 4, 2 * BLOCK, 2 * BLOCK), whole),
            pl.BlockSpec((4, 2 * BLOCK, 1), lambda b, n: (0, 0, 0)),
            pl.BlockSpec((1, SWA_TILE, d), cur),
            pl.BlockSpec((1, SWA_TILE, CONV_CH), cur),
            pl.BlockSpec((d, d), const),
            pl.BlockSpec((1, d), const),
            pl.BlockSpec((1, d), const),
            pl.BlockSpec((1, d), const),
        ],
        out_specs=pl.BlockSpec((1, SWA_TILE, d), cur),
        out_shape=jax.ShapeDtypeStruct((bsz, seq, d), F32),
        compiler_params=_cparams(("arbitrary", "arbitrary")),
        name="swa",
    )(q, k, k, v, v, bias_tab, sink_tab, x, conv_out, w_out, b_out, g, beta)


def _t5_bucket(dist):
    n = jnp.maximum(dist, 0)
    exact = REL_BUCKETS // 2
    large = exact + (jnp.log(jnp.maximum(n, 1).astype(F32) / exact)
                     / math.log(REL_MAX_DIST / exact) * (REL_BUCKETS - exact)).astype(jnp.int32)
    large = jnp.minimum(large, REL_BUCKETS - 1)
    return jnp.where(n < exact, n, large)


def _band_tables(rel_bias, sinks):
    qi = jnp.arange(BLOCK)[:, None]
    kj = jnp.arange(2 * BLOCK)[None, :]
    dist = qi + BLOCK - kj
    bucket = _t5_bucket(dist)
    bias = jnp.zeros((ATT_HEADS, BLOCK, 2 * BLOCK), F32)
    for bkt in range(REL_BUCKETS):
        bias = jnp.where(bucket[None] == bkt, rel_bias[bkt].astype(F32)[:, None, None], bias)
    in_window = (dist >= 0) & (dist < WINDOW)
    masks = jnp.stack([in_window & (kj >= BLOCK), in_window])
    masked = jnp.where(masks[:, None], bias[None], NEG_INF)
    pairs = ((0, 2), (1, 3), (4, 6), (5, 7))
    bias_tab = jnp.stack([jnp.concatenate([masked[:, a], masked[:, b]], axis=1) for a, b in pairs], axis=1)
    s = sinks.astype(F32)
    sink_tab = jnp.stack([jnp.concatenate([jnp.full((BLOCK, 1), s[a]), jnp.full((BLOCK, 1), s[b])], axis=0)
                          for a, b in pairs])
    return bias_tab, sink_tab


def _cross_kernel(x_ref, mem_ref, wkv_ref, wq_ref, wo_ref, g_ref, beta_ref, rw_ref, rb_ref, tri_ref,
                  o_ref, pos_ref, gate_ref, cnt_ref, k_ref, v_ref):
    @pl.when(pl.program_id(1) == 0)
    def _():
        kv = jnp.dot(mem_ref[0].astype(BF16), wkv_ref[...], preferred_element_type=F32)
        k_ref[...] = kv[:, :D_MODEL].astype(BF16)
        v_ref[...] = kv[:, D_MODEL:].astype(BF16)

    sub = CROSS_TILE // CROSS_SPLIT
    groups = [slice(s * sub, (s + 1) * sub) for s in range(CROSS_SPLIT)]
    xs = [x_ref[0, g, :] for g in groups]
    qs = [(jnp.dot(x.astype(BF16), wq_ref[...], preferred_element_type=F32) * (X_HEAD_DIM ** -0.5)).astype(BF16)
          for x in xs]
    heads = [[] for _ in groups]
    for h in range(X_HEADS):
        cols = slice(h * X_HEAD_DIM, (h + 1) * X_HEAD_DIM)
        logits = [lax.dot_general(q[:, cols], k_ref[:, cols], (((1,), (1,)), ((), ())),
                                  preferred_element_type=F32) for q in qs]
        ps, dens = [], []
        for lg in logits:
            p = jnp.exp(lg - jnp.max(lg, axis=-1, keepdims=True))
            ps.append(p)
            dens.append(jnp.sum(p, axis=-1, keepdims=True))
        for s, (p, den) in enumerate(zip(ps, dens)):
            o = jnp.dot(p.astype(BF16), v_ref[:, cols], preferred_element_type=F32)
            heads[s].append((o / den).astype(BF16))
    crosses = [jnp.dot(jnp.concatenate(hs, axis=-1), wo_ref[...], preferred_element_type=F32) for hs in heads]
    for s, (g, x, cross) in enumerate(zip(groups, xs, crosses)):
        y = _layer_norm(ALPHA * x + cross, g_ref[...], beta_ref[...])
        o_ref[0, g, :] = y
        logits_t = lax.dot_general(rw_ref[...], y.astype(BF16), (((1,), (1,)), ((), ())),
                                   preferred_element_type=F32)
        positions, gates, run_len = _route_tile(logits_t, rb_ref[...], tri_ref[...])
        cnt_ref[s] = run_len
        for r in range(TOP_K):
            pos_ref[s, r:r + 1, :] = positions[r]
            gate_ref[s, r:r + 1, :] = gates[r]


def _cross(x1, mem_in, wkv, wq, wo, g, beta, rw_t, router_b, tri):
    bsz, seq, d = x1.shape
    mem_len = mem_in.shape[1]
    nt = seq // CROSS_TILE
    assert CROSS_TILE // CROSS_SPLIT == ROUTE_TILE
    n_tiles = bsz * seq // ROUTE_TILE
    row = lambda b, j: (b, j, 0)
    mem = lambda b, j: (b, 0, 0)
    const = lambda b, j: (0, 0)
    tiles = lambda b, j: (b * nt + j, 0, 0)
    return pl.pallas_call(
        _cross_kernel,
        grid=(bsz, nt),
        in_specs=[
            pl.BlockSpec((1, CROSS_TILE, d), row),
            pl.BlockSpec((1, mem_len, d), mem),
            pl.BlockSpec((d, 2 * d), const),
            pl.BlockSpec((d, d), const),
            pl.BlockSpec((d, d), const),
            pl.BlockSpec((1, d), const),
            pl.BlockSpec((1, d), const),
            pl.BlockSpec((N_EXPERTS, d), const),
            pl.BlockSpec((N_EXPERTS, 1), const),
            pl.BlockSpec((ROUTE_TILE, ROUTE_TILE), const),
        ],
        out_specs=[
            pl.BlockSpec((1, CROSS_TILE, d), row),
            pl.BlockSpec((CROSS_SPLIT, TOP_K, ROUTE_TILE), tiles),
            pl.BlockSpec((CROSS_SPLIT, TOP_K, ROUTE_TILE), tiles),
            pl.BlockSpec((CROSS_SPLIT, N_EXPERTS, LANES), tiles),
        ],
        out_shape=[
            jax.ShapeDtypeStruct((bsz, seq, d), F32),
            jax.ShapeDtypeStruct((n_tiles, TOP_K, ROUTE_TILE), jnp.int32),
            jax.ShapeDtypeStruct((n_tiles, TOP_K, ROUTE_TILE), F32),
            jax.ShapeDtypeStruct((n_tiles, N_EXPERTS, LANES), jnp.int32),
        ],
        scratch_shapes=[pltpu.VMEM((mem_len, d), BF16), pltpu.VMEM((mem_len, d), BF16)],
        compiler_params=_cparams(("arbitrary", "arbitrary")),
        name="cross",
    )(x1, mem_in, wkv, wq, wo, g, beta, rw_t, router_b, tri)


def _route_tile(logits_t, router_b, tri):
    tn = ROUTE_TILE
    scores = _sigmoid(logits_t)
    choice = scores + router_b

    gscore = []
    member = lax.broadcasted_iota(jnp.int32, (GROUP_SIZE, tn), 0).astype(F32)
    for g in range(N_GROUPS):
        c = choice[g * GROUP_SIZE:(g + 1) * GROUP_SIZE, :]
        m1 = jnp.max(c, axis=0, keepdims=True)
        first = jnp.min(jnp.where(c == m1, member, float(GROUP_SIZE)), axis=0, keepdims=True)
        m2 = jnp.max(jnp.where(member == first, -jnp.inf, c), axis=0, keepdims=True)
        gscore.append(m1 + m2)

    keep_rows = []
    for g in range(N_GROUPS):
        beaten = jnp.zeros((1, tn), F32)
        for o in range(N_GROUPS):
            if o == g:
                continue
            ahead = (gscore[o] >= gscore[g]) if o < g else (gscore[o] > gscore[g])
            beaten = beaten + jnp.where(ahead, 1.0, 0.0)
        keep_rows.append(jnp.broadcast_to(beaten, (GROUP_SIZE, tn)))
    beaten_all = jnp.concatenate(keep_rows, axis=0)

    masked = jnp.where(beaten_all < TOPK_GROUPS, choice, -jnp.inf)
    eidx = lax.broadcasted_iota(jnp.int32, (N_EXPERTS, tn), 0).astype(F32)
    sel = jnp.zeros((N_EXPERTS, tn), F32)
    picks, weights = [], []
    for r in range(TOP_K):
        mx = jnp.max(masked, axis=0, keepdims=True)
        first = jnp.min(jnp.where(masked == mx, eidx, float(N_EXPERTS)), axis=0, keepdims=True)
        pick = eidx == first
        picks.append((pick, first))
        weights.append(jnp.sum(jnp.where(pick, scores, 0.0), axis=0, keepdims=True))
        masked = jnp.where(pick, -jnp.inf, masked)
        sel = jnp.where(pick, 1.0, sel)

    wsum = weights[0]
    for r in range(1, TOP_K):
        wsum = wsum + weights[r]

    count = jnp.sum(sel, axis=1, keepdims=True)
    run_len = jnp.floor((count + (RUN_ALIGN - 1.0)) * (1.0 / RUN_ALIGN)) * RUN_ALIGN
    run_len_b = jnp.broadcast_to(run_len, (N_EXPERTS, LANES))
    er = lax.broadcasted_iota(jnp.int32, (N_EXPERTS, N_EXPERTS), 0)
    ec = lax.broadcasted_iota(jnp.int32, (N_EXPERTS, N_EXPERTS), 1)
    before = jnp.where(ec < er, 1.0, 0.0).astype(BF16)
    run_start = jnp.dot(before, run_len_b.astype(BF16), preferred_element_type=F32)[:, 0:1]
    incl = jnp.dot(sel.astype(BF16), tri, preferred_element_type=F32)
    pos_mat = run_start + incl - sel
    positions = [jnp.sum(jnp.where(pick, pos_mat, 0.0), axis=0, keepdims=True).astype(jnp.int32)
                 for pick, _ in picks]
    gates = [w / wsum * ROUTED_SCALE for w in weights]
    return positions, gates, run_len_b.astype(jnp.int32)


def _for_each_piece(tables, tile, fn):
    lists = [(BIG_PIECE, BIG_PER_TILE)] + [(size, N_EXPERTS) for size in SMALL_PIECES]
    for k, (n_rows, per_tile) in enumerate(lists):
        local_ref, global_ref, count_ref = tables[3 * k:3 * k + 3]
        count = count_ref[tile]

        def body(g, carry, n_rows=n_rows, per_tile=per_tile, local_ref=local_ref, global_ref=global_ref,
                 count=count):
            for u in range(PIECE_UNROLL):
                j = g * PIECE_UNROLL + u

                @pl.when(j < count)
                def _():
                    fn(pl.multiple_of(local_ref[tile * per_tile + j], RUN_ALIGN),
                       pl.multiple_of(global_ref[tile * per_tile + j], RUN_ALIGN), n_rows)
            return carry

        lax.fori_loop(0, (count + PIECE_UNROLL - 1) // PIECE_UNROLL, body, 0)


def _word_rows(start, size):
    if isinstance(start, int):
        first = start // WORD_ROWS
    else:
        first = pl.multiple_of(start >> (WORD_ROWS.bit_length() - 1), RUN_ALIGN // WORD_ROWS)
    return pl.ds(first, size // WORD_ROWS)


def _for_each_total_piece(total, fn):
    for piece in TOTAL_PIECES:
        @pl.when((total & piece) != 0)
        def _():
            fn(piece)


def _dispatch_kernel(*refs):
    tables = refs[:N_PIECE_TABLES]
    rows_ref, total_ref, gap_ref, pos_ref, x_ref, xs_ref, stage, zeros, sem, zsem = refs[N_PIECE_TABLES:]
    i = pl.program_id(0)
    nt = pl.num_programs(0)
    slot = i % 2
    tn = ROUTE_TILE

    def drain(tile, buf):
        def wait_rows(n):
            pltpu.make_async_copy(stage.at[buf, _word_rows(0, n), :], xs_ref.at[_word_rows(0, n), :],
                                  sem.at[buf]).wait()
        _for_each_total_piece(rows_ref[tile], wait_rows)

    @pl.when(i >= 2)
    def _():
        drain(i - 2, slot)

    xb = x_ref[...].astype(BF16)
    pos = pos_ref[0]
    row_iota = lax.broadcasted_iota(jnp.int32, (SORT_CHUNK, tn), 0).astype(F32).astype(BF16)
    one = jnp.ones((SORT_CHUNK, tn), BF16)
    zero = jnp.zeros((SORT_CHUNK, tn), BF16)

    def onehot_rows(c):
        rel = jnp.clip(pos - c * SORT_CHUNK, -1, SORT_CHUNK).astype(F32).astype(BF16)
        hit = rel[0:1, :] == row_iota
        for k in range(1, TOP_K):
            hit = hit | (rel[k:k + 1, :] == row_iota)
        return jnp.where(hit, one, zero)

    def sort_chunks(c2, carry):
        for c in (2 * c2, 2 * c2 + 1):
            p0 = pl.multiple_of(c * SORT_CHUNK, SORT_CHUNK)
            rows = jnp.dot(onehot_rows(c), xb, preferred_element_type=F32)
            stage[slot, _word_rows(p0, SORT_CHUNK), :] = pltpu.bitcast(rows.astype(BF16), jnp.int32)
        return carry

    lax.fori_loop(0, (rows_ref[i] + 2 * SORT_CHUNK - 1) // (2 * SORT_CHUNK), sort_chunks, 0)

    def start_piece(local, dst, n):
        pltpu.make_async_copy(stage.at[slot, _word_rows(local, n), :], xs_ref.at[_word_rows(dst, n), :],
                              sem.at[slot]).start()

    _for_each_piece(tables, i, start_piece)

    @pl.when(i == nt - 1)
    def _():
        @pl.when(nt >= 2)
        def _():
            drain(i - 1, 1 - slot)

        drain(i, slot)

        zeros[...] = jnp.zeros_like(zeros)

        def gap_copies(e, act):
            gap = gap_ref[N_EXPERTS + e]
            for piece in TOTAL_PIECES:
                if piece >= EXPERT_ROWS:
                    continue

                @pl.when((gap & piece) != 0)
                def _():
                    dst = gap_ref[e] + (gap & (-2 * piece))
                    act(pltpu.make_async_copy(zeros.at[_word_rows(0, piece), :],
                                              xs_ref.at[_word_rows(dst, piece), :], zsem))

        def zero_block(b):
            return pltpu.make_async_copy(zeros, xs_ref.at[_word_rows(b * EXPERT_ROWS, EXPERT_ROWS), :], zsem)

        first_free = total_ref[0] // EXPERT_ROWS
        n_blocks = xs_ref.shape[0] * WORD_ROWS // EXPERT_ROWS
        lax.fori_loop(0, N_EXPERTS, lambda e, c: (gap_copies(e, lambda cp: cp.start()), c)[1], 0)
        lax.fori_loop(first_free, n_blocks, lambda b, c: (zero_block(b).start(), c)[1], 0)
        lax.fori_loop(0, N_EXPERTS, lambda e, c: (gap_copies(e, lambda cp: cp.wait()), c)[1], 0)
        lax.fori_loop(first_free, n_blocks, lambda b, c: (zero_block(b).wait(), c)[1], 0)


def _dispatch(pieces, tile_rows, total, gaps, pos, x2d, n_rows):
    t, d = x2d.shape
    nt = t // ROUTE_TILE
    return pl.pallas_call(
        _dispatch_kernel,
        grid_spec=pltpu.PrefetchScalarGridSpec(
            num_scalar_prefetch=len(pieces) + 3,
            grid=(nt,),
            in_specs=[pl.BlockSpec((1, TOP_K, ROUTE_TILE), lambda i, *_: (i, 0, 0)),
                      pl.BlockSpec((ROUTE_TILE, d), lambda i, *_: (i, 0))],
            out_specs=pl.BlockSpec(memory_space=pl.ANY),
            scratch_shapes=[pltpu.VMEM((2, TILE_CAP // WORD_ROWS, d), jnp.int32),
                            pltpu.VMEM((EXPERT_ROWS // WORD_ROWS, d), jnp.int32),
                            pltpu.SemaphoreType.DMA((2,)),
                            pltpu.SemaphoreType.DMA(())],
        ),
        out_shape=jax.ShapeDtypeStruct((n_rows // WORD_ROWS, d), jnp.int32),
        compiler_params=_cparams(("arbitrary",)),
        name="dispatch",
    )(*pieces, tile_rows, total, gaps, pos, x2d)


def _experts_kernel(ie_ref, flag_ref, xs_ref, wg_ref, wu_ref, wd_ref, y_ref, wg_bf, wu_bf, wd_bf):
    w = pl.program_id(0)
    flags = flag_ref[w]
    valid = (flags & 1) != 0
    new_expert = (flags & 2) != 0

    @pl.when(new_expert)
    def _():
        wg_bf[...] = wg_ref[0].astype(BF16)
        wu_bf[...] = wu_ref[0].astype(BF16)
        wd_bf[...] = wd_ref[0].astype(BF16)

    @pl.when(valid)
    def _():
        sub = EXPERT_ROWS // WORD_ROWS // EXPERT_SPLIT
        groups = [slice(s * sub, (s + 1) * sub) for s in range(EXPERT_SPLIT)]
        xs = [pltpu.bitcast(xs_ref[g, :], BF16) for g in groups]
        gu = [(jnp.dot(x, wg_bf[...], preferred_element_type=F32),
               jnp.dot(x, wu_bf[...], preferred_element_type=F32)) for x in xs]
        hs = [(_silu(gte) * up).astype(BF16) for gte, up in gu]
        for g, h in zip(groups, hs):
            y = jnp.dot(h, wd_bf[...], preferred_element_type=F32)
            y_ref[g, :] = pltpu.bitcast(y.astype(BF16), jnp.int32)


def _experts(block_expert, block_flags, xs, wg, wu, wd):
    n, d = xs.shape
    ff = wg.shape[-1]
    rows_block = lambda w, ie, fl: (jnp.minimum(w, fl[fl.shape[0] - 1]), 0)
    return pl.pallas_call(
        _experts_kernel,
        grid_spec=pltpu.PrefetchScalarGridSpec(
            num_scalar_prefetch=2,
            grid=(block_expert.shape[0],),
            in_specs=[
                pl.BlockSpec((EXPERT_ROWS // WORD_ROWS, d), rows_block),
                pl.BlockSpec((1, d, ff), lambda w, ie, fl: (ie[w], 0, 0)),
                pl.BlockSpec((1, d, ff), lambda w, ie, fl: (ie[w], 0, 0)),
                pl.BlockSpec((1, ff, d), lambda w, ie, fl: (ie[w], 0, 0)),
            ],
            out_specs=pl.BlockSpec((EXPERT_ROWS // WORD_ROWS, d), rows_block),
            scratch_shapes=[pltpu.VMEM((d, ff), BF16), pltpu.VMEM((d, ff), BF16), pltpu.VMEM((ff, d), BF16)],
        ),
        out_shape=jax.ShapeDtypeStruct((n, d), jnp.int32),
        input_output_aliases={2: 0},
        compiler_params=_cparams(("arbitrary",)),
        name="experts",
    )(block_expert, block_flags, xs, wg, wu, wd)


def _block_table(seg_rows, n_rows):
    block_end = jnp.cumsum(seg_rows // EXPERT_ROWS)
    w = jnp.arange(n_rows // EXPERT_ROWS, dtype=jnp.int32)
    valid = w < block_end[-1]
    e = jnp.sum((block_end[None, :] <= jnp.minimum(w, block_end[-1] - 1)[:, None]).astype(jnp.int32), axis=1)
    e = jnp.minimum(e, N_EXPERTS - 1)
    prev_e = jnp.concatenate([jnp.full((1,), -1, jnp.int32), e[:-1]])
    flags = valid.astype(jnp.int32) + 2 * (valid & (e != prev_e)).astype(jnp.int32)
    return e, jnp.concatenate([flags, block_end[-1:].astype(jnp.int32) - 1])


def _combine_kernel(*refs):
    tables = refs[:N_PIECE_TABLES]
    (rows_ref, pos_ref, gate_ref, x_ref, sg_ref, su_ref, sd_ref, g_ref, beta_ref, yb_ref, o_ref,
     stage, acc, lane_tile, lane_gate, sem) = refs[N_PIECE_TABLES:]
    i = pl.program_id(0)
    nt = pl.num_programs(0)
    slot = i % 2
    tn = ROUTE_TILE

    def start_tile(tile, buf):
        def start_piece(local, src, n):
            pltpu.make_async_copy(yb_ref.at[_word_rows(src, n), :], stage.at[buf, _word_rows(local, n), :],
                                  sem.at[buf]).start()
        _for_each_piece(tables, tile, start_piece)

    @pl.when(i == 0)
    def _():
        stage[...] = jnp.zeros_like(stage)
        start_tile(0, 0)

    @pl.when(i + 1 < nt)
    def _():
        start_tile(i + 1, 1 - slot)

    x = x_ref[...]
    xb = x.astype(BF16)
    gte = jnp.dot(xb, sg_ref[...], preferred_element_type=F32)
    up = jnp.dot(xb, su_ref[...], preferred_element_type=F32)
    h = (_silu(gte) * up).astype(BF16)
    acc[...] = jnp.dot(h, sd_ref[...], preferred_element_type=F32)

    def wait_rows(n):
        pltpu.make_async_copy(yb_ref.at[_word_rows(0, n), :], stage.at[slot, _word_rows(0, n), :],
                              sem.at[slot]).wait()

    _for_each_total_piece(rows_ref[i], wait_rows)

    lanes = LANES
    lane_iota = lax.broadcasted_iota(jnp.int32, (tn, lanes), 1)
    for k in range(TOP_K):
        p = jnp.broadcast_to(pos_ref[:, k:k + 1], (tn, lanes))
        g = jnp.broadcast_to(gate_ref[:, k:k + 1], (tn, lanes))
        lane_tile[k] = (p >> (lanes.bit_length() - 1)).astype(F32).astype(BF16)
        lane_gate[k] = jnp.where((p & (lanes - 1)) == lane_iota, g, 0.0).astype(BF16)
    zero = jnp.zeros((tn, lanes), BF16)

    def gate_matrix(c):
        cols = []
        for j in range(COMBINE_CHUNK // lanes):
            tile = jnp.asarray(c * (COMBINE_CHUNK // lanes) + j, jnp.int32).astype(F32).astype(BF16)
            w = zero
            for k in range(TOP_K):
                w = w + jnp.where(lane_tile[k] == tile, lane_gate[k], zero)
            cols.append(w)
        return jnp.concatenate(cols, axis=1)

    def weighed(c):
        p0 = pl.multiple_of(c * COMBINE_CHUNK, COMBINE_CHUNK)
        rows = pltpu.bitcast(stage[slot, _word_rows(p0, COMBINE_CHUNK), :], BF16)
        return jnp.dot(gate_matrix(c), rows, preferred_element_type=F32)

    def weigh_chunks(c2, carry):
        acc[...] += weighed(2 * c2) + weighed(2 * c2 + 1)
        return carry

    n_chunks = (rows_ref[i] + COMBINE_CHUNK - 1) // COMBINE_CHUNK
    lax.fori_loop(0, n_chunks // 2, weigh_chunks, 0)

    @pl.when(n_chunks % 2 == 1)
    def _():
        acc[...] += weighed(n_chunks - 1)
    o_ref[...] = _layer_norm(ALPHA * x + acc[...], g_ref[...], beta_ref[...])


def _combine(pieces, tile_rows, pos_t, gate_t, x2d, sg, su, sd, g, beta, yb):
    t, d = x2d.shape
    nt = t // ROUTE_TILE
    ff = sg.shape[-1]
    row = lambda i, *_: (i, 0)
    const = lambda i, *_: (0, 0)
    return pl.pallas_call(
        _combine_kernel,
        grid_spec=pltpu.PrefetchScalarGridSpec(
            num_scalar_prefetch=len(pieces) + 1,
            grid=(nt,),
            in_specs=[
                pl.BlockSpec((ROUTE_TILE, TOP_K), row),
                pl.BlockSpec((ROUTE_TILE, TOP_K), row),
                pl.BlockSpec((ROUTE_TILE, d), row),
                pl.BlockSpec((d, ff), const),
                pl.BlockSpec((d, ff), const),
                pl.BlockSpec((ff, d), const),
                pl.BlockSpec((1, d), const),
                pl.BlockSpec((1, d), const),
                pl.BlockSpec(memory_space=pl.ANY),
            ],
            out_specs=pl.BlockSpec((ROUTE_TILE, d), row),
            scratch_shapes=[pltpu.VMEM((2, TILE_CAP // WORD_ROWS, d), jnp.int32),
                            pltpu.VMEM((ROUTE_TILE, d), F32),
                            pltpu.VMEM((TOP_K, ROUTE_TILE, LANES), BF16),
                            pltpu.VMEM((TOP_K, ROUTE_TILE, LANES), BF16),
                            pltpu.SemaphoreType.DMA((2,))],
        ),
        out_shape=jax.ShapeDtypeStruct((t, d), F32),
        compiler_params=_cparams(("arbitrary",)),
        name="combine",
    )(*pieces, tile_rows, pos_t, gate_t, x2d, sg, su, sd, g, beta, yb)


def _piece_tables(cnt):
    lbase = jnp.cumsum(cnt, axis=1) - cnt
    per_expert = jnp.sum(cnt, axis=0)
    seg_rows = -(-per_expert // EXPERT_ROWS) * EXPERT_ROWS
    seg_start = jnp.cumsum(seg_rows) - seg_rows
    gbase = seg_start[None, :] + jnp.cumsum(cnt, axis=0) - cnt
    gaps = jnp.concatenate([seg_start + per_expert, seg_rows - per_expert]).astype(jnp.int32)

    def listed(n_pieces, start_off, piece_rows, length):
        last = jnp.cumsum(n_pieces, axis=1)
        first = last - n_pieces
        j = jnp.arange(length, dtype=jnp.int32)[None, :, None]
        own = (j >= first[:, None, :]) & (j < last[:, None, :])
        off = (start_off[:, None, :] + (j - first[:, None, :]) * piece_rows)
        pick = lambda base: jnp.sum(jnp.where(own, base[:, None, :] + off, 0), axis=-1).reshape(-1).astype(jnp.int32)
        return pick(lbase), pick(gbase), last[:, -1].astype(jnp.int32)

    tables = listed(cnt // BIG_PIECE, jnp.zeros_like(cnt), BIG_PIECE, BIG_PER_TILE)
    for size in SMALL_PIECES:
        tables += listed((cnt // size) % 2, cnt // (2 * size) * (2 * size), size, N_EXPERTS)
    return tables, jnp.sum(cnt, axis=1).astype(jnp.int32), seg_rows, gaps


def kernel(x, mem, w_in, b_in, conv_w, conv_b, conv_ln_g, conv_ln_b, attn_sinks, rel_bias, w_out, b_out, ln1_g, ln1_b, xq_w, xkv_w, xo_w, ln2_g, ln2_b, router_w, router_b, exp_gate, exp_up, exp_down, sh_gate, sh_up, sh_down, ln3_g, ln3_b):
    bsz, seq, d = x.shape
    t = bsz * seq
    bias_tab, sink_tab = _band_tables(rel_bias, attn_sinks[0])
    tri = (jnp.arange(ROUTE_TILE)[:, None] <= jnp.arange(ROUTE_TILE)[None, :]).astype(BF16)
    row = lambda p: p.reshape(1, -1)
    for l in range(DEPTH):
        conv_out, q, k, v = _mix(x, w_in[l].astype(BF16), row(b_in[l]), conv_w[l], row(conv_b[l]),
                                 row(conv_ln_g[l]), row(conv_ln_b[l]))
        x1 = _swa(q, k, v, bias_tab, sink_tab, x, conv_out, w_out[l].astype(BF16), row(b_out[l]),
                  row(ln1_g[l]), row(ln1_b[l]))
        x2, pos, gate, cnt = _cross(x1, mem, xkv_w[l].astype(BF16), xq_w[l].astype(BF16), xo_w[l].astype(BF16),
                                    row(ln2_g[l]), row(ln2_b[l]), router_w[l].T.astype(BF16),
                                    router_b[l].reshape(-1, 1), tri)
        x2 = x2.reshape(t, d)
        nt = t // ROUTE_TILE
        n_rows = (-(-(t * TOP_K + nt * N_EXPERTS * (RUN_ALIGN - 1)) // EXPERT_ROWS) + N_EXPERTS) * EXPERT_ROWS
        pieces, tile_rows, seg_rows, gaps = _piece_tables(cnt[:, :, 0])
        block_expert, block_flags = _block_table(seg_rows, n_rows)
        xs = _dispatch(pieces, tile_rows, jnp.sum(seg_rows).reshape(1).astype(jnp.int32), gaps, pos, x2, n_rows)
        yb = _experts(block_expert, block_flags, xs, exp_gate[l], exp_up[l], exp_down[l])
        pos_t = jnp.transpose(pos, (0, 2, 1)).reshape(t, TOP_K)
        gate_t = jnp.transpose(gate, (0, 2, 1)).reshape(t, TOP_K)
        x = _combine(pieces, tile_rows, pos_t, gate_t, x2, sh_gate[l].astype(BF16),
                     sh_up[l].astype(BF16), sh_down[l].astype(BF16), row(ln3_g[l]), row(ln3_b[l]),
                     yb).reshape(bsz, seq, d)
    return x
```

```python
import math

import jax
import jax.numpy as jnp
from jax import lax
from jax.experimental import pallas as pl
from jax.experimental.pallas import tpu as pltpu

D_MODEL = 1024
HEAD_DIM = 64
CONV_CH = D_MODEL // 2
CONV_WIDTH = 31
ATT_HEADS = 8
KV_HEADS = 2
WINDOW = 128
BLOCK = 128
REL_BUCKETS = 32
REL_MAX_DIST = 128
Q_COLS = ATT_HEADS * HEAD_DIM
KV_COLS = KV_HEADS * HEAD_DIM
IN_COLS = 2 * CONV_CH + Q_COLS + 2 * KV_COLS
X_HEADS = 4
X_HEAD_DIM = D_MODEL // X_HEADS
N_EXPERTS = 64
TOP_K = 8
N_GROUPS = 8
GROUP_SIZE = N_EXPERTS // N_GROUPS
TOPK_GROUPS = 4
EXPERT_FF = D_MODEL // 4
ROUTED_SCALE = 2.5
DEPTH = 1
ALPHA = (2 * DEPTH) ** 0.25
LN_EPS = 1e-5
NEG_INF = -1e30

F32 = jnp.float32
BF16 = jnp.bfloat16

VMEM_LIMIT_BYTES = 56 * 1024 * 1024

ROW_TILE = 512
CONV_ROWS = 256
SUBLANES = 8
LANES = 128
CONV_HALO = 32
CROSS_TILE = 1024
CROSS_SPLIT = 2
SWA_TILE = 512
ROUTE_TILE = 512
EXPERT_ROWS = 1024
EXPERT_SPLIT = 2
RUN_ALIGN = 16
RUN_ALIGN_LOG2 = RUN_ALIGN.bit_length() - 1
WORD_ROWS = 2
TILE_CAP = ROUTE_TILE * TOP_K + N_EXPERTS * RUN_ALIGN
BIG_PIECE = 32
BIG_PER_TILE = TILE_CAP // BIG_PIECE
SMALL_PIECES = tuple(1 << b for b in range(BIG_PIECE.bit_length() - 2, RUN_ALIGN_LOG2 - 1, -1))
N_PIECE_TABLES = 3 * (1 + len(SMALL_PIECES))
PIECE_UNROLL = 8
TOTAL_PIECES = tuple(1 << b for b in range(TILE_CAP.bit_length() - 1, RUN_ALIGN_LOG2 - 1, -1))
SORT_CHUNK = 256
COMBINE_CHUNK = 512


def _cparams(sem):
    return pltpu.CompilerParams(dimension_semantics=sem, vmem_limit_bytes=VMEM_LIMIT_BYTES)


def _layer_norm(h, g, b):
    mu = jnp.mean(h, axis=-1, keepdims=True)
    d = h - mu
    var = jnp.mean(d * d, axis=-1, keepdims=True)
    return d * lax.rsqrt(var + LN_EPS) * g + b


def _sigmoid(x):
    return 1.0 / (1.0 + jnp.exp(-x))


def _silu(x):
    return x * (0.5 * jnp.tanh(0.5 * x) + 0.5)


def _mix_kernel(x_ref, w_ref, b_ref, cw_ref, cb_ref, cg_ref, cbeta_ref,
                conv_ref, q_ref, k_ref, v_ref, u_ext, u_sh):
    j = pl.program_id(1)
    xb = x_ref[0].astype(BF16)
    proj = jnp.dot(xb, w_ref[...], preferred_element_type=F32) + b_ref[...]
    a = proj[:, :CONV_CH]
    g = proj[:, CONV_CH:2 * CONV_CH]
    q_ref[0] = (proj[:, 2 * CONV_CH:2 * CONV_CH + Q_COLS] * (HEAD_DIM ** -0.5)).astype(BF16)
    k_ref[0] = proj[:, 2 * CONV_CH + Q_COLS:2 * CONV_CH + Q_COLS + KV_COLS].astype(BF16)
    v_ref[0] = proj[:, 2 * CONV_CH + Q_COLS + KV_COLS:].astype(BF16)

    @pl.when(j == 0)
    def _():
        u_ext[0:CONV_HALO, :] = jnp.zeros((CONV_HALO, CONV_CH), F32)

    u_ext[CONV_HALO:CONV_HALO + ROW_TILE, :] = a * _sigmoid(g)

    first_tap = CONV_HALO - (CONV_WIDTH - 1)
    shifted_rows = u_sh.shape[1]
    for r in range(1, SUBLANES):
        u_sh[r - 1] = u_ext[r:r + shifted_rows, :]

    for c in range(ROW_TILE // CONV_ROWS):
        acc = jnp.zeros((CONV_ROWS, CONV_CH), F32) + cb_ref[...]
        for t in range(CONV_WIDTH):
            r = (first_tap + t) % SUBLANES
            base = c * CONV_ROWS + (first_tap + t) - r
            if r == 0:
                taps = u_ext[base:base + CONV_ROWS, :]
            else:
                taps = u_sh[r - 1, base:base + CONV_ROWS, :]
            acc = acc + taps * cw_ref[t:t + 1, :]
        y = _layer_norm(acc, cg_ref[...], cbeta_ref[...])
        conv_ref[0, c * CONV_ROWS:(c + 1) * CONV_ROWS, :] = (y * _sigmoid(y)).astype(BF16)

    u_ext[0:CONV_HALO, :] = u_ext[ROW_TILE:ROW_TILE + CONV_HALO, :]


def _mix(x, w_in, b_in, conv_w, conv_b, conv_g, conv_beta):
    bsz, seq, d = x.shape
    nt = seq // ROW_TILE
    row = lambda b, j: (b, j, 0)
    const2 = lambda b, j: (0, 0)
    return pl.pallas_call(
        _mix_kernel,
        grid=(bsz, nt),
        in_specs=[
            pl.BlockSpec((1, ROW_TILE, d), row),
            pl.BlockSpec((d, IN_COLS), const2),
            pl.BlockSpec((1, IN_COLS), const2),
            pl.BlockSpec((CONV_WIDTH, CONV_CH), const2),
            pl.BlockSpec((1, CONV_CH), const2),
            pl.BlockSpec((1, CONV_CH), const2),
            pl.BlockSpec((1, CONV_CH), const2),
        ],
        out_specs=[
            pl.BlockSpec((1, ROW_TILE, CONV_CH), row),
            pl.BlockSpec((1, ROW_TILE, Q_COLS), row),
            pl.BlockSpec((1, ROW_TILE, KV_COLS), row),
            pl.BlockSpec((1, ROW_TILE, KV_COLS), row),
        ],
        out_shape=[
            jax.ShapeDtypeStruct((bsz, seq, CONV_CH), BF16),
            jax.ShapeDtypeStruct((bsz, seq, Q_COLS), BF16),
            jax.ShapeDtypeStruct((bsz, seq, KV_COLS), BF16),
            jax.ShapeDtypeStruct((bsz, seq, KV_COLS), BF16),
        ],
        scratch_shapes=[pltpu.VMEM((ROW_TILE + CONV_HALO, CONV_CH), F32),
                        pltpu.VMEM((SUBLANES - 1, ROW_TILE + CONV_HALO - SUBLANES, CONV_CH), F32)],
        compiler_params=_cparams(("arbitrary", "arbitrary")),
        name="mix",
    )(x, w_in, b_in, conv_w, conv_b, conv_g, conv_beta)


def _swa_kernel(q_ref, kp_ref, kc_ref, vp_ref, vc_ref, bias_ref, sink_ref, x_ref, conv_ref, w_ref, b_ref,
                g_ref, beta_ref, o_ref):
    j = pl.program_id(1)
    mix = jnp.dot(conv_ref[0], w_ref[0:CONV_CH, :], preferred_element_type=F32)
    rows = BLOCK + SWA_TILE
    lane = lax.broadcasted_iota(jnp.int32, (rows, 2 * HEAD_DIM), 1)
    low = lane < HEAD_DIM

    def placements(prev_ref, cur_ref):
        t = jnp.concatenate([prev_ref[0], cur_ref[0]], axis=0).astype(F32)
        tr = pltpu.roll(t, HEAD_DIM, 1)
        zero = jnp.zeros_like(t)
        kv0_low = jnp.where(low, t, zero).astype(BF16)
        kv1_high = jnp.where(low, zero, t).astype(BF16)
        kv1_low = jnp.where(low, tr, zero).astype(BF16)
        kv0_high = jnp.where(low, zero, tr).astype(BF16)
        return (kv0_low, kv0_high, kv1_low, kv1_high)

    ks = placements(kp_ref, kc_ref)
    vs = placements(vp_ref, vc_ref)
    slab = 2 * HEAD_DIM
    att = []
    for i in range(SWA_TILE // BLOCK):
        q = q_ref[0, i * BLOCK:(i + 1) * BLOCK, :]
        q_kv0 = jnp.concatenate([q[:, 0:slab], q[:, slab:2 * slab]], axis=0)
        q_kv1 = jnp.concatenate([q[:, 2 * slab:3 * slab], q[:, 3 * slab:4 * slab]], axis=0)
        band = slice(i * BLOCK, (i + 2) * BLOCK)
        logits = []
        for s in range(4):
            qs = q_kv0 if s < 2 else q_kv1
            bias = bias_ref[1, s]
            if i == 0:
                bias = jnp.where(j == 0, bias_ref[0, s], bias)
            logits.append(lax.dot_general(qs, ks[s][band], (((1,), (1,)), ((), ())),
                                          preferred_element_type=F32) + bias)
        ps, dens = [], []
        for s in range(4):
            sink = sink_ref[s]
            m = jnp.maximum(jnp.max(logits[s], axis=-1, keepdims=True), sink)
            p = jnp.exp(logits[s] - m)
            ps.append(p.astype(BF16))
            dens.append(jnp.sum(p, axis=-1, keepdims=True) + jnp.exp(sink - m))
        outs = [jnp.dot(ps[s], vs[s][band], preferred_element_type=F32) / dens[s] for s in range(4)]
        o_kv0 = outs[0] + outs[1]
        o_kv1 = outs[2] + outs[3]
        att.append(jnp.concatenate([o_kv0[0:BLOCK], o_kv0[BLOCK:2 * BLOCK], o_kv1[0:BLOCK],
                                    o_kv1[BLOCK:2 * BLOCK]], axis=1).astype(BF16))

    mix = mix + jnp.dot(jnp.concatenate(att, axis=0), w_ref[CONV_CH:, :], preferred_element_type=F32)
    h = ALPHA * x_ref[0] + mix + b_ref[...]
    o_ref[0] = _layer_norm(h, g_ref[...], beta_ref[...])


def _swa(q, k, v, bias_tab, sink_tab, x, conv_out, w_out, b_out, g, beta):
    bsz, seq, d = x.shape
    per = SWA_TILE // BLOCK
    cur = lambda b, n: (b, n, 0)
    prev = lambda b, n: (b, jnp.maximum(n * per - 1, 0), 0)
    whole = lambda b, n: (0, 0, 0, 0)
    const = lambda b, n: (0, 0)
    return pl.pallas_call(
        _swa_kernel,
        grid=(bsz, seq // SWA_TILE),
        in_specs=[
            pl.BlockSpec((1, SWA_TILE, Q_COLS), cur),
            pl.BlockSpec((1, BLOCK, KV_COLS), prev),
            pl.BlockSpec((1, SWA_TILE, KV_COLS), cur),
            pl.BlockSpec((1, BLOCK, KV_COLS), prev),
            pl.BlockSpec((1, SWA_TILE, KV_COLS), cur),
            pl.BlockSpec((2, 4, 2 * BLOCK, 2 * BLOCK), whole),
            pl.BlockSpec((4, 2 * BLOCK, 1), lambda b, n: (0, 0, 0)),
            pl.BlockSpec((1, SWA_TILE, d), cur),
            pl.BlockSpec((1, SWA_TILE, CONV_CH), cur),
            pl.BlockSpec((d, d), const),
            pl.BlockSpec((1, d), const),
            pl.BlockSpec((1, d), const),
            pl.BlockSpec((1, d), const),
        ],
        out_specs=pl.BlockSpec((1, SWA_TILE, d), cur),
        out_shape=jax.ShapeDtypeStruct((bsz, seq, d), F32),
        compiler_params=_cparams(("arbitrary", "arbitrary")),
        name="swa",
    )(q, k, k, v, v, bias_tab, sink_tab, x, conv_out, w_out, b_out, g, beta)


def _t5_bucket(dist):
    n = jnp.maximum(dist, 0)
    exact = REL_BUCKETS // 2
    large = exact + (jnp.log(jnp.maximum(n, 1).astype(F32) / exact)
                     / math.log(REL_MAX_DIST / exact) * (REL_BUCKETS - exact)).astype(jnp.int32)
    large = jnp.minimum(large, REL_BUCKETS - 1)
    return jnp.where(n < exact, n, large)


def _band_tables(rel_bias, sinks):
    qi = jnp.arange(BLOCK)[:, None]
    kj = jnp.arange(2 * BLOCK)[None, :]
    dist = qi + BLOCK - kj
    bucket = _t5_bucket(dist)
    bias = jnp.zeros((ATT_HEADS, BLOCK, 2 * BLOCK), F32)
    for bkt in range(REL_BUCKETS):
        bias = jnp.where(bucket[None] == bkt, rel_bias[bkt].astype(F32)[:, None, None], bias)
    in_window = (dist >= 0) & (dist < WINDOW)
    masks = jnp.stack([in_window & (kj >= BLOCK), in_window])
    masked = jnp.where(masks[:, None], bias[None], NEG_INF)
    pairs = ((0, 2), (1, 3), (4, 6), (5, 7))
    bias_tab = jnp.stack([jnp.concatenate([masked[:, a], masked[:, b]], axis=1) for a, b in pairs], axis=1)
    s = sinks.astype(F32)
    sink_tab = jnp.stack([jnp.concatenate([jnp.full((BLOCK, 1), s[a]), jnp.full((BLOCK, 1), s[b])], axis=0)
                          for a, b in pairs])
    return bias_tab, sink_tab


def _cross_kernel(x_ref, mem_ref, wkv_ref, wq_ref, wo_ref, g_ref, beta_ref, rw_ref, rb_ref, tri_ref,
                  o_ref, pos_ref, gate_ref, cnt_ref, k_ref, v_ref):
    @pl.when(pl.program_id(1) == 0)
    def _():
        kv = jnp.dot(mem_ref[0].astype(BF16), wkv_ref[...], preferred_element_type=F32)
        k_ref[...] = kv[:, :D_MODEL].astype(BF16)
        v_ref[...] = kv[:, D_MODEL:].astype(BF16)

    sub = CROSS_TILE // CROSS_SPLIT
    groups = [slice(s * sub, (s + 1) * sub) for s in range(CROSS_SPLIT)]
    xs = [x_ref[0, g, :] for g in groups]
    qs = [(jnp.dot(x.astype(BF16), wq_ref[...], preferred_element_type=F32) * (X_HEAD_DIM ** -0.5)).astype(BF16)
          for x in xs]
    heads = [[] for _ in groups]
    for h in range(X_HEADS):
        cols = slice(h * X_HEAD_DIM, (h + 1) * X_HEAD_DIM)
        logits = [lax.dot_general(q[:, cols], k_ref[:, cols], (((1,), (1,)), ((), ())),
                                  preferred_element_type=F32) for q in qs]
        ps, dens = [], []
        for lg in logits:
            p = jnp.exp(lg - jnp.max(lg, axis=-1, keepdims=True))
            ps.append(p)
            dens.append(jnp.sum(p, axis=-1, keepdims=True))
        for s, (p, den) in enumerate(zip(ps, dens)):
            o = jnp.dot(p.astype(BF16), v_ref[:, cols], preferred_element_type=F32)
            heads[s].append((o / den).astype(BF16))
    crosses = [jnp.dot(jnp.concatenate(hs, axis=-1), wo_ref[...], preferred_element_type=F32) for hs in heads]
    for s, (g, x, cross) in enumerate(zip(groups, xs, crosses)):
        y = _layer_norm(ALPHA * x + cross, g_ref[...], beta_ref[...])
        o_ref[0, g, :] = y
        logits_t = lax.dot_general(rw_ref[...], y.astype(BF16), (((1,), (1,)), ((), ())),
                                   preferred_element_type=F32)
        positions, gates, run_len = _route_tile(logits_t, rb_ref[...], tri_ref[...])
        cnt_ref[s] = run_len
        for r in range(TOP_K):
            pos_ref[s, r:r + 1, :] = positions[r]
            gate_ref[s, r:r + 1, :] = gates[r]


def _cross(x1, mem_in, wkv, wq, wo, g, beta, rw_t, router_b, tri):
    bsz, seq, d = x1.shape
    mem_len = mem_in.shape[1]
    nt = seq // CROSS_TILE
    assert CROSS_TILE // CROSS_SPLIT == ROUTE_TILE
    n_tiles = bsz * seq // ROUTE_TILE
    row = lambda b, j: (b, j, 0)
    mem = lambda b, j: (b, 0, 0)
    const = lambda b, j: (0, 0)
    tiles = lambda b, j: (b * nt + j, 0, 0)
    return pl.pallas_call(
        _cross_kernel,
        grid=(bsz, nt),
        in_specs=[
            pl.BlockSpec((1, CROSS_TILE, d), row),
            pl.BlockSpec((1, mem_len, d), mem),
            pl.BlockSpec((d, 2 * d), const),
            pl.BlockSpec((d, d), const),
            pl.BlockSpec((d, d), const),
            pl.BlockSpec((1, d), const),
            pl.BlockSpec((1, d), const),
            pl.BlockSpec((N_EXPERTS, d), const),
            pl.BlockSpec((N_EXPERTS, 1), const),
            pl.BlockSpec((ROUTE_TILE, ROUTE_TILE), const),
        ],
        out_specs=[
            pl.BlockSpec((1, CROSS_TILE, d), row),
            pl.BlockSpec((CROSS_SPLIT, TOP_K, ROUTE_TILE), tiles),
            pl.BlockSpec((CROSS_SPLIT, TOP_K, ROUTE_TILE), tiles),
            pl.BlockSpec((CROSS_SPLIT, N_EXPERTS, LANES), tiles),
        ],
        out_shape=[
            jax.ShapeDtypeStruct((bsz, seq, d), F32),
            jax.ShapeDtypeStruct((n_tiles, TOP_K, ROUTE_TILE), jnp.int32),
            jax.ShapeDtypeStruct((n_tiles, TOP_K, ROUTE_TILE), F32),
            jax.ShapeDtypeStruct((n_tiles, N_EXPERTS, LANES), jnp.int32),
        ],
        scratch_shapes=[pltpu.VMEM((mem_len, d), BF16), pltpu.VMEM((mem_len, d), BF16)],
        compiler_params=_cparams(("arbitrary", "arbitrary")),
        name="cross",
    )(x1, mem_in, wkv, wq, wo, g, beta, rw_t, router_b, tri)


def _route_tile(logits_t, router_b, tri):
    tn = ROUTE_TILE
    scores = _sigmoid(logits_t)
    choice = scores + router_b

    gscore = []
    member = lax.broadcasted_iota(jnp.int32, (GROUP_SIZE, tn), 0).astype(F32)
    for g in range(N_GROUPS):
        c = choice[g * GROUP_SIZE:(g + 1) * GROUP_SIZE, :]
        m1 = jnp.max(c, axis=0, keepdims=True)
        first = jnp.min(jnp.where(c == m1, member, float(GROUP_SIZE)), axis=0, keepdims=True)
        m2 = jnp.max(jnp.where(member == first, -jnp.inf, c), axis=0, keepdims=True)
        gscore.append(m1 + m2)

    keep_rows = []
    for g in range(N_GROUPS):
        beaten = jnp.zeros((1, tn), F32)
        for o in range(N_GROUPS):
            if o == g:
                continue
            ahead = (gscore[o] >= gscore[g]) if o < g else (gscore[o] > gscore[g])
            beaten = beaten + jnp.where(ahead, 1.0, 0.0)
        keep_rows.append(jnp.broadcast_to(beaten, (GROUP_SIZE, tn)))
    beaten_all = jnp.concatenate(keep_rows, axis=0)

    masked = jnp.where(beaten_all < TOPK_GROUPS, choice, -jnp.inf)
    eidx = lax.broadcasted_iota(jnp.int32, (N_EXPERTS, tn), 0).astype(F32)
    sel = jnp.zeros((N_EXPERTS, tn), F32)
    picks, weights = [], []
    for r in range(TOP_K):
        mx = jnp.max(masked, axis=0, keepdims=True)
        first = jnp.min(jnp.where(masked == mx, eidx, float(N_EXPERTS)), axis=0, keepdims=True)
        pick = eidx == first
        picks.append((pick, first))
        weights.append(jnp.sum(jnp.where(pick, scores, 0.0), axis=0, keepdims=True))
        masked = jnp.where(pick, -jnp.inf, masked)
        sel = jnp.where(pick, 1.0, sel)

    wsum = weights[0]
    for r in range(1, TOP_K):
        wsum = wsum + weights[r]

    count = jnp.sum(sel, axis=1, keepdims=True)
    run_len = jnp.floor((count + (RUN_ALIGN - 1.0)) * (1.0 / RUN_ALIGN)) * RUN_ALIGN
    run_len_b = jnp.broadcast_to(run_len, (N_EXPERTS, LANES))
    er = lax.broadcasted_iota(jnp.int32, (N_EXPERTS, N_EXPERTS), 0)
    ec = lax.broadcasted_iota(jnp.int32, (N_EXPERTS, N_EXPERTS), 1)
    before = jnp.where(ec < er, 1.0, 0.0).astype(BF16)
    run_start = jnp.dot(before, run_len_b.astype(BF16), preferred_element_type=F32)[:, 0:1]
    incl = jnp.dot(sel.astype(BF16), tri, preferred_element_type=F32)
    pos_mat = run_start + incl - sel
    positions = [jnp.sum(jnp.where(pick, pos_mat, 0.0), axis=0, keepdims=True).astype(jnp.int32)
                 for pick, _ in picks]
    gates = [w / wsum * ROUTED_SCALE for w in weights]
    return positions, gates, run_len_b.astype(jnp.int32)


def _for_each_piece(tables, tile, fn):
    lists = [(BIG_PIECE, BIG_PER_TILE)] + [(size, N_EXPERTS) for size in SMALL_PIECES]
    for k, (n_rows, per_tile) in enumerate(lists):
        local_ref, global_ref, count_ref = tables[3 * k:3 * k + 3]
        count = count_ref[tile]

        def body(g, carry, n_rows=n_rows, per_tile=per_tile, local_ref=local_ref, global_ref=global_ref,
                 count=count):
            for u in range(PIECE_UNROLL):
                j = g * PIECE_UNROLL + u

                @pl.when(j < count)
                def _(u=u):
                    fn(pl.multiple_of(local_ref[tile * per_tile + j], RUN_ALIGN),
                       pl.multiple_of(global_ref[tile * per_tile + j], RUN_ALIGN), n_rows, u % 2)
            return carry

        lax.fori_loop(0, (count + PIECE_UNROLL - 1) // PIECE_UNROLL, body, 0)


def _word_rows(start, size):
    if isinstance(start, int):
        first = start // WORD_ROWS
    else:
        first = pl.multiple_of(start >> (WORD_ROWS.bit_length() - 1), RUN_ALIGN // WORD_ROWS)
    return pl.ds(first, size // WORD_ROWS)


def _for_each_total_piece(total, fn):
    for piece in TOTAL_PIECES:
        @pl.when((total & piece) != 0)
        def _():
            fn(piece)


def _dispatch_kernel(*refs):
    tables = refs[:N_PIECE_TABLES]
    rows_ref, total_ref, gap_ref, pos_ref, x_ref, xs_ref, stage, zeros, sem, zsem = refs[N_PIECE_TABLES:]
    i = pl.program_id(0)
    nt = pl.num_programs(0)
    slot = i % 2
    tn = ROUTE_TILE

    def drain(tile, buf):
        def wait_rows(n):
            pltpu.make_async_copy(stage.at[buf, _word_rows(0, n), :], xs_ref.at[_word_rows(0, n), :],
                                  sem.at[buf]).wait()
        _for_each_total_piece(rows_ref[tile], wait_rows)

    @pl.when(i >= 2)
    def _():
        drain(i - 2, slot)

    xb = x_ref[...].astype(BF16)
    pos = pos_ref[0]
    row_iota = lax.broadcasted_iota(jnp.int32, (SORT_CHUNK, tn), 0).astype(F32).astype(BF16)
    one = jnp.ones((SORT_CHUNK, tn), BF16)
    zero = jnp.zeros((SORT_CHUNK, tn), BF16)

    def onehot_rows(c):
        rel = jnp.clip(pos - c * SORT_CHUNK, -1, SORT_CHUNK).astype(F32).astype(BF16)
        hit = rel[0:1, :] == row_iota
        for k in range(1, TOP_K):
            hit = hit | (rel[k:k + 1, :] == row_iota)
        return jnp.where(hit, one, zero)

    def sort_chunks(c2, carry):
        for c in (2 * c2, 2 * c2 + 1):
            p0 = pl.multiple_of(c * SORT_CHUNK, SORT_CHUNK)
            rows = jnp.dot(onehot_rows(c), xb, preferred_element_type=F32)
            stage[slot, _word_rows(p0, SORT_CHUNK), :] = pltpu.bitcast(rows.astype(BF16), jnp.int32)
        return carry

    lax.fori_loop(0, (rows_ref[i] + 2 * SORT_CHUNK - 1) // (2 * SORT_CHUNK), sort_chunks, 0)

    def start_piece(local, dst, n, priority):
        pltpu.make_async_copy(stage.at[slot, _word_rows(local, n), :], xs_ref.at[_word_rows(dst, n), :],
                              sem.at[slot]).start(priority=priority)

    _for_each_piece(tables, i, start_piece)

    @pl.when(i == nt - 1)
    def _():
        @pl.when(nt >= 2)
        def _():
            drain(i - 1, 1 - slot)

        drain(i, slot)

        zeros[...] = jnp.zeros_like(zeros)

        def gap_copies(e, act):
            gap = gap_ref[N_EXPERTS + e]
            for piece in TOTAL_PIECES:
                if piece >= EXPERT_ROWS:
                    continue

                @pl.when((gap & piece) != 0)
                def _():
                    dst = gap_ref[e] + (gap & (-2 * piece))
                    act(pltpu.make_async_copy(zeros.at[_word_rows(0, piece), :],
                                              xs_ref.at[_word_rows(dst, piece), :], zsem))

        def zero_block(b):
            return pltpu.make_async_copy(zeros, xs_ref.at[_word_rows(b * EXPERT_ROWS, EXPERT_ROWS), :], zsem)

        first_free = total_ref[0] // EXPERT_ROWS
        n_blocks = xs_ref.shape[0] * WORD_ROWS // EXPERT_ROWS
        lax.fori_loop(0, N_EXPERTS, lambda e, c: (gap_copies(e, lambda cp: cp.start()), c)[1], 0)
        lax.fori_loop(first_free, n_blocks, lambda b, c: (zero_block(b).start(), c)[1], 0)
        lax.fori_loop(0, N_EXPERTS, lambda e, c: (gap_copies(e, lambda cp: cp.wait()), c)[1], 0)
        lax.fori_loop(first_free, n_blocks, lambda b, c: (zero_block(b).wait(), c)[1], 0)


def _dispatch(pieces, tile_rows, total, gaps, pos, x2d, n_rows):
    t, d = x2d.shape
    nt = t // ROUTE_TILE
    return pl.pallas_call(
        _dispatch_kernel,
        grid_spec=pltpu.PrefetchScalarGridSpec(
            num_scalar_prefetch=len(pieces) + 3,
            grid=(nt,),
            in_specs=[pl.BlockSpec((1, TOP_K, ROUTE_TILE), lambda i, *_: (i, 0, 0)),
                      pl.BlockSpec((ROUTE_TILE, d), lambda i, *_: (i, 0))],
            out_specs=pl.BlockSpec(memory_space=pl.ANY),
            scratch_shapes=[pltpu.VMEM((2, TILE_CAP // WORD_ROWS, d), jnp.int32),
                            pltpu.VMEM((EXPERT_ROWS // WORD_ROWS, d), jnp.int32),
                            pltpu.SemaphoreType.DMA((2,)),
                            pltpu.SemaphoreType.DMA(())],
        ),
        out_shape=jax.ShapeDtypeStruct((n_rows // WORD_ROWS, d), jnp.int32),
        compiler_params=_cparams(("arbitrary",)),
        name="dispatch",
    )(*pieces, tile_rows, total, gaps, pos, x2d)


def _experts_kernel(ie_ref, flag_ref, xs_ref, wg_ref, wu_ref, wd_ref, y_ref, wg_bf, wu_bf, wd_bf):
    w = pl.program_id(0)
    flags = flag_ref[w]
    valid = (flags & 1) != 0
    new_expert = (flags & 2) != 0

    @pl.when(new_expert)
    def _():
        wg_bf[...] = wg_ref[0].astype(BF16)
        wu_bf[...] = wu_ref[0].astype(BF16)
        wd_bf[...] = wd_ref[0].astype(BF16)

    @pl.when(valid)
    def _():
        sub = EXPERT_ROWS // WORD_ROWS // EXPERT_SPLIT
        groups = [slice(s * sub, (s + 1) * sub) for s in range(EXPERT_SPLIT)]
        xs = [pltpu.bitcast(xs_ref[g, :], BF16) for g in groups]
        gu = [(jnp.dot(x, wg_bf[...], preferred_element_type=F32),
               jnp.dot(x, wu_bf[...], preferred_element_type=F32)) for x in xs]
        hs = [(_silu(gte) * up).astype(BF16) for gte, up in gu]
        for g, h in zip(groups, hs):
            y = jnp.dot(h, wd_bf[...], preferred_element_type=F32)
            y_ref[g, :] = pltpu.bitcast(y.astype(BF16), jnp.int32)


def _experts(block_expert, block_flags, xs, wg, wu, wd):
    n, d = xs.shape
    ff = wg.shape[-1]
    rows_block = lambda w, ie, fl: (jnp.minimum(w, fl[fl.shape[0] - 1]), 0)
    return pl.pallas_call(
        _experts_kernel,
        grid_spec=pltpu.PrefetchScalarGridSpec(
            num_scalar_prefetch=2,
            grid=(block_expert.shape[0],),
            in_specs=[
                pl.BlockSpec((EXPERT_ROWS // WORD_ROWS, d), rows_block),
                pl.BlockSpec((1, d, ff), lambda w, ie, fl: (ie[w], 0, 0)),
                pl.BlockSpec((1, d, ff), lambda w, ie, fl: (ie[w], 0, 0)),
                pl.BlockSpec((1, ff, d), lambda w, ie, fl: (ie[w], 0, 0)),
            ],
            out_specs=pl.BlockSpec((EXPERT_ROWS // WORD_ROWS, d), rows_block),
            scratch_shapes=[pltpu.VMEM((d, ff), BF16), pltpu.VMEM((d, ff), BF16), pltpu.VMEM((ff, d), BF16)],
        ),
        out_shape=jax.ShapeDtypeStruct((n, d), jnp.int32),
        input_output_aliases={2: 0},
        compiler_params=_cparams(("arbitrary",)),
        name="experts",
    )(block_expert, block_flags, xs, wg, wu, wd)


def _block_table(seg_rows, n_rows):
    block_end = jnp.cumsum(seg_rows // EXPERT_ROWS)
    w = jnp.arange(n_rows // EXPERT_ROWS, dtype=jnp.int32)
    valid = w < block_end[-1]
    e = jnp.sum((block_end[None, :] <= jnp.minimum(w, block_end[-1] - 1)[:, None]).astype(jnp.int32), axis=1)
    e = jnp.minimum(e, N_EXPERTS - 1)
    prev_e = jnp.concatenate([jnp.full((1,), -1, jnp.int32), e[:-1]])
    flags = valid.astype(jnp.int32) + 2 * (valid & (e != prev_e)).astype(jnp.int32)
    return e, jnp.concatenate([flags, block_end[-1:].astype(jnp.int32) - 1])


def _combine_kernel(*refs):
    tables = refs[:N_PIECE_TABLES]
    (rows_ref, pos_ref, gate_ref, x_ref, sg_ref, su_ref, sd_ref, g_ref, beta_ref, yb_ref, o_ref,
     stage, acc, lane_tile, lane_gate, sem) = refs[N_PIECE_TABLES:]
    i = pl.program_id(0)
    nt = pl.num_programs(0)
    slot = i % 2
    tn = ROUTE_TILE

    def start_tile(tile, buf):
        def start_piece(local, src, n, priority):
            pltpu.make_async_copy(yb_ref.at[_word_rows(src, n), :], stage.at[buf, _word_rows(local, n), :],
                                  sem.at[buf]).start(priority=priority)
        _for_each_piece(tables, tile, start_piece)

    @pl.when(i == 0)
    def _():
        stage[...] = jnp.zeros_like(stage)
        start_tile(0, 0)

    @pl.when(i + 1 < nt)
    def _():
        start_tile(i + 1, 1 - slot)

    x = x_ref[...]
    xb = x.astype(BF16)
    gte = jnp.dot(xb, sg_ref[...], preferred_element_type=F32)
    up = jnp.dot(xb, su_ref[...], preferred_element_type=F32)
    h = (_silu(gte) * up).astype(BF16)
    acc[...] = jnp.dot(h, sd_ref[...], preferred_element_type=F32)

    def wait_rows(n):
        pltpu.make_async_copy(yb_ref.at[_word_rows(0, n), :], stage.at[slot, _word_rows(0, n), :],
                              sem.at[slot]).wait()

    _for_each_total_piece(rows_ref[i], wait_rows)

    lanes = LANES
    lane_iota = lax.broadcasted_iota(jnp.int32, (tn, lanes), 1)
    for k in range(TOP_K):
        p = jnp.broadcast_to(pos_ref[:, k:k + 1], (tn, lanes))
        g = jnp.broadcast_to(gate_ref[:, k:k + 1], (tn, lanes))
        lane_tile[k] = (p >> (lanes.bit_length() - 1)).astype(F32).astype(BF16)
        lane_gate[k] = jnp.where((p & (lanes - 1)) == lane_iota, g, 0.0).astype(BF16)
    zero = jnp.zeros((tn, lanes), BF16)

    def gate_matrix(c):
        cols = []
        for j in range(COMBINE_CHUNK // lanes):
            tile = jnp.asarray(c * (COMBINE_CHUNK // lanes) + j, jnp.int32).astype(F32).astype(BF16)
            w = zero
            for k in range(TOP_K):
                w = w + jnp.where(lane_tile[k] == tile, lane_gate[k], zero)
            cols.append(w)
        return jnp.concatenate(cols, axis=1)

    def weighed(c):
        p0 = pl.multiple_of(c * COMBINE_CHUNK, COMBINE_CHUNK)
        rows = pltpu.bitcast(stage[slot, _word_rows(p0, COMBINE_CHUNK), :], BF16)
        return jnp.dot(gate_matrix(c), rows, preferred_element_type=F32)

    def weigh_chunks(c2, carry):
        acc[...] += weighed(2 * c2) + weighed(2 * c2 + 1)
        return carry

    n_chunks = (rows_ref[i] + COMBINE_CHUNK - 1) // COMBINE_CHUNK
    lax.fori_loop(0, n_chunks // 2, weigh_chunks, 0)

    @pl.when(n_chunks % 2 == 1)
    def _():
        acc[...] += weighed(n_chunks - 1)
    o_ref[...] = _layer_norm(ALPHA * x + acc[...], g_ref[...], beta_ref[...])


def _combine(pieces, tile_rows, pos_t, gate_t, x2d, sg, su, sd, g, beta, yb):
    t, d = x2d.shape
    nt = t // ROUTE_TILE
    ff = sg.shape[-1]
    row = lambda i, *_: (i, 0)
    const = lambda i, *_: (0, 0)
    return pl.pallas_call(
        _combine_kernel,
        grid_spec=pltpu.PrefetchScalarGridSpec(
            num_scalar_prefetch=len(pieces) + 1,
            grid=(nt,),
            in_specs=[
                pl.BlockSpec((ROUTE_TILE, TOP_K), row),
                pl.BlockSpec((ROUTE_TILE, TOP_K), row),
                pl.BlockSpec((ROUTE_TILE, d), row),
                pl.BlockSpec((d, ff), const),
                pl.BlockSpec((d, ff), const),
                pl.BlockSpec((ff, d), const),
                pl.BlockSpec((1, d), const),
                pl.BlockSpec((1, d), const),
                pl.BlockSpec(memory_space=pl.ANY),
            ],
            out_specs=pl.BlockSpec((ROUTE_TILE, d), row),
            scratch_shapes=[pltpu.VMEM((2, TILE_CAP // WORD_ROWS, d), jnp.int32),
                            pltpu.VMEM((ROUTE_TILE, d), F32),
                            pltpu.VMEM((TOP_K, ROUTE_TILE, LANES), BF16),
                            pltpu.VMEM((TOP_K, ROUTE_TILE, LANES), BF16),
                            pltpu.SemaphoreType.DMA((2,))],
        ),
        out_shape=jax.ShapeDtypeStruct((t, d), F32),
        compiler_params=_cparams(("arbitrary",)),
        name="combine",
    )(*pieces, tile_rows, pos_t, gate_t, x2d, sg, su, sd, g, beta, yb)


def _piece_tables(cnt):
    lbase = jnp.cumsum(cnt, axis=1) - cnt
    per_expert = jnp.sum(cnt, axis=0)
    seg_rows = -(-per_expert // EXPERT_ROWS) * EXPERT_ROWS
    seg_start = jnp.cumsum(seg_rows) - seg_rows
    gbase = seg_start[None, :] + jnp.cumsum(cnt, axis=0) - cnt
    gaps = jnp.concatenate([seg_start + per_expert, seg_rows - per_expert]).astype(jnp.int32)

    def listed(n_pieces, start_off, piece_rows, length):
        last = jnp.cumsum(n_pieces, axis=1)
        first = last - n_pieces
        j = jnp.arange(length, dtype=jnp.int32)[None, :, None]
        own = (j >= first[:, None, :]) & (j < last[:, None, :])
        off = (start_off[:, None, :] + (j - first[:, None, :]) * piece_rows)
        pick = lambda base: jnp.sum(jnp.where(own, base[:, None, :] + off, 0), axis=-1).reshape(-1).astype(jnp.int32)
        return pick(lbase), pick(gbase), last[:, -1].astype(jnp.int32)

    tables = listed(cnt // BIG_PIECE, jnp.zeros_like(cnt), BIG_PIECE, BIG_PER_TILE)
    for size in SMALL_PIECES:
        tables += listed((cnt // size) % 2, cnt // (2 * size) * (2 * size), size, N_EXPERTS)
    return tables, jnp.sum(cnt, axis=1).astype(jnp.int32), seg_rows, gaps


def kernel(x, mem, w_in, b_in, conv_w, conv_b, conv_ln_g, conv_ln_b, attn_sinks, rel_bias, w_out, b_out, ln1_g, ln1_b, xq_w, xkv_w, xo_w, ln2_g, ln2_b, router_w, router_b, exp_gate, exp_up, exp_down, sh_gate, sh_up, sh_down, ln3_g, ln3_b):
    bsz, seq, d = x.shape
    t = bsz * seq
    bias_tab, sink_tab = _band_tables(rel_bias, attn_sinks[0])
    tri = (jnp.arange(ROUTE_TILE)[:, None] <= jnp.arange(ROUTE_TILE)[None, :]).astype(BF16)
    row = lambda p: p.reshape(1, -1)
    for l in range(DEPTH):
        conv_out, q, k, v = _mix(x, w_in[l].astype(BF16), row(b_in[l]), conv_w[l], row(conv_b[l]),
                                 row(conv_ln_g[l]), row(conv_ln_b[l]))
        x1 = _swa(q, k, v, bias_tab, sink_tab, x, conv_out, w_out[l].astype(BF16), row(b_out[l]),
                  row(ln1_g[l]), row(ln1_b[l]))
        x2, pos, gate, cnt = _cross(x1, mem, xkv_w[l].astype(BF16), xq_w[l].astype(BF16), xo_w[l].astype(BF16),
                                    row(ln2_g[l]), row(ln2_b[l]), router_w[l].T.astype(BF16),
                                    router_b[l].reshape(-1, 1), tri)
        x2 = x2.reshape(t, d)
        nt = t // ROUTE_TILE
        n_rows = (-(-(t * TOP_K + nt * N_EXPERTS * (RUN_ALIGN - 1)) // EXPERT_ROWS) + N_EXPERTS) * EXPERT_ROWS
        pieces, tile_rows, seg_rows, gaps = _piece_tables(cnt[:, :, 0])
        block_expert, block_flags = _block_table(seg_rows, n_rows)
        xs = _dispatch(pieces, tile_rows, jnp.sum(seg_rows).reshape(1).astype(jnp.int32), gaps, pos, x2, n_rows)
        yb = _experts(block_expert, block_flags, xs, exp_gate[l], exp_up[l], exp_down[l])
        pos_t = jnp.transpose(pos, (0, 2, 1)).reshape(t, TOP_K)
        gate_t = jnp.transpose(gate, (0, 2, 1)).reshape(t, TOP_K)
        x = _combine(pieces, tile_rows, pos_t, gate_t, x2, sh_gate[l].astype(BF16),
                     sh_up[l].astype(BF16), sh_down[l].astype(BF16), row(ln3_g[l]), row(ln3_b[l]),
                     yb).reshape(bsz, seq, d)
    return x
```
